```python
import jax, jax.numpy as jnp
from jax import lax
import numpy as np

D_MODEL = 1024
BATCH = 8
SEQ = 4096
DEPTH = 4
DEC_BATCH = 16
DEC_SEQ = 32
PAST_LEN = 2048

CHUNK = 64
N_MIXERS = 4
LAYERS_PER_MIXER = DEPTH // N_MIXERS
Q_BLOCK = 128
HEAD_DIM = 64
ROPE_THETA = 500000.0
NORM_EPS = 1e-6
A_HEADS = 16
A_Q_LORA = 384
A_KV_LORA = 256
A_NOPE = 64
A_ROPE = 32
A_V = 64
B_HEADS = 16
C_HEADS = 16
C_KV_HEADS = 4
C_WINDOW = 128
C_PREV_CHUNKS = C_WINDOW // CHUNK
C_ROT = HEAD_DIM // 4
D_HEADS = 16
D_PREV_CHUNKS = 8
D_PAST = D_PREV_CHUNKS * CHUNK
D_REL_CLIP = 128
FFN_HIDDEN = ((8 * D_MODEL // 3 + 255) // 256) * 256

kernel_name = 'hybrid_streaming_encoder_step'


def rmsnorm(x, g):
    x32 = x.astype(jnp.float32)
    y = x32 * lax.rsqrt(jnp.mean(x32 * x32, axis=-1, keepdims=True) + NORM_EPS)
    return (y * g.astype(jnp.float32)).astype(x.dtype)


def rope(x, pos, n_rot):
    half = n_rot // 2
    inv = ROPE_THETA ** (-jnp.arange(half, dtype=jnp.float32) * 2.0 / n_rot)
    ang = pos.astype(jnp.float32)[:, None] * inv[None, :]
    cos = jnp.cos(ang)[:, None, :]
    sin = jnp.sin(ang)[:, None, :]
    x32 = x.astype(jnp.float32)
    x1 = x32[..., :half]
    x2 = x32[..., half:n_rot]
    out = jnp.concatenate([x1 * cos - x2 * sin, x2 * cos + x1 * sin, x32[..., n_rot:]], axis=-1)
    return out.astype(x.dtype)


def adaln(c, w, b):
    return jnp.split(jax.nn.silu(c) @ w + b, 6, axis=-1)


def modulate(x, g, shift, scale):
    return rmsnorm(x, g) * (1 + scale[:, None, :]) + shift[:, None, :]


def flat_heads(o):
    return o.reshape(o.shape[:2] + (-1,))


def chunk_mask(q_pos, k_pos, n_prev):
    qc = q_pos[:, None] // CHUNK
    kc = k_pos[None, :] // CHUNK
    m = (kc <= qc) & (k_pos[None, :] >= 0)
    if n_prev is not None:
        m = m & (kc >= qc - n_prev)
    return m


def map_blocks(fn, xs, block):
    B, S = xs[0].shape[:2]
    nb = S // block
    blk = tuple(jnp.moveaxis(a.reshape((B, nb, block) + a.shape[2:]), 1, 0) for a in xs)
    out = lax.map(lambda args: fn(args[0], *args[1:]), (jnp.arange(nb),) + blk)
    return jnp.moveaxis(out, 0, 1).reshape((B, S) + out.shape[3:])


def mla_project(h, pos, w_down, g_q, g_kv, w_uq, w_uk):
    B, S, _ = h.shape
    cq, ckv, kr = jnp.split(h @ w_down, [A_Q_LORA, A_Q_LORA + A_KV_LORA], axis=-1)
    cq = rmsnorm(cq, g_q)
    ckv = rmsnorm(ckv, g_kv)
    q = (cq @ w_uq).reshape(B, S, A_HEADS, A_NOPE + A_ROPE)
    q_rope = rope(q[..., A_NOPE:], pos, A_ROPE)
    q_lat = jnp.einsum('bshn,chn->bshc', q[..., :A_NOPE], w_uk)
    k_rope = rope(kr[:, :, None, :], pos, A_ROPE)[:, :, 0, :]
    return q_lat, q_rope, ckv, k_rope


def mla_attend(q_lat, q_rope, ckv, k_rope, q_pos, k_pos, w_uv):
    s = jnp.einsum('bqhc,bkc->bhqk', q_lat, ckv) + jnp.einsum('bqhr,bkr->bhqk', q_rope, k_rope)
    s = s.astype(jnp.float32) * (A_NOPE + A_ROPE) ** -0.5
    s = jnp.where(chunk_mask(q_pos, k_pos, None), s, -jnp.inf)
    p = jax.nn.softmax(s, axis=-1).astype(ckv.dtype)
    o_lat = jnp.einsum('bhqk,bkc->bqhc', p, ckv)
    o = jnp.einsum('bqhc,chv->bqhv', o_lat, w_uv)
    return flat_heads(o)


def mixer_a(hp, hs, cache_ckv, cache_kr, past_len, w_down, g_q, g_kv, w_uq, w_uk, w_uv, w_o):
    S, T = hp.shape[1], hs.shape[1]
    pos_p = jnp.arange(S)
    ql, qr, ckv_p, kr_p = mla_project(hp, pos_p, w_down, g_q, g_kv, w_uq, w_uk)

    def blk(i, ql_b, qr_b):
        return mla_attend(ql_b, qr_b, ckv_p, kr_p, i * Q_BLOCK + jnp.arange(Q_BLOCK), pos_p, w_uv)

    op = map_blocks(blk, (ql, qr), Q_BLOCK)
    pos_s = past_len + jnp.arange(T)
    qls, qrs, ckv_s, kr_s = mla_project(hs, pos_s, w_down, g_q, g_kv, w_uq, w_uk)
    os_ = mla_attend(qls, qrs, jnp.concatenate([cache_ckv, ckv_s], axis=1),
                     jnp.concatenate([cache_kr, kr_s], axis=1), pos_s, jnp.arange(past_len + T), w_uv)
    return op @ w_o, os_ @ w_o, (ckv_p, kr_p, ckv_s, kr_s)


def stick_breaking(q, k, v, q_pos, k_pos):
    z = jnp.einsum('bqhd,bkhd->bhqk', q, k).astype(jnp.float32) * q.shape[-1] ** -0.5
    causal = k_pos[None, :] < q_pos[:, None]
    log_stay = jnp.where(causal, jax.nn.log_sigmoid(-z), 0.0)
    after = lax.cumsum(log_stay, axis=3, reverse=True) - log_stay
    a = jnp.where(causal, jnp.exp(jax.nn.log_sigmoid(z) + after), 0.0)
    return jnp.einsum('bhqk,bkhd->bqhd', a.astype(v.dtype), v)


def mixer_b(hp, hs, cache_k, cache_v, past_len, w_qkv, w_o):
    def proj(h):
        B, S, _ = h.shape
        return jnp.split((h @ w_qkv).reshape(B, S, 3 * B_HEADS, HEAD_DIM), 3, axis=2)

    S, T = hp.shape[1], hs.shape[1]
    qp, kp, vp = proj(hp)
    pos_p = jnp.arange(S)
    op = map_blocks(lambda i, qb: stick_breaking(qb, kp, vp, i * Q_BLOCK + jnp.arange(Q_BLOCK), pos_p),
                    (qp,), Q_BLOCK)
    qs, ks_, vs = proj(hs)
    os_ = stick_breaking(qs, jnp.concatenate([cache_k, ks_], axis=1), jnp.concatenate([cache_v, vs], axis=1),
                         past_len + jnp.arange(T), jnp.arange(past_len + T))
    return flat_heads(op) @ w_o, flat_heads(os_) @ w_o, (kp, vp, ks_, vs)


def softmax_attend(q, k, v, q_pos, k_pos, n_prev, rel_bias, sink):
    B, Q, H, d = q.shape
    K, G = k.shape[1], k.shape[2]
    R = H // G
    s = jnp.einsum('bqgrd,bkgd->bgrqk', q.reshape(B, Q, G, R, d), k)
    s = s.reshape(B, H, Q, K).astype(jnp.float32) * d ** -0.5
    if rel_bias is not None:
        rel = jnp.clip(q_pos[:, None] - k_pos[None, :], -D_REL_CLIP, D_REL_CLIP) + D_REL_CLIP
        s = s + rel_bias.astype(jnp.float32)[:, rel][None]
    s = jnp.where(chunk_mask(q_pos, k_pos, n_prev), s, -jnp.inf)
    if sink is None:
        p = jax.nn.softmax(s, axis=-1)
    else:
        sk = sink.astype(jnp.float32)[None, :, None, None]
        m = jnp.maximum(jnp.max(s, axis=-1, keepdims=True), sk)
        e = jnp.exp(s - m)
        p = e / (jnp.sum(e, axis=-1, keepdims=True) + jnp.exp(sk - m))
    o = jnp.einsum('bgrqk,bkgd->bqgrd', p.reshape(B, G, R, Q, K).astype(v.dtype), v)
    return o.reshape(B, Q, H, d)


def band_mixer(qp, kp, vp, qs, ks_, vs, cache_k, cache_v, past_len, n_prev, rel_bias, sink):
    def attend(q, k, v, q_pos, k_pos):
        return softmax_attend(q, k, v, q_pos, k_pos, n_prev, rel_bias, sink)

    S, T, Lc = qp.shape[1], qs.shape[1], cache_k.shape[1]
    pad = n_prev * CHUNK
    band = pad + CHUNK
    kpad = jnp.pad(kp, ((0, 0), (pad, 0), (0, 0), (0, 0)))
    vpad = jnp.pad(vp, ((0, 0), (pad, 0), (0, 0), (0, 0)))

    def blk(i, qb):
        start = i * CHUNK
        kb = lax.dynamic_slice_in_dim(kpad, start, band, axis=1)
        vb = lax.dynamic_slice_in_dim(vpad, start, band, axis=1)
        return attend(qb, kb, vb, start + jnp.arange(CHUNK), start - pad + jnp.arange(band))

    op = map_blocks(blk, (qp,), CHUNK)
    k_all = jnp.concatenate([cache_k, ks_], axis=1)
    v_all = jnp.concatenate([cache_v, vs], axis=1)
    os_ = attend(qs, k_all, v_all, past_len + jnp.arange(T), past_len - Lc + jnp.arange(Lc + T))
    keep = min(pad, S)
    return op, os_, (kp[:, S - keep:], vp[:, S - keep:], k_all[:, T:], v_all[:, T:])


def mixer_c(hp, hs, cache_k, cache_v, past_len, w_qkv, b_qkv, sink, w_o):
    def proj(h, pos):
        B, S, _ = h.shape
        qkv = (h @ w_qkv + b_qkv).reshape(B, S, C_HEADS + 2 * C_KV_HEADS, HEAD_DIM)
        q = rope(qkv[:, :, :C_HEADS], pos, C_ROT)
        k = rope(qkv[:, :, C_HEADS:C_HEADS + C_KV_HEADS], pos, C_ROT)
        return q, k, qkv[:, :, C_HEADS + C_KV_HEADS:]

    qp, kp, vp = proj(hp, jnp.arange(hp.shape[1]))
    qs, ks_, vs = proj(hs, past_len + jnp.arange(hs.shape[1]))
    op, os_, st = band_mixer(qp, kp, vp, qs, ks_, vs, cache_k, cache_v, past_len, C_PREV_CHUNKS, None, sink)
    return flat_heads(op) @ w_o, flat_heads(os_) @ w_o, st


def mixer_d(hp, hs, cache_k, cache_v, past_len, w_qkv, rel_bias, w_o):
    def proj(h):
        B, S, _ = h.shape
        return jnp.split((h @ w_qkv).reshape(B, S, 3 * D_HEADS, HEAD_DIM), 3, axis=2)

    qp, kp, vp = proj(hp)
    qs, ks_, vs = proj(hs)
    op, os_, st = band_mixer(qp, kp, vp, qs, ks_, vs, cache_k, cache_v, past_len, D_PREV_CHUNKS, rel_bias, None)
    return flat_heads(op) @ w_o, flat_heads(os_) @ w_o, st


def swiglu(h, w_in, w_out):
    g, u = jnp.split(h @ w_in, 2, axis=-1)
    return (jax.nn.silu(g) * u) @ w_out


def stack_states(entries):
    return tuple(jnp.stack(parts, axis=0) for parts in zip(*entries))


def setup_inputs(seed: int = 0) -> dict:
    key = jax.random.key(seed)
    ks = iter(jax.random.split(key, 48))

    def nrm(shape, scale=1.0):
        return jax.random.normal(next(ks), shape, jnp.float32) * scale

    L = LAYERS_PER_MIXER
    D = D_MODEL
    c_len = min(C_WINDOW, PAST_LEN)
    d_len = min(D_PAST, PAST_LEN)
    a_down = A_Q_LORA + A_KV_LORA + A_ROPE
    c_qkv = (C_HEADS + 2 * C_KV_HEADS) * HEAD_DIM
    return {
        'x_prompt': nrm((BATCH, SEQ, D)),
        'x_sample': nrm((DEC_BATCH, DEC_SEQ, D)),
        'c_prompt': nrm((BATCH, D)),
        'c_sample': nrm((DEC_BATCH, D)),
        'cache_a_ckv': nrm((L, DEC_BATCH, PAST_LEN, A_KV_LORA)),
        'cache_a_krope': nrm((L, DEC_BATCH, PAST_LEN, A_ROPE)),
        'cache_b_k': nrm((L, DEC_BATCH, PAST_LEN, B_HEADS, HEAD_DIM)),
        'cache_b_v': nrm((L, DEC_BATCH, PAST_LEN, B_HEADS, HEAD_DIM)),
        'cache_c_k': nrm((L, DEC_BATCH, c_len, C_KV_HEADS, HEAD_DIM)),
        'cache_c_v': nrm((L, DEC_BATCH, c_len, C_KV_HEADS, HEAD_DIM)),
        'cache_d_k': nrm((L, DEC_BATCH, d_len, D_HEADS, HEAD_DIM)),
        'cache_d_v': nrm((L, DEC_BATCH, d_len, D_HEADS, HEAD_DIM)),
        'w_mod': nrm((DEPTH, D, 6 * D), 0.5 * D ** -0.5),
        'b_mod': nrm((DEPTH, 6 * D), 0.02),
        'g_mix': 1.0 + nrm((DEPTH, D), 0.05),
        'g_ffn': 1.0 + nrm((DEPTH, D), 0.05),
        'w_ffn_in': nrm((DEPTH, D, 2 * FFN_HIDDEN), D ** -0.5),
        'w_ffn_out': nrm((DEPTH, FFN_HIDDEN, D), FFN_HIDDEN ** -0.5),
        'w_a_down': nrm((L, D, a_down), D ** -0.5),
        'g_a_q': 1.0 + nrm((L, A_Q_LORA), 0.05),
        'g_a_kv': 1.0 + nrm((L, A_KV_LORA), 0.05),
        'w_a_uq': nrm((L, A_Q_LORA, A_HEADS * (A_NOPE + A_ROPE)), A_Q_LORA ** -0.5),
        'w_a_uk': nrm((L, A_KV_LORA, A_HEADS, A_NOPE), A_KV_LORA ** -0.5),
        'w_a_uv': nrm((L, A_KV_LORA, A_HEADS, A_V), A_KV_LORA ** -0.5),
        'w_a_o': nrm((L, A_HEADS * A_V, D), (A_HEADS * A_V) ** -0.5),
        'w_b_qkv': nrm((L, D, 3 * B_HEADS * HEAD_DIM), D ** -0.5),
        'w_b_o': nrm((L, B_HEADS * HEAD_DIM, D), (B_HEADS * HEAD_DIM) ** -0.5),
        'w_c_qkv': nrm((L, D, c_qkv), D ** -0.5),
        'b_c_qkv': nrm((L, c_qkv), 0.02),
        'sink_c': nrm((L, C_HEADS)),
        'w_c_o': nrm((L, C_HEADS * HEAD_DIM, D), (C_HEADS * HEAD_DIM) ** -0.5),
        'w_d_qkv': nrm((L, D, 3 * D_HEADS * HEAD_DIM), D ** -0.5),
        'rel_bias_d': nrm((L, D_HEADS, 2 * D_REL_CLIP + 1), 0.5),
        'w_d_o': nrm((L, D_HEADS * HEAD_DIM, D), (D_HEADS * HEAD_DIM) ** -0.5),
        'g_final': 1.0 + nrm((D,), 0.05),
    }


def reference(x_prompt, x_sample, c_prompt, c_sample, cache_a_ckv, cache_a_krope, cache_b_k, cache_b_v,
              cache_c_k, cache_c_v, cache_d_k, cache_d_v, w_mod, b_mod, g_mix, g_ffn, w_ffn_in, w_ffn_out,
              w_a_down, g_a_q, g_a_kv, w_a_uq, w_a_uk, w_a_uv, w_a_o, w_b_qkv, w_b_o,
              w_c_qkv, b_c_qkv, sink_c, w_c_o, w_d_qkv, rel_bias_d, w_d_o, g_final):
    past_len = cache_a_ckv.shape[2]
    xp, xs = x_prompt, x_sample
    states = [[], [], [], []]
    for i in range(DEPTH):
        m, j = i % N_MIXERS, i // N_MIXERS
        mp = adaln(c_prompt, w_mod[i], b_mod[i])
        ms = adaln(c_sample, w_mod[i], b_mod[i])
        hp = modulate(xp, g_mix[i], mp[0], mp[1])
        hs = modulate(xs, g_mix[i], ms[0], ms[1])
        if m == 0:
            op, os_, st = mixer_a(hp, hs, cache_a_ckv[j], cache_a_krope[j], past_len, w_a_down[j], g_a_q[j],
                                  g_a_kv[j], w_a_uq[j], w_a_uk[j], w_a_uv[j], w_a_o[j])
        elif m == 1:
            op, os_, st = mixer_b(hp, hs, cache_b_k[j], cache_b_v[j], past_len, w_b_qkv[j], w_b_o[j])
        elif m == 2:
            op, os_, st = mixer_c(hp, hs, cache_c_k[j], cache_c_v[j], past_len, w_c_qkv[j], b_c_qkv[j],
                                  sink_c[j], w_c_o[j])
        else:
            op, os_, st = mixer_d(hp, hs, cache_d_k[j], cache_d_v[j], past_len, w_d_qkv[j], rel_bias_d[j], w_d_o[j])
        states[m].append(st)
        xp = xp + mp[2][:, None, :] * op
        xs = xs + ms[2][:, None, :] * os_
        hp = modulate(xp, g_ffn[i], mp[3], mp[4])
        hs = modulate(xs, g_ffn[i], ms[3], ms[4])
        xp = xp + mp[5][:, None, :] * swiglu(hp, w_ffn_in[i], w_ffn_out[i])
        xs = xs + ms[5][:, None, :] * swiglu(hs, w_ffn_in[i], w_ffn_out[i])
    y_prompt = rmsnorm(xp, g_final)
    y_sample = rmsnorm(xs, g_final)
    a_ckv_p, a_kr_p, a_ckv_s, a_kr_s = stack_states(states[0])
    b_k_p, b_v_p, b_k_s, b_v_s = stack_states(states[1])
    c_k_p, c_v_p, c_k_s, c_v_s = stack_states(states[2])
    d_k_p, d_v_p, d_k_s, d_v_s = stack_states(states[3])
    return (y_prompt, y_sample, a_ckv_p, a_kr_p, a_ckv_s, a_kr_s, b_k_p, b_v_p, b_k_s, b_v_s,
            c_k_p, c_v_p, c_k_s, c_v_s, d_k_p, d_v_p, d_k_s, d_v_s)
```

```python
import functools
import math

import numpy as np
import jax
import jax.numpy as jnp
from jax import lax
from jax.experimental import pallas as pl
from jax.experimental.pallas import tpu as pltpu

F32 = jnp.float32
BF16 = jnp.bfloat16

CHUNK = 64
HEAD_DIM = 64
ROPE_THETA = 500000.0
NORM_EPS = 1e-6
N_MIXERS = 4
A_HEADS, A_Q_LORA, A_KV_LORA, A_NOPE, A_ROPE, A_V = 16, 384, 256, 64, 32, 64
B_HEADS = 16
C_HEADS, C_KV_HEADS, C_WINDOW, C_ROT = 16, 4, 128, 16
D_HEADS, D_PREV_CHUNKS, D_REL_CLIP = 16, 8, 128
C_PREV_CHUNKS = C_WINDOW // CHUNK

LANES = 128
V7X_VMEM_LIMIT = 56 * 1024 * 1024

TOKEN_TILE = 512
Q_TILE = 256
KV_BLOCK = 256
MASK_VALUE = -1e30


def _cparams(n_axes):
    return pltpu.CompilerParams(
        dimension_semantics=("parallel",) * n_axes, vmem_limit_bytes=V7X_VMEM_LIMIT)


def _adaln_kernel(c_ref, w_ref, b_ref, o_ref):
    c = c_ref[...]
    a = (c * jax.nn.sigmoid(c)).astype(BF16)
    y = jnp.dot(a, w_ref[0].astype(BF16), preferred_element_type=F32) + b_ref[0]
    o_ref[0, 0] = y


def _adaln(c_all, w_mod, b_mod):
    depth, d, d6 = w_mod.shape
    n = c_all.shape[0]
    return pl.pallas_call(
        _adaln_kernel,
        grid=(depth, d6 // d),
        in_specs=[
            pl.BlockSpec((n, d), lambda i, k: (0, 0)),
            pl.BlockSpec((1, d, d), lambda i, k: (i, 0, k)),
            pl.BlockSpec((1, 1, d), lambda i, k: (i, 0, k)),
        ],
        out_specs=pl.BlockSpec((1, 1, n, d), lambda i, k: (i, k, 0, 0)),
        out_shape=jax.ShapeDtypeStruct((depth, d6 // d, n, d), F32),
        compiler_params=_cparams(2),
    )(c_all, w_mod, b_mod.reshape(depth, 1, d6))


def _token_tiling(bx, sx):
    if sx >= TOKEN_TILE:
        ts = TOKEN_TILE
        while sx % ts:
            ts //= 2
        return 1, ts
    nb = min(bx, TOKEN_TILE // sx)
    while bx % nb:
        nb -= 1
    return nb, sx


def _tok_spec(nb, ts, width):
    return pl.BlockSpec((nb, ts, width), lambda b, s: (b, s, 0))


def _mod_spec(nb, d):
    return pl.BlockSpec((nb, 1, d), lambda b, s: (b, 0, 0))


def _const_spec(shape):
    nd = len(shape)
    return pl.BlockSpec(shape, lambda b, s: (0,) * nd)


def _table_spec(rows):
    return pl.BlockSpec((1, rows, LANES), lambda b, s: (s, 0, 0))


def _rms(x):
    return x * lax.rsqrt(jnp.mean(x * x, axis=-1, keepdims=True) + NORM_EPS)


def _modulated(x_ref, g_ref, shift_ref, scale_ref):
    y = _rms(x_ref[...]) * g_ref[...]
    h = y * (1.0 + scale_ref[...]) + shift_ref[...]
    return h.reshape(-1, h.shape[-1]).astype(BF16)


def _rope_lanes(x, cos_t, sin_up, sin_dn, half):
    return (x * cos_t + pltpu.roll(x, half, 1) * sin_up
            + pltpu.roll(x, LANES - half, 1) * sin_dn)


def _store_tok(ref, lo, val):
    nb, ts = ref.shape[0], ref.shape[1]
    w = val.shape[-1]
    ref[:, :, lo:lo + w] = val.reshape(nb, ts, w).astype(ref.dtype)


def _proj_qkv_kernel(x_ref, g_ref, sh_ref, sc_ref, w_ref, q_ref, kf_ref, vf_ref, kb_ref, vb_ref,
                     *, width, q_scale):
    h = _modulated(x_ref, g_ref, sh_ref, sc_ref)
    q = jnp.dot(h, w_ref[:, 0:width], preferred_element_type=F32)
    _store_tok(q_ref, 0, q * q_scale)
    k = jnp.dot(h, w_ref[:, width:2 * width], preferred_element_type=F32)
    _store_tok(kf_ref, 0, k)
    _store_tok(kb_ref, 0, k)
    v = jnp.dot(h, w_ref[:, 2 * width:3 * width], preferred_element_type=F32)
    _store_tok(vf_ref, 0, v)
    _store_tok(vb_ref, 0, v)


def _proj_qkv(x, g, shift, scale, w_bf16, n_heads):
    bx, sx, d = x.shape
    width = n_heads * HEAD_DIM
    nb, ts = _token_tiling(bx, sx)
    out = lambda dt: jax.ShapeDtypeStruct((bx, sx, width), dt)
    return pl.pallas_call(
        functools.partial(_proj_qkv_kernel, width=width, q_scale=HEAD_DIM ** -0.5),
        grid=(bx // nb, sx // ts),
        in_specs=[_tok_spec(nb, ts, d), _const_spec((1, d)), _mod_spec(nb, d), _mod_spec(nb, d),
                  _const_spec(w_bf16.shape)],
        out_specs=[_tok_spec(nb, ts, width)] * 5,
        out_shape=[out(BF16), out(F32), out(F32), out(BF16), out(BF16)],
        compiler_params=_cparams(2),
    )(x, g, shift, scale, w_bf16)


def _proj_c_kernel(x_ref, g_ref, sh_ref, sc_ref, w_ref, b_ref, tc_ref, tu_ref, td_ref,
                   q_ref, kf_ref, vf_ref, kb_ref, vb_ref, *, q_width, kv_width, q_scale):
    h = _modulated(x_ref, g_ref, sh_ref, sc_ref)
    qkv = jnp.dot(h, w_ref[...], preferred_element_type=F32) + b_ref[...]
    cos_t, sin_up, sin_dn = tc_ref[0], tu_ref[0], td_ref[0]
    half = C_ROT // 2
    for j in range(q_width // LANES):
        xg = qkv[:, j * LANES:(j + 1) * LANES]
        _store_tok(q_ref, j * LANES, _rope_lanes(xg, cos_t, sin_up, sin_dn, half) * q_scale)
    for j in range(kv_width // LANES):
        lo = q_width + j * LANES
        kg = _rope_lanes(qkv[:, lo:lo + LANES], cos_t, sin_up, sin_dn, half)
        _store_tok(kf_ref, j * LANES, kg)
        _store_tok(kb_ref, j * LANES, kg)
    v = qkv[:, q_width + kv_width:q_width + 2 * kv_width]
    _store_tok(vf_ref, 0, v)
    _store_tok(vb_ref, 0, v)


def _proj_c(x, g, shift, scale, w_bf16, bias, tables):
    bx, sx, d = x.shape
    q_width, kv_width = C_HEADS * HEAD_DIM, C_KV_HEADS * HEAD_DIM
    nb, ts = _token_tiling(bx, sx)
    out = lambda w, dt: jax.ShapeDtypeStruct((bx, sx, w), dt)
    return pl.pallas_call(
        functools.partial(_proj_c_kernel, q_width=q_width, kv_width=kv_width,
                          q_scale=HEAD_DIM ** -0.5),
        grid=(bx // nb, sx // ts),
        in_specs=[_tok_spec(nb, ts, d), _const_spec((1, d)), _mod_spec(nb, d), _mod_spec(nb, d),
                  _const_spec(w_bf16.shape), _const_spec(bias.shape)] + [_table_spec(nb * ts)] * 3,
        out_specs=[_tok_spec(nb, ts, q_width)] + [_tok_spec(nb, ts, kv_width)] * 4,
        out_shape=[out(q_width, BF16), out(kv_width, F32), out(kv_width, F32),
                   out(kv_width, BF16), out(kv_width, BF16)],
        compiler_params=_cparams(2),
    )(x, g, shift, scale, w_bf16, bias, *tables)


def _proj_a_kernel(x_ref, g_ref, sh_ref, sc_ref, wd_ref, gq_ref, gkv_ref, wq_ref,
                   qc_ref, qu_ref, qd_ref, kc_ref, ku_ref, kd_ref,
                   q_ref, ckv_ref, kr_ref, *, q_scale):
    h = _modulated(x_ref, g_ref, sh_ref, sc_ref)
    down = jnp.dot(h, wd_ref[...], preferred_element_type=F32)
    cq = (_rms(down[:, :A_Q_LORA]) * gq_ref[...]).astype(BF16)
    ckv = _rms(down[:, A_Q_LORA:A_Q_LORA + A_KV_LORA]) * gkv_ref[...]
    _store_tok(ckv_ref, 0, ckv)
    lo = A_Q_LORA + A_KV_LORA
    half = A_ROPE // 2
    kr = _rope_lanes(down[:, lo:lo + LANES], kc_ref[0], ku_ref[0], kd_ref[0], half)
    _store_tok(kr_ref, 0, kr[:, :A_ROPE])
    q = jnp.dot(cq, wq_ref[...], preferred_element_type=F32)
    qc, qu, qd = qc_ref[0], qu_ref[0], qd_ref[0]
    for j in range(A_HEADS):
        qg = _rope_lanes(q[:, j * LANES:(j + 1) * LANES], qc, qu, qd, half)
        _store_tok(q_ref, j * LANES, qg * q_scale)


def _proj_a(x, g, shift, scale, wd_bf16, g_q, g_kv, wq_bf16, q_tables, k_tables):
    bx, sx, d = x.shape
    nb, ts = _token_tiling(bx, sx)
    return pl.pallas_call(
        functools.partial(_proj_a_kernel, q_scale=(A_NOPE + A_ROPE) ** -0.5),
        grid=(bx // nb, sx // ts),
        in_specs=[_tok_spec(nb, ts, d), _const_spec((1, d)), _mod_spec(nb, d), _mod_spec(nb, d),
                  _const_spec(wd_bf16.shape), _const_spec(g_q.shape), _const_spec(g_kv.shape),
                  _const_spec(wq_bf16.shape)] + [_table_spec(nb * ts)] * 6,
        out_specs=[_tok_spec(nb, ts, A_HEADS * LANES), _tok_spec(nb, ts, A_KV_LORA),
                   _tok_spec(nb, ts, A_ROPE)],
        out_shape=[jax.ShapeDtypeStruct((bx, sx, A_HEADS * LANES), BF16),
                   jax.ShapeDtypeStruct((bx, sx, A_KV_LORA), F32),
                   jax.ShapeDtypeStruct((bx, sx, A_ROPE), F32)],
        compiler_params=_cparams(2),
    )(x, g, shift, scale, wd_bf16, g_q, g_kv, wq_bf16, *q_tables, *k_tables)


def _expand_a_kernel(ckv_ref, kr_ref, wk_ref, wr_ref, wv_ref, k_ref, v_ref):
    ckv = ckv_ref[...]
    ckv = ckv.reshape(-1, ckv.shape[-1]).astype(BF16)
    kr = kr_ref[...]
    kr = kr.reshape(-1, kr.shape[-1]).astype(BF16)
    k = (jnp.dot(ckv, wk_ref[...], preferred_element_type=F32)
         + jnp.dot(kr, wr_ref[...], preferred_element_type=F32))
    _store_tok(k_ref, 0, k)
    _store_tok(v_ref, 0, jnp.dot(ckv, wv_ref[...], preferred_element_type=F32))


def _expand_a(ckv, kr, wk, wr, wv):
    bx, sx, _ = ckv.shape
    nb, ts = _token_tiling(bx, sx)
    return pl.pallas_call(
        _expand_a_kernel,
        grid=(bx // nb, sx // ts),
        in_specs=[_tok_spec(nb, ts, A_KV_LORA), _tok_spec(nb, ts, A_ROPE),
                  _const_spec(wk.shape), _const_spec(wr.shape), _const_spec(wv.shape)],
        out_specs=[_tok_spec(nb, ts, A_HEADS * LANES), _tok_spec(nb, ts, A_HEADS * A_V)],
        out_shape=[jax.ShapeDtypeStruct((bx, sx, A_HEADS * LANES), BF16),
                   jax.ShapeDtypeStruct((bx, sx, A_HEADS * A_V), BF16)],
        compiler_params=_cparams(2),
    )(ckv, kr, wk, wr, wv)


def _block_kernel(x_ref, o_ref, wo_ref, gm_ref, g_ref, sh_ref, sc_ref, gf_ref, wi_ref, wout_ref,
                  y_ref, *, hidden, chunk):
    nb, ts, d = x_ref.shape
    o = o_ref[...].reshape(nb * ts, -1)
    mix = jnp.dot(o, wo_ref[...], preferred_element_type=F32).reshape(nb, ts, d)
    x1 = x_ref[...] + gm_ref[...] * mix
    h = (_rms(x1) * g_ref[...]) * (1.0 + sc_ref[...]) + sh_ref[...]
    h = h.reshape(nb * ts, d).astype(BF16)
    acc = jnp.zeros((nb * ts, d), F32)
    for c in range(hidden // chunk):
        gate = jnp.dot(h, wi_ref[:, c * chunk:(c + 1) * chunk], preferred_element_type=F32)
        up = jnp.dot(h, wi_ref[:, hidden + c * chunk:hidden + (c + 1) * chunk],
                     preferred_element_type=F32)
        act = (gate * jax.nn.sigmoid(gate) * up).astype(BF16)
        acc = acc + jnp.dot(act, wout_ref[c * chunk:(c + 1) * chunk, :],
                            preferred_element_type=F32)
    y_ref[...] = x1 + gf_ref[...] * acc.reshape(nb, ts, d)


def _block(x, o, wo, gate_m, g_ffn, shift_f, scale_f, gate_f, w_in, w_out):
    bx, sx, d = x.shape
    hidden = w_out.shape[0]
    nb, ts = _token_tiling(bx, sx)
    resident = lambda shape: pl.BlockSpec(shape, lambda b, s: (0,) * len(shape),
                                          pipeline_mode=pl.Buffered(1))
    return pl.pallas_call(
        functools.partial(_block_kernel, hidden=hidden, chunk=256),
        grid=(bx // nb, sx // ts),
        in_specs=[_tok_spec(nb, ts, d), _tok_spec(nb, ts, o.shape[-1]), resident(wo.shape),
                  _mod_spec(nb, d), _const_spec((1, d)), _mod_spec(nb, d), _mod_spec(nb, d),
                  _mod_spec(nb, d), resident(w_in.shape), resident(w_out.shape)],
        out_specs=_tok_spec(nb, ts, d),
        out_shape=jax.ShapeDtypeStruct((bx, sx, d), F32),
        compiler_params=_cparams(2),
    )(x, o, wo, gate_m, g_ffn, shift_f, scale_f, gate_f, w_in, w_out)


def _final_norm_kernel(x_ref, g_ref, y_ref):
    y_ref[...] = _rms(x_ref[...]) * g_ref[...]


def _final_norm(x, g):
    bx, sx, d = x.shape
    nb, ts = _token_tiling(bx, sx)
    return pl.pallas_call(
        _final_norm_kernel,
        grid=(bx // nb, sx // ts),
        in_specs=[_tok_spec(nb, ts, d), _const_spec((1, d))],
        out_specs=_tok_spec(nb, ts, d),
        out_shape=jax.ShapeDtypeStruct((bx, sx, d), F32),
        compiler_params=_cparams(2),
    )(x, g)


def _attn_kernel(*refs, mode, dq, tq, bk, sk_valid, q_off, k_off, n_prev, use_sink, use_bias):
    refs = list(refs)
    q_ref, k_ref, v_ref = refs[:3]
    rest = refs[3:]
    sink_ref = rest.pop(0) if use_sink else None
    bias_ref = rest.pop(0) if use_bias else None
    o_ref = rest.pop(0)

    pair = pl.program_id(1)
    q0 = pl.program_id(2) * tq
    qpos0 = q0 + q_off
    stick = mode == "stick"

    if stick:
        hi_idx = qpos0 + (tq - 1) - k_off
    else:
        hi_idx = ((qpos0 + tq - 1) // CHUNK + 1) * CHUNK - k_off
    hi_idx = jnp.minimum(hi_idx, sk_valid)
    if n_prev is not None:
        band_idx = (qpos0 // CHUNK - n_prev) * CHUNK - k_off
        lo_idx = jnp.maximum(band_idx, 0)
    else:
        band_idx = 0
        lo_idx = 0
    kb_lo = lo_idx // bk
    kb_hi = (hi_idx + bk - 1) // bk

    def fully_visible(kb):
        kstart = kb * bk + k_off
        kend = kstart + bk
        ok = (kb + 1) * bk <= sk_valid
        if stick:
            return ok & (kend <= qpos0)
        ok = ok & (kend <= (qpos0 // CHUNK + 1) * CHUNK)
        if n_prev is not None:
            ok = ok & (kstart // CHUNK >= (qpos0 + tq - 1) // CHUNK - n_prev)
        return ok

    def visible(kb):
        iq = lax.broadcasted_iota(jnp.int32, (tq, 1), 0)
        ik = kb * bk + lax.broadcasted_iota(jnp.int32, (1, bk), 1)
        qpos = iq + qpos0
        kpos = ik + k_off
        valid = ik < sk_valid
        if stick:
            return (kpos < qpos) & valid
        shift = int(math.log2(CHUNK))
        qc = jnp.right_shift(qpos, shift)
        kc = jnp.right_shift(kpos, shift)
        m = (kc <= qc) & valid
        if n_prev is not None:
            m = m & (kc >= qc - n_prev)
        return m

    lane_half = lax.broadcasted_iota(jnp.int32, (1, LANES), 1) // HEAD_DIM
    if stick:
        r_i = lax.broadcasted_iota(jnp.int32, (bk, bk), 0)
        c_i = lax.broadcasted_iota(jnp.int32, (bk, bk), 1)
        tri = (r_i >= c_i).astype(BF16)

    outs = []
    for hh in range(2):
        if dq == HEAD_DIM:
            qh = jnp.where(lane_half == hh, q_ref[0], jnp.zeros((), BF16))
            k_lanes = slice(0, LANES)
        else:
            qh = q_ref[0, :, hh * dq:(hh + 1) * dq]
            k_lanes = slice(hh * dq, (hh + 1) * dq)

        def scores(kb):
            kblk = k_ref[0, pl.ds(pl.multiple_of(kb * bk, bk), bk), k_lanes]
            return lax.dot_general(qh, kblk, (((1,), (1,)), ((), ())),
                                   preferred_element_type=F32)

        def values(kb):
            return v_ref[0, pl.ds(pl.multiple_of(kb * bk, bk), bk), :]

        if stick:
            def stick_block(kb, carry, masked):
                run, acc = carry
                z = scores(kb)
                log_stay = -(jnp.maximum(z, 0.0) + jnp.log(1.0 + jnp.exp(-jnp.abs(z))))
                if masked:
                    vis = visible(kb)
                    log_stay = jnp.where(vis, log_stay, 0.0)
                hi = log_stay.astype(BF16)
                lo = (log_stay - hi.astype(F32)).astype(BF16)
                suffix = (jnp.dot(hi, tri, preferred_element_type=F32)
                          + jnp.dot(lo, tri, preferred_element_type=F32))
                a = jnp.exp(z + suffix + run)
                if masked:
                    a = jnp.where(vis, a, 0.0)
                acc = acc + jnp.dot(a.astype(BF16), values(kb), preferred_element_type=F32)
                run = run + jnp.sum(log_stay, axis=1, keepdims=True)
                return run, acc

            def body(i, carry):
                kb = kb_hi - 1 - i
                return lax.cond(fully_visible(kb),
                                lambda c: stick_block(kb, c, False),
                                lambda c: stick_block(kb, c, True), carry)

            init = (jnp.zeros((tq, 1), F32), jnp.zeros((tq, LANES), F32))
            _, acc = lax.fori_loop(0, kb_hi - kb_lo, body, init)
            outs.append(acc)
        else:
            def softmax_block(kb, carry, masked):
                m, l, acc = carry
                s = scores(kb)
                if use_bias:
                    s = s + bias_ref[hh, (kb * bk - band_idx) // bk]
                if masked:
                    s = jnp.where(visible(kb), s, MASK_VALUE)
                m_new = jnp.maximum(m, jnp.max(s, axis=1, keepdims=True))
                alpha = jnp.exp(m - m_new)
                p = jnp.exp(s - m_new)
                l = alpha * l + jnp.sum(p, axis=1, keepdims=True)
                acc = alpha * acc + jnp.dot(p.astype(BF16), values(kb),
                                            preferred_element_type=F32)
                return m_new, l, acc

            def body(kb, carry):
                return lax.cond(fully_visible(kb),
                                lambda c: softmax_block(kb, c, False),
                                lambda c: softmax_block(kb, c, True), carry)

            if use_sink:
                m0 = jnp.full((tq, 1), sink_ref[2 * pair + hh], F32)
                l0 = jnp.ones((tq, 1), F32)
            else:
                m0 = jnp.full((tq, 1), MASK_VALUE, F32)
                l0 = jnp.zeros((tq, 1), F32)
            _, l, acc = lax.fori_loop(kb_lo, kb_hi, body, (m0, l0, jnp.zeros((tq, LANES), F32)))
            outs.append(acc / l)
    o_ref[0] = jnp.where(lane_half == 0, outs[0], outs[1]).astype(o_ref.dtype)


def _attention(q, k, v, *, mode, dq, sk_valid, q_off, k_off, n_prev=None, kv_group=1,
               sink=None, bias=None):
    bx, sq, qw = q.shape
    skp = k.shape[1]
    n_pairs = qw // (2 * dq)
    tq = min(Q_TILE, sq)
    bk = KV_BLOCK
    assert sq % tq == 0 and skp % bk == 0 and q_off % CHUNK == 0 and tq % 8 == 0
    assert k_off >= 0 and (tq % CHUNK == 0 or sq == tq)
    in_specs = [
        pl.BlockSpec((1, tq, 2 * dq), lambda b, p, i: (b, i, p)),
        pl.BlockSpec((1, skp, 2 * dq), lambda b, p, i: (b, 0, p // kv_group)),
        pl.BlockSpec((1, skp, LANES), lambda b, p, i: (b, 0, p // kv_group)),
    ]
    args = [q, k, v]
    if sink is not None:
        in_specs.append(pl.BlockSpec(memory_space=pltpu.SMEM))
        args.append(sink)
    if bias is not None:
        in_specs.append(pl.BlockSpec((2,) + bias.shape[1:], lambda b, p, i: (p, 0, 0, 0)))
        args.append(bias)
    return pl.pallas_call(
        functools.partial(_attn_kernel, mode=mode, dq=dq, tq=tq, bk=bk, sk_valid=sk_valid,
                          q_off=q_off, k_off=k_off, n_prev=n_prev,
                          use_sink=sink is not None, use_bias=bias is not None),
        grid=(bx, n_pairs, sq // tq),
        in_specs=in_specs,
        out_specs=pl.BlockSpec((1, tq, LANES), lambda b, p, i: (b, i, p)),
        out_shape=jax.ShapeDtypeStruct((bx, sq, n_pairs * LANES), BF16),
        compiler_params=_cparams(3),
    )(*args)


def _rope_tables(pos, n_rot, lane_offsets, rows_repeat=1):
    half = n_rot // 2
    inv = ROPE_THETA ** (-jnp.arange(half, dtype=F32) * 2.0 / n_rot)
    ang = pos.astype(F32)[:, None] * inv[None, :]
    cos, sin = jnp.cos(ang), jnp.sin(ang)
    n = pos.shape[0]
    cos_t = jnp.ones((n, LANES), F32)
    sin_up = jnp.zeros((n, LANES), F32)
    sin_dn = jnp.zeros((n, LANES), F32)
    for o in lane_offsets:
        cos_t = cos_t.at[:, o:o + half].set(cos).at[:, o + half:o + n_rot].set(cos)
        sin_dn = sin_dn.at[:, o:o + half].set(-sin)
        sin_up = sin_up.at[:, o + half:o + n_rot].set(sin)
    def tiles(t):
        if rows_repeat > 1:
            return jnp.tile(t, (rows_repeat, 1))[None]
        rows = min(n, TOKEN_TILE)
        return t.reshape(n // rows, rows, LANES)
    return tiles(cos_t), tiles(sin_up), tiles(sin_dn)


def _tables_for(bx, sx, pos, n_rot, lane_offsets):
    nb, ts = _token_tiling(bx, sx)
    return _rope_tables(pos, n_rot, lane_offsets, rows_repeat=nb if nb > 1 else 1)


def _c_head_order():
    rep = C_HEADS // C_KV_HEADS
    order = []
    for p in range(C_HEADS // 2):
        g2, i = divmod(p, rep)
        order += [rep * (2 * g2) + i, rep * (2 * g2 + 1) + i]
    return np.asarray(order)


def _pad_rows(a, rows):
    return jnp.pad(a, ((0, 0), (0, rows - a.shape[1]), (0, 0)))


def _with_cache(cache, new, dtype):
    full = jnp.concatenate([cache.reshape(cache.shape[0], cache.shape[1], -1).astype(dtype),
                            new.astype(dtype)], axis=1)
    rows = -(-full.shape[1] // KV_BLOCK) * KV_BLOCK
    return _pad_rows(full, rows), full.shape[1]


def kernel(x_prompt, x_sample, c_prompt, c_sample, cache_a_ckv, cache_a_krope, cache_b_k, cache_b_v,
           cache_c_k, cache_c_v, cache_d_k, cache_d_v, w_mod, b_mod, g_mix, g_ffn, w_ffn_in, w_ffn_out,
           w_a_down, g_a_q, g_a_kv, w_a_uq, w_a_uk, w_a_uv, w_a_o, w_b_qkv, w_b_o,
           w_c_qkv, b_c_qkv, sink_c, w_c_o, w_d_qkv, rel_bias_d, w_d_o, g_final):
    bp, sp, d = x_prompt.shape
    bs, t, _ = x_sample.shape
    depth = w_mod.shape[0]
    past = cache_a_ckv.shape[2]
    pos_p = jnp.arange(sp)
    pos_s = past + jnp.arange(t)

    mods = _adaln(jnp.concatenate([c_prompt, c_sample], axis=0), w_mod, b_mod)

    def mod(i, k):
        m = mods[i, k][:, None, :]
        return m[:bp], m[bp:]

    xp, xs = x_prompt, x_sample
    states = [[] for _ in range(N_MIXERS)]
    for i in range(depth):
        m, j = i % N_MIXERS, i // N_MIXERS
        (sh_p, sh_s), (sc_p, sc_s), (gm_p, gm_s) = mod(i, 0), mod(i, 1), mod(i, 2)
        (shf_p, shf_s), (scf_p, scf_s), (gf_p, gf_s) = mod(i, 3), mod(i, 4), mod(i, 5)
        g_m = g_mix[i][None, :]
        if m == 0:
            n_down = A_Q_LORA + A_KV_LORA + A_ROPE
            wd = jnp.pad(w_a_down[j], ((0, 0), (0, A_Q_LORA + A_KV_LORA + LANES - n_down))).astype(BF16)
            wq = w_a_uq[j].reshape(A_Q_LORA, A_HEADS, A_NOPE + A_ROPE)
            wq = jnp.pad(wq, ((0, 0), (0, 0), (0, LANES - A_NOPE - A_ROPE)))
            wq = wq.reshape(A_Q_LORA, A_HEADS * LANES).astype(BF16)
            wk = jnp.pad(w_a_uk[j], ((0, 0), (0, 0), (0, LANES - A_NOPE)))
            wk = wk.reshape(A_KV_LORA, A_HEADS * LANES).astype(BF16)
            place = jnp.zeros((A_ROPE, A_HEADS, LANES), F32)
            place = place.at[jnp.arange(A_ROPE), :, A_NOPE + jnp.arange(A_ROPE)].set(1.0)
            wr = place.reshape(A_ROPE, A_HEADS * LANES).astype(BF16)
            wv = w_a_uv[j].reshape(A_KV_LORA, A_HEADS * A_V).astype(BF16)
            g_q, g_kv = g_a_q[j][None, :], g_a_kv[j][None, :]

            def project(x, sh, sc, pos):
                qt = _tables_for(x.shape[0], x.shape[1], pos, A_ROPE, [A_NOPE])
                kt = _tables_for(x.shape[0], x.shape[1], pos, A_ROPE, [0])
                return _proj_a(x, g_m, sh, sc, wd, g_q, g_kv, wq, qt, kt)

            q_p, ckv_p, kr_p = project(xp, sh_p, sc_p, pos_p)
            q_s, ckv_s, kr_s = project(xs, sh_s, sc_s, pos_s)
            k_p, v_p = _expand_a(ckv_p, kr_p, wk, wr, wv)
            ckv_all, n_all = _with_cache(cache_a_ckv[j], ckv_s, F32)
            kr_all, _ = _with_cache(cache_a_krope[j], kr_s, F32)
            k_s, v_s = _expand_a(ckv_all, kr_all, wk, wr, wv)
            o_p = _attention(q_p, k_p, v_p, mode="softmax", dq=LANES, sk_valid=sp, q_off=0, k_off=0)
            o_s = _attention(q_s, k_s, v_s, mode="softmax", dq=LANES, sk_valid=n_all,
                             q_off=past, k_off=0)
            wo = w_a_o[j].astype(BF16)
            states[0].append((ckv_p, kr_p, ckv_s, kr_s))
        elif m == 1 or m == 3:
            heads = B_HEADS if m == 1 else D_HEADS
            w_qkv = (w_b_qkv if m == 1 else w_d_qkv)[j].astype(BF16)
            cache_k, cache_v = (cache_b_k, cache_b_v) if m == 1 else (cache_d_k, cache_d_v)
            q_p, kf_p, vf_p, kb_p, vb_p = _proj_qkv(xp, g_m, sh_p, sc_p, w_qkv, heads)
            q_s, kf_s, vf_s, kb_s, vb_s = _proj_qkv(xs, g_m, sh_s, sc_s, w_qkv, heads)
            k_s, n_all = _with_cache(cache_k[j], kb_s, BF16)
            v_s, _ = _with_cache(cache_v[j], vb_s, BF16)
            lc = cache_k.shape[2]
            if m == 1:
                o_p = _attention(q_p, kb_p, vb_p, mode="stick", dq=HEAD_DIM, sk_valid=sp,
                                 q_off=0, k_off=0)
                o_s = _attention(q_s, k_s, v_s, mode="stick", dq=HEAD_DIM, sk_valid=n_all,
                                 q_off=past, k_off=past - lc)
                wo = w_b_o[j].astype(BF16)
                states[1].append((kf_p, vf_p, kf_s, vf_s))
            else:
                n_band = D_PREV_CHUNKS * CHUNK // KV_BLOCK + 1
                rows = jnp.arange(Q_TILE)[:, None]
                cols = jnp.arange(n_band * KV_BLOCK)[None, :]
                rel = jnp.clip(D_PREV_CHUNKS * CHUNK + rows - cols, -D_REL_CLIP, D_REL_CLIP) + D_REL_CLIP
                bias = rel_bias_d[j][:, rel]
                bias = bias.reshape(heads, Q_TILE, n_band, KV_BLOCK).transpose(0, 2, 1, 3)
                o_p = _attention(q_p, kb_p, vb_p, mode="softmax", dq=HEAD_DIM, sk_valid=sp,
                                 q_off=0, k_off=0, n_prev=D_PREV_CHUNKS, bias=bias)
                o_s = _attention(q_s, k_s, v_s, mode="softmax", dq=HEAD_DIM, sk_valid=n_all,
                                 q_off=past, k_off=past - lc, n_prev=D_PREV_CHUNKS,
                                 bias=bias[:, :, :t, :])
                wo = w_d_o[j].astype(BF16)
                keep = min(D_PREV_CHUNKS * CHUNK, sp)
                k_roll = jnp.concatenate([cache_k[j].reshape(bs, lc, -1), kf_s], axis=1)[:, t:]
                v_roll = jnp.concatenate([cache_v[j].reshape(bs, lc, -1), vf_s], axis=1)[:, t:]
                states[3].append((kf_p[:, sp - keep:], vf_p[:, sp - keep:], k_roll, v_roll))
        else:
            order = _c_head_order()
            q_width, kv_width = C_HEADS * HEAD_DIM, C_KV_HEADS * HEAD_DIM
            col = np.concatenate([(order[:, None] * HEAD_DIM + np.arange(HEAD_DIM)).reshape(-1),
                                  np.arange(q_width, q_width + 2 * kv_width)])
            w_qkv = w_c_qkv[j][:, col].astype(BF16)
            b_qkv = b_c_qkv[j][col][None, :]
            sink = sink_c[j][order]
            wo = w_c_o[j].reshape(C_HEADS, HEAD_DIM, d)[order].reshape(q_width, d).astype(BF16)
            lanes = [0, HEAD_DIM]

            def project(x, sh, sc, pos):
                tb = _tables_for(x.shape[0], x.shape[1], pos, C_ROT, lanes)
                return _proj_c(x, g_m, sh, sc, w_qkv, b_qkv, tb)

            q_p, kf_p, vf_p, kb_p, vb_p = project(xp, sh_p, sc_p, pos_p)
            q_s, kf_s, vf_s, kb_s, vb_s = project(xs, sh_s, sc_s, pos_s)
            lc = cache_c_k.shape[2]
            k_s, n_all = _with_cache(cache_c_k[j], kb_s, BF16)
            v_s, _ = _with_cache(cache_c_v[j], vb_s, BF16)
            group = C_HEADS // C_KV_HEADS
            o_p = _attention(q_p, kb_p, vb_p, mode="softmax", dq=HEAD_DIM, sk_valid=sp, q_off=0,
                             k_off=0, n_prev=C_PREV_CHUNKS, kv_group=group, sink=sink)
            o_s = _attention(q_s, k_s, v_s, mode="softmax", dq=HEAD_DIM, sk_valid=n_all,
                             q_off=past, k_off=past - lc, n_prev=C_PREV_CHUNKS, kv_group=group,
                             sink=sink)
            keep = min(C_PREV_CHUNKS * CHUNK, sp)
            k_roll = jnp.concatenate([cache_c_k[j].reshape(bs, lc, -1), kf_s], axis=1)[:, t:]
            v_roll = jnp.concatenate([cache_c_v[j].reshape(bs, lc, -1), vf_s], axis=1)[:, t:]
            states[2].append((kf_p[:, sp - keep:], vf_p[:, sp - keep:], k_roll, v_roll))

        g_f = g_ffn[i][None, :]
        w_in, w_out = w_ffn_in[i].astype(BF16), w_ffn_out[i].astype(BF16)
        xp = _block(xp, o_p, wo, gm_p, g_f, shf_p, scf_p, gf_p, w_in, w_out)
        xs = _block(xs, o_s, wo, gm_s, g_f, shf_s, scf_s, gf_s, w_in, w_out)

    y_p = _final_norm(xp, g_final[None, :])
    y_s = _final_norm(xs, g_final[None, :])

    def stacked(entries, n_heads=None):
        outs = []
        for parts in zip(*entries):
            a = jnp.stack(parts, axis=0)
            if n_heads is not None:
                a = a.reshape(a.shape[:3] + (n_heads, HEAD_DIM))
            outs.append(a)
        return tuple(outs)

    return ((y_p, y_s) + stacked(states[0]) + stacked(states[1], B_HEADS)
            + stacked(states[2], C_KV_HEADS) + stacked(states[3], D_HEADS))
```

```python
import functools
import math

import numpy as np
import jax
import jax.numpy as jnp
from jax import lax
from jax.experimental import pallas as pl
from jax.experimental.pallas import tpu as pltpu

F32 = jnp.float32
BF16 = jnp.bfloat16

CHUNK = 64
HEAD_DIM = 64
ROPE_THETA = 500000.0
NORM_EPS = 1e-6
N_MIXERS = 4
A_HEADS, A_Q_LORA, A_KV_LORA, A_NOPE, A_ROPE, A_V = 16, 384, 256, 64, 32, 64
B_HEADS = 16
C_HEADS, C_KV_HEADS, C_WINDOW, C_ROT = 16, 4, 128, 16
D_HEADS, D_PREV_CHUNKS, D_REL_CLIP = 16, 8, 128
C_PREV_CHUNKS = C_WINDOW // CHUNK

LANES = 128
V7X_VMEM_LIMIT = 56 * 1024 * 1024

TOKEN_TILE = 512
Q_TILE = 256
KV_BLOCK = 256
A_KV_BLOCK = 512
A_Q_TILE = 256
MASK_VALUE = -1e30
STICK_DEAD_BITS = 150.0


def _cparams(n_axes):
    return pltpu.CompilerParams(
        dimension_semantics=("parallel",) * n_axes, vmem_limit_bytes=V7X_VMEM_LIMIT)


def _adaln_kernel(c_ref, w_ref, b_ref, o_ref):
    c = c_ref[...]
    a = (c * jax.nn.sigmoid(c)).astype(BF16)
    y = jnp.dot(a, w_ref[0].astype(BF16), preferred_element_type=F32) + b_ref[0]
    o_ref[0, 0] = y


def _adaln(c_all, w_mod, b_mod):
    depth, d, d6 = w_mod.shape
    n = c_all.shape[0]
    return pl.pallas_call(
        _adaln_kernel,
        grid=(depth, d6 // d),
        in_specs=[
            pl.BlockSpec((n, d), lambda i, k: (0, 0)),
            pl.BlockSpec((1, d, d), lambda i, k: (i, 0, k)),
            pl.BlockSpec((1, 1, d), lambda i, k: (i, 0, k)),
        ],
        out_specs=pl.BlockSpec((1, 1, n, d), lambda i, k: (i, k, 0, 0)),
        out_shape=jax.ShapeDtypeStruct((depth, d6 // d, n, d), F32),
        compiler_params=_cparams(2),
    )(c_all, w_mod, b_mod.reshape(depth, 1, d6))


def _token_tiling(bx, sx):
    if sx >= TOKEN_TILE:
        ts = TOKEN_TILE
        while sx % ts:
            ts //= 2
        return 1, ts
    nb = min(bx, TOKEN_TILE // sx)
    while bx % nb:
        nb -= 1
    return nb, sx


def _tok_spec(nb, ts, width):
    return pl.BlockSpec((nb, ts, width), lambda b, s: (b, s, 0))


def _mod_spec(nb, d):
    return pl.BlockSpec((nb, 1, d), lambda b, s: (b, 0, 0))


def _const_spec(shape):
    nd = len(shape)
    return pl.BlockSpec(shape, lambda b, s: (0,) * nd)


def _table_spec(rows):
    return pl.BlockSpec((1, rows, LANES), lambda b, s: (s, 0, 0))


def _rms(x):
    return x * lax.rsqrt(jnp.mean(x * x, axis=-1, keepdims=True) + NORM_EPS)


def _modulated(x_ref, g_ref, shift_ref, scale_ref):
    y = _rms(x_ref[...]) * g_ref[...]
    h = y * (1.0 + scale_ref[...]) + shift_ref[...]
    return h.reshape(-1, h.shape[-1]).astype(BF16)


def _rope_lanes(x, cos_t, sin_up, sin_dn, half):
    return (x * cos_t + pltpu.roll(x, half, 1) * sin_up
            + pltpu.roll(x, LANES - half, 1) * sin_dn)


def _store_tok(ref, lo, val):
    nb, ts = ref.shape[0], ref.shape[1]
    w = val.shape[-1]
    ref[:, :, lo:lo + w] = val.reshape(nb, ts, w).astype(ref.dtype)


def _proj_qkv_kernel(x_ref, g_ref, sh_ref, sc_ref, w_ref, q_ref, kf_ref, vf_ref, kb_ref, vb_ref,
                     *, width, q_scale):
    h = _modulated(x_ref, g_ref, sh_ref, sc_ref)
    q = jnp.dot(h, w_ref[:, 0:width], preferred_element_type=F32)
    _store_tok(q_ref, 0, q * q_scale)
    k = jnp.dot(h, w_ref[:, width:2 * width], preferred_element_type=F32)
    _store_tok(kf_ref, 0, k)
    _store_tok(kb_ref, 0, k)
    v = jnp.dot(h, w_ref[:, 2 * width:3 * width], preferred_element_type=F32)
    _store_tok(vf_ref, 0, v)
    _store_tok(vb_ref, 0, v)


def _proj_qkv(x, g, shift, scale, w_bf16, n_heads, q_scale):
    bx, sx, d = x.shape
    width = n_heads * HEAD_DIM
    nb, ts = _token_tiling(bx, sx)
    out = lambda dt: jax.ShapeDtypeStruct((bx, sx, width), dt)
    return pl.pallas_call(
        functools.partial(_proj_qkv_kernel, width=width, q_scale=q_scale),
        grid=(bx // nb, sx // ts),
        in_specs=[_tok_spec(nb, ts, d), _const_spec((1, d)), _mod_spec(nb, d), _mod_spec(nb, d),
                  _const_spec(w_bf16.shape)],
        out_specs=[_tok_spec(nb, ts, width)] * 5,
        out_shape=[out(BF16), out(F32), out(F32), out(BF16), out(BF16)],
        compiler_params=_cparams(2),
    )(x, g, shift, scale, w_bf16)


def _proj_c_kernel(x_ref, g_ref, sh_ref, sc_ref, w_ref, b_ref, tc_ref, tu_ref, td_ref,
                   q_ref, kf_ref, vf_ref, kb_ref, vb_ref, *, q_width, kv_width, q_scale):
    h = _modulated(x_ref, g_ref, sh_ref, sc_ref)
    qkv = jnp.dot(h, w_ref[...], preferred_element_type=F32) + b_ref[...]
    cos_t, sin_up, sin_dn = tc_ref[0], tu_ref[0], td_ref[0]
    half = C_ROT // 2
    for j in range(q_width // LANES):
        xg = qkv[:, j * LANES:(j + 1) * LANES]
        _store_tok(q_ref, j * LANES, _rope_lanes(xg, cos_t, sin_up, sin_dn, half) * q_scale)
    for j in range(kv_width // LANES):
        lo = q_width + j * LANES
        kg = _rope_lanes(qkv[:, lo:lo + LANES], cos_t, sin_up, sin_dn, half)
        _store_tok(kf_ref, j * LANES, kg)
        _store_tok(kb_ref, j * LANES, kg)
    v = qkv[:, q_width + kv_width:q_width + 2 * kv_width]
    _store_tok(vf_ref, 0, v)
    _store_tok(vb_ref, 0, v)


def _proj_c(x, g, shift, scale, w_bf16, bias, tables):
    bx, sx, d = x.shape
    q_width, kv_width = C_HEADS * HEAD_DIM, C_KV_HEADS * HEAD_DIM
    nb, ts = _token_tiling(bx, sx)
    out = lambda w, dt: jax.ShapeDtypeStruct((bx, sx, w), dt)
    return pl.pallas_call(
        functools.partial(_proj_c_kernel, q_width=q_width, kv_width=kv_width,
                          q_scale=HEAD_DIM ** -0.5),
        grid=(bx // nb, sx // ts),
        in_specs=[_tok_spec(nb, ts, d), _const_spec((1, d)), _mod_spec(nb, d), _mod_spec(nb, d),
                  _const_spec(w_bf16.shape), _const_spec(bias.shape)] + [_table_spec(nb * ts)] * 3,
        out_specs=[_tok_spec(nb, ts, q_width)] + [_tok_spec(nb, ts, kv_width)] * 4,
        out_shape=[out(q_width, BF16), out(kv_width, F32), out(kv_width, F32),
                   out(kv_width, BF16), out(kv_width, BF16)],
        compiler_params=_cparams(2),
    )(x, g, shift, scale, w_bf16, bias, *tables)


def _proj_a_kernel(x_ref, g_ref, sh_ref, sc_ref, wd_ref, gq_ref, gkv_ref, wq_ref,
                   qc_ref, qu_ref, qd_ref, kc_ref, ku_ref, kd_ref,
                   q_ref, ckv_ref, kr_ref, *, q_scale):
    h = _modulated(x_ref, g_ref, sh_ref, sc_ref)
    down = jnp.dot(h, wd_ref[...], preferred_element_type=F32)
    cq = (_rms(down[:, :A_Q_LORA]) * gq_ref[...]).astype(BF16)
    ckv = _rms(down[:, A_Q_LORA:A_Q_LORA + A_KV_LORA]) * gkv_ref[...]
    _store_tok(ckv_ref, 0, ckv)
    lo = A_Q_LORA + A_KV_LORA
    half = A_ROPE // 2
    kr = _rope_lanes(down[:, lo:lo + LANES], kc_ref[0], ku_ref[0], kd_ref[0], half)
    _store_tok(kr_ref, 0, kr[:, :A_ROPE])
    q = jnp.dot(cq, wq_ref[...], preferred_element_type=F32)
    qc, qu, qd = qc_ref[0], qu_ref[0], qd_ref[0]
    for j in range(A_HEADS):
        qg = _rope_lanes(q[:, j * LANES:(j + 1) * LANES], qc, qu, qd, half)
        _store_tok(q_ref, j * LANES, qg * q_scale)


def _proj_a(x, g, shift, scale, wd_bf16, g_q, g_kv, wq_bf16, q_tables, k_tables):
    bx, sx, d = x.shape
    nb, ts = _token_tiling(bx, sx)
    return pl.pallas_call(
        functools.partial(_proj_a_kernel, q_scale=(A_NOPE + A_ROPE) ** -0.5),
        grid=(bx // nb, sx // ts),
        in_specs=[_tok_spec(nb, ts, d), _const_spec((1, d)), _mod_spec(nb, d), _mod_spec(nb, d),
                  _const_spec(wd_bf16.shape), _const_spec(g_q.shape), _const_spec(g_kv.shape),
                  _const_spec(wq_bf16.shape)] + [_table_spec(nb * ts)] * 6,
        out_specs=[_tok_spec(nb, ts, A_HEADS * LANES), _tok_spec(nb, ts, A_KV_LORA),
                   _tok_spec(nb, ts, A_ROPE)],
        out_shape=[jax.ShapeDtypeStruct((bx, sx, A_HEADS * LANES), BF16),
                   jax.ShapeDtypeStruct((bx, sx, A_KV_LORA), F32),
                   jax.ShapeDtypeStruct((bx, sx, A_ROPE), F32)],
        compiler_params=_cparams(2),
    )(x, g, shift, scale, wd_bf16, g_q, g_kv, wq_bf16, *q_tables, *k_tables)


def _expand_a_kernel(ckv_ref, kr_ref, wk_ref, wr_ref, wv_ref, k_ref, v_ref):
    ckv = ckv_ref[...]
    ckv = ckv.reshape(-1, ckv.shape[-1]).astype(BF16)
    kr = kr_ref[...]
    kr = kr.reshape(-1, kr.shape[-1]).astype(BF16)
    k = (jnp.dot(ckv, wk_ref[...], preferred_element_type=F32)
         + jnp.dot(kr, wr_ref[...], preferred_element_type=F32))
    _store_tok(k_ref, 0, k)
    _store_tok(v_ref, 0, jnp.dot(ckv, wv_ref[...], preferred_element_type=F32))


def _expand_a(ckv, kr, wk, wr, wv):
    bx, sx, _ = ckv.shape
    nb, ts = _token_tiling(bx, sx)
    return pl.pallas_call(
        _expand_a_kernel,
        grid=(bx // nb, sx // ts),
        in_specs=[_tok_spec(nb, ts, A_KV_LORA), _tok_spec(nb, ts, A_ROPE),
                  _const_spec(wk.shape), _const_spec(wr.shape), _const_spec(wv.shape)],
        out_specs=[_tok_spec(nb, ts, A_HEADS * LANES), _tok_spec(nb, ts, A_HEADS * A_V)],
        out_shape=[jax.ShapeDtypeStruct((bx, sx, A_HEADS * LANES), BF16),
                   jax.ShapeDtypeStruct((bx, sx, A_HEADS * A_V), BF16)],
        compiler_params=_cparams(2),
    )(ckv, kr, wk, wr, wv)


def _block_kernel(x_ref, o_ref, wo_ref, gm_ref, g_ref, sh_ref, sc_ref, gf_ref, wi_ref, wout_ref,
                  y_ref, *, hidden, chunk):
    nb, ts, d = x_ref.shape
    o = o_ref[...].reshape(nb * ts, -1)
    mix = jnp.dot(o, wo_ref[...], preferred_element_type=F32).reshape(nb, ts, d)
    x1 = x_ref[...] + gm_ref[...] * mix
    h = (_rms(x1) * g_ref[...]) * (1.0 + sc_ref[...]) + sh_ref[...]
    h = h.reshape(nb * ts, d).astype(BF16)
    acc = jnp.zeros((nb * ts, d), F32)
    for c in range(hidden // chunk):
        gate = jnp.dot(h, wi_ref[:, c * chunk:(c + 1) * chunk], preferred_element_type=F32)
        up = jnp.dot(h, wi_ref[:, hidden + c * chunk:hidden + (c + 1) * chunk],
                     preferred_element_type=F32)
        act = (gate * jax.nn.sigmoid(gate) * up).astype(BF16)
        acc = acc + jnp.dot(act, wout_ref[c * chunk:(c + 1) * chunk, :],
                            preferred_element_type=F32)
    y_ref[...] = x1 + gf_ref[...] * acc.reshape(nb, ts, d)


def _block(x, o, wo, gate_m, g_ffn, shift_f, scale_f, gate_f, w_in, w_out):
    bx, sx, d = x.shape
    hidden = w_out.shape[0]
    nb, ts = _token_tiling(bx, sx)
    resident = lambda shape: pl.BlockSpec(shape, lambda b, s: (0,) * len(shape),
                                          pipeline_mode=pl.Buffered(1))
    return pl.pallas_call(
        functools.partial(_block_kernel, hidden=hidden, chunk=256),
        grid=(bx // nb, sx // ts),
        in_specs=[_tok_spec(nb, ts, d), _tok_spec(nb, ts, o.shape[-1]), resident(wo.shape),
                  _mod_spec(nb, d), _const_spec((1, d)), _mod_spec(nb, d), _mod_spec(nb, d),
                  _mod_spec(nb, d), resident(w_in.shape), resident(w_out.shape)],
        out_specs=_tok_spec(nb, ts, d),
        out_shape=jax.ShapeDtypeStruct((bx, sx, d), F32),
        compiler_params=_cparams(2),
    )(x, o, wo, gate_m, g_ffn, shift_f, scale_f, gate_f, w_in, w_out)


def _final_norm_kernel(x_ref, g_ref, y_ref):
    y_ref[...] = _rms(x_ref[...]) * g_ref[...]


def _final_norm(x, g):
    bx, sx, d = x.shape
    nb, ts = _token_tiling(bx, sx)
    return pl.pallas_call(
        _final_norm_kernel,
        grid=(bx // nb, sx // ts),
        in_specs=[_tok_spec(nb, ts, d), _const_spec((1, d))],
        out_specs=_tok_spec(nb, ts, d),
        out_shape=jax.ShapeDtypeStruct((bx, sx, d), F32),
        compiler_params=_cparams(2),
    )(x, g)


def _attn_kernel(*refs, mode, dq, tq, bk, sk_valid, q_off, k_off, n_prev, use_sink, use_bias):
    refs = list(refs)
    q_ref, k_ref, v_ref = refs[:3]
    rest = refs[3:]
    tri_ref = rest.pop(0) if mode == "stick" else None
    sink_ref = rest.pop(0) if use_sink else None
    bias_ref = rest.pop(0) if use_bias else None
    o_ref, stat_ref, acc_ref = rest

    pair = pl.program_id(1)
    q0 = pl.program_id(2) * tq
    qpos0 = q0 + q_off
    stick = mode == "stick"

    if stick:
        hi_idx = qpos0 + (tq - 1) - k_off
    else:
        hi_idx = ((qpos0 + tq - 1) // CHUNK + 1) * CHUNK - k_off
    hi_idx = jnp.minimum(hi_idx, sk_valid)
    if n_prev is not None:
        band_idx = (qpos0 // CHUNK - n_prev) * CHUNK - k_off
        lo_idx = jnp.maximum(band_idx, 0)
    else:
        band_idx = 0
        lo_idx = 0
    kb_lo = lo_idx // bk
    kb_hi = (hi_idx + bk - 1) // bk

    if stick:
        full_end = qpos0 - k_off
    else:
        full_end = (qpos0 // CHUNK + 1) * CHUNK - k_off
    f_hi = jnp.minimum(full_end, sk_valid) // bk
    if n_prev is not None:
        full_start = jnp.maximum(((qpos0 + tq - 1) // CHUNK - n_prev) * CHUNK - k_off, 0)
        f_lo = (full_start + bk - 1) // bk
    else:
        f_lo = kb_lo
    f_lo = jnp.minimum(jnp.maximum(f_lo, kb_lo), kb_hi)
    f_hi = jnp.minimum(jnp.maximum(f_hi, f_lo), kb_hi)

    def visible(kb):
        iq = lax.broadcasted_iota(jnp.int32, (tq, 1), 0)
        ik = kb * bk + lax.broadcasted_iota(jnp.int32, (1, bk), 1)
        qpos = iq + qpos0
        kpos = ik + k_off
        valid = ik < sk_valid
        if stick:
            return (kpos < qpos) & valid
        shift = int(math.log2(CHUNK))
        qc = jnp.right_shift(qpos, shift)
        kc = jnp.right_shift(kpos, shift)
        m = (kc <= qc) & valid
        if n_prev is not None:
            m = m & (kc >= qc - n_prev)
        return m

    lane_half = lax.broadcasted_iota(jnp.int32, (1, LANES), 1) // HEAD_DIM

    if dq == HEAD_DIM:
        q_pair = q_ref[0]
        q_heads = [jnp.where(lane_half == hh, q_pair, jnp.zeros((), BF16)) for hh in range(2)]
        k_lanes = [slice(0, LANES)] * 2
    else:
        q_heads = [q_ref[0, :, hh * dq:(hh + 1) * dq] for hh in range(2)]
        k_lanes = [slice(hh * dq, (hh + 1) * dq) for hh in range(2)]

    def rows(kb):
        return pl.ds(pl.multiple_of(kb * bk, bk), bk)

    def scores(hh, kb):
        return lax.dot_general(q_heads[hh], k_ref[0, rows(kb), k_lanes[hh]],
                               (((1,), (1,)), ((), ())), preferred_element_type=F32)

    if stick:
        acc_ref[...] = jnp.zeros_like(acc_ref)
        stat_ref[0] = jnp.zeros_like(stat_ref[0])

        def block(kb, masked):
            vis = visible(kb) if masked else None
            v_blk = v_ref[0, rows(kb), :]
            tri = tri_ref[...]
            least = None
            for hh in range(2):
                y = scores(hh, kb)
                sp = jnp.maximum(y, 0.0) + jnp.log2(1.0 + jnp.exp2(-jnp.abs(y)))
                if masked:
                    sp = jnp.where(vis, sp, 0.0)
                run = stat_ref[0, hh]
                run_new = run + jnp.sum(sp, axis=1, keepdims=True)
                stat_ref[0, hh] = run_new
                hi = sp.astype(BF16)
                lo = (sp - hi.astype(F32)).astype(BF16)
                suffix = (jnp.dot(hi, tri, preferred_element_type=F32)
                          + jnp.dot(lo, tri, preferred_element_type=F32))
                a = jnp.exp2(y - suffix - run)
                if masked:
                    a = jnp.where(vis, a, 0.0)
                acc_ref[hh] += jnp.dot(a.astype(BF16), v_blk, preferred_element_type=F32)
                head_least = jnp.min(run_new)
                least = head_least if least is None else jnp.minimum(least, head_least)
            return least

        least = lax.fori_loop(0, kb_hi - f_hi, lambda i, c: block(kb_hi - 1 - i, True),
                              jnp.zeros((), F32))
        n_full = f_hi - kb_lo
        lax.while_loop(lambda st: (st[0] < n_full) & (st[1] < STICK_DEAD_BITS),
                       lambda st: (st[0] + 1, block(f_hi - 1 - st[0], False)),
                       (jnp.zeros((), jnp.int32), least))
        outs = [acc_ref[0], acc_ref[1]]
    else:
        for hh in range(2):
            if use_sink:
                stat_ref[0, hh] = jnp.full((tq, 1), sink_ref[2 * pair + hh], F32)
                stat_ref[1, hh] = jnp.ones((tq, 1), F32)
            else:
                stat_ref[0, hh] = jnp.full((tq, 1), MASK_VALUE, F32)
                stat_ref[1, hh] = jnp.zeros((tq, 1), F32)
        acc_ref[...] = jnp.zeros_like(acc_ref)

        def block(kb, masked):
            vis = visible(kb) if masked else None
            v_blk = v_ref[0, rows(kb), :]
            for hh in range(2):
                s = scores(hh, kb)
                if use_bias:
                    s = s + bias_ref[hh, (kb * bk - band_idx) // bk]
                if masked:
                    s = jnp.where(vis, s, MASK_VALUE)
                m = stat_ref[0, hh]
                m_new = jnp.maximum(m, jnp.max(s, axis=1, keepdims=True))
                alpha = jnp.exp(m - m_new)
                p = jnp.exp(s - m_new)
                stat_ref[0, hh] = m_new
                stat_ref[1, hh] = alpha * stat_ref[1, hh] + jnp.sum(p, axis=1, keepdims=True)
                acc_ref[hh] = alpha * acc_ref[hh] + jnp.dot(p.astype(BF16), v_blk,
                                                            preferred_element_type=F32)

        def sweep(lo, hi, masked):
            def body(kb, c):
                block(kb, masked)
                return c
            lax.fori_loop(lo, hi, body, 0)

        if n_prev is not None:
            sweep(kb_lo, f_lo, True)
        sweep(f_lo, f_hi, False)
        sweep(f_hi, kb_hi, True)
        outs = [acc_ref[hh] / stat_ref[1, hh] for hh in range(2)]
    o_ref[0] = jnp.where(lane_half == 0, outs[0], outs[1]).astype(o_ref.dtype)


def _attention(q, k, v, *, mode, dq, sk_valid, q_off, k_off, n_prev=None, kv_group=1,
               sink=None, bias=None, bk=KV_BLOCK, q_tile=Q_TILE):
    bx, sq, qw = q.shape
    skp = k.shape[1]
    n_pairs = qw // (2 * dq)
    tq = min(q_tile, sq)
    assert sq % tq == 0 and skp % bk == 0 and q_off % CHUNK == 0 and tq % 8 == 0
    assert k_off >= 0 and (tq % CHUNK == 0 or sq == tq)
    in_specs = [
        pl.BlockSpec((1, tq, 2 * dq), lambda b, p, i: (b, i, p)),
        pl.BlockSpec((1, skp, 2 * dq), lambda b, p, i: (b, 0, p // kv_group)),
        pl.BlockSpec((1, skp, LANES), lambda b, p, i: (b, 0, p // kv_group)),
    ]
    args = [q, k, v]
    if mode == "stick":
        idx = np.arange(bk)
        in_specs.append(pl.BlockSpec((bk, bk), lambda b, p, i: (0, 0)))
        args.append(jnp.asarray(idx[:, None] >= idx[None, :], BF16))
    if sink is not None:
        in_specs.append(pl.BlockSpec(memory_space=pltpu.SMEM))
        args.append(sink)
    if bias is not None:
        in_specs.append(pl.BlockSpec((2,) + bias.shape[1:], lambda b, p, i: (p, 0, 0, 0)))
        args.append(bias)
    return pl.pallas_call(
        functools.partial(_attn_kernel, mode=mode, dq=dq, tq=tq, bk=bk, sk_valid=sk_valid,
                          q_off=q_off, k_off=k_off, n_prev=n_prev,
                          use_sink=sink is not None, use_bias=bias is not None),
        grid=(bx, n_pairs, sq // tq),
        in_specs=in_specs,
        out_specs=pl.BlockSpec((1, tq, LANES), lambda b, p, i: (b, i, p)),
        out_shape=jax.ShapeDtypeStruct((bx, sq, n_pairs * LANES), BF16),
        scratch_shapes=[pltpu.VMEM((2, 2, tq, 1), F32), pltpu.VMEM((2, tq, LANES), F32)],
        compiler_params=_cparams(3),
    )(*args)


def _band_bias_kernel(e_ref, o_ref, *, n_band):
    w = e_ref.shape[-1]
    x = jnp.broadcast_to(e_ref[0], (Q_TILE, w))
    toeplitz = pltpu.roll(x, 0, 1, stride=1, stride_axis=0)
    for nb in range(n_band):
        o_ref[0, nb] = toeplitz[:, nb * KV_BLOCK:(nb + 1) * KV_BLOCK]


def _band_bias(rel_bias):
    heads = rel_bias.shape[0]
    clip = (rel_bias.shape[1] - 1) // 2
    band = D_PREV_CHUNKS * CHUNK
    n_band = band // KV_BLOCK + 1
    width = n_band * KV_BLOCK
    w = width + Q_TILE
    assert band >= clip and width > band + clip and band % KV_BLOCK == 0
    top = jnp.broadcast_to(rel_bias[:, -1:], (heads, band - clip + 1))
    mid = jnp.flip(rel_bias[:, :2 * clip], axis=1)
    low = jnp.broadcast_to(rel_bias[:, :1], (heads, width - (band + clip + 1)))
    neg = jnp.broadcast_to(rel_bias[:, -1:], (heads, w - width))
    e = jnp.concatenate([top, mid, low, neg], axis=1)[:, None, :]
    return pl.pallas_call(
        functools.partial(_band_bias_kernel, n_band=n_band),
        grid=(heads,),
        in_specs=[pl.BlockSpec((1, 1, w), lambda h: (h, 0, 0))],
        out_specs=pl.BlockSpec((1, n_band, Q_TILE, KV_BLOCK), lambda h: (h, 0, 0, 0)),
        out_shape=jax.ShapeDtypeStruct((heads, n_band, Q_TILE, KV_BLOCK), F32),
        compiler_params=_cparams(1),
    )(e)


def _rope_tables(pos, n_rot, lane_offsets, rows_repeat=1):
    half = n_rot // 2
    inv = ROPE_THETA ** (-jnp.arange(half, dtype=F32) * 2.0 / n_rot)
    ang = pos.astype(F32)[:, None] * inv[None, :]
    cos, sin = jnp.cos(ang), jnp.sin(ang)
    n = pos.shape[0]
    cos_t = jnp.ones((n, LANES), F32)
    sin_up = jnp.zeros((n, LANES), F32)
    sin_dn = jnp.zeros((n, LANES), F32)
    for o in lane_offsets:
        cos_t = cos_t.at[:, o:o + half].set(cos).at[:, o + half:o + n_rot].set(cos)
        sin_dn = sin_dn.at[:, o:o + half].set(-sin)
        sin_up = sin_up.at[:, o + half:o + n_rot].set(sin)
    def tiles(t):
        if rows_repeat > 1:
            return jnp.tile(t, (rows_repeat, 1))[None]
        rows = min(n, TOKEN_TILE)
        return t.reshape(n // rows, rows, LANES)
    return tiles(cos_t), tiles(sin_up), tiles(sin_dn)


def _tables_for(bx, sx, pos, n_rot, lane_offsets):
    nb, ts = _token_tiling(bx, sx)
    return _rope_tables(pos, n_rot, lane_offsets, rows_repeat=nb if nb > 1 else 1)


def _c_head_order():
    rep = C_HEADS // C_KV_HEADS
    order = []
    for p in range(C_HEADS // 2):
        g2, i = divmod(p, rep)
        order += [rep * (2 * g2) + i, rep * (2 * g2 + 1) + i]
    return np.asarray(order)


def _pad_rows(a, rows):
    return jnp.pad(a, ((0, 0), (0, rows - a.shape[1]), (0, 0)))


def _with_cache(cache, new, dtype, block=KV_BLOCK):
    full = jnp.concatenate([cache.reshape(cache.shape[0], cache.shape[1], -1).astype(dtype),
                            new.astype(dtype)], axis=1)
    rows = -(-full.shape[1] // block) * block
    return _pad_rows(full, rows), full.shape[1]


def kernel(x_prompt, x_sample, c_prompt, c_sample, cache_a_ckv, cache_a_krope, cache_b_k, cache_b_v,
           cache_c_k, cache_c_v, cache_d_k, cache_d_v, w_mod, b_mod, g_mix, g_ffn, w_ffn_in, w_ffn_out,
           w_a_down, g_a_q, g_a_kv, w_a_uq, w_a_uk, w_a_uv, w_a_o, w_b_qkv, w_b_o,
           w_c_qkv, b_c_qkv, sink_c, w_c_o, w_d_qkv, rel_bias_d, w_d_o, g_final):
    bp, sp, d = x_prompt.shape
    bs, t, _ = x_sample.shape
    depth = w_mod.shape[0]
    past = cache_a_ckv.shape[2]
    pos_p = jnp.arange(sp)
    pos_s = past + jnp.arange(t)

    mods = _adaln(jnp.concatenate([c_prompt, c_sample], axis=0), w_mod, b_mod)

    def mod(i, k):
        m = mods[i, k][:, None, :]
        return m[:bp], m[bp:]

    xp, xs = x_prompt, x_sample
    states = [[] for _ in range(N_MIXERS)]
    for i in range(depth):
        m, j = i % N_MIXERS, i // N_MIXERS
        (sh_p, sh_s), (sc_p, sc_s), (gm_p, gm_s) = mod(i, 0), mod(i, 1), mod(i, 2)
        (shf_p, shf_s), (scf_p, scf_s), (gf_p, gf_s) = mod(i, 3), mod(i, 4), mod(i, 5)
        g_m = g_mix[i][None, :]
        if m == 0:
            n_down = A_Q_LORA + A_KV_LORA + A_ROPE
            wd = jnp.pad(w_a_down[j], ((0, 0), (0, A_Q_LORA + A_KV_LORA + LANES - n_down))).astype(BF16)
            wq = w_a_uq[j].reshape(A_Q_LORA, A_HEADS, A_NOPE + A_ROPE)
            wq = jnp.pad(wq, ((0, 0), (0, 0), (0, LANES - A_NOPE - A_ROPE)))
            wq = wq.reshape(A_Q_LORA, A_HEADS * LANES).astype(BF16)
            wk = jnp.pad(w_a_uk[j], ((0, 0), (0, 0), (0, LANES - A_NOPE)))
            wk = wk.reshape(A_KV_LORA, A_HEADS * LANES).astype(BF16)
            place = jnp.zeros((A_ROPE, A_HEADS, LANES), F32)
            place = place.at[jnp.arange(A_ROPE), :, A_NOPE + jnp.arange(A_ROPE)].set(1.0)
            wr = place.reshape(A_ROPE, A_HEADS * LANES).astype(BF16)
            wv = w_a_uv[j].reshape(A_KV_LORA, A_HEADS * A_V).astype(BF16)
            g_q, g_kv = g_a_q[j][None, :], g_a_kv[j][None, :]

            def project(x, sh, sc, pos):
                qt = _tables_for(x.shape[0], x.shape[1], pos, A_ROPE, [A_NOPE])
                kt = _tables_for(x.shape[0], x.shape[1], pos, A_ROPE, [0])
                return _proj_a(x, g_m, sh, sc, wd, g_q, g_kv, wq, qt, kt)

            q_p, ckv_p, kr_p = project(xp, sh_p, sc_p, pos_p)
            q_s, ckv_s, kr_s = project(xs, sh_s, sc_s, pos_s)
            k_p, v_p = _expand_a(ckv_p, kr_p, wk, wr, wv)
            ckv_all, n_all = _with_cache(cache_a_ckv[j], ckv_s, F32, A_KV_BLOCK)
            kr_all, _ = _with_cache(cache_a_krope[j], kr_s, F32, A_KV_BLOCK)
            k_s, v_s = _expand_a(ckv_all, kr_all, wk, wr, wv)
            o_p = _attention(q_p, k_p, v_p, mode="softmax", dq=LANES, sk_valid=sp, q_off=0, k_off=0,
                             bk=A_KV_BLOCK, q_tile=A_Q_TILE)
            o_s = _attention(q_s, k_s, v_s, mode="softmax", dq=LANES, sk_valid=n_all,
                             q_off=past, k_off=0, bk=A_KV_BLOCK)
            wo = w_a_o[j].astype(BF16)
            states[0].append((ckv_p, kr_p, ckv_s, kr_s))
        elif m == 1 or m == 3:
            heads = B_HEADS if m == 1 else D_HEADS
            w_qkv = (w_b_qkv if m == 1 else w_d_qkv)[j].astype(BF16)
            cache_k, cache_v = (cache_b_k, cache_b_v) if m == 1 else (cache_d_k, cache_d_v)
            q_scale = HEAD_DIM ** -0.5 * (math.log2(math.e) if m == 1 else 1.0)
            q_p, kf_p, vf_p, kb_p, vb_p = _proj_qkv(xp, g_m, sh_p, sc_p, w_qkv, heads, q_scale)
            q_s, kf_s, vf_s, kb_s, vb_s = _proj_qkv(xs, g_m, sh_s, sc_s, w_qkv, heads, q_scale)
            k_s, n_all = _with_cache(cache_k[j], kb_s, BF16)
            v_s, _ = _with_cache(cache_v[j], vb_s, BF16)
            lc = cache_k.shape[2]
            if m == 1:
                o_p = _attention(q_p, kb_p, vb_p, mode="stick", dq=HEAD_DIM, sk_valid=sp,
                                 q_off=0, k_off=0)
                o_s = _attention(q_s, k_s, v_s, mode="stick", dq=HEAD_DIM, sk_valid=n_all,
                                 q_off=past, k_off=past - lc)
                wo = w_b_o[j].astype(BF16)
                states[1].append((kf_p, vf_p, kf_s, vf_s))
            else:
                bias = _band_bias(rel_bias_d[j])
                o_p = _attention(q_p, kb_p, vb_p, mode="softmax", dq=HEAD_DIM, sk_valid=sp,
                                 q_off=0, k_off=0, n_prev=D_PREV_CHUNKS, bias=bias)
                o_s = _attention(q_s, k_s, v_s, mode="softmax", dq=HEAD_DIM, sk_valid=n_all,
                                 q_off=past, k_off=past - lc, n_prev=D_PREV_CHUNKS,
                                 bias=bias[:, :, :t, :])
                wo = w_d_o[j].astype(BF16)
                keep = min(D_PREV_CHUNKS * CHUNK, sp)
                k_roll = jnp.concatenate([cache_k[j].reshape(bs, lc, -1), kf_s], axis=1)[:, t:]
                v_roll = jnp.concatenate([cache_v[j].reshape(bs, lc, -1), vf_s], axis=1)[:, t:]
                states[3].append((kf_p[:, sp - keep:], vf_p[:, sp - keep:], k_roll, v_roll))
        else:
            order = _c_head_order()
            q_width, kv_width = C_HEADS * HEAD_DIM, C_KV_HEADS * HEAD_DIM
            col = np.concatenate([(order[:, None] * HEAD_DIM + np.arange(HEAD_DIM)).reshape(-1),
                                  np.arange(q_width, q_width + 2 * kv_width)])
            w_qkv = w_c_qkv[j][:, col].astype(BF16)
            b_qkv = b_c_qkv[j][col][None, :]
            sink = sink_c[j][order]
            wo = w_c_o[j].reshape(C_HEADS, HEAD_DIM, d)[order].reshape(q_width, d).astype(BF16)
            lanes = [0, HEAD_DIM]

            def project(x, sh, sc, pos):
                tb = _tables_for(x.shape[0], x.shape[1], pos, C_ROT, lanes)
                return _proj_c(x, g_m, sh, sc, w_qkv, b_qkv, tb)

            q_p, kf_p, vf_p, kb_p, vb_p = project(xp, sh_p, sc_p, pos_p)
            q_s, kf_s, vf_s, kb_s, vb_s = project(xs, sh_s, sc_s, pos_s)
            lc = cache_c_k.shape[2]
            k_s, n_all = _with_cache(cache_c_k[j], kb_s, BF16)
            v_s, _ = _with_cache(cache_c_v[j], vb_s, BF16)
            group = C_HEADS // C_KV_HEADS
            o_p = _attention(q_p, kb_p, vb_p, mode="softmax", dq=HEAD_DIM, sk_valid=sp, q_off=0,
                             k_off=0, n_prev=C_PREV_CHUNKS, kv_group=group, sink=sink)
            o_s = _attention(q_s, k_s, v_s, mode="softmax", dq=HEAD_DIM, sk_valid=n_all,
                             q_off=past, k_off=past - lc, n_prev=C_PREV_CHUNKS, kv_group=group,
                             sink=sink)
            keep = min(C_PREV_CHUNKS * CHUNK, sp)
            k_roll = jnp.concatenate([cache_c_k[j].reshape(bs, lc, -1), kf_s], axis=1)[:, t:]
            v_roll = jnp.concatenate([cache_c_v[j].reshape(bs, lc, -1), vf_s], axis=1)[:, t:]
            states[2].append((kf_p[:, sp - keep:], vf_p[:, sp - keep:], k_roll, v_roll))

        g_f = g_ffn[i][None, :]
        w_in, w_out = w_ffn_in[i].astype(BF16), w_ffn_out[i].astype(BF16)
        xp = _block(xp, o_p, wo, gm_p, g_f, shf_p, scf_p, gf_p, w_in, w_out)
        xs = _block(xs, o_s, wo, gm_s, g_f, shf_s, scf_s, gf_s, w_in, w_out)

    y_p = _final_norm(xp, g_final[None, :])
    y_s = _final_norm(xs, g_final[None, :])

    def stacked(entries, n_heads=None):
        outs = []
        for parts in zip(*entries):
            a = jnp.stack(parts, axis=0)
            if n_heads is not None:
                a = a.reshape(a.shape[:3] + (n_heads, HEAD_DIM))
            outs.append(a)
        return tuple(outs)

    return ((y_p, y_s) + stacked(states[0]) + stacked(states[1], B_HEADS)
            + stacked(states[2], C_KV_HEADS) + stacked(states[3], D_HEADS))
```

```python
import functools
import math

import numpy as np
import jax
import jax.numpy as jnp
from jax import lax
from jax.experimental import pallas as pl
from jax.experimental.pallas import tpu as pltpu

F32 = jnp.float32
BF16 = jnp.bfloat16

CHUNK = 64
HEAD_DIM = 64
ROPE_THETA = 500000.0
NORM_EPS = 1e-6
N_MIXERS = 4
A_HEADS, A_Q_LORA, A_KV_LORA, A_NOPE, A_ROPE, A_V = 16, 384, 256, 64, 32, 64
B_HEADS = 16
C_HEADS, C_KV_HEADS, C_WINDOW, C_ROT = 16, 4, 128, 16
D_HEADS, D_PREV_CHUNKS, D_REL_CLIP = 16, 8, 128
C_PREV_CHUNKS = C_WINDOW // CHUNK

LANES = 128
V7X_VMEM_LIMIT = 56 * 1024 * 1024

TOKEN_TILE = 512
Q_TILE = 256
KV_BLOCK = 256
A_KV_BLOCK = 1024
C_BAND_WIDTH = C_PREV_CHUNKS * CHUNK + Q_TILE
D_BAND_WIDTH = D_PREV_CHUNKS * CHUNK + Q_TILE
A_Q_TILE = 256
MASK_VALUE = -1e30
STICK_DEAD_BITS = 150.0


def _cparams(n_axes):
    return pltpu.CompilerParams(
        dimension_semantics=("parallel",) * n_axes, vmem_limit_bytes=V7X_VMEM_LIMIT)


def _adaln_kernel(c_ref, w_ref, b_ref, o_ref):
    c = c_ref[...]
    a = (c * jax.nn.sigmoid(c)).astype(BF16)
    y = jnp.dot(a, w_ref[0].astype(BF16), preferred_element_type=F32) + b_ref[0]
    o_ref[0, 0] = y


def _adaln(c_all, w_mod, b_mod):
    depth, d, d6 = w_mod.shape
    n = c_all.shape[0]
    return pl.pallas_call(
        _adaln_kernel,
        grid=(depth, d6 // d),
        in_specs=[
            pl.BlockSpec((n, d), lambda i, k: (0, 0)),
            pl.BlockSpec((1, d, d), lambda i, k: (i, 0, k)),
            pl.BlockSpec((1, 1, d), lambda i, k: (i, 0, k)),
        ],
        out_specs=pl.BlockSpec((1, 1, n, d), lambda i, k: (i, k, 0, 0)),
        out_shape=jax.ShapeDtypeStruct((depth, d6 // d, n, d), F32),
        compiler_params=_cparams(2),
    )(c_all, w_mod, b_mod.reshape(depth, 1, d6))


def _token_tiling(bx, sx):
    if sx >= TOKEN_TILE:
        ts = TOKEN_TILE
        while sx % ts:
            ts //= 2
        return 1, ts
    nb = min(bx, TOKEN_TILE // sx)
    while bx % nb:
        nb -= 1
    return nb, sx


def _tok_spec(nb, ts, width):
    return pl.BlockSpec((nb, ts, width), lambda b, s: (b, s, 0))


def _mod_spec(nb, d):
    return pl.BlockSpec((nb, 1, d), lambda b, s: (b, 0, 0))


def _const_spec(shape):
    nd = len(shape)
    return pl.BlockSpec(shape, lambda b, s: (0,) * nd)


def _table_spec(rows):
    return pl.BlockSpec((1, rows, LANES), lambda b, s: (s, 0, 0))


def _rms(x):
    return x * lax.rsqrt(jnp.mean(x * x, axis=-1, keepdims=True) + NORM_EPS)


def _modulated(x_ref, g_ref, shift_ref, scale_ref):
    y = _rms(x_ref[...]) * g_ref[...]
    h = y * (1.0 + scale_ref[...]) + shift_ref[...]
    return h.reshape(-1, h.shape[-1]).astype(BF16)


def _rope_lanes(x, cos_t, sin_up, sin_dn, half):
    return (x * cos_t + pltpu.roll(x, half, 1) * sin_up
            + pltpu.roll(x, LANES - half, 1) * sin_dn)


def _store_tok(ref, lo, val):
    nb, ts = ref.shape[0], ref.shape[1]
    w = val.shape[-1]
    ref[:, :, lo:lo + w] = val.reshape(nb, ts, w).astype(ref.dtype)


def _proj_qkv_kernel(x_ref, g_ref, sh_ref, sc_ref, w_ref, q_ref, kf_ref, vf_ref, kb_ref, vb_ref,
                     *, width, q_scale):
    h = _modulated(x_ref, g_ref, sh_ref, sc_ref)
    q = jnp.dot(h, w_ref[:, 0:width], preferred_element_type=F32)
    _store_tok(q_ref, 0, q * q_scale)
    k = jnp.dot(h, w_ref[:, width:2 * width], preferred_element_type=F32)
    _store_tok(kf_ref, 0, k)
    _store_tok(kb_ref, 0, k)
    v = jnp.dot(h, w_ref[:, 2 * width:3 * width], preferred_element_type=F32)
    _store_tok(vf_ref, 0, v)
    _store_tok(vb_ref, 0, v)


def _proj_qkv(x, g, shift, scale, w_bf16, n_heads, q_scale):
    bx, sx, d = x.shape
    width = n_heads * HEAD_DIM
    nb, ts = _token_tiling(bx, sx)
    out = lambda dt: jax.ShapeDtypeStruct((bx, sx, width), dt)
    return pl.pallas_call(
        functools.partial(_proj_qkv_kernel, width=width, q_scale=q_scale),
        grid=(bx // nb, sx // ts),
        in_specs=[_tok_spec(nb, ts, d), _const_spec((1, d)), _mod_spec(nb, d), _mod_spec(nb, d),
                  _const_spec(w_bf16.shape)],
        out_specs=[_tok_spec(nb, ts, width)] * 5,
        out_shape=[out(BF16), out(F32), out(F32), out(BF16), out(BF16)],
        compiler_params=_cparams(2),
    )(x, g, shift, scale, w_bf16)


def _proj_c_kernel(x_ref, g_ref, sh_ref, sc_ref, w_ref, b_ref, tc_ref, tu_ref, td_ref,
                   q_ref, kf_ref, vf_ref, kb_ref, vb_ref, *, q_width, kv_width, q_scale):
    h = _modulated(x_ref, g_ref, sh_ref, sc_ref)
    qkv = jnp.dot(h, w_ref[...], preferred_element_type=F32) + b_ref[...]
    cos_t, sin_up, sin_dn = tc_ref[0], tu_ref[0], td_ref[0]
    half = C_ROT // 2
    for j in range(q_width // LANES):
        xg = qkv[:, j * LANES:(j + 1) * LANES]
        _store_tok(q_ref, j * LANES, _rope_lanes(xg, cos_t, sin_up, sin_dn, half) * q_scale)
    for j in range(kv_width // LANES):
        lo = q_width + j * LANES
        kg = _rope_lanes(qkv[:, lo:lo + LANES], cos_t, sin_up, sin_dn, half)
        _store_tok(kf_ref, j * LANES, kg)
        _store_tok(kb_ref, j * LANES, kg)
    v = qkv[:, q_width + kv_width:q_width + 2 * kv_width]
    _store_tok(vf_ref, 0, v)
    _store_tok(vb_ref, 0, v)


def _proj_c(x, g, shift, scale, w_bf16, bias, tables):
    bx, sx, d = x.shape
    q_width, kv_width = C_HEADS * HEAD_DIM, C_KV_HEADS * HEAD_DIM
    nb, ts = _token_tiling(bx, sx)
    out = lambda w, dt: jax.ShapeDtypeStruct((bx, sx, w), dt)
    return pl.pallas_call(
        functools.partial(_proj_c_kernel, q_width=q_width, kv_width=kv_width,
                          q_scale=HEAD_DIM ** -0.5),
        grid=(bx // nb, sx // ts),
        in_specs=[_tok_spec(nb, ts, d), _const_spec((1, d)), _mod_spec(nb, d), _mod_spec(nb, d),
                  _const_spec(w_bf16.shape), _const_spec(bias.shape)] + [_table_spec(nb * ts)] * 3,
        out_specs=[_tok_spec(nb, ts, q_width)] + [_tok_spec(nb, ts, kv_width)] * 4,
        out_shape=[out(q_width, BF16), out(kv_width, F32), out(kv_width, F32),
                   out(kv_width, BF16), out(kv_width, BF16)],
        compiler_params=_cparams(2),
    )(x, g, shift, scale, w_bf16, bias, *tables)


def _proj_a_kernel(x_ref, g_ref, sh_ref, sc_ref, wd_ref, gq_ref, gkv_ref, wq_ref,
                   qc_ref, qu_ref, qd_ref, kc_ref, ku_ref, kd_ref,
                   q_ref, ckv_ref, kr_ref, *, q_scale):
    h = _modulated(x_ref, g_ref, sh_ref, sc_ref)
    down = jnp.dot(h, wd_ref[...], preferred_element_type=F32)
    cq = (_rms(down[:, :A_Q_LORA]) * gq_ref[...]).astype(BF16)
    ckv = _rms(down[:, A_Q_LORA:A_Q_LORA + A_KV_LORA]) * gkv_ref[...]
    _store_tok(ckv_ref, 0, ckv)
    lo = A_Q_LORA + A_KV_LORA
    half = A_ROPE // 2
    kr = _rope_lanes(down[:, lo:lo + LANES], kc_ref[0], ku_ref[0], kd_ref[0], half)
    _store_tok(kr_ref, 0, kr[:, :A_ROPE])
    q = jnp.dot(cq, wq_ref[...], preferred_element_type=F32)
    qc, qu, qd = qc_ref[0], qu_ref[0], qd_ref[0]
    for j in range(A_HEADS):
        qg = _rope_lanes(q[:, j * LANES:(j + 1) * LANES], qc, qu, qd, half)
        _store_tok(q_ref, j * LANES, qg * q_scale)


def _proj_a(x, g, shift, scale, wd_bf16, g_q, g_kv, wq_bf16, q_tables, k_tables):
    bx, sx, d = x.shape
    nb, ts = _token_tiling(bx, sx)
    return pl.pallas_call(
        functools.partial(_proj_a_kernel, q_scale=(A_NOPE + A_ROPE) ** -0.5),
        grid=(bx // nb, sx // ts),
        in_specs=[_tok_spec(nb, ts, d), _const_spec((1, d)), _mod_spec(nb, d), _mod_spec(nb, d),
                  _const_spec(wd_bf16.shape), _const_spec(g_q.shape), _const_spec(g_kv.shape),
                  _const_spec(wq_bf16.shape)] + [_table_spec(nb * ts)] * 6,
        out_specs=[_tok_spec(nb, ts, A_HEADS * LANES), _tok_spec(nb, ts, A_KV_LORA),
                   _tok_spec(nb, ts, A_ROPE)],
        out_shape=[jax.ShapeDtypeStruct((bx, sx, A_HEADS * LANES), BF16),
                   jax.ShapeDtypeStruct((bx, sx, A_KV_LORA), F32),
                   jax.ShapeDtypeStruct((bx, sx, A_ROPE), F32)],
        compiler_params=_cparams(2),
    )(x, g, shift, scale, wd_bf16, g_q, g_kv, wq_bf16, *q_tables, *k_tables)


def _expand_a_kernel(ckv_ref, kr_ref, wk_ref, wr_ref, wv_ref, k_ref, v_ref):
    ckv = ckv_ref[...]
    ckv = ckv.reshape(-1, ckv.shape[-1]).astype(BF16)
    kr = kr_ref[...]
    kr = kr.reshape(-1, kr.shape[-1]).astype(BF16)
    k = (jnp.dot(ckv, wk_ref[...], preferred_element_type=F32)
         + jnp.dot(kr, wr_ref[...], preferred_element_type=F32))
    _store_tok(k_ref, 0, k)
    _store_tok(v_ref, 0, jnp.dot(ckv, wv_ref[...], preferred_element_type=F32))


def _expand_a(ckv, kr, wk, wr, wv):
    bx, sx, _ = ckv.shape
    nb, ts = _token_tiling(bx, sx)
    return pl.pallas_call(
        _expand_a_kernel,
        grid=(bx // nb, sx // ts),
        in_specs=[_tok_spec(nb, ts, A_KV_LORA), _tok_spec(nb, ts, A_ROPE),
                  _const_spec(wk.shape), _const_spec(wr.shape), _const_spec(wv.shape)],
        out_specs=[_tok_spec(nb, ts, A_HEADS * LANES), _tok_spec(nb, ts, A_HEADS * A_V)],
        out_shape=[jax.ShapeDtypeStruct((bx, sx, A_HEADS * LANES), BF16),
                   jax.ShapeDtypeStruct((bx, sx, A_HEADS * A_V), BF16)],
        compiler_params=_cparams(2),
    )(ckv, kr, wk, wr, wv)


def _block_kernel(x_ref, o_ref, wo_ref, gm_ref, g_ref, sh_ref, sc_ref, gf_ref, wi_ref, wout_ref,
                  y_ref, *, hidden, chunk):
    nb, ts, d = x_ref.shape
    o = o_ref[...].reshape(nb * ts, -1)
    mix = jnp.dot(o, wo_ref[...], preferred_element_type=F32).reshape(nb, ts, d)
    x1 = x_ref[...] + gm_ref[...] * mix
    h = (_rms(x1) * g_ref[...]) * (1.0 + sc_ref[...]) + sh_ref[...]
    h = h.reshape(nb * ts, d).astype(BF16)
    acc = jnp.zeros((nb * ts, d), F32)
    for c in range(hidden // chunk):
        gate = jnp.dot(h, wi_ref[:, c * chunk:(c + 1) * chunk], preferred_element_type=F32)
        up = jnp.dot(h, wi_ref[:, hidden + c * chunk:hidden + (c + 1) * chunk],
                     preferred_element_type=F32)
        act = (gate * jax.nn.sigmoid(gate) * up).astype(BF16)
        acc = acc + jnp.dot(act, wout_ref[c * chunk:(c + 1) * chunk, :],
                            preferred_element_type=F32)
    y_ref[...] = x1 + gf_ref[...] * acc.reshape(nb, ts, d)


def _block(x, o, wo, gate_m, g_ffn, shift_f, scale_f, gate_f, w_in, w_out):
    bx, sx, d = x.shape
    hidden = w_out.shape[0]
    nb, ts = _token_tiling(bx, sx)
    resident = lambda shape: pl.BlockSpec(shape, lambda b, s: (0,) * len(shape),
                                          pipeline_mode=pl.Buffered(1))
    return pl.pallas_call(
        functools.partial(_block_kernel, hidden=hidden, chunk=256),
        grid=(bx // nb, sx // ts),
        in_specs=[_tok_spec(nb, ts, d), _tok_spec(nb, ts, o.shape[-1]), resident(wo.shape),
                  _mod_spec(nb, d), _const_spec((1, d)), _mod_spec(nb, d), _mod_spec(nb, d),
                  _mod_spec(nb, d), resident(w_in.shape), resident(w_out.shape)],
        out_specs=_tok_spec(nb, ts, d),
        out_shape=jax.ShapeDtypeStruct((bx, sx, d), F32),
        compiler_params=_cparams(2),
    )(x, o, wo, gate_m, g_ffn, shift_f, scale_f, gate_f, w_in, w_out)


def _final_norm_kernel(x_ref, g_ref, y_ref):
    y_ref[...] = _rms(x_ref[...]) * g_ref[...]


def _final_norm(x, g):
    bx, sx, d = x.shape
    nb, ts = _token_tiling(bx, sx)
    return pl.pallas_call(
        _final_norm_kernel,
        grid=(bx // nb, sx // ts),
        in_specs=[_tok_spec(nb, ts, d), _const_spec((1, d))],
        out_specs=_tok_spec(nb, ts, d),
        out_shape=jax.ShapeDtypeStruct((bx, sx, d), F32),
        compiler_params=_cparams(2),
    )(x, g)


def _attn_kernel(*refs, mode, dq, tq, bk, sk_valid, q_off, k_off, n_prev, use_sink, use_bias):
    refs = list(refs)
    q_ref, k_ref, v_ref = refs[:3]
    rest = refs[3:]
    tri_ref = rest.pop(0) if mode == "stick" else None
    sink_ref = rest.pop(0) if use_sink else None
    bias_ref = rest.pop(0) if use_bias else None
    o_ref, stat_ref, acc_ref = rest

    pair = pl.program_id(1)
    q0 = pl.program_id(2) * tq
    qpos0 = q0 + q_off
    stick = mode == "stick"

    if stick:
        hi_idx = qpos0 + (tq - 1) - k_off
    else:
        hi_idx = ((qpos0 + tq - 1) // CHUNK + 1) * CHUNK - k_off
    hi_idx = jnp.minimum(hi_idx, sk_valid)
    if n_prev is not None:
        band_idx = (qpos0 // CHUNK - n_prev) * CHUNK - k_off
        lo_idx = jnp.maximum(band_idx, 0)
    else:
        band_idx = 0
        lo_idx = 0
    kb_lo = lo_idx // bk
    kb_hi = (hi_idx + bk - 1) // bk

    if stick:
        full_end = qpos0 - k_off
    else:
        full_end = (qpos0 // CHUNK + 1) * CHUNK - k_off
    f_hi = jnp.minimum(full_end, sk_valid) // bk
    if n_prev is not None:
        full_start = jnp.maximum(((qpos0 + tq - 1) // CHUNK - n_prev) * CHUNK - k_off, 0)
        f_lo = (full_start + bk - 1) // bk
    else:
        f_lo = kb_lo
    f_lo = jnp.minimum(jnp.maximum(f_lo, kb_lo), kb_hi)
    f_hi = jnp.minimum(jnp.maximum(f_hi, f_lo), kb_hi)

    def visible(kb):
        iq = lax.broadcasted_iota(jnp.int32, (tq, 1), 0)
        ik = kb * bk + lax.broadcasted_iota(jnp.int32, (1, bk), 1)
        qpos = iq + qpos0
        kpos = ik + k_off
        valid = ik < sk_valid
        if stick:
            return (kpos < qpos) & valid
        shift = int(math.log2(CHUNK))
        qc = jnp.right_shift(qpos, shift)
        kc = jnp.right_shift(kpos, shift)
        m = (kc <= qc) & valid
        if n_prev is not None:
            m = m & (kc >= qc - n_prev)
        return m

    lane_half = lax.broadcasted_iota(jnp.int32, (1, LANES), 1) // HEAD_DIM

    if dq == HEAD_DIM:
        q_pair = q_ref[0]
        q_heads = [jnp.where(lane_half == hh, q_pair, jnp.zeros((), BF16)) for hh in range(2)]
        k_lanes = [slice(0, LANES)] * 2
    else:
        q_heads = [q_ref[0, :, hh * dq:(hh + 1) * dq] for hh in range(2)]
        k_lanes = [slice(hh * dq, (hh + 1) * dq) for hh in range(2)]

    def rows(kb):
        return pl.ds(pl.multiple_of(kb * bk, bk), bk)

    def scores(hh, kb):
        return lax.dot_general(q_heads[hh], k_ref[0, rows(kb), k_lanes[hh]],
                               (((1,), (1,)), ((), ())), preferred_element_type=F32)

    if stick:
        acc_ref[...] = jnp.zeros_like(acc_ref)
        stat_ref[0] = jnp.zeros_like(stat_ref[0])

        def block(kb, masked):
            vis = visible(kb) if masked else None
            v_blk = v_ref[0, rows(kb), :]
            tri = tri_ref[...]
            least = None
            for hh in range(2):
                y = scores(hh, kb)
                sp = jnp.maximum(y, 0.0) + jnp.log2(1.0 + jnp.exp2(-jnp.abs(y)))
                if masked:
                    sp = jnp.where(vis, sp, 0.0)
                run = stat_ref[0, hh]
                run_new = run + jnp.sum(sp, axis=1, keepdims=True)
                stat_ref[0, hh] = run_new
                hi = sp.astype(BF16)
                lo = (sp - hi.astype(F32)).astype(BF16)
                suffix = (jnp.dot(hi, tri, preferred_element_type=F32)
                          + jnp.dot(lo, tri, preferred_element_type=F32))
                a = jnp.exp2(y - suffix - run)
                if masked:
                    a = jnp.where(vis, a, 0.0)
                acc_ref[hh] += jnp.dot(a.astype(BF16), v_blk, preferred_element_type=F32)
                head_least = jnp.min(run_new)
                least = head_least if least is None else jnp.minimum(least, head_least)
            return least

        least = lax.fori_loop(0, kb_hi - f_hi, lambda i, c: block(kb_hi - 1 - i, True),
                              jnp.zeros((), F32))
        n_full = f_hi - kb_lo
        lax.while_loop(lambda st: (st[0] < n_full) & (st[1] < STICK_DEAD_BITS),
                       lambda st: (st[0] + 1, block(f_hi - 1 - st[0], False)),
                       (jnp.zeros((), jnp.int32), least))
        outs = [acc_ref[0], acc_ref[1]]
    else:
        for hh in range(2):
            if use_sink:
                stat_ref[0, hh] = jnp.full((tq, 1), sink_ref[2 * pair + hh], F32)
                stat_ref[1, hh] = jnp.ones((tq, 1), F32)
            else:
                stat_ref[0, hh] = jnp.full((tq, 1), MASK_VALUE, F32)
                stat_ref[1, hh] = jnp.zeros((tq, 1), F32)
        acc_ref[...] = jnp.zeros_like(acc_ref)

        def block(kb, masked):
            vis = visible(kb) if masked else None
            v_blk = v_ref[0, rows(kb), :]
            for hh in range(2):
                s = scores(hh, kb)
                if use_bias:
                    s = s + bias_ref[hh, (kb * bk - band_idx) // bk]
                if masked:
                    s = jnp.where(vis, s, MASK_VALUE)
                m = stat_ref[0, hh]
                m_new = jnp.maximum(m, jnp.max(s, axis=1, keepdims=True))
                alpha = jnp.exp(m - m_new)
                p = jnp.exp(s - m_new)
                stat_ref[0, hh] = m_new
                stat_ref[1, hh] = alpha * stat_ref[1, hh] + jnp.sum(p, axis=1, keepdims=True)
                acc_ref[hh] = alpha * acc_ref[hh] + jnp.dot(p.astype(BF16), v_blk,
                                                            preferred_element_type=F32)

        def sweep(lo, hi, masked):
            def body(kb, c):
                block(kb, masked)
                return c
            lax.fori_loop(lo, hi, body, 0)

        if n_prev is not None:
            sweep(kb_lo, f_lo, True)
        sweep(f_lo, f_hi, False)
        sweep(f_hi, kb_hi, True)
        outs = [acc_ref[hh] / stat_ref[1, hh] for hh in range(2)]
    o_ref[0] = jnp.where(lane_half == 0, outs[0], outs[1]).astype(o_ref.dtype)


def _attention(q, k, v, *, mode, dq, sk_valid, q_off, k_off, n_prev=None, kv_group=1,
               sink=None, bias=None, bk=KV_BLOCK, q_tile=Q_TILE):
    bx, sq, qw = q.shape
    skp = k.shape[1]
    n_pairs = qw // (2 * dq)
    tq = min(q_tile, sq)
    assert sq % tq == 0 and skp % bk == 0 and q_off % CHUNK == 0 and tq % 8 == 0
    assert k_off >= 0 and (tq % CHUNK == 0 or sq == tq)
    in_specs = [
        pl.BlockSpec((1, tq, 2 * dq), lambda b, p, i: (b, i, p)),
        pl.BlockSpec((1, skp, 2 * dq), lambda b, p, i: (b, 0, p // kv_group)),
        pl.BlockSpec((1, skp, LANES), lambda b, p, i: (b, 0, p // kv_group)),
    ]
    args = [q, k, v]
    if mode == "stick":
        idx = np.arange(bk)
        in_specs.append(pl.BlockSpec((bk, bk), lambda b, p, i: (0, 0)))
        args.append(jnp.asarray(idx[:, None] >= idx[None, :], BF16))
    if sink is not None:
        in_specs.append(pl.BlockSpec(memory_space=pltpu.SMEM))
        args.append(sink)
    if bias is not None:
        in_specs.append(pl.BlockSpec((2,) + bias.shape[1:], lambda b, p, i: (p, 0, 0, 0)))
        args.append(bias)
    return pl.pallas_call(
        functools.partial(_attn_kernel, mode=mode, dq=dq, tq=tq, bk=bk, sk_valid=sk_valid,
                          q_off=q_off, k_off=k_off, n_prev=n_prev,
                          use_sink=sink is not None, use_bias=bias is not None),
        grid=(bx, n_pairs, sq // tq),
        in_specs=in_specs,
        out_specs=pl.BlockSpec((1, tq, LANES), lambda b, p, i: (b, i, p)),
        out_shape=jax.ShapeDtypeStruct((bx, sq, n_pairs * LANES), BF16),
        scratch_shapes=[pltpu.VMEM((2, 2, tq, 1), F32), pltpu.VMEM((2, tq, LANES), F32)],
        compiler_params=_cparams(3),
    )(*args)


def _visible_end(qpos0, tq, k_off, sk_valid):
    return jnp.minimum(((qpos0 + tq - 1) // CHUNK + 1) * CHUNK - k_off, sk_valid)


def _causal_kernel(q_ref, k_ref, v_ref, o_ref, *, tq, span, n_spans, sk_valid, q_off, k_off):
    qpos0 = pl.program_id(2) * tq + q_off
    n_needed = (_visible_end(qpos0, tq, k_off, sk_valid) - 1) // span
    shift = int(math.log2(CHUNK))
    lane_half = lax.broadcasted_iota(jnp.int32, (1, LANES), 1) // HEAD_DIM
    nt = (((1,), (1,)), ((), ()))

    for n_full in range(n_spans):
        @pl.when(n_needed == n_full)
        def _(n_full=n_full):
            full, width = n_full * span, (n_full + 1) * span
            qpos = lax.broadcasted_iota(jnp.int32, (tq, 1), 0) + qpos0
            kidx = lax.broadcasted_iota(jnp.int32, (1, span), 1) + full
            vis = (kidx < sk_valid) & (jnp.right_shift(kidx + k_off, shift)
                                       <= jnp.right_shift(qpos, shift))
            outs = []
            for hh in range(2):
                lanes = slice(hh * LANES, (hh + 1) * LANES)
                q_h = q_ref[0, :, lanes]
                s_tail = lax.dot_general(q_h, k_ref[0, full:width, lanes], nt,
                                         preferred_element_type=F32)
                s_tail = jnp.where(vis, s_tail, MASK_VALUE)
                m = jnp.max(s_tail, axis=1, keepdims=True)
                if n_full:
                    s_full = lax.dot_general(q_h, k_ref[0, 0:full, lanes], nt,
                                             preferred_element_type=F32)
                    m = jnp.maximum(m, jnp.max(s_full, axis=1, keepdims=True))
                p_tail = jnp.exp(s_tail - m)
                l = jnp.sum(p_tail, axis=1, keepdims=True)
                o = jnp.dot(p_tail.astype(BF16), v_ref[0, full:width, :],
                            preferred_element_type=F32)
                if n_full:
                    p_full = jnp.exp(s_full - m)
                    l = l + jnp.sum(p_full, axis=1, keepdims=True)
                    o = o + jnp.dot(p_full.astype(BF16), v_ref[0, 0:full, :],
                                    preferred_element_type=F32)
                outs.append(o / l)
            o_ref[0] = jnp.where(lane_half == 0, outs[0], outs[1]).astype(o_ref.dtype)


def _causal_attention(q, k, v, *, sk_valid, q_off, k_off):
    bx, sq, qw = q.shape
    rows = k.shape[1]
    n_pairs = qw // (2 * LANES)
    tq = min(Q_TILE, sq)
    span = A_KV_BLOCK
    assert sq % tq == 0 and rows % span == 0 and q_off % CHUNK == 0 and k_off >= 0
    for q0 in range(0, sq, tq):
        end = min(((q_off + q0 + tq - 1) // CHUNK + 1) * CHUNK - k_off, sk_valid)
        first_row_end = min(((q_off + q0) // CHUNK + 1) * CHUNK - k_off, sk_valid)
        assert 0 < end <= rows and (end - 1) // span * span <= first_row_end
    return pl.pallas_call(
        functools.partial(_causal_kernel, tq=tq, span=span, n_spans=rows // span,
                          sk_valid=sk_valid, q_off=q_off, k_off=k_off),
        grid=(bx, n_pairs, sq // tq),
        in_specs=[
            pl.BlockSpec((1, tq, 2 * LANES), lambda b, p, i: (b, i, p)),
            pl.BlockSpec((1, rows, 2 * LANES), lambda b, p, i: (b, 0, p)),
            pl.BlockSpec((1, rows, LANES), lambda b, p, i: (b, 0, p)),
        ],
        out_specs=pl.BlockSpec((1, tq, LANES), lambda b, p, i: (b, i, p)),
        out_shape=jax.ShapeDtypeStruct((bx, sq, n_pairs * LANES), BF16),
        compiler_params=_cparams(3),
    )(q, k, v)


def _band_kernel(*refs, tq, width, sk_valid, q_off, k_off, front, n_prev, use_sink, use_bias):
    refs = list(refs)
    q_ref, k_ref, v_ref = refs[:3]
    rest = refs[3:]
    sink_ref = rest.pop(0) if use_sink else None
    bias_ref = rest.pop(0) if use_bias else None
    (o_ref,) = rest

    pair = pl.program_id(1)
    qpos0 = pl.program_id(2) * tq + q_off
    start = pl.multiple_of((qpos0 // CHUNK - n_prev) * CHUNK - k_off + front, CHUNK)
    band = pl.ds(start, width)

    shift = int(math.log2(CHUNK))
    qpos = lax.broadcasted_iota(jnp.int32, (tq, 1), 0) + qpos0
    kidx = lax.broadcasted_iota(jnp.int32, (1, width), 1) + (start - front)
    kpos = kidx + k_off
    qc = jnp.right_shift(qpos, shift)
    kc = jnp.right_shift(kpos, shift)
    vis = (kidx >= 0) & (kidx < sk_valid) & (kc <= qc) & (kc >= qc - n_prev)

    lane_half = lax.broadcasted_iota(jnp.int32, (1, LANES), 1) // HEAD_DIM
    q_pair = q_ref[0]
    k_band = k_ref[0, band, :]
    v_band = v_ref[0, band, :]
    outs = []
    for hh in range(2):
        q_h = jnp.where(lane_half == hh, q_pair, jnp.zeros((), BF16))
        s = lax.dot_general(q_h, k_band, (((1,), (1,)), ((), ())), preferred_element_type=F32)
        if use_bias:
            s = s + bias_ref[hh]
        s = jnp.where(vis, s, MASK_VALUE)
        m = jnp.max(s, axis=1, keepdims=True)
        if use_sink:
            sink = sink_ref[2 * pair + hh]
            m = jnp.maximum(m, sink)
        p = jnp.exp(s - m)
        l = jnp.sum(p, axis=1, keepdims=True)
        if use_sink:
            l = l + jnp.exp(sink - m)
        outs.append(jnp.dot(p.astype(BF16), v_band, preferred_element_type=F32) / l)
    o_ref[0] = jnp.where(lane_half == 0, outs[0], outs[1]).astype(o_ref.dtype)


def _band_attention(q, k, v, *, width, sk_valid, q_off, k_off, front, n_prev, kv_group=1,
                    sink=None, bias=None):
    bx, sq, qw = q.shape
    rows = k.shape[1]
    n_pairs = qw // LANES
    tq = min(Q_TILE, sq)
    assert sq % tq == 0 and q_off % CHUNK == 0 and (tq % CHUNK == 0 or sq == tq)
    assert ((q_off + sq - tq) // CHUNK - n_prev) * CHUNK - k_off + front + width <= rows
    assert (q_off // CHUNK - n_prev) * CHUNK - k_off + front >= 0
    in_specs = [
        pl.BlockSpec((1, tq, LANES), lambda b, p, i: (b, i, p)),
        pl.BlockSpec((1, rows, LANES), lambda b, p, i: (b, 0, p // kv_group)),
        pl.BlockSpec((1, rows, LANES), lambda b, p, i: (b, 0, p // kv_group)),
    ]
    args = [q, k, v]
    if sink is not None:
        in_specs.append(pl.BlockSpec(memory_space=pltpu.SMEM))
        args.append(sink)
    if bias is not None:
        in_specs.append(pl.BlockSpec((2, tq, width), lambda b, p, i: (p, 0, 0)))
        args.append(bias)
    return pl.pallas_call(
        functools.partial(_band_kernel, tq=tq, width=width, sk_valid=sk_valid, q_off=q_off,
                          k_off=k_off, front=front, n_prev=n_prev,
                          use_sink=sink is not None, use_bias=bias is not None),
        grid=(bx, n_pairs, sq // tq),
        in_specs=in_specs,
        out_specs=pl.BlockSpec((1, tq, LANES), lambda b, p, i: (b, i, p)),
        out_shape=jax.ShapeDtypeStruct((bx, sq, n_pairs * LANES), BF16),
        compiler_params=_cparams(3),
    )(*args)


def _band_bias_kernel(e_ref, o_ref, *, width):
    w = e_ref.shape[-1]
    x = jnp.broadcast_to(e_ref[0], (Q_TILE, w))
    toeplitz = pltpu.roll(x, 0, 1, stride=1, stride_axis=0)
    o_ref[0] = toeplitz[:, :width]


def _band_bias(rel_bias):
    heads = rel_bias.shape[0]
    clip = (rel_bias.shape[1] - 1) // 2
    band = D_PREV_CHUNKS * CHUNK
    width = D_BAND_WIDTH
    w = width + Q_TILE
    assert band >= clip and width > band + clip
    top = jnp.broadcast_to(rel_bias[:, -1:], (heads, band - clip + 1))
    mid = jnp.flip(rel_bias[:, :2 * clip], axis=1)
    low = jnp.broadcast_to(rel_bias[:, :1], (heads, width - (band + clip + 1)))
    neg = jnp.broadcast_to(rel_bias[:, -1:], (heads, w - width))
    e = jnp.concatenate([top, mid, low, neg], axis=1)[:, None, :]
    return pl.pallas_call(
        functools.partial(_band_bias_kernel, width=width),
        grid=(heads,),
        in_specs=[pl.BlockSpec((1, 1, w), lambda h: (h, 0, 0))],
        out_specs=pl.BlockSpec((1, Q_TILE, width), lambda h: (h, 0, 0)),
        out_shape=jax.ShapeDtypeStruct((heads, Q_TILE, width), F32),
        compiler_params=_cparams(1),
    )(e)


def _rope_tables(pos, n_rot, lane_offsets, rows_repeat=1):
    half = n_rot // 2
    inv = ROPE_THETA ** (-jnp.arange(half, dtype=F32) * 2.0 / n_rot)
    ang = pos.astype(F32)[:, None] * inv[None, :]
    cos, sin = jnp.cos(ang), jnp.sin(ang)
    n = pos.shape[0]
    cos_t = jnp.ones((n, LANES), F32)
    sin_up = jnp.zeros((n, LANES), F32)
    sin_dn = jnp.zeros((n, LANES), F32)
    for o in lane_offsets:
        cos_t = cos_t.at[:, o:o + half].set(cos).at[:, o + half:o + n_rot].set(cos)
        sin_dn = sin_dn.at[:, o:o + half].set(-sin)
        sin_up = sin_up.at[:, o + half:o + n_rot].set(sin)
    def tiles(t):
        if rows_repeat > 1:
            return jnp.tile(t, (rows_repeat, 1))[None]
        rows = min(n, TOKEN_TILE)
        return t.reshape(n // rows, rows, LANES)
    return tiles(cos_t), tiles(sin_up), tiles(sin_dn)


def _tables_for(bx, sx, pos, n_rot, lane_offsets):
    nb, ts = _token_tiling(bx, sx)
    return _rope_tables(pos, n_rot, lane_offsets, rows_repeat=nb if nb > 1 else 1)


def _c_head_order():
    rep = C_HEADS // C_KV_HEADS
    order = []
    for p in range(C_HEADS // 2):
        g2, i = divmod(p, rep)
        order += [rep * (2 * g2) + i, rep * (2 * g2 + 1) + i]
    return np.asarray(order)


def _pad_rows(a, rows):
    return jnp.pad(a, ((0, 0), (0, rows - a.shape[1]), (0, 0)))


def _front_pad(a, rows):
    return jnp.pad(a, ((0, 0), (rows, 0), (0, 0)))


def _with_cache(cache, new, dtype, block=KV_BLOCK):
    full = jnp.concatenate([cache.reshape(cache.shape[0], cache.shape[1], -1).astype(dtype),
                            new.astype(dtype)], axis=1)
    rows = -(-full.shape[1] // block) * block
    return _pad_rows(full, rows), full.shape[1]


def kernel(x_prompt, x_sample, c_prompt, c_sample, cache_a_ckv, cache_a_krope, cache_b_k, cache_b_v,
           cache_c_k, cache_c_v, cache_d_k, cache_d_v, w_mod, b_mod, g_mix, g_ffn, w_ffn_in, w_ffn_out,
           w_a_down, g_a_q, g_a_kv, w_a_uq, w_a_uk, w_a_uv, w_a_o, w_b_qkv, w_b_o,
           w_c_qkv, b_c_qkv, sink_c, w_c_o, w_d_qkv, rel_bias_d, w_d_o, g_final):
    bp, sp, d = x_prompt.shape
    bs, t, _ = x_sample.shape
    depth = w_mod.shape[0]
    past = cache_a_ckv.shape[2]
    pos_p = jnp.arange(sp)
    pos_s = past + jnp.arange(t)

    mods = _adaln(jnp.concatenate([c_prompt, c_sample], axis=0), w_mod, b_mod)

    def mod(i, k):
        m = mods[i, k][:, None, :]
        return m[:bp], m[bp:]

    xp, xs = x_prompt, x_sample
    states = [[] for _ in range(N_MIXERS)]
    for i in range(depth):
        m, j = i % N_MIXERS, i // N_MIXERS
        (sh_p, sh_s), (sc_p, sc_s), (gm_p, gm_s) = mod(i, 0), mod(i, 1), mod(i, 2)
        (shf_p, shf_s), (scf_p, scf_s), (gf_p, gf_s) = mod(i, 3), mod(i, 4), mod(i, 5)
        g_m = g_mix[i][None, :]
        if m == 0:
            n_down = A_Q_LORA + A_KV_LORA + A_ROPE
            wd = jnp.pad(w_a_down[j], ((0, 0), (0, A_Q_LORA + A_KV_LORA + LANES - n_down))).astype(BF16)
            wq = w_a_uq[j].reshape(A_Q_LORA, A_HEADS, A_NOPE + A_ROPE)
            wq = jnp.pad(wq, ((0, 0), (0, 0), (0, LANES - A_NOPE - A_ROPE)))
            wq = wq.reshape(A_Q_LORA, A_HEADS * LANES).astype(BF16)
            wk = jnp.pad(w_a_uk[j], ((0, 0), (0, 0), (0, LANES - A_NOPE)))
            wk = wk.reshape(A_KV_LORA, A_HEADS * LANES).astype(BF16)
            place = jnp.zeros((A_ROPE, A_HEADS, LANES), F32)
            place = place.at[jnp.arange(A_ROPE), :, A_NOPE + jnp.arange(A_ROPE)].set(1.0)
            wr = place.reshape(A_ROPE, A_HEADS * LANES).astype(BF16)
            wv = w_a_uv[j].reshape(A_KV_LORA, A_HEADS * A_V).astype(BF16)
            g_q, g_kv = g_a_q[j][None, :], g_a_kv[j][None, :]

            def project(x, sh, sc, pos):
                qt = _tables_for(x.shape[0], x.shape[1], pos, A_ROPE, [A_NOPE])
                kt = _tables_for(x.shape[0], x.shape[1], pos, A_ROPE, [0])
                return _proj_a(x, g_m, sh, sc, wd, g_q, g_kv, wq, qt, kt)

            q_p, ckv_p, kr_p = project(xp, sh_p, sc_p, pos_p)
            q_s, ckv_s, kr_s = project(xs, sh_s, sc_s, pos_s)
            k_p, v_p = _expand_a(ckv_p, kr_p, wk, wr, wv)
            ckv_all, n_all = _with_cache(cache_a_ckv[j], ckv_s, F32, A_KV_BLOCK)
            kr_all, _ = _with_cache(cache_a_krope[j], kr_s, F32, A_KV_BLOCK)
            k_s, v_s = _expand_a(ckv_all, kr_all, wk, wr, wv)
            o_p = _causal_attention(q_p, k_p, v_p, sk_valid=sp, q_off=0, k_off=0)
            o_s = _causal_attention(q_s, k_s, v_s, sk_valid=n_all, q_off=past, k_off=0)
            wo = w_a_o[j].astype(BF16)
            states[0].append((ckv_p, kr_p, ckv_s, kr_s))
        elif m == 1 or m == 3:
            heads = B_HEADS if m == 1 else D_HEADS
            w_qkv = (w_b_qkv if m == 1 else w_d_qkv)[j].astype(BF16)
            cache_k, cache_v = (cache_b_k, cache_b_v) if m == 1 else (cache_d_k, cache_d_v)
            q_scale = HEAD_DIM ** -0.5 * (math.log2(math.e) if m == 1 else 1.0)
            q_p, kf_p, vf_p, kb_p, vb_p = _proj_qkv(xp, g_m, sh_p, sc_p, w_qkv, heads, q_scale)
            q_s, kf_s, vf_s, kb_s, vb_s = _proj_qkv(xs, g_m, sh_s, sc_s, w_qkv, heads, q_scale)
            lc = cache_k.shape[2]
            block = KV_BLOCK if m == 1 else D_BAND_WIDTH
            k_s, n_all = _with_cache(cache_k[j], kb_s, BF16, block)
            v_s, _ = _with_cache(cache_v[j], vb_s, BF16, block)
            if m == 1:
                o_p = _attention(q_p, kb_p, vb_p, mode="stick", dq=HEAD_DIM, sk_valid=sp,
                                 q_off=0, k_off=0)
                o_s = _attention(q_s, k_s, v_s, mode="stick", dq=HEAD_DIM, sk_valid=n_all,
                                 q_off=past, k_off=past - lc)
                wo = w_b_o[j].astype(BF16)
                states[1].append((kf_p, vf_p, kf_s, vf_s))
            else:
                bias = _band_bias(rel_bias_d[j])
                front = D_PREV_CHUNKS * CHUNK
                o_p = _band_attention(q_p, _front_pad(kb_p, front), _front_pad(vb_p, front),
                                      width=D_BAND_WIDTH, sk_valid=sp, q_off=0, k_off=0,
                                      front=front, n_prev=D_PREV_CHUNKS, bias=bias)
                o_s = _band_attention(q_s, k_s, v_s, width=D_BAND_WIDTH, sk_valid=n_all,
                                      q_off=past, k_off=past - lc, front=0,
                                      n_prev=D_PREV_CHUNKS, bias=bias[:, :t, :])
                wo = w_d_o[j].astype(BF16)
                keep = min(D_PREV_CHUNKS * CHUNK, sp)
                k_roll = jnp.concatenate([cache_k[j].reshape(bs, lc, -1), kf_s], axis=1)[:, t:]
                v_roll = jnp.concatenate([cache_v[j].reshape(bs, lc, -1), vf_s], axis=1)[:, t:]
                states[3].append((kf_p[:, sp - keep:], vf_p[:, sp - keep:], k_roll, v_roll))
        else:
            order = _c_head_order()
            q_width, kv_width = C_HEADS * HEAD_DIM, C_KV_HEADS * HEAD_DIM
            col = np.concatenate([(order[:, None] * HEAD_DIM + np.arange(HEAD_DIM)).reshape(-1),
                                  np.arange(q_width, q_width + 2 * kv_width)])
            w_qkv = w_c_qkv[j][:, col].astype(BF16)
            b_qkv = b_c_qkv[j][col][None, :]
            sink = sink_c[j][order]
            wo = w_c_o[j].reshape(C_HEADS, HEAD_DIM, d)[order].reshape(q_width, d).astype(BF16)
            lanes = [0, HEAD_DIM]

            def project(x, sh, sc, pos):
                tb = _tables_for(x.shape[0], x.shape[1], pos, C_ROT, lanes)
                return _proj_c(x, g_m, sh, sc, w_qkv, b_qkv, tb)

            q_p, kf_p, vf_p, kb_p, vb_p = project(xp, sh_p, sc_p, pos_p)
            q_s, kf_s, vf_s, kb_s, vb_s = project(xs, sh_s, sc_s, pos_s)
            lc = cache_c_k.shape[2]
            k_s, n_all = _with_cache(cache_c_k[j], kb_s, BF16, C_BAND_WIDTH)
            v_s, _ = _with_cache(cache_c_v[j], vb_s, BF16, C_BAND_WIDTH)
            group = C_HEADS // C_KV_HEADS
            front = C_PREV_CHUNKS * CHUNK
            o_p = _band_attention(q_p, _front_pad(kb_p, front), _front_pad(vb_p, front),
                                  width=C_BAND_WIDTH, sk_valid=sp, q_off=0, k_off=0, front=front,
                                  n_prev=C_PREV_CHUNKS, kv_group=group, sink=sink)
            o_s = _band_attention(q_s, k_s, v_s, width=C_BAND_WIDTH, sk_valid=n_all, q_off=past,
                                  k_off=past - lc, front=0, n_prev=C_PREV_CHUNKS, kv_group=group,
                                  sink=sink)
            keep = min(C_PREV_CHUNKS * CHUNK, sp)
            k_roll = jnp.concatenate([cache_c_k[j].reshape(bs, lc, -1), kf_s], axis=1)[:, t:]
            v_roll = jnp.concatenate([cache_c_v[j].reshape(bs, lc, -1), vf_s], axis=1)[:, t:]
            states[2].append((kf_p[:, sp - keep:], vf_p[:, sp - keep:], k_roll, v_roll))

        g_f = g_ffn[i][None, :]
        w_in, w_out = w_ffn_in[i].astype(BF16), w_ffn_out[i].astype(BF16)
        xp = _block(xp, o_p, wo, gm_p, g_f, shf_p, scf_p, gf_p, w_in, w_out)
        xs = _block(xs, o_s, wo, gm_s, g_f, shf_s, scf_s, gf_s, w_in, w_out)

    y_p = _final_norm(xp, g_final[None, :])
    y_s = _final_norm(xs, g_final[None, :])

    def stacked(entries, n_heads=None):
        outs = []
        for parts in zip(*entries):
            a = jnp.stack(parts, axis=0)
            if n_heads is not None:
                a = a.reshape(a.shape[:3] + (n_heads, HEAD_DIM))
            outs.append(a)
        return tuple(outs)

    return ((y_p, y_s) + stacked(states[0]) + stacked(states[1], B_HEADS)
            + stacked(states[2], C_KV_HEADS) + stacked(states[3], D_HEADS))
```

```python
import functools
import math

import numpy as np
import jax
import jax.numpy as jnp
from jax import lax
from jax.experimental import pallas as pl
from jax.experimental.pallas import tpu as pltpu

F32 = jnp.float32
BF16 = jnp.bfloat16

CHUNK = 64
HEAD_DIM = 64
ROPE_THETA = 500000.0
NORM_EPS = 1e-6
N_MIXERS = 4
A_HEADS, A_Q_LORA, A_KV_LORA, A_NOPE, A_ROPE, A_V = 16, 384, 256, 64, 32, 64
B_HEADS = 16
C_HEADS, C_KV_HEADS, C_WINDOW, C_ROT = 16, 4, 128, 16
D_HEADS, D_PREV_CHUNKS, D_REL_CLIP = 16, 8, 128
C_PREV_CHUNKS = C_WINDOW // CHUNK

LANES = 128
V7X_VMEM_LIMIT = 56 * 1024 * 1024

TOKEN_TILE = 512
Q_TILE = 256
KV_BLOCK = 256
A_KV_BLOCK = 1024
C_BAND_WIDTH = C_PREV_CHUNKS * CHUNK + Q_TILE
D_BAND_WIDTH = D_PREV_CHUNKS * CHUNK + Q_TILE
MASK_VALUE = -1e30
LOG2E = math.log2(math.e)
STICK_DEAD_BITS = 150.0


def _cparams(n_axes):
    return pltpu.CompilerParams(
        dimension_semantics=("parallel",) * n_axes, vmem_limit_bytes=V7X_VMEM_LIMIT)


def _adaln_kernel(c_ref, w_ref, b_ref, o_ref):
    c = c_ref[...]
    a = (c * jax.nn.sigmoid(c)).astype(BF16)
    y = jnp.dot(a, w_ref[0].astype(BF16), preferred_element_type=F32) + b_ref[0]
    o_ref[0, 0] = y


def _adaln(c_all, w_mod, b_mod):
    depth, d, d6 = w_mod.shape
    n = c_all.shape[0]
    return pl.pallas_call(
        _adaln_kernel,
        grid=(depth, d6 // d),
        in_specs=[
            pl.BlockSpec((n, d), lambda i, k: (0, 0)),
            pl.BlockSpec((1, d, d), lambda i, k: (i, 0, k)),
            pl.BlockSpec((1, 1, d), lambda i, k: (i, 0, k)),
        ],
        out_specs=pl.BlockSpec((1, 1, n, d), lambda i, k: (i, k, 0, 0)),
        out_shape=jax.ShapeDtypeStruct((depth, d6 // d, n, d), F32),
        compiler_params=_cparams(2),
    )(c_all, w_mod, b_mod.reshape(depth, 1, d6))


def _token_tiling(bx, sx):
    if sx >= TOKEN_TILE:
        ts = TOKEN_TILE
        while sx % ts:
            ts //= 2
        return 1, ts
    nb = min(bx, TOKEN_TILE // sx)
    while bx % nb:
        nb -= 1
    return nb, sx


def _tok_spec(nb, ts, width):
    return pl.BlockSpec((nb, ts, width), lambda b, s: (b, s, 0))


def _mod_spec(nb, d):
    return pl.BlockSpec((nb, 1, d), lambda b, s: (b, 0, 0))


def _const_spec(shape):
    nd = len(shape)
    return pl.BlockSpec(shape, lambda b, s: (0,) * nd)


def _table_spec(rows):
    return pl.BlockSpec((1, rows, LANES), lambda b, s: (s, 0, 0))


def _rms(x):
    return x * lax.rsqrt(jnp.mean(x * x, axis=-1, keepdims=True) + NORM_EPS)


def _modulated(x_ref, g_ref, shift_ref, scale_ref):
    y = _rms(x_ref[...]) * g_ref[...]
    h = y * (1.0 + scale_ref[...]) + shift_ref[...]
    return h.reshape(-1, h.shape[-1]).astype(BF16)


def _rope_lanes(x, cos_t, sin_up, sin_dn, half):
    return (x * cos_t + pltpu.roll(x, half, 1) * sin_up
            + pltpu.roll(x, LANES - half, 1) * sin_dn)


def _store_tok(ref, lo, val):
    nb, ts = ref.shape[0], ref.shape[1]
    w = val.shape[-1]
    ref[:, :, lo:lo + w] = val.reshape(nb, ts, w).astype(ref.dtype)


def _proj_qkv_kernel(x_ref, g_ref, sh_ref, sc_ref, w_ref, q_ref, kf_ref, vf_ref, kb_ref, vb_ref,
                     *, width, q_scale):
    h = _modulated(x_ref, g_ref, sh_ref, sc_ref)
    q = jnp.dot(h, w_ref[:, 0:width], preferred_element_type=F32)
    _store_tok(q_ref, 0, q * q_scale)
    k = jnp.dot(h, w_ref[:, width:2 * width], preferred_element_type=F32)
    _store_tok(kf_ref, 0, k)
    _store_tok(kb_ref, 0, k)
    v = jnp.dot(h, w_ref[:, 2 * width:3 * width], preferred_element_type=F32)
    _store_tok(vf_ref, 0, v)
    _store_tok(vb_ref, 0, v)


def _proj_qkv(x, g, shift, scale, w_bf16, n_heads, q_scale):
    bx, sx, d = x.shape
    width = n_heads * HEAD_DIM
    nb, ts = _token_tiling(bx, sx)
    out = lambda dt: jax.ShapeDtypeStruct((bx, sx, width), dt)
    return pl.pallas_call(
        functools.partial(_proj_qkv_kernel, width=width, q_scale=q_scale),
        grid=(bx // nb, sx // ts),
        in_specs=[_tok_spec(nb, ts, d), _const_spec((1, d)), _mod_spec(nb, d), _mod_spec(nb, d),
                  _const_spec(w_bf16.shape)],
        out_specs=[_tok_spec(nb, ts, width)] * 5,
        out_shape=[out(BF16), out(F32), out(F32), out(BF16), out(BF16)],
        compiler_params=_cparams(2),
    )(x, g, shift, scale, w_bf16)


def _proj_c_kernel(x_ref, g_ref, sh_ref, sc_ref, w_ref, b_ref, tc_ref, tu_ref, td_ref,
                   q_ref, kf_ref, vf_ref, kb_ref, vb_ref, *, q_width, kv_width, q_scale):
    h = _modulated(x_ref, g_ref, sh_ref, sc_ref)
    qkv = jnp.dot(h, w_ref[...], preferred_element_type=F32) + b_ref[...]
    cos_t, sin_up, sin_dn = tc_ref[0], tu_ref[0], td_ref[0]
    half = C_ROT // 2
    for j in range(q_width // LANES):
        xg = qkv[:, j * LANES:(j + 1) * LANES]
        _store_tok(q_ref, j * LANES, _rope_lanes(xg, cos_t, sin_up, sin_dn, half) * q_scale)
    for j in range(kv_width // LANES):
        lo = q_width + j * LANES
        kg = _rope_lanes(qkv[:, lo:lo + LANES], cos_t, sin_up, sin_dn, half)
        _store_tok(kf_ref, j * LANES, kg)
        _store_tok(kb_ref, j * LANES, kg)
    v = qkv[:, q_width + kv_width:q_width + 2 * kv_width]
    _store_tok(vf_ref, 0, v)
    _store_tok(vb_ref, 0, v)


def _proj_c(x, g, shift, scale, w_bf16, bias, tables):
    bx, sx, d = x.shape
    q_width, kv_width = C_HEADS * HEAD_DIM, C_KV_HEADS * HEAD_DIM
    nb, ts = _token_tiling(bx, sx)
    out = lambda w, dt: jax.ShapeDtypeStruct((bx, sx, w), dt)
    return pl.pallas_call(
        functools.partial(_proj_c_kernel, q_width=q_width, kv_width=kv_width,
                          q_scale=HEAD_DIM ** -0.5 * LOG2E),
        grid=(bx // nb, sx // ts),
        in_specs=[_tok_spec(nb, ts, d), _const_spec((1, d)), _mod_spec(nb, d), _mod_spec(nb, d),
                  _const_spec(w_bf16.shape), _const_spec(bias.shape)] + [_table_spec(nb * ts)] * 3,
        out_specs=[_tok_spec(nb, ts, q_width)] + [_tok_spec(nb, ts, kv_width)] * 4,
        out_shape=[out(q_width, BF16), out(kv_width, F32), out(kv_width, F32),
                   out(kv_width, BF16), out(kv_width, BF16)],
        compiler_params=_cparams(2),
    )(x, g, shift, scale, w_bf16, bias, *tables)


def _proj_a_kernel(x_ref, g_ref, sh_ref, sc_ref, wd_ref, gq_ref, gkv_ref, wq_ref,
                   qc_ref, qu_ref, qd_ref, kc_ref, ku_ref, kd_ref,
                   q_ref, ckv_ref, kr_ref, *, q_scale):
    h = _modulated(x_ref, g_ref, sh_ref, sc_ref)
    down = jnp.dot(h, wd_ref[...], preferred_element_type=F32)
    cq = (_rms(down[:, :A_Q_LORA]) * gq_ref[...]).astype(BF16)
    ckv = _rms(down[:, A_Q_LORA:A_Q_LORA + A_KV_LORA]) * gkv_ref[...]
    _store_tok(ckv_ref, 0, ckv)
    lo = A_Q_LORA + A_KV_LORA
    half = A_ROPE // 2
    kr = _rope_lanes(down[:, lo:lo + LANES], kc_ref[0], ku_ref[0], kd_ref[0], half)
    _store_tok(kr_ref, 0, kr[:, :A_ROPE])
    q = jnp.dot(cq, wq_ref[...], preferred_element_type=F32)
    qc, qu, qd = qc_ref[0], qu_ref[0], qd_ref[0]
    for j in range(A_HEADS):
        qg = _rope_lanes(q[:, j * LANES:(j + 1) * LANES], qc, qu, qd, half)
        _store_tok(q_ref, j * LANES, qg * q_scale)


def _proj_a(x, g, shift, scale, wd_bf16, g_q, g_kv, wq_bf16, q_tables, k_tables):
    bx, sx, d = x.shape
    nb, ts = _token_tiling(bx, sx)
    return pl.pallas_call(
        functools.partial(_proj_a_kernel, q_scale=(A_NOPE + A_ROPE) ** -0.5 * LOG2E),
        grid=(bx // nb, sx // ts),
        in_specs=[_tok_spec(nb, ts, d), _const_spec((1, d)), _mod_spec(nb, d), _mod_spec(nb, d),
                  _const_spec(wd_bf16.shape), _const_spec(g_q.shape), _const_spec(g_kv.shape),
                  _const_spec(wq_bf16.shape)] + [_table_spec(nb * ts)] * 6,
        out_specs=[_tok_spec(nb, ts, A_HEADS * LANES), _tok_spec(nb, ts, A_KV_LORA),
                   _tok_spec(nb, ts, A_ROPE)],
        out_shape=[jax.ShapeDtypeStruct((bx, sx, A_HEADS * LANES), BF16),
                   jax.ShapeDtypeStruct((bx, sx, A_KV_LORA), F32),
                   jax.ShapeDtypeStruct((bx, sx, A_ROPE), F32)],
        compiler_params=_cparams(2),
    )(x, g, shift, scale, wd_bf16, g_q, g_kv, wq_bf16, *q_tables, *k_tables)


def _expand_a_kernel(ckv_ref, kr_ref, wk_ref, wr_ref, wv_ref, k_ref, v_ref):
    ckv = ckv_ref[...]
    ckv = ckv.reshape(-1, ckv.shape[-1]).astype(BF16)
    kr = kr_ref[...]
    kr = kr.reshape(-1, kr.shape[-1]).astype(BF16)
    k = (jnp.dot(ckv, wk_ref[...], preferred_element_type=F32)
         + jnp.dot(kr, wr_ref[...], preferred_element_type=F32))
    _store_tok(k_ref, 0, k)
    _store_tok(v_ref, 0, jnp.dot(ckv, wv_ref[...], preferred_element_type=F32))


def _expand_a(ckv, kr, wk, wr, wv):
    bx, sx, _ = ckv.shape
    nb, ts = _token_tiling(bx, sx)
    return pl.pallas_call(
        _expand_a_kernel,
        grid=(bx // nb, sx // ts),
        in_specs=[_tok_spec(nb, ts, A_KV_LORA), _tok_spec(nb, ts, A_ROPE),
                  _const_spec(wk.shape), _const_spec(wr.shape), _const_spec(wv.shape)],
        out_specs=[_tok_spec(nb, ts, A_HEADS * LANES), _tok_spec(nb, ts, A_HEADS * A_V)],
        out_shape=[jax.ShapeDtypeStruct((bx, sx, A_HEADS * LANES), BF16),
                   jax.ShapeDtypeStruct((bx, sx, A_HEADS * A_V), BF16)],
        compiler_params=_cparams(2),
    )(ckv, kr, wk, wr, wv)


def _block_kernel(x_ref, o_ref, wo_ref, gm_ref, g_ref, sh_ref, sc_ref, gf_ref, wi_ref, wout_ref,
                  y_ref, *, hidden, chunk):
    nb, ts, d = x_ref.shape
    o = o_ref[...].reshape(nb * ts, -1)
    mix = jnp.dot(o, wo_ref[...], preferred_element_type=F32).reshape(nb, ts, d)
    x1 = x_ref[...] + gm_ref[...] * mix
    h = (_rms(x1) * g_ref[...]) * (1.0 + sc_ref[...]) + sh_ref[...]
    h = h.reshape(nb * ts, d).astype(BF16)
    acc = jnp.zeros((nb * ts, d), F32)
    for c in range(hidden // chunk):
        gate = jnp.dot(h, wi_ref[:, c * chunk:(c + 1) * chunk], preferred_element_type=F32)
        up = jnp.dot(h, wi_ref[:, hidden + c * chunk:hidden + (c + 1) * chunk],
                     preferred_element_type=F32)
        act = (gate * jax.nn.sigmoid(gate) * up).astype(BF16)
        acc = acc + jnp.dot(act, wout_ref[c * chunk:(c + 1) * chunk, :],
                            preferred_element_type=F32)
    y_ref[...] = x1 + gf_ref[...] * acc.reshape(nb, ts, d)


def _block(x, o, wo, gate_m, g_ffn, shift_f, scale_f, gate_f, w_in, w_out):
    bx, sx, d = x.shape
    hidden = w_out.shape[0]
    nb, ts = _token_tiling(bx, sx)
    resident = lambda shape: pl.BlockSpec(shape, lambda b, s: (0,) * len(shape),
                                          pipeline_mode=pl.Buffered(1))
    return pl.pallas_call(
        functools.partial(_block_kernel, hidden=hidden, chunk=256),
        grid=(bx // nb, sx // ts),
        in_specs=[_tok_spec(nb, ts, d), _tok_spec(nb, ts, o.shape[-1]), resident(wo.shape),
                  _mod_spec(nb, d), _const_spec((1, d)), _mod_spec(nb, d), _mod_spec(nb, d),
                  _mod_spec(nb, d), resident(w_in.shape), resident(w_out.shape)],
        out_specs=_tok_spec(nb, ts, d),
        out_shape=jax.ShapeDtypeStruct((bx, sx, d), F32),
        compiler_params=_cparams(2),
    )(x, o, wo, gate_m, g_ffn, shift_f, scale_f, gate_f, w_in, w_out)


def _final_norm_kernel(x_ref, g_ref, y_ref):
    y_ref[...] = _rms(x_ref[...]) * g_ref[...]


def _final_norm(x, g):
    bx, sx, d = x.shape
    nb, ts = _token_tiling(bx, sx)
    return pl.pallas_call(
        _final_norm_kernel,
        grid=(bx // nb, sx // ts),
        in_specs=[_tok_spec(nb, ts, d), _const_spec((1, d))],
        out_specs=_tok_spec(nb, ts, d),
        out_shape=jax.ShapeDtypeStruct((bx, sx, d), F32),
        compiler_params=_cparams(2),
    )(x, g)


def _stick_kernel(q_ref, k_ref, v_ref, tri_ref, o_ref, run_ref, acc_ref,
                  *, tq, bk, sk_valid, q_off, k_off):
    qpos0 = pl.program_id(2) * tq + q_off
    kb_hi = (jnp.minimum(qpos0 + (tq - 1) - k_off, sk_valid) + bk - 1) // bk
    f_hi = jnp.minimum(jnp.minimum(qpos0 - k_off, sk_valid) // bk, kb_hi)

    def visible(kb):
        qpos = lax.broadcasted_iota(jnp.int32, (tq, 1), 0) + qpos0
        ik = kb * bk + lax.broadcasted_iota(jnp.int32, (1, bk), 1)
        return (ik + k_off < qpos) & (ik < sk_valid)

    lane_half = lax.broadcasted_iota(jnp.int32, (1, LANES), 1) // HEAD_DIM
    q_pair = q_ref[0]
    q_heads = [jnp.where(lane_half == hh, q_pair, jnp.zeros((), BF16)) for hh in range(2)]

    def rows(kb):
        return pl.ds(pl.multiple_of(kb * bk, bk), bk)

    acc_ref[...] = jnp.zeros_like(acc_ref)
    run_ref[...] = jnp.zeros_like(run_ref)

    def block(kb, masked):
        vis = visible(kb) if masked else None
        k_blk = k_ref[0, rows(kb), :]
        v_blk = v_ref[0, rows(kb), :]
        tri = tri_ref[...]
        least = None
        for hh in range(2):
            y = lax.dot_general(q_heads[hh], k_blk, (((1,), (1,)), ((), ())),
                                preferred_element_type=F32)
            sp = jnp.maximum(y, 0.0) + jnp.log2(1.0 + jnp.exp2(-jnp.abs(y)))
            if masked:
                sp = jnp.where(vis, sp, 0.0)
            run = run_ref[hh]
            run_new = run + jnp.sum(sp, axis=1, keepdims=True)
            run_ref[hh] = run_new
            hi = sp.astype(BF16)
            lo = (sp - hi.astype(F32)).astype(BF16)
            suffix = (jnp.dot(hi, tri, preferred_element_type=F32)
                      + jnp.dot(lo, tri, preferred_element_type=F32))
            a = jnp.exp2(y - suffix - run)
            if masked:
                a = jnp.where(vis, a, 0.0)
            acc_ref[hh] += jnp.dot(a.astype(BF16), v_blk, preferred_element_type=F32)
            head_least = jnp.min(run_new)
            least = head_least if least is None else jnp.minimum(least, head_least)
        return least

    least = lax.fori_loop(0, kb_hi - f_hi, lambda i, c: block(kb_hi - 1 - i, True),
                          jnp.zeros((), F32))
    lax.while_loop(lambda st: (st[0] < f_hi) & (st[1] < STICK_DEAD_BITS),
                   lambda st: (st[0] + 1, block(f_hi - 1 - st[0], False)),
                   (jnp.zeros((), jnp.int32), least))
    o_ref[0] = jnp.where(lane_half == 0, acc_ref[0], acc_ref[1]).astype(o_ref.dtype)


def _stick_attention(q, k, v, *, sk_valid, q_off, k_off):
    bx, sq, qw = q.shape
    rows = k.shape[1]
    n_pairs = qw // LANES
    tq = min(Q_TILE, sq)
    bk = KV_BLOCK
    assert sq % tq == 0 and rows % bk == 0 and tq % 8 == 0 and q_off >= k_off >= 0
    idx = np.arange(bk)
    return pl.pallas_call(
        functools.partial(_stick_kernel, tq=tq, bk=bk, sk_valid=sk_valid, q_off=q_off,
                          k_off=k_off),
        grid=(bx, n_pairs, sq // tq),
        in_specs=[
            pl.BlockSpec((1, tq, LANES), lambda b, p, i: (b, i, p)),
            pl.BlockSpec((1, rows, LANES), lambda b, p, i: (b, 0, p)),
            pl.BlockSpec((1, rows, LANES), lambda b, p, i: (b, 0, p)),
            pl.BlockSpec((bk, bk), lambda b, p, i: (0, 0)),
        ],
        out_specs=pl.BlockSpec((1, tq, LANES), lambda b, p, i: (b, i, p)),
        out_shape=jax.ShapeDtypeStruct((bx, sq, n_pairs * LANES), BF16),
        scratch_shapes=[pltpu.VMEM((2, tq, 1), F32), pltpu.VMEM((2, tq, LANES), F32)],
        compiler_params=_cparams(3),
    )(q, k, v, jnp.asarray(idx[:, None] >= idx[None, :], BF16))


def _row_end(qpos, k_off):
    return (qpos // CHUNK + 1) * CHUNK - k_off


def _causal_kernel(q_ref, k_ref, v_ref, o_ref, *, tq, span, n_spans, sk_valid, q_off, k_off,
                   single_tile):
    qpos0 = pl.program_id(2) * tq + q_off
    n_needed = (jnp.minimum(_row_end(qpos0 + tq - 1, k_off), sk_valid) - 1) // span
    shift = int(math.log2(CHUNK))
    lane_half = lax.broadcasted_iota(jnp.int32, (1, LANES), 1) // HEAD_DIM
    nt = (((1,), (1,)), ((), ()))

    for n_full in range(n_spans):
        if single_tile and n_full != (min(_row_end(q_off + tq - 1, k_off), sk_valid) - 1) // span:
            continue

        @pl.when(n_needed == n_full)
        def _(n_full=n_full):
            full, width = n_full * span, (n_full + 1) * span
            qpos = lax.broadcasted_iota(jnp.int32, (tq, 1), 0) + qpos0
            kidx = lax.broadcasted_iota(jnp.int32, (1, span), 1) + full
            vis = (kidx < sk_valid) & (jnp.right_shift(kidx + k_off, shift)
                                       <= jnp.right_shift(qpos, shift))
            outs = []
            for hh in range(2):
                lanes = slice(hh * LANES, (hh + 1) * LANES)
                q_h = q_ref[0, :, lanes]
                s_tail = lax.dot_general(q_h, k_ref[0, full:width, lanes], nt,
                                         preferred_element_type=F32)
                s_tail = jnp.where(vis, s_tail, MASK_VALUE)
                m = jnp.max(s_tail, axis=1, keepdims=True)
                if n_full:
                    s_full = lax.dot_general(q_h, k_ref[0, 0:full, lanes], nt,
                                             preferred_element_type=F32)
                    m = jnp.maximum(m, jnp.max(s_full, axis=1, keepdims=True))
                p_tail = jnp.exp2(s_tail - m)
                l = jnp.sum(p_tail, axis=1, keepdims=True)
                o = jnp.dot(p_tail.astype(BF16), v_ref[0, full:width, :],
                            preferred_element_type=F32)
                if n_full:
                    p_full = jnp.exp2(s_full - m)
                    l = l + jnp.sum(p_full, axis=1, keepdims=True)
                    o = o + jnp.dot(p_full.astype(BF16), v_ref[0, 0:full, :],
                                    preferred_element_type=F32)
                outs.append(o / l)
            o_ref[0] = jnp.where(lane_half == 0, outs[0], outs[1]).astype(o_ref.dtype)


def _causal_attention(q, k, v, *, span, sk_valid, q_off, k_off):
    bx, sq, qw = q.shape
    rows = k.shape[1]
    n_pairs = qw // (2 * LANES)
    tq = min(Q_TILE, sq)
    assert sq % tq == 0 and rows % span == 0 and q_off % CHUNK == 0 and k_off >= 0
    for q0 in range(0, sq, tq):
        end = min(_row_end(q_off + q0 + tq - 1, k_off), sk_valid)
        first_row_end = min(_row_end(q_off + q0, k_off), sk_valid)
        assert 0 < end <= rows and (end - 1) // span * span <= first_row_end
    return pl.pallas_call(
        functools.partial(_causal_kernel, tq=tq, span=span, n_spans=rows // span,
                          sk_valid=sk_valid, q_off=q_off, k_off=k_off, single_tile=sq == tq),
        grid=(bx, n_pairs, sq // tq),
        in_specs=[
            pl.BlockSpec((1, tq, 2 * LANES), lambda b, p, i: (b, i, p)),
            pl.BlockSpec((1, rows, 2 * LANES), lambda b, p, i: (b, 0, p)),
            pl.BlockSpec((1, rows, LANES), lambda b, p, i: (b, 0, p)),
        ],
        out_specs=pl.BlockSpec((1, tq, LANES), lambda b, p, i: (b, i, p)),
        out_shape=jax.ShapeDtypeStruct((bx, sq, n_pairs * LANES), BF16),
        compiler_params=_cparams(3),
    )(q, k, v)


def _band_kernel(*refs, tq, width, sk_valid, q_off, k_off, front, n_prev, use_sink, per_head_bias):
    refs = list(refs)
    q_ref, k_ref, v_ref = refs[:3]
    rest = refs[3:]
    sink_ref = rest.pop(0) if use_sink else None
    bias_ref, o_ref = rest

    group = q_ref.shape[-1] // LANES
    first_pair = pl.program_id(1) * group
    qpos0 = pl.program_id(2) * tq + q_off
    start = pl.multiple_of((qpos0 // CHUNK - n_prev) * CHUNK - k_off + front, CHUNK)
    band = pl.ds(start, width)
    kidx = lax.broadcasted_iota(jnp.int32, (1, width), 1) + (start - front)
    invalid = jnp.where((kidx >= 0) & (kidx < sk_valid), 0.0, MASK_VALUE)

    lane_half = lax.broadcasted_iota(jnp.int32, (1, LANES), 1) // HEAD_DIM
    k_band = k_ref[0, band, :]
    v_band = v_ref[0, band, :]
    for j in range(group):
        q_pair = q_ref[0, :, j * LANES:(j + 1) * LANES]
        outs = []
        for hh in range(2):
            q_h = jnp.where(lane_half == hh, q_pair, jnp.zeros((), BF16))
            s = lax.dot_general(q_h, k_band, (((1,), (1,)), ((), ())),
                                preferred_element_type=F32)
            s = s + (bias_ref[hh] if per_head_bias else bias_ref[...]) + invalid
            m = jnp.max(s, axis=1, keepdims=True)
            if use_sink:
                sink = sink_ref[2 * (first_pair + j) + hh] * LOG2E
                m = jnp.maximum(m, sink)
            p = jnp.exp2(s - m)
            l = jnp.sum(p, axis=1, keepdims=True)
            if use_sink:
                l = l + jnp.exp2(sink - m)
            outs.append(jnp.dot(p.astype(BF16), v_band, preferred_element_type=F32) / l)
        o_ref[0, :, j * LANES:(j + 1) * LANES] = jnp.where(
            lane_half == 0, outs[0], outs[1]).astype(o_ref.dtype)


def _band_mask_tile(rows, width, n_prev):
    r_chunk = np.arange(rows)[:, None] // CHUNK
    c_chunk = np.arange(width)[None, :] // CHUNK - n_prev
    return np.where((c_chunk <= r_chunk) & (c_chunk >= r_chunk - n_prev), 0.0, MASK_VALUE
                    ).astype(np.float32)


def _band_attention(q, k, v, *, width, sk_valid, q_off, k_off, front, n_prev, kv_group=1,
                    sink=None, bias=None):
    bx, sq, qw = q.shape
    rows = k.shape[1]
    n_pairs = qw // LANES
    tq = min(Q_TILE, sq)
    assert sq % tq == 0 and q_off % CHUNK == 0 and (tq % CHUNK == 0 or sq == tq)
    assert ((q_off + sq - tq) // CHUNK - n_prev) * CHUNK - k_off + front + width <= rows
    assert (q_off // CHUNK - n_prev) * CHUNK - k_off + front >= 0
    in_specs = [
        pl.BlockSpec((1, tq, kv_group * LANES), lambda b, g, i: (b, i, g)),
        pl.BlockSpec((1, rows, LANES), lambda b, g, i: (b, 0, g)),
        pl.BlockSpec((1, rows, LANES), lambda b, g, i: (b, 0, g)),
    ]
    args = [q, k, v]
    if sink is not None:
        in_specs.append(pl.BlockSpec(memory_space=pltpu.SMEM))
        args.append(sink)
    per_head_bias = bias.ndim == 3
    if per_head_bias:
        assert kv_group == 1
        in_specs.append(pl.BlockSpec((2, tq, width), lambda b, g, i: (g, 0, 0)))
    else:
        in_specs.append(pl.BlockSpec((tq, width), lambda b, g, i: (0, 0)))
    args.append(bias)
    return pl.pallas_call(
        functools.partial(_band_kernel, tq=tq, width=width, sk_valid=sk_valid, q_off=q_off,
                          k_off=k_off, front=front, n_prev=n_prev,
                          use_sink=sink is not None, per_head_bias=per_head_bias),
        grid=(bx, n_pairs // kv_group, sq // tq),
        in_specs=in_specs,
        out_specs=pl.BlockSpec((1, tq, kv_group * LANES), lambda b, g, i: (b, i, g)),
        out_shape=jax.ShapeDtypeStruct((bx, sq, n_pairs * LANES), BF16),
        compiler_params=_cparams(3),
    )(*args)


def _band_bias_kernel(e_ref, mask_ref, o_ref, *, width):
    w = e_ref.shape[-1]
    x = jnp.broadcast_to(e_ref[0], (Q_TILE, w))
    toeplitz = pltpu.roll(x, 0, 1, stride=1, stride_axis=0)
    o_ref[0] = toeplitz[:, :width] * LOG2E + mask_ref[...]


def _band_bias(rel_bias):
    heads = rel_bias.shape[0]
    clip = (rel_bias.shape[1] - 1) // 2
    band = D_PREV_CHUNKS * CHUNK
    width = D_BAND_WIDTH
    w = width + Q_TILE
    assert band >= clip and width > band + clip
    top = jnp.broadcast_to(rel_bias[:, -1:], (heads, band - clip + 1))
    mid = jnp.flip(rel_bias[:, :2 * clip], axis=1)
    low = jnp.broadcast_to(rel_bias[:, :1], (heads, width - (band + clip + 1)))
    neg = jnp.broadcast_to(rel_bias[:, -1:], (heads, w - width))
    e = jnp.concatenate([top, mid, low, neg], axis=1)[:, None, :]
    return pl.pallas_call(
        functools.partial(_band_bias_kernel, width=width),
        grid=(heads,),
        in_specs=[pl.BlockSpec((1, 1, w), lambda h: (h, 0, 0)),
                  pl.BlockSpec((Q_TILE, width), lambda h: (0, 0))],
        out_specs=pl.BlockSpec((1, Q_TILE, width), lambda h: (h, 0, 0)),
        out_shape=jax.ShapeDtypeStruct((heads, Q_TILE, width), F32),
        compiler_params=_cparams(1),
    )(e, jnp.asarray(_band_mask_tile(Q_TILE, width, D_PREV_CHUNKS)))


def _rope_tables(pos, n_rot, lane_offsets, rows_repeat=1):
    half = n_rot // 2
    inv = ROPE_THETA ** (-jnp.arange(half, dtype=F32) * 2.0 / n_rot)
    ang = pos.astype(F32)[:, None] * inv[None, :]
    cos, sin = jnp.cos(ang), jnp.sin(ang)
    n = pos.shape[0]
    cos_t = jnp.ones((n, LANES), F32)
    sin_up = jnp.zeros((n, LANES), F32)
    sin_dn = jnp.zeros((n, LANES), F32)
    for o in lane_offsets:
        cos_t = cos_t.at[:, o:o + half].set(cos).at[:, o + half:o + n_rot].set(cos)
        sin_dn = sin_dn.at[:, o:o + half].set(-sin)
        sin_up = sin_up.at[:, o + half:o + n_rot].set(sin)
    def tiles(t):
        if rows_repeat > 1:
            return jnp.tile(t, (rows_repeat, 1))[None]
        rows = min(n, TOKEN_TILE)
        return t.reshape(n // rows, rows, LANES)
    return tiles(cos_t), tiles(sin_up), tiles(sin_dn)


def _tables_for(bx, sx, pos, n_rot, lane_offsets):
    nb, ts = _token_tiling(bx, sx)
    return _rope_tables(pos, n_rot, lane_offsets, rows_repeat=nb if nb > 1 else 1)


def _c_head_order():
    rep = C_HEADS // C_KV_HEADS
    order = []
    for p in range(C_HEADS // 2):
        g2, i = divmod(p, rep)
        order += [rep * (2 * g2) + i, rep * (2 * g2 + 1) + i]
    return np.asarray(order)


def _pad_rows(a, rows):
    return jnp.pad(a, ((0, 0), (0, rows - a.shape[1]), (0, 0)))


def _front_pad(a, rows):
    return jnp.pad(a, ((0, 0), (rows, 0), (0, 0)))


def _with_cache(cache, new, dtype, block=KV_BLOCK):
    full = jnp.concatenate([cache.reshape(cache.shape[0], cache.shape[1], -1).astype(dtype),
                            new.astype(dtype)], axis=1)
    rows = -(-full.shape[1] // block) * block
    return _pad_rows(full, rows), full.shape[1]


def kernel(x_prompt, x_sample, c_prompt, c_sample, cache_a_ckv, cache_a_krope, cache_b_k, cache_b_v,
           cache_c_k, cache_c_v, cache_d_k, cache_d_v, w_mod, b_mod, g_mix, g_ffn, w_ffn_in, w_ffn_out,
           w_a_down, g_a_q, g_a_kv, w_a_uq, w_a_uk, w_a_uv, w_a_o, w_b_qkv, w_b_o,
           w_c_qkv, b_c_qkv, sink_c, w_c_o, w_d_qkv, rel_bias_d, w_d_o, g_final):
    bp, sp, d = x_prompt.shape
    bs, t, _ = x_sample.shape
    depth = w_mod.shape[0]
    past = cache_a_ckv.shape[2]
    pos_p = jnp.arange(sp)
    pos_s = past + jnp.arange(t)

    mods = _adaln(jnp.concatenate([c_prompt, c_sample], axis=0), w_mod, b_mod)

    def mod(i, k):
        m = mods[i, k][:, None, :]
        return m[:bp], m[bp:]

    xp, xs = x_prompt, x_sample
    states = [[] for _ in range(N_MIXERS)]
    for i in range(depth):
        m, j = i % N_MIXERS, i // N_MIXERS
        (sh_p, sh_s), (sc_p, sc_s), (gm_p, gm_s) = mod(i, 0), mod(i, 1), mod(i, 2)
        (shf_p, shf_s), (scf_p, scf_s), (gf_p, gf_s) = mod(i, 3), mod(i, 4), mod(i, 5)
        g_m = g_mix[i][None, :]
        if m == 0:
            n_down = A_Q_LORA + A_KV_LORA + A_ROPE
            wd = jnp.pad(w_a_down[j], ((0, 0), (0, A_Q_LORA + A_KV_LORA + LANES - n_down))).astype(BF16)
            wq = w_a_uq[j].reshape(A_Q_LORA, A_HEADS, A_NOPE + A_ROPE)
            wq = jnp.pad(wq, ((0, 0), (0, 0), (0, LANES - A_NOPE - A_ROPE)))
            wq = wq.reshape(A_Q_LORA, A_HEADS * LANES).astype(BF16)
            wk = jnp.pad(w_a_uk[j], ((0, 0), (0, 0), (0, LANES - A_NOPE)))
            wk = wk.reshape(A_KV_LORA, A_HEADS * LANES).astype(BF16)
            place = jnp.zeros((A_ROPE, A_HEADS, LANES), F32)
            place = place.at[jnp.arange(A_ROPE), :, A_NOPE + jnp.arange(A_ROPE)].set(1.0)
            wr = place.reshape(A_ROPE, A_HEADS * LANES).astype(BF16)
            wv = w_a_uv[j].reshape(A_KV_LORA, A_HEADS * A_V).astype(BF16)
            g_q, g_kv = g_a_q[j][None, :], g_a_kv[j][None, :]

            def project(x, sh, sc, pos):
                qt = _tables_for(x.shape[0], x.shape[1], pos, A_ROPE, [A_NOPE])
                kt = _tables_for(x.shape[0], x.shape[1], pos, A_ROPE, [0])
                return _proj_a(x, g_m, sh, sc, wd, g_q, g_kv, wq, qt, kt)

            q_p, ckv_p, kr_p = project(xp, sh_p, sc_p, pos_p)
            q_s, ckv_s, kr_s = project(xs, sh_s, sc_s, pos_s)
            k_p, v_p = _expand_a(ckv_p, kr_p, wk, wr, wv)
            ckv_all, n_all = _with_cache(cache_a_ckv[j], ckv_s, F32, KV_BLOCK)
            kr_all, _ = _with_cache(cache_a_krope[j], kr_s, F32, KV_BLOCK)
            k_s, v_s = _expand_a(ckv_all, kr_all, wk, wr, wv)
            o_p = _causal_attention(q_p, k_p, v_p, span=min(A_KV_BLOCK, sp), sk_valid=sp,
                                    q_off=0, k_off=0)
            o_s = _causal_attention(q_s, k_s, v_s, span=KV_BLOCK, sk_valid=n_all, q_off=past,
                                    k_off=0)
            wo = w_a_o[j].astype(BF16)
            states[0].append((ckv_p, kr_p, ckv_s, kr_s))
        elif m == 1 or m == 3:
            heads = B_HEADS if m == 1 else D_HEADS
            w_qkv = (w_b_qkv if m == 1 else w_d_qkv)[j].astype(BF16)
            cache_k, cache_v = (cache_b_k, cache_b_v) if m == 1 else (cache_d_k, cache_d_v)
            q_scale = HEAD_DIM ** -0.5 * LOG2E
            q_p, kf_p, vf_p, kb_p, vb_p = _proj_qkv(xp, g_m, sh_p, sc_p, w_qkv, heads, q_scale)
            q_s, kf_s, vf_s, kb_s, vb_s = _proj_qkv(xs, g_m, sh_s, sc_s, w_qkv, heads, q_scale)
            lc = cache_k.shape[2]
            block = KV_BLOCK if m == 1 else D_BAND_WIDTH
            k_s, n_all = _with_cache(cache_k[j], kb_s, BF16, block)
            v_s, _ = _with_cache(cache_v[j], vb_s, BF16, block)
            if m == 1:
                o_p = _stick_attention(q_p, kb_p, vb_p, sk_valid=sp, q_off=0, k_off=0)
                o_s = _stick_attention(q_s, k_s, v_s, sk_valid=n_all, q_off=past,
                                       k_off=past - lc)
                wo = w_b_o[j].astype(BF16)
                states[1].append((kf_p, vf_p, kf_s, vf_s))
            else:
                bias = _band_bias(rel_bias_d[j])
                front = D_PREV_CHUNKS * CHUNK
                o_p = _band_attention(q_p, _front_pad(kb_p, front), _front_pad(vb_p, front),
                                      width=D_BAND_WIDTH, sk_valid=sp, q_off=0, k_off=0,
                                      front=front, n_prev=D_PREV_CHUNKS, bias=bias)
                o_s = _band_attention(q_s, k_s, v_s, width=D_BAND_WIDTH, sk_valid=n_all,
                                      q_off=past, k_off=past - lc, front=0,
                                      n_prev=D_PREV_CHUNKS, bias=bias[:, :t, :])
                wo = w_d_o[j].astype(BF16)
                keep = min(D_PREV_CHUNKS * CHUNK, sp)
                k_roll = jnp.concatenate([cache_k[j].reshape(bs, lc, -1), kf_s], axis=1)[:, t:]
                v_roll = jnp.concatenate([cache_v[j].reshape(bs, lc, -1), vf_s], axis=1)[:, t:]
                states[3].append((kf_p[:, sp - keep:], vf_p[:, sp - keep:], k_roll, v_roll))
        else:
            order = _c_head_order()
            q_width, kv_width = C_HEADS * HEAD_DIM, C_KV_HEADS * HEAD_DIM
            col = np.concatenate([(order[:, None] * HEAD_DIM + np.arange(HEAD_DIM)).reshape(-1),
                                  np.arange(q_width, q_width + 2 * kv_width)])
            w_qkv = w_c_qkv[j][:, col].astype(BF16)
            b_qkv = b_c_qkv[j][col][None, :]
            sink = sink_c[j][order]
            wo = w_c_o[j].reshape(C_HEADS, HEAD_DIM, d)[order].reshape(q_width, d).astype(BF16)
            lanes = [0, HEAD_DIM]

            def project(x, sh, sc, pos):
                tb = _tables_for(x.shape[0], x.shape[1], pos, C_ROT, lanes)
                return _proj_c(x, g_m, sh, sc, w_qkv, b_qkv, tb)

            q_p, kf_p, vf_p, kb_p, vb_p = project(xp, sh_p, sc_p, pos_p)
            q_s, kf_s, vf_s, kb_s, vb_s = project(xs, sh_s, sc_s, pos_s)
            lc = cache_c_k.shape[2]
            k_s, n_all = _with_cache(cache_c_k[j], kb_s, BF16, C_BAND_WIDTH)
            v_s, _ = _with_cache(cache_c_v[j], vb_s, BF16, C_BAND_WIDTH)
            group = C_HEADS // C_KV_HEADS
            front = C_PREV_CHUNKS * CHUNK
            band_mask = jnp.asarray(_band_mask_tile(Q_TILE, C_BAND_WIDTH, C_PREV_CHUNKS))
            o_p = _band_attention(q_p, _front_pad(kb_p, front), _front_pad(vb_p, front),
                                  width=C_BAND_WIDTH, sk_valid=sp, q_off=0, k_off=0, front=front,
                                  n_prev=C_PREV_CHUNKS, kv_group=group, sink=sink, bias=band_mask)
            o_s = _band_attention(q_s, k_s, v_s, width=C_BAND_WIDTH, sk_valid=n_all, q_off=past,
                                  k_off=past - lc, front=0, n_prev=C_PREV_CHUNKS, kv_group=group,
                                  sink=sink, bias=band_mask[:t])
            keep = min(C_PREV_CHUNKS * CHUNK, sp)
            k_roll = jnp.concatenate([cache_c_k[j].reshape(bs, lc, -1), kf_s], axis=1)[:, t:]
            v_roll = jnp.concatenate([cache_c_v[j].reshape(bs, lc, -1), vf_s], axis=1)[:, t:]
            states[2].append((kf_p[:, sp - keep:], vf_p[:, sp - keep:], k_roll, v_roll))

        g_f = g_ffn[i][None, :]
        w_in, w_out = w_ffn_in[i].astype(BF16), w_ffn_out[i].astype(BF16)
        xp = _block(xp, o_p, wo, gm_p, g_f, shf_p, scf_p, gf_p, w_in, w_out)
        xs = _block(xs, o_s, wo, gm_s, g_f, shf_s, scf_s, gf_s, w_in, w_out)

    y_p = _final_norm(xp, g_final[None, :])
    y_s = _final_norm(xs, g_final[None, :])

    def stacked(entries, n_heads=None):
        outs = []
        for parts in zip(*entries):
            a = jnp.stack(parts, axis=0)
            if n_heads is not None:
                a = a.reshape(a.shape[:3] + (n_heads, HEAD_DIM))
            outs.append(a)
        return tuple(outs)

    return ((y_p, y_s) + stacked(states[0]) + stacked(states[1], B_HEADS)
            + stacked(states[2], C_KV_HEADS) + stacked(states[3], D_HEADS))
```

```python
import functools
import math

import numpy as np
import jax
import jax.numpy as jnp
from jax import lax
from jax.experimental import pallas as pl
from jax.experimental.pallas import tpu as pltpu

F32 = jnp.float32
BF16 = jnp.bfloat16

CHUNK = 64
HEAD_DIM = 64
ROPE_THETA = 500000.0
NORM_EPS = 1e-6
N_MIXERS = 4
A_HEADS, A_Q_LORA, A_KV_LORA, A_NOPE, A_ROPE, A_V = 16, 384, 256, 64, 32, 64
B_HEADS = 16
C_HEADS, C_KV_HEADS, C_WINDOW, C_ROT = 16, 4, 128, 16
D_HEADS, D_PREV_CHUNKS, D_REL_CLIP = 16, 8, 128
C_PREV_CHUNKS = C_WINDOW // CHUNK

LANES = 128
V7X_VMEM_LIMIT = 56 * 1024 * 1024

TOKEN_TILE = 512
Q_TILE = 256
KV_BLOCK = 256
A_KV_BLOCK = 1024
C_BAND_WIDTH = C_PREV_CHUNKS * CHUNK + Q_TILE
D_BAND_WIDTH = D_PREV_CHUNKS * CHUNK + Q_TILE
D_PAIRS_PER_STEP = 2
MASK_VALUE = -1e30
LOG2E = math.log2(math.e)
STICK_DEAD_BITS = 150.0


def _cparams(n_axes):
    return pltpu.CompilerParams(
        dimension_semantics=("parallel",) * n_axes, vmem_limit_bytes=V7X_VMEM_LIMIT)


def _adaln_kernel(c_ref, w_ref, b_ref, o_ref):
    c = c_ref[...]
    a = (c * jax.nn.sigmoid(c)).astype(BF16)
    y = jnp.dot(a, w_ref[0].astype(BF16), preferred_element_type=F32) + b_ref[0]
    o_ref[0, 0] = y


def _adaln(c_all, w_mod, b_mod):
    depth, d, d6 = w_mod.shape
    n = c_all.shape[0]
    return pl.pallas_call(
        _adaln_kernel,
        grid=(depth, d6 // d),
        in_specs=[
            pl.BlockSpec((n, d), lambda i, k: (0, 0)),
            pl.BlockSpec((1, d, d), lambda i, k: (i, 0, k)),
            pl.BlockSpec((1, 1, d), lambda i, k: (i, 0, k)),
        ],
        out_specs=pl.BlockSpec((1, 1, n, d), lambda i, k: (i, k, 0, 0)),
        out_shape=jax.ShapeDtypeStruct((depth, d6 // d, n, d), F32),
        compiler_params=_cparams(2),
    )(c_all, w_mod, b_mod.reshape(depth, 1, d6))


def _token_tiling(bx, sx):
    if sx >= TOKEN_TILE:
        ts = TOKEN_TILE
        while sx % ts:
            ts //= 2
        return 1, ts
    nb = min(bx, TOKEN_TILE // sx)
    while bx % nb:
        nb -= 1
    return nb, sx


def _tok_spec(nb, ts, width):
    return pl.BlockSpec((nb, ts, width), lambda b, s: (b, s, 0))


def _mod_spec(nb, d):
    return pl.BlockSpec((nb, 1, d), lambda b, s: (b, 0, 0))


def _const_spec(shape):
    nd = len(shape)
    return pl.BlockSpec(shape, lambda b, s: (0,) * nd)


def _table_spec(rows):
    return pl.BlockSpec((1, rows, LANES), lambda b, s: (s, 0, 0))


def _rms(x):
    return x * lax.rsqrt(jnp.mean(x * x, axis=-1, keepdims=True) + NORM_EPS)


def _modulated(x_ref, g_ref, shift_ref, scale_ref):
    y = _rms(x_ref[...]) * g_ref[...]
    h = y * (1.0 + scale_ref[...]) + shift_ref[...]
    return h.reshape(-1, h.shape[-1]).astype(BF16)


def _rope_lanes(x, cos_t, sin_up, sin_dn, half):
    return (x * cos_t + pltpu.roll(x, half, 1) * sin_up
            + pltpu.roll(x, LANES - half, 1) * sin_dn)


def _store_tok(ref, lo, val):
    nb, ts = ref.shape[0], ref.shape[1]
    w = val.shape[-1]
    ref[:, :, lo:lo + w] = val.reshape(nb, ts, w).astype(ref.dtype)


def _proj_qkv_kernel(x_ref, g_ref, sh_ref, sc_ref, w_ref, q_ref, kf_ref, vf_ref, kb_ref, vb_ref,
                     *, width, q_scale):
    h = _modulated(x_ref, g_ref, sh_ref, sc_ref)
    q = jnp.dot(h, w_ref[:, 0:width], preferred_element_type=F32)
    _store_tok(q_ref, 0, q * q_scale)
    k = jnp.dot(h, w_ref[:, width:2 * width], preferred_element_type=F32)
    _store_tok(kf_ref, 0, k)
    _store_tok(kb_ref, 0, k)
    v = jnp.dot(h, w_ref[:, 2 * width:3 * width], preferred_element_type=F32)
    _store_tok(vf_ref, 0, v)
    _store_tok(vb_ref, 0, v)


def _proj_qkv(x, g, shift, scale, w_bf16, n_heads, q_scale):
    bx, sx, d = x.shape
    width = n_heads * HEAD_DIM
    nb, ts = _token_tiling(bx, sx)
    out = lambda dt: jax.ShapeDtypeStruct((bx, sx, width), dt)
    return pl.pallas_call(
        functools.partial(_proj_qkv_kernel, width=width, q_scale=q_scale),
        grid=(bx // nb, sx // ts),
        in_specs=[_tok_spec(nb, ts, d), _const_spec((1, d)), _mod_spec(nb, d), _mod_spec(nb, d),
                  _const_spec(w_bf16.shape)],
        out_specs=[_tok_spec(nb, ts, width)] * 5,
        out_shape=[out(BF16), out(F32), out(F32), out(BF16), out(BF16)],
        compiler_params=_cparams(2),
    )(x, g, shift, scale, w_bf16)


def _proj_c_kernel(x_ref, g_ref, sh_ref, sc_ref, w_ref, b_ref, tc_ref, tu_ref, td_ref,
                   q_ref, kf_ref, vf_ref, kb_ref, vb_ref, *, q_width, kv_width, q_scale):
    h = _modulated(x_ref, g_ref, sh_ref, sc_ref)
    qkv = jnp.dot(h, w_ref[...], preferred_element_type=F32) + b_ref[...]
    cos_t, sin_up, sin_dn = tc_ref[0], tu_ref[0], td_ref[0]
    half = C_ROT // 2
    for j in range(q_width // LANES):
        xg = qkv[:, j * LANES:(j + 1) * LANES]
        _store_tok(q_ref, j * LANES, _rope_lanes(xg, cos_t, sin_up, sin_dn, half) * q_scale)
    for j in range(kv_width // LANES):
        lo = q_width + j * LANES
        kg = _rope_lanes(qkv[:, lo:lo + LANES], cos_t, sin_up, sin_dn, half)
        _store_tok(kf_ref, j * LANES, kg)
        _store_tok(kb_ref, j * LANES, kg)
    v = qkv[:, q_width + kv_width:q_width + 2 * kv_width]
    _store_tok(vf_ref, 0, v)
    _store_tok(vb_ref, 0, v)


def _proj_c(x, g, shift, scale, w_bf16, bias, tables):
    bx, sx, d = x.shape
    q_width, kv_width = C_HEADS * HEAD_DIM, C_KV_HEADS * HEAD_DIM
    nb, ts = _token_tiling(bx, sx)
    out = lambda w, dt: jax.ShapeDtypeStruct((bx, sx, w), dt)
    return pl.pallas_call(
        functools.partial(_proj_c_kernel, q_width=q_width, kv_width=kv_width,
                          q_scale=HEAD_DIM ** -0.5 * LOG2E),
        grid=(bx // nb, sx // ts),
        in_specs=[_tok_spec(nb, ts, d), _const_spec((1, d)), _mod_spec(nb, d), _mod_spec(nb, d),
                  _const_spec(w_bf16.shape), _const_spec(bias.shape)] + [_table_spec(nb * ts)] * 3,
        out_specs=[_tok_spec(nb, ts, q_width)] + [_tok_spec(nb, ts, kv_width)] * 4,
        out_shape=[out(q_width, BF16), out(kv_width, F32), out(kv_width, F32),
                   out(kv_width, BF16), out(kv_width, BF16)],
        compiler_params=_cparams(2),
    )(x, g, shift, scale, w_bf16, bias, *tables)


def _proj_a_kernel(x_ref, g_ref, sh_ref, sc_ref, wd_ref, gq_ref, gkv_ref, wq_ref, wqr_ref,
                   qc_ref, qs_ref, kc_ref, ku_ref, kd_ref,
                   q_ref, ckv_ref, kr_ref, *, q_scale):
    h = _modulated(x_ref, g_ref, sh_ref, sc_ref)
    down = jnp.dot(h, wd_ref[...], preferred_element_type=F32)
    cq = (_rms(down[:, :A_Q_LORA]) * gq_ref[...]).astype(BF16)
    ckv = _rms(down[:, A_Q_LORA:A_Q_LORA + A_KV_LORA]) * gkv_ref[...]
    _store_tok(ckv_ref, 0, ckv)
    lo = A_Q_LORA + A_KV_LORA
    half = A_ROPE // 2
    kr = _rope_lanes(down[:, lo:lo + LANES], kc_ref[0], ku_ref[0], kd_ref[0], half)
    _store_tok(kr_ref, 0, kr[:, :A_ROPE])
    q = jnp.dot(cq, wq_ref[...], preferred_element_type=F32)
    q_rot = jnp.dot(cq, wqr_ref[...], preferred_element_type=F32)
    qc, qs = qc_ref[0], qs_ref[0]
    for j in range(A_HEADS):
        lanes = slice(j * LANES, (j + 1) * LANES)
        _store_tok(q_ref, j * LANES, (q[:, lanes] * qc + q_rot[:, lanes] * qs) * q_scale)


def _proj_a(x, g, shift, scale, wd_bf16, g_q, g_kv, wq_bf16, wq_rot_bf16, q_tables, k_tables):
    bx, sx, d = x.shape
    nb, ts = _token_tiling(bx, sx)
    return pl.pallas_call(
        functools.partial(_proj_a_kernel, q_scale=(A_NOPE + A_ROPE) ** -0.5 * LOG2E),
        grid=(bx // nb, sx // ts),
        in_specs=[_tok_spec(nb, ts, d), _const_spec((1, d)), _mod_spec(nb, d), _mod_spec(nb, d),
                  _const_spec(wd_bf16.shape), _const_spec(g_q.shape), _const_spec(g_kv.shape),
                  _const_spec(wq_bf16.shape), _const_spec(wq_rot_bf16.shape)]
                 + [_table_spec(nb * ts)] * 5,
        out_specs=[_tok_spec(nb, ts, A_HEADS * LANES), _tok_spec(nb, ts, A_KV_LORA),
                   _tok_spec(nb, ts, A_ROPE)],
        out_shape=[jax.ShapeDtypeStruct((bx, sx, A_HEADS * LANES), BF16),
                   jax.ShapeDtypeStruct((bx, sx, A_KV_LORA), F32),
                   jax.ShapeDtypeStruct((bx, sx, A_ROPE), F32)],
        compiler_params=_cparams(2),
    )(x, g, shift, scale, wd_bf16, g_q, g_kv, wq_bf16, wq_rot_bf16, *q_tables, *k_tables)


def _expand_a_kernel(ckv_ref, kr_ref, wk_ref, wr_ref, wv_ref, k_ref, v_ref):
    ckv = ckv_ref[...]
    ckv = ckv.reshape(-1, ckv.shape[-1]).astype(BF16)
    kr = kr_ref[...]
    kr = kr.reshape(-1, kr.shape[-1]).astype(BF16)
    k = (jnp.dot(ckv, wk_ref[...], preferred_element_type=F32)
         + jnp.dot(kr, wr_ref[...], preferred_element_type=F32))
    _store_tok(k_ref, 0, k)
    _store_tok(v_ref, 0, jnp.dot(ckv, wv_ref[...], preferred_element_type=F32))


def _expand_a(ckv, kr, wk, wr, wv):
    bx, sx, _ = ckv.shape
    nb, ts = _token_tiling(bx, sx)
    return pl.pallas_call(
        _expand_a_kernel,
        grid=(bx // nb, sx // ts),
        in_specs=[_tok_spec(nb, ts, A_KV_LORA), _tok_spec(nb, ts, A_ROPE),
                  _const_spec(wk.shape), _const_spec(wr.shape), _const_spec(wv.shape)],
        out_specs=[_tok_spec(nb, ts, A_HEADS * LANES), _tok_spec(nb, ts, A_HEADS * A_V)],
        out_shape=[jax.ShapeDtypeStruct((bx, sx, A_HEADS * LANES), BF16),
                   jax.ShapeDtypeStruct((bx, sx, A_HEADS * A_V), BF16)],
        compiler_params=_cparams(2),
    )(ckv, kr, wk, wr, wv)


def _block_kernel(x_ref, o_ref, wo_ref, gm_ref, g_ref, sh_ref, sc_ref, gf_ref, wi_ref, wout_ref,
                  *out_refs, hidden, chunk, final_norm):
    gout_ref, y_ref = out_refs if final_norm else (None,) + out_refs
    nb, ts, d = x_ref.shape
    o = o_ref[...].reshape(nb * ts, -1)
    mix = jnp.dot(o, wo_ref[...], preferred_element_type=F32).reshape(nb, ts, d)
    x1 = x_ref[...] + gm_ref[...] * mix
    h = (_rms(x1) * g_ref[...]) * (1.0 + sc_ref[...]) + sh_ref[...]
    h = h.reshape(nb * ts, d).astype(BF16)
    acc = jnp.zeros((nb * ts, d), F32)
    for c in range(hidden // chunk):
        gate = jnp.dot(h, wi_ref[:, c * chunk:(c + 1) * chunk], preferred_element_type=F32)
        up = jnp.dot(h, wi_ref[:, hidden + c * chunk:hidden + (c + 1) * chunk],
                     preferred_element_type=F32)
        act = (gate * jax.nn.sigmoid(gate) * up).astype(BF16)
        acc = acc + jnp.dot(act, wout_ref[c * chunk:(c + 1) * chunk, :],
                            preferred_element_type=F32)
    x2 = x1 + gf_ref[...] * acc.reshape(nb, ts, d)
    y_ref[...] = _rms(x2) * gout_ref[...] if final_norm else x2


def _block(x, o, wo, gate_m, g_ffn, shift_f, scale_f, gate_f, w_in, w_out, g_out=None):
    bx, sx, d = x.shape
    hidden = w_out.shape[0]
    nb, ts = _token_tiling(bx, sx)
    resident = lambda shape: pl.BlockSpec(shape, lambda b, s: (0,) * len(shape),
                                          pipeline_mode=pl.Buffered(1))
    in_specs = [_tok_spec(nb, ts, d), _tok_spec(nb, ts, o.shape[-1]), resident(wo.shape),
                _mod_spec(nb, d), _const_spec((1, d)), _mod_spec(nb, d), _mod_spec(nb, d),
                _mod_spec(nb, d), resident(w_in.shape), resident(w_out.shape)]
    args = [x, o, wo, gate_m, g_ffn, shift_f, scale_f, gate_f, w_in, w_out]
    if g_out is not None:
        in_specs.append(_const_spec((1, d)))
        args.append(g_out)
    return pl.pallas_call(
        functools.partial(_block_kernel, hidden=hidden, chunk=256, final_norm=g_out is not None),
        grid=(bx // nb, sx // ts),
        in_specs=in_specs,
        out_specs=_tok_spec(nb, ts, d),
        out_shape=jax.ShapeDtypeStruct((bx, sx, d), F32),
        compiler_params=_cparams(2),
    )(*args)


def _stick_kernel(q_ref, k_ref, v_ref, tri_ref, o_ref, run_ref, acc_ref,
                  *, tq, bk, sk_valid, q_off, k_off):
    qpos0 = pl.program_id(2) * tq + q_off
    kb_hi = (jnp.minimum(qpos0 + (tq - 1) - k_off, sk_valid) + bk - 1) // bk
    f_hi = jnp.minimum(jnp.minimum(qpos0 - k_off, sk_valid) // bk, kb_hi)

    def visible(kb):
        row = lax.broadcasted_iota(jnp.int32, (2 * tq, 1), 0)
        qpos = jnp.where(row < tq, row, row - tq) + qpos0
        ik = kb * bk + lax.broadcasted_iota(jnp.int32, (1, bk), 1)
        return (ik + k_off < qpos) & (ik < sk_valid)

    lane_half = lax.broadcasted_iota(jnp.int32, (1, LANES), 1) // HEAD_DIM
    q_pair = q_ref[0]
    q_both = jnp.concatenate(
        [jnp.where(lane_half == hh, q_pair, jnp.zeros((), BF16)) for hh in range(2)], axis=0)

    def rows(kb):
        return pl.ds(pl.multiple_of(kb * bk, bk), bk)

    acc_ref[...] = jnp.zeros_like(acc_ref)
    run_ref[...] = jnp.zeros_like(run_ref)

    def block(kb, masked):
        y = lax.dot_general(q_both, k_ref[0, rows(kb), :], (((1,), (1,)), ((), ())),
                            preferred_element_type=F32)
        sp = jnp.maximum(y, 0.0) + jnp.log2(1.0 + jnp.exp2(-jnp.abs(y)))
        if masked:
            vis = visible(kb)
            sp = jnp.where(vis, sp, 0.0)
        run = run_ref[...]
        run_new = run + jnp.sum(sp, axis=1, keepdims=True)
        run_ref[...] = run_new
        hi = sp.astype(BF16)
        lo = (sp - hi.astype(F32)).astype(BF16)
        both = jnp.dot(jnp.concatenate([hi, lo], axis=0), tri_ref[...],
                       preferred_element_type=F32)
        suffix = both[:2 * tq] + both[2 * tq:]
        a = jnp.exp2(y - suffix - run)
        if masked:
            a = jnp.where(vis, a, 0.0)
        acc_ref[...] += jnp.dot(a.astype(BF16), v_ref[0, rows(kb), :],
                                preferred_element_type=F32)
        return jnp.min(run_new)

    least = lax.fori_loop(0, kb_hi - f_hi, lambda i, c: block(kb_hi - 1 - i, True),
                          jnp.zeros((), F32))
    lax.while_loop(lambda st: (st[0] < f_hi) & (st[1] < STICK_DEAD_BITS),
                   lambda st: (st[0] + 1, block(f_hi - 1 - st[0], False)),
                   (jnp.zeros((), jnp.int32), least))
    o_ref[0] = jnp.where(lane_half == 0, acc_ref[:tq], acc_ref[tq:]).astype(o_ref.dtype)


def _stick_attention(q, k, v, *, sk_valid, q_off, k_off):
    bx, sq, qw = q.shape
    rows = k.shape[1]
    n_pairs = qw // LANES
    tq = min(Q_TILE, sq)
    bk = KV_BLOCK
    assert sq % tq == 0 and rows % bk == 0 and tq % 8 == 0 and q_off >= k_off >= 0
    idx = np.arange(bk)
    return pl.pallas_call(
        functools.partial(_stick_kernel, tq=tq, bk=bk, sk_valid=sk_valid, q_off=q_off,
                          k_off=k_off),
        grid=(bx, n_pairs, sq // tq),
        in_specs=[
            pl.BlockSpec((1, tq, LANES), lambda b, p, i: (b, i, p)),
            pl.BlockSpec((1, rows, LANES), lambda b, p, i: (b, 0, p)),
            pl.BlockSpec((1, rows, LANES), lambda b, p, i: (b, 0, p)),
            pl.BlockSpec((bk, bk), lambda b, p, i: (0, 0)),
        ],
        out_specs=pl.BlockSpec((1, tq, LANES), lambda b, p, i: (b, i, p)),
        out_shape=jax.ShapeDtypeStruct((bx, sq, n_pairs * LANES), BF16),
        scratch_shapes=[pltpu.VMEM((2 * tq, 1), F32), pltpu.VMEM((2 * tq, LANES), F32)],
        compiler_params=_cparams(3),
    )(q, k, v, jnp.asarray(idx[:, None] >= idx[None, :], BF16))


def _row_end(qpos, k_off):
    return (qpos // CHUNK + 1) * CHUNK - k_off


def _causal_kernel(q_ref, k_ref, v_ref, o_ref, *, tq, span, n_spans, sk_valid, q_off, k_off,
                   single_tile):
    qpos0 = pl.program_id(2) * tq + q_off
    n_needed = (jnp.minimum(_row_end(qpos0 + tq - 1, k_off), sk_valid) - 1) // span
    shift = int(math.log2(CHUNK))
    lane_half = lax.broadcasted_iota(jnp.int32, (1, LANES), 1) // HEAD_DIM
    nt = (((1,), (1,)), ((), ()))

    for n_full in range(n_spans):
        if single_tile and n_full != (min(_row_end(q_off + tq - 1, k_off), sk_valid) - 1) // span:
            continue

        @pl.when(n_needed == n_full)
        def _(n_full=n_full):
            full, width = n_full * span, (n_full + 1) * span
            qpos = lax.broadcasted_iota(jnp.int32, (tq, 1), 0) + qpos0
            kidx = lax.broadcasted_iota(jnp.int32, (1, span), 1) + full
            vis = (kidx < sk_valid) & (jnp.right_shift(kidx + k_off, shift)
                                       <= jnp.right_shift(qpos, shift))
            outs = []
            for hh in range(2):
                lanes = slice(hh * LANES, (hh + 1) * LANES)
                q_h = q_ref[0, :, lanes]
                s_tail = lax.dot_general(q_h, k_ref[0, full:width, lanes], nt,
                                         preferred_element_type=F32)
                s_tail = jnp.where(vis, s_tail, MASK_VALUE)
                m = jnp.max(s_tail, axis=1, keepdims=True)
                if n_full:
                    s_full = lax.dot_general(q_h, k_ref[0, 0:full, lanes], nt,
                                             preferred_element_type=F32)
                    m = jnp.maximum(m, jnp.max(s_full, axis=1, keepdims=True))
                p_tail = jnp.exp2(s_tail - m)
                l = jnp.sum(p_tail, axis=1, keepdims=True)
                o = jnp.dot(p_tail.astype(BF16), v_ref[0, full:width, :],
                            preferred_element_type=F32)
                if n_full:
                    p_full = jnp.exp2(s_full - m)
                    l = l + jnp.sum(p_full, axis=1, keepdims=True)
                    o = o + jnp.dot(p_full.astype(BF16), v_ref[0, 0:full, :],
                                    preferred_element_type=F32)
                outs.append(o / l)
            o_ref[0] = jnp.where(lane_half == 0, outs[0], outs[1]).astype(o_ref.dtype)


def _causal_attention(q, k, v, *, span, sk_valid, q_off, k_off):
    bx, sq, qw = q.shape
    rows = k.shape[1]
    n_pairs = qw // (2 * LANES)
    tq = min(Q_TILE, sq)
    assert sq % tq == 0 and rows % span == 0 and q_off % CHUNK == 0 and k_off >= 0
    for q0 in range(0, sq, tq):
        end = min(_row_end(q_off + q0 + tq - 1, k_off), sk_valid)
        first_row_end = min(_row_end(q_off + q0, k_off), sk_valid)
        assert 0 < end <= rows and (end - 1) // span * span <= first_row_end
    return pl.pallas_call(
        functools.partial(_causal_kernel, tq=tq, span=span, n_spans=rows // span,
                          sk_valid=sk_valid, q_off=q_off, k_off=k_off, single_tile=sq == tq),
        grid=(bx, n_pairs, sq // tq),
        in_specs=[
            pl.BlockSpec((1, tq, 2 * LANES), lambda b, p, i: (b, i, p)),
            pl.BlockSpec((1, rows, 2 * LANES), lambda b, p, i: (b, 0, p)),
            pl.BlockSpec((1, rows, LANES), lambda b, p, i: (b, 0, p)),
        ],
        out_specs=pl.BlockSpec((1, tq, LANES), lambda b, p, i: (b, i, p)),
        out_shape=jax.ShapeDtypeStruct((bx, sq, n_pairs * LANES), BF16),
        compiler_params=_cparams(3),
    )(q, k, v)


def _band_kernel(*refs, tq, width, sk_valid, q_off, k_off, front, n_prev, use_sink, per_head_bias):
    refs = list(refs)
    q_ref, k_ref, v_ref = refs[:3]
    rest = refs[3:]
    sink_ref = rest.pop(0) if use_sink else None
    bias_ref, o_ref = rest

    group = q_ref.shape[-1] // LANES
    first_pair = pl.program_id(1) * group
    qpos0 = pl.program_id(2) * tq + q_off
    start = pl.multiple_of((qpos0 // CHUNK - n_prev) * CHUNK - k_off + front, CHUNK)
    band = pl.ds(start, width)
    kidx = lax.broadcasted_iota(jnp.int32, (1, width), 1) + (start - front)
    invalid = jnp.where((kidx >= 0) & (kidx < sk_valid), 0.0, MASK_VALUE)

    lane_half = lax.broadcasted_iota(jnp.int32, (1, LANES), 1) // HEAD_DIM
    shared_kv = k_ref.shape[-1] == LANES
    for j in range(group):
        q_pair = q_ref[0, :, j * LANES:(j + 1) * LANES]
        kv_lanes = slice(0, LANES) if shared_kv else slice(j * LANES, (j + 1) * LANES)
        k_band = k_ref[0, band, kv_lanes]
        v_band = v_ref[0, band, kv_lanes]
        outs = []
        for hh in range(2):
            q_h = jnp.where(lane_half == hh, q_pair, jnp.zeros((), BF16))
            s = lax.dot_general(q_h, k_band, (((1,), (1,)), ((), ())),
                                preferred_element_type=F32)
            s = s + (bias_ref[2 * j + hh] if per_head_bias else bias_ref[...]) + invalid
            m = jnp.max(s, axis=1, keepdims=True)
            if use_sink:
                sink = sink_ref[2 * (first_pair + j) + hh] * LOG2E
                m = jnp.maximum(m, sink)
            p = jnp.exp2(s - m)
            l = jnp.sum(p, axis=1, keepdims=True)
            if use_sink:
                l = l + jnp.exp2(sink - m)
            outs.append(jnp.dot(p.astype(BF16), v_band, preferred_element_type=F32) / l)
        o_ref[0, :, j * LANES:(j + 1) * LANES] = jnp.where(
            lane_half == 0, outs[0], outs[1]).astype(o_ref.dtype)


def _band_mask_tile(rows, width, n_prev):
    r_chunk = np.arange(rows)[:, None] // CHUNK
    c_chunk = np.arange(width)[None, :] // CHUNK - n_prev
    return np.where((c_chunk <= r_chunk) & (c_chunk >= r_chunk - n_prev), 0.0, MASK_VALUE
                    ).astype(np.float32)


def _band_attention(q, k, v, *, width, sk_valid, q_off, k_off, front, n_prev, pairs_per_step,
                    shared_kv, sink=None, bias=None):
    kv_group = pairs_per_step
    kv_lanes = LANES if shared_kv else pairs_per_step * LANES
    bx, sq, qw = q.shape
    rows = k.shape[1]
    n_pairs = qw // LANES
    tq = min(Q_TILE, sq)
    assert sq % tq == 0 and q_off % CHUNK == 0 and (tq % CHUNK == 0 or sq == tq)
    assert ((q_off + sq - tq) // CHUNK - n_prev) * CHUNK - k_off + front + width <= rows
    assert (q_off // CHUNK - n_prev) * CHUNK - k_off + front >= 0
    in_specs = [
        pl.BlockSpec((1, tq, kv_group * LANES), lambda b, g, i: (b, i, g)),
        pl.BlockSpec((1, rows, kv_lanes), lambda b, g, i: (b, 0, g)),
        pl.BlockSpec((1, rows, kv_lanes), lambda b, g, i: (b, 0, g)),
    ]
    args = [q, k, v]
    if sink is not None:
        in_specs.append(pl.BlockSpec(memory_space=pltpu.SMEM))
        args.append(sink)
    per_head_bias = bias.ndim == 3
    if per_head_bias:
        in_specs.append(pl.BlockSpec((2 * kv_group, tq, width), lambda b, g, i: (g, 0, 0)))
    else:
        in_specs.append(pl.BlockSpec((tq, width), lambda b, g, i: (0, 0)))
    args.append(bias)
    return pl.pallas_call(
        functools.partial(_band_kernel, tq=tq, width=width, sk_valid=sk_valid, q_off=q_off,
                          k_off=k_off, front=front, n_prev=n_prev,
                          use_sink=sink is not None, per_head_bias=per_head_bias),
        grid=(bx, n_pairs // kv_group, sq // tq),
        in_specs=in_specs,
        out_specs=pl.BlockSpec((1, tq, kv_group * LANES), lambda b, g, i: (b, i, g)),
        out_shape=jax.ShapeDtypeStruct((bx, sq, n_pairs * LANES), BF16),
        compiler_params=_cparams(3),
    )(*args)


def _band_bias_kernel(e_ref, mask_ref, o_ref, *, width):
    w = e_ref.shape[-1]
    x = jnp.broadcast_to(e_ref[0], (Q_TILE, w))
    toeplitz = pltpu.roll(x, 0, 1, stride=1, stride_axis=0)
    o_ref[0] = toeplitz[:, :width] * LOG2E + mask_ref[...]


def _band_bias(rel_bias):
    heads = rel_bias.shape[0]
    clip = (rel_bias.shape[1] - 1) // 2
    band = D_PREV_CHUNKS * CHUNK
    width = D_BAND_WIDTH
    w = width + Q_TILE
    assert band >= clip and width > band + clip
    top = jnp.broadcast_to(rel_bias[:, -1:], (heads, band - clip + 1))
    mid = jnp.flip(rel_bias[:, :2 * clip], axis=1)
    low = jnp.broadcast_to(rel_bias[:, :1], (heads, width - (band + clip + 1)))
    neg = jnp.broadcast_to(rel_bias[:, -1:], (heads, w - width))
    e = jnp.concatenate([top, mid, low, neg], axis=1)[:, None, :]
    return pl.pallas_call(
        functools.partial(_band_bias_kernel, width=width),
        grid=(heads,),
        in_specs=[pl.BlockSpec((1, 1, w), lambda h: (h, 0, 0)),
                  pl.BlockSpec((Q_TILE, width), lambda h: (0, 0))],
        out_specs=pl.BlockSpec((1, Q_TILE, width), lambda h: (h, 0, 0)),
        out_shape=jax.ShapeDtypeStruct((heads, Q_TILE, width), F32),
        compiler_params=_cparams(1),
    )(e, jnp.asarray(_band_mask_tile(Q_TILE, width, D_PREV_CHUNKS)))


def _rope_tables(pos, n_rot, lane_offsets, rows_repeat=1):
    half = n_rot // 2
    inv = ROPE_THETA ** (-jnp.arange(half, dtype=F32) * 2.0 / n_rot)
    ang = pos.astype(F32)[:, None] * inv[None, :]
    cos, sin = jnp.cos(ang), jnp.sin(ang)
    n = pos.shape[0]
    cos_t = jnp.ones((n, LANES), F32)
    sin_up = jnp.zeros((n, LANES), F32)
    sin_dn = jnp.zeros((n, LANES), F32)
    for o in lane_offsets:
        cos_t = cos_t.at[:, o:o + half].set(cos).at[:, o + half:o + n_rot].set(cos)
        sin_dn = sin_dn.at[:, o:o + half].set(-sin)
        sin_up = sin_up.at[:, o + half:o + n_rot].set(sin)
    def tiles(t):
        if rows_repeat > 1:
            return jnp.tile(t, (rows_repeat, 1))[None]
        rows = min(n, TOKEN_TILE)
        return t.reshape(n // rows, rows, LANES)
    return tiles(cos_t), tiles(sin_up), tiles(sin_dn)


def _tables_for(bx, sx, pos, n_rot, lane_offsets):
    nb, ts = _token_tiling(bx, sx)
    return _rope_tables(pos, n_rot, lane_offsets, rows_repeat=nb if nb > 1 else 1)


def _c_head_order():
    rep = C_HEADS // C_KV_HEADS
    order = []
    for p in range(C_HEADS // 2):
        g2, i = divmod(p, rep)
        order += [rep * (2 * g2) + i, rep * (2 * g2 + 1) + i]
    return np.asarray(order)


def _pad_rows(a, rows):
    return jnp.pad(a, ((0, 0), (0, rows - a.shape[1]), (0, 0)))


def _front_pad(a, rows):
    return jnp.pad(a, ((0, 0), (rows, 0), (0, 0)))


def _with_cache(cache, new, dtype, block=KV_BLOCK):
    full = jnp.concatenate([cache.reshape(cache.shape[0], cache.shape[1], -1).astype(dtype),
                            new.astype(dtype)], axis=1)
    rows = -(-full.shape[1] // block) * block
    return _pad_rows(full, rows), full.shape[1]


def kernel(x_prompt, x_sample, c_prompt, c_sample, cache_a_ckv, cache_a_krope, cache_b_k, cache_b_v,
           cache_c_k, cache_c_v, cache_d_k, cache_d_v, w_mod, b_mod, g_mix, g_ffn, w_ffn_in, w_ffn_out,
           w_a_down, g_a_q, g_a_kv, w_a_uq, w_a_uk, w_a_uv, w_a_o, w_b_qkv, w_b_o,
           w_c_qkv, b_c_qkv, sink_c, w_c_o, w_d_qkv, rel_bias_d, w_d_o, g_final):
    bp, sp, d = x_prompt.shape
    bs, t, _ = x_sample.shape
    depth = w_mod.shape[0]
    past = cache_a_ckv.shape[2]
    pos_p = jnp.arange(sp)
    pos_s = past + jnp.arange(t)

    mods = _adaln(jnp.concatenate([c_prompt, c_sample], axis=0), w_mod, b_mod)

    def mod(i, k):
        m = mods[i, k][:, None, :]
        return m[:bp], m[bp:]

    xp, xs = x_prompt, x_sample
    states = [[] for _ in range(N_MIXERS)]
    for i in range(depth):
        m, j = i % N_MIXERS, i // N_MIXERS
        (sh_p, sh_s), (sc_p, sc_s), (gm_p, gm_s) = mod(i, 0), mod(i, 1), mod(i, 2)
        (shf_p, shf_s), (scf_p, scf_s), (gf_p, gf_s) = mod(i, 3), mod(i, 4), mod(i, 5)
        g_m = g_mix[i][None, :]
        if m == 0:
            n_down = A_Q_LORA + A_KV_LORA + A_ROPE
            wd = jnp.pad(w_a_down[j], ((0, 0), (0, A_Q_LORA + A_KV_LORA + LANES - n_down))).astype(BF16)
            wq = w_a_uq[j].reshape(A_Q_LORA, A_HEADS, A_NOPE + A_ROPE)
            wq = jnp.pad(wq, ((0, 0), (0, 0), (0, LANES - A_NOPE - A_ROPE)))
            x1 = wq[:, :, A_NOPE:A_NOPE + A_ROPE // 2]
            x2 = wq[:, :, A_NOPE + A_ROPE // 2:A_NOPE + A_ROPE]
            wq_rot = jnp.zeros_like(wq).at[:, :, A_NOPE:A_NOPE + A_ROPE].set(
                jnp.concatenate([-x2, x1], axis=-1))
            wq = wq.reshape(A_Q_LORA, A_HEADS * LANES).astype(BF16)
            wq_rot = wq_rot.reshape(A_Q_LORA, A_HEADS * LANES).astype(BF16)
            wk = jnp.pad(w_a_uk[j], ((0, 0), (0, 0), (0, LANES - A_NOPE)))
            wk = wk.reshape(A_KV_LORA, A_HEADS * LANES).astype(BF16)
            place = jnp.zeros((A_ROPE, A_HEADS, LANES), F32)
            place = place.at[jnp.arange(A_ROPE), :, A_NOPE + jnp.arange(A_ROPE)].set(1.0)
            wr = place.reshape(A_ROPE, A_HEADS * LANES).astype(BF16)
            wv = w_a_uv[j].reshape(A_KV_LORA, A_HEADS * A_V).astype(BF16)
            g_q, g_kv = g_a_q[j][None, :], g_a_kv[j][None, :]

            def project(x, sh, sc, pos):
                q_cos, q_up, q_dn = _tables_for(x.shape[0], x.shape[1], pos, A_ROPE, [A_NOPE])
                kt = _tables_for(x.shape[0], x.shape[1], pos, A_ROPE, [0])
                return _proj_a(x, g_m, sh, sc, wd, g_q, g_kv, wq, wq_rot, (q_cos, q_up - q_dn), kt)

            q_p, ckv_p, kr_p = project(xp, sh_p, sc_p, pos_p)
            q_s, ckv_s, kr_s = project(xs, sh_s, sc_s, pos_s)
            k_p, v_p = _expand_a(ckv_p, kr_p, wk, wr, wv)
            ckv_all, n_all = _with_cache(cache_a_ckv[j], ckv_s, F32, KV_BLOCK)
            kr_all, _ = _with_cache(cache_a_krope[j], kr_s, F32, KV_BLOCK)
            k_s, v_s = _expand_a(ckv_all, kr_all, wk, wr, wv)
            o_p = _causal_attention(q_p, k_p, v_p, span=min(A_KV_BLOCK, sp), sk_valid=sp,
                                    q_off=0, k_off=0)
            o_s = _causal_attention(q_s, k_s, v_s, span=KV_BLOCK, sk_valid=n_all, q_off=past,
                                    k_off=0)
            wo = w_a_o[j].astype(BF16)
            states[0].append((ckv_p, kr_p, ckv_s, kr_s))
        elif m == 1 or m == 3:
            heads = B_HEADS if m == 1 else D_HEADS
            w_qkv = (w_b_qkv if m == 1 else w_d_qkv)[j].astype(BF16)
            cache_k, cache_v = (cache_b_k, cache_b_v) if m == 1 else (cache_d_k, cache_d_v)
            q_scale = HEAD_DIM ** -0.5 * LOG2E
            q_p, kf_p, vf_p, kb_p, vb_p = _proj_qkv(xp, g_m, sh_p, sc_p, w_qkv, heads, q_scale)
            q_s, kf_s, vf_s, kb_s, vb_s = _proj_qkv(xs, g_m, sh_s, sc_s, w_qkv, heads, q_scale)
            lc = cache_k.shape[2]
            block = KV_BLOCK if m == 1 else D_BAND_WIDTH
            k_s, n_all = _with_cache(cache_k[j], kb_s, BF16, block)
            v_s, _ = _with_cache(cache_v[j], vb_s, BF16, block)
            if m == 1:
                o_p = _stick_attention(q_p, kb_p, vb_p, sk_valid=sp, q_off=0, k_off=0)
                o_s = _stick_attention(q_s, k_s, v_s, sk_valid=n_all, q_off=past,
                                       k_off=past - lc)
                wo = w_b_o[j].astype(BF16)
                states[1].append((kf_p, vf_p, kf_s, vf_s))
            else:
                bias = _band_bias(rel_bias_d[j])
                front = D_PREV_CHUNKS * CHUNK
                o_p = _band_attention(q_p, _front_pad(kb_p, front), _front_pad(vb_p, front),
                                      width=D_BAND_WIDTH, sk_valid=sp, q_off=0, k_off=0,
                                      front=front, n_prev=D_PREV_CHUNKS, bias=bias,
                                      pairs_per_step=D_PAIRS_PER_STEP, shared_kv=False)
                o_s = _band_attention(q_s, k_s, v_s, width=D_BAND_WIDTH, sk_valid=n_all,
                                      q_off=past, k_off=past - lc, front=0,
                                      n_prev=D_PREV_CHUNKS, bias=bias[:, :t, :],
                                      pairs_per_step=D_PAIRS_PER_STEP, shared_kv=False)
                wo = w_d_o[j].astype(BF16)
                keep = min(D_PREV_CHUNKS * CHUNK, sp)
                k_roll = jnp.concatenate([cache_k[j].reshape(bs, lc, -1), kf_s], axis=1)[:, t:]
                v_roll = jnp.concatenate([cache_v[j].reshape(bs, lc, -1), vf_s], axis=1)[:, t:]
                states[3].append((kf_p[:, sp - keep:], vf_p[:, sp - keep:], k_roll, v_roll))
        else:
            order = _c_head_order()
            q_width, kv_width = C_HEADS * HEAD_DIM, C_KV_HEADS * HEAD_DIM
            col = np.concatenate([(order[:, None] * HEAD_DIM + np.arange(HEAD_DIM)).reshape(-1),
                                  np.arange(q_width, q_width + 2 * kv_width)])
            w_qkv = w_c_qkv[j][:, col].astype(BF16)
            b_qkv = b_c_qkv[j][col][None, :]
            sink = sink_c[j][order]
            wo = w_c_o[j].reshape(C_HEADS, HEAD_DIM, d)[order].reshape(q_width, d).astype(BF16)
            lanes = [0, HEAD_DIM]

            def project(x, sh, sc, pos):
                tb = _tables_for(x.shape[0], x.shape[1], pos, C_ROT, lanes)
                return _proj_c(x, g_m, sh, sc, w_qkv, b_qkv, tb)

            q_p, kf_p, vf_p, kb_p, vb_p = project(xp, sh_p, sc_p, pos_p)
            q_s, kf_s, vf_s, kb_s, vb_s = project(xs, sh_s, sc_s, pos_s)
            lc = cache_c_k.shape[2]
            k_s, n_all = _with_cache(cache_c_k[j], kb_s, BF16, C_BAND_WIDTH)
            v_s, _ = _with_cache(cache_c_v[j], vb_s, BF16, C_BAND_WIDTH)
            group = C_HEADS // C_KV_HEADS
            front = C_PREV_CHUNKS * CHUNK
            band_mask = jnp.asarray(_band_mask_tile(Q_TILE, C_BAND_WIDTH, C_PREV_CHUNKS))
            o_p = _band_attention(q_p, _front_pad(kb_p, front), _front_pad(vb_p, front),
                                  width=C_BAND_WIDTH, sk_valid=sp, q_off=0, k_off=0, front=front,
                                  n_prev=C_PREV_CHUNKS, pairs_per_step=group, shared_kv=True, sink=sink,
                                  bias=band_mask)
            o_s = _band_attention(q_s, k_s, v_s, width=C_BAND_WIDTH, sk_valid=n_all, q_off=past,
                                  k_off=past - lc, front=0, n_prev=C_PREV_CHUNKS, pairs_per_step=group, shared_kv=True,
                                  sink=sink, bias=band_mask[:t])
            keep = min(C_PREV_CHUNKS * CHUNK, sp)
            k_roll = jnp.concatenate([cache_c_k[j].reshape(bs, lc, -1), kf_s], axis=1)[:, t:]
            v_roll = jnp.concatenate([cache_c_v[j].reshape(bs, lc, -1), vf_s], axis=1)[:, t:]
            states[2].append((kf_p[:, sp - keep:], vf_p[:, sp - keep:], k_roll, v_roll))

        g_f = g_ffn[i][None, :]
        w_in, w_out = w_ffn_in[i].astype(BF16), w_ffn_out[i].astype(BF16)
        g_out = g_final[None, :] if i == depth - 1 else None
        xp = _block(xp, o_p, wo, gm_p, g_f, shf_p, scf_p, gf_p, w_in, w_out, g_out)
        xs = _block(xs, o_s, wo, gm_s, g_f, shf_s, scf_s, gf_s, w_in, w_out, g_out)

    y_p, y_s = xp, xs

    def stacked(entries, n_heads=None):
        outs = []
        for parts in zip(*entries):
            a = jnp.stack(parts, axis=0)
            if n_heads is not None:
                a = a.reshape(a.shape[:3] + (n_heads, HEAD_DIM))
            outs.append(a)
        return tuple(outs)

    return ((y_p, y_s) + stacked(states[0]) + stacked(states[1], B_HEADS)
            + stacked(states[2], C_KV_HEADS) + stacked(states[3], D_HEADS))
```

```python
import functools
import math

import numpy as np
import jax
import jax.numpy as jnp
from jax import lax
from jax.experimental import pallas as pl
from jax.experimental.pallas import tpu as pltpu

F32 = jnp.float32
BF16 = jnp.bfloat16

CHUNK = 64
HEAD_DIM = 64
ROPE_THETA = 500000.0
NORM_EPS = 1e-6
N_MIXERS = 4
A_HEADS, A_Q_LORA, A_KV_LORA, A_NOPE, A_ROPE, A_V = 16, 384, 256, 64, 32, 64
B_HEADS = 16
C_HEADS, C_KV_HEADS, C_WINDOW, C_ROT = 16, 4, 128, 16
D_HEADS, D_PREV_CHUNKS, D_REL_CLIP = 16, 8, 128
C_PREV_CHUNKS = C_WINDOW // CHUNK

LANES = 128
V7X_VMEM_LIMIT = 56 * 1024 * 1024

TOKEN_TILE = 512
Q_TILE = 256
KV_BLOCK = 256
A_KV_BLOCK = 512
C_BAND_WIDTH = C_PREV_CHUNKS * CHUNK + Q_TILE
D_BAND_WIDTH = D_PREV_CHUNKS * CHUNK + Q_TILE
A_Q_TILE = 256
A_PAIRS_PER_STEP = 2
A_PV_CHUNK = 512
D_PAIRS_PER_STEP = 2
MASK_VALUE = -1e30
LOG2E = math.log2(math.e)
STICK_DEAD_BITS = 150.0


def _cparams(n_axes):
    return pltpu.CompilerParams(
        dimension_semantics=("parallel",) * n_axes, vmem_limit_bytes=V7X_VMEM_LIMIT)


def _adaln_kernel(c_ref, w_ref, b_ref, o_ref):
    c = c_ref[...]
    a = (c * jax.nn.sigmoid(c)).astype(BF16)
    y = jnp.dot(a, w_ref[0].astype(BF16), preferred_element_type=F32) + b_ref[0]
    o_ref[0, 0] = y


def _adaln(c_all, w_mod, b_mod):
    depth, d, d6 = w_mod.shape
    n = c_all.shape[0]
    return pl.pallas_call(
        _adaln_kernel,
        grid=(depth, d6 // d),
        in_specs=[
            pl.BlockSpec((n, d), lambda i, k: (0, 0)),
            pl.BlockSpec((1, d, d), lambda i, k: (i, 0, k)),
            pl.BlockSpec((1, 1, d), lambda i, k: (i, 0, k)),
        ],
        out_specs=pl.BlockSpec((1, 1, n, d), lambda i, k: (i, k, 0, 0)),
        out_shape=jax.ShapeDtypeStruct((depth, d6 // d, n, d), F32),
        compiler_params=_cparams(2),
    )(c_all, w_mod, b_mod.reshape(depth, 1, d6))


def _token_tiling(bx, sx):
    if sx >= TOKEN_TILE:
        ts = TOKEN_TILE
        while sx % ts:
            ts //= 2
        return 1, ts
    nb = min(bx, TOKEN_TILE // sx)
    while bx % nb:
        nb -= 1
    return nb, sx


def _tok_spec(nb, ts, width):
    return pl.BlockSpec((nb, ts, width), lambda b, s: (b, s, 0))


def _mod_spec(nb, d):
    return pl.BlockSpec((nb, 1, d), lambda b, s: (b, 0, 0))


def _const_spec(shape):
    nd = len(shape)
    return pl.BlockSpec(shape, lambda b, s: (0,) * nd)


def _table_spec(rows):
    return pl.BlockSpec((1, rows, LANES), lambda b, s: (s, 0, 0))


def _rms(x):
    return x * lax.rsqrt(jnp.mean(x * x, axis=-1, keepdims=True) + NORM_EPS)


def _modulated(x_ref, g_ref, shift_ref, scale_ref):
    y = _rms(x_ref[...]) * g_ref[...]
    h = y * (1.0 + scale_ref[...]) + shift_ref[...]
    return h.reshape(-1, h.shape[-1]).astype(BF16)


def _rope_lanes(x, cos_t, sin_up, sin_dn, half):
    return (x * cos_t + pltpu.roll(x, half, 1) * sin_up
            + pltpu.roll(x, LANES - half, 1) * sin_dn)


def _store_tok(ref, lo, val):
    nb, ts = ref.shape[0], ref.shape[1]
    w = val.shape[-1]
    ref[:, :, lo:lo + w] = val.reshape(nb, ts, w).astype(ref.dtype)


def _proj_qkv_kernel(x_ref, g_ref, sh_ref, sc_ref, w_ref, q_ref, kf_ref, vf_ref, kb_ref, vb_ref,
                     *, width, q_scale):
    h = _modulated(x_ref, g_ref, sh_ref, sc_ref)
    q = jnp.dot(h, w_ref[:, 0:width], preferred_element_type=F32)
    _store_tok(q_ref, 0, q * q_scale)
    k = jnp.dot(h, w_ref[:, width:2 * width], preferred_element_type=F32)
    _store_tok(kf_ref, 0, k)
    _store_tok(kb_ref, 0, k)
    v = jnp.dot(h, w_ref[:, 2 * width:3 * width], preferred_element_type=F32)
    _store_tok(vf_ref, 0, v)
    _store_tok(vb_ref, 0, v)


def _proj_qkv(x, g, shift, scale, w_bf16, n_heads, q_scale):
    bx, sx, d = x.shape
    width = n_heads * HEAD_DIM
    nb, ts = _token_tiling(bx, sx)
    out = lambda dt: jax.ShapeDtypeStruct((bx, sx, width), dt)
    return pl.pallas_call(
        functools.partial(_proj_qkv_kernel, width=width, q_scale=q_scale),
        grid=(bx // nb, sx // ts),
        in_specs=[_tok_spec(nb, ts, d), _const_spec((1, d)), _mod_spec(nb, d), _mod_spec(nb, d),
                  _const_spec(w_bf16.shape)],
        out_specs=[_tok_spec(nb, ts, width)] * 5,
        out_shape=[out(BF16), out(F32), out(F32), out(BF16), out(BF16)],
        compiler_params=_cparams(2),
    )(x, g, shift, scale, w_bf16)


def _proj_c_kernel(x_ref, g_ref, sh_ref, sc_ref, w_ref, b_ref, tc_ref, tu_ref, td_ref,
                   q_ref, kf_ref, vf_ref, kb_ref, vb_ref, *, q_width, kv_width, q_scale):
    h = _modulated(x_ref, g_ref, sh_ref, sc_ref)
    qkv = jnp.dot(h, w_ref[...], preferred_element_type=F32) + b_ref[...]
    cos_t, sin_up, sin_dn = tc_ref[0], tu_ref[0], td_ref[0]
    half = C_ROT // 2
    for j in range(q_width // LANES):
        xg = qkv[:, j * LANES:(j + 1) * LANES]
        _store_tok(q_ref, j * LANES, _rope_lanes(xg, cos_t, sin_up, sin_dn, half) * q_scale)
    for j in range(kv_width // LANES):
        lo = q_width + j * LANES
        kg = _rope_lanes(qkv[:, lo:lo + LANES], cos_t, sin_up, sin_dn, half)
        _store_tok(kf_ref, j * LANES, kg)
        _store_tok(kb_ref, j * LANES, kg)
    v = qkv[:, q_width + kv_width:q_width + 2 * kv_width]
    _store_tok(vf_ref, 0, v)
    _store_tok(vb_ref, 0, v)


def _proj_c(x, g, shift, scale, w_bf16, bias, tables):
    bx, sx, d = x.shape
    q_width, kv_width = C_HEADS * HEAD_DIM, C_KV_HEADS * HEAD_DIM
    nb, ts = _token_tiling(bx, sx)
    out = lambda w, dt: jax.ShapeDtypeStruct((bx, sx, w), dt)
    return pl.pallas_call(
        functools.partial(_proj_c_kernel, q_width=q_width, kv_width=kv_width,
                          q_scale=HEAD_DIM ** -0.5 * LOG2E),
        grid=(bx // nb, sx // ts),
        in_specs=[_tok_spec(nb, ts, d), _const_spec((1, d)), _mod_spec(nb, d), _mod_spec(nb, d),
                  _const_spec(w_bf16.shape), _const_spec(bias.shape)] + [_table_spec(nb * ts)] * 3,
        out_specs=[_tok_spec(nb, ts, q_width)] + [_tok_spec(nb, ts, kv_width)] * 4,
        out_shape=[out(q_width, BF16), out(kv_width, F32), out(kv_width, F32),
                   out(kv_width, BF16), out(kv_width, BF16)],
        compiler_params=_cparams(2),
    )(x, g, shift, scale, w_bf16, bias, *tables)


def _proj_a_kernel(x_ref, g_ref, sh_ref, sc_ref, wd_ref, gq_ref, gkv_ref, wq_ref, wqr_ref,
                   qc_ref, qs_ref, kc_ref, ku_ref, kd_ref,
                   q_ref, ckv_ref, kr_ref, *, q_scale):
    h = _modulated(x_ref, g_ref, sh_ref, sc_ref)
    down = jnp.dot(h, wd_ref[...], preferred_element_type=F32)
    cq = (_rms(down[:, :A_Q_LORA]) * gq_ref[...]).astype(BF16)
    ckv = _rms(down[:, A_Q_LORA:A_Q_LORA + A_KV_LORA]) * gkv_ref[...]
    _store_tok(ckv_ref, 0, ckv)
    lo = A_Q_LORA + A_KV_LORA
    half = A_ROPE // 2
    kr = _rope_lanes(down[:, lo:lo + LANES], kc_ref[0], ku_ref[0], kd_ref[0], half)
    _store_tok(kr_ref, 0, kr[:, :A_ROPE])
    q = jnp.dot(cq, wq_ref[...], preferred_element_type=F32)
    q_rot = jnp.dot(cq, wqr_ref[...], preferred_element_type=F32)
    qc, qs = qc_ref[0], qs_ref[0]
    for j in range(A_HEADS):
        lanes = slice(j * LANES, (j + 1) * LANES)
        _store_tok(q_ref, j * LANES, (q[:, lanes] * qc + q_rot[:, lanes] * qs) * q_scale)


def _proj_a(x, g, shift, scale, wd_bf16, g_q, g_kv, wq_bf16, wq_rot_bf16, q_tables, k_tables):
    bx, sx, d = x.shape
    nb, ts = _token_tiling(bx, sx)
    return pl.pallas_call(
        functools.partial(_proj_a_kernel, q_scale=(A_NOPE + A_ROPE) ** -0.5 * LOG2E),
        grid=(bx // nb, sx // ts),
        in_specs=[_tok_spec(nb, ts, d), _const_spec((1, d)), _mod_spec(nb, d), _mod_spec(nb, d),
                  _const_spec(wd_bf16.shape), _const_spec(g_q.shape), _const_spec(g_kv.shape),
                  _const_spec(wq_bf16.shape), _const_spec(wq_rot_bf16.shape)]
                 + [_table_spec(nb * ts)] * 5,
        out_specs=[_tok_spec(nb, ts, A_HEADS * LANES), _tok_spec(nb, ts, A_KV_LORA),
                   _tok_spec(nb, ts, A_ROPE)],
        out_shape=[jax.ShapeDtypeStruct((bx, sx, A_HEADS * LANES), BF16),
                   jax.ShapeDtypeStruct((bx, sx, A_KV_LORA), F32),
                   jax.ShapeDtypeStruct((bx, sx, A_ROPE), F32)],
        compiler_params=_cparams(2),
    )(x, g, shift, scale, wd_bf16, g_q, g_kv, wq_bf16, wq_rot_bf16, *q_tables, *k_tables)


def _expand_a_kernel(ckv_ref, kr_ref, wk_ref, wr_ref, wv_ref, k_ref, v_ref):
    ckv = ckv_ref[...]
    ckv = ckv.reshape(-1, ckv.shape[-1]).astype(BF16)
    kr = kr_ref[...]
    kr = kr.reshape(-1, kr.shape[-1]).astype(BF16)
    k = (jnp.dot(ckv, wk_ref[...], preferred_element_type=F32)
         + jnp.dot(kr, wr_ref[...], preferred_element_type=F32))
    _store_tok(k_ref, 0, k)
    _store_tok(v_ref, 0, jnp.dot(ckv, wv_ref[...], preferred_element_type=F32))


def _expand_a(ckv, kr, wk, wr, wv):
    bx, sx, _ = ckv.shape
    nb, ts = _token_tiling(bx, sx)
    return pl.pallas_call(
        _expand_a_kernel,
        grid=(bx // nb, sx // ts),
        in_specs=[_tok_spec(nb, ts, A_KV_LORA), _tok_spec(nb, ts, A_ROPE),
                  _const_spec(wk.shape), _const_spec(wr.shape), _const_spec(wv.shape)],
        out_specs=[_tok_spec(nb, ts, A_HEADS * LANES), _tok_spec(nb, ts, A_HEADS * A_V)],
        out_shape=[jax.ShapeDtypeStruct((bx, sx, A_HEADS * LANES), BF16),
                   jax.ShapeDtypeStruct((bx, sx, A_HEADS * A_V), BF16)],
        compiler_params=_cparams(2),
    )(ckv, kr, wk, wr, wv)


def _block_kernel(x_ref, o_ref, wo_ref, gm_ref, g_ref, sh_ref, sc_ref, gf_ref, wi_ref, wout_ref,
                  *out_refs, hidden, chunk, final_norm):
    gout_ref, y_ref = out_refs if final_norm else (None,) + out_refs
    nb, ts, d = x_ref.shape
    o = o_ref[...].reshape(nb * ts, -1)
    mix = jnp.dot(o, wo_ref[...], preferred_element_type=F32).reshape(nb, ts, d)
    x1 = x_ref[...] + gm_ref[...] * mix
    h = (_rms(x1) * g_ref[...]) * (1.0 + sc_ref[...]) + sh_ref[...]
    h = h.reshape(nb * ts, d).astype(BF16)
    acc = jnp.zeros((nb * ts, d), F32)
    for c in range(hidden // chunk):
        gate = jnp.dot(h, wi_ref[:, c * chunk:(c + 1) * chunk], preferred_element_type=F32)
        up = jnp.dot(h, wi_ref[:, hidden + c * chunk:hidden + (c + 1) * chunk],
                     preferred_element_type=F32)
        act = (gate * jax.nn.sigmoid(gate) * up).astype(BF16)
        acc = acc + jnp.dot(act, wout_ref[c * chunk:(c + 1) * chunk, :],
                            preferred_element_type=F32)
    x2 = x1 + gf_ref[...] * acc.reshape(nb, ts, d)
    y_ref[...] = _rms(x2) * gout_ref[...] if final_norm else x2


def _block(x, o, wo, gate_m, g_ffn, shift_f, scale_f, gate_f, w_in, w_out, g_out=None):
    bx, sx, d = x.shape
    hidden = w_out.shape[0]
    nb, ts = _token_tiling(bx, sx)
    resident = lambda shape: pl.BlockSpec(shape, lambda b, s: (0,) * len(shape),
                                          pipeline_mode=pl.Buffered(1))
    in_specs = [_tok_spec(nb, ts, d), _tok_spec(nb, ts, o.shape[-1]), resident(wo.shape),
                _mod_spec(nb, d), _const_spec((1, d)), _mod_spec(nb, d), _mod_spec(nb, d),
                _mod_spec(nb, d), resident(w_in.shape), resident(w_out.shape)]
    args = [x, o, wo, gate_m, g_ffn, shift_f, scale_f, gate_f, w_in, w_out]
    if g_out is not None:
        in_specs.append(_const_spec((1, d)))
        args.append(g_out)
    return pl.pallas_call(
        functools.partial(_block_kernel, hidden=hidden, chunk=256, final_norm=g_out is not None),
        grid=(bx // nb, sx // ts),
        in_specs=in_specs,
        out_specs=_tok_spec(nb, ts, d),
        out_shape=jax.ShapeDtypeStruct((bx, sx, d), F32),
        compiler_params=_cparams(2),
    )(*args)


def _stick_kernel(q_ref, k_ref, v_ref, tri_ref, o_ref, run_ref, acc_ref,
                  *, tq, bk, sk_valid, q_off, k_off):
    qpos0 = pl.program_id(2) * tq + q_off
    kb_hi = (jnp.minimum(qpos0 + (tq - 1) - k_off, sk_valid) + bk - 1) // bk
    f_hi = jnp.minimum(jnp.minimum(qpos0 - k_off, sk_valid) // bk, kb_hi)

    def visible(kb):
        row = lax.broadcasted_iota(jnp.int32, (2 * tq, 1), 0)
        qpos = jnp.where(row < tq, row, row - tq) + qpos0
        ik = kb * bk + lax.broadcasted_iota(jnp.int32, (1, bk), 1)
        return (ik + k_off < qpos) & (ik < sk_valid)

    lane_half = lax.broadcasted_iota(jnp.int32, (1, LANES), 1) // HEAD_DIM
    q_pair = q_ref[0]
    q_both = jnp.concatenate(
        [jnp.where(lane_half == hh, q_pair, jnp.zeros((), BF16)) for hh in range(2)], axis=0)

    def rows(kb):
        return pl.ds(pl.multiple_of(kb * bk, bk), bk)

    acc_ref[...] = jnp.zeros_like(acc_ref)
    run_ref[...] = jnp.zeros_like(run_ref)

    def block(kb, masked):
        y = lax.dot_general(q_both, k_ref[0, rows(kb), :], (((1,), (1,)), ((), ())),
                            preferred_element_type=F32)
        sp = jnp.maximum(y, 0.0) + jnp.log2(1.0 + jnp.exp2(-jnp.abs(y)))
        if masked:
            vis = visible(kb)
            sp = jnp.where(vis, sp, 0.0)
        run = run_ref[...]
        run_new = run + jnp.sum(sp, axis=1, keepdims=True)
        run_ref[...] = run_new
        hi = sp.astype(BF16)
        lo = (sp - hi.astype(F32)).astype(BF16)
        both = jnp.dot(jnp.concatenate([hi, lo], axis=0), tri_ref[...],
                       preferred_element_type=F32)
        suffix = both[:2 * tq] + both[2 * tq:]
        a = jnp.exp2(y - suffix - run)
        if masked:
            a = jnp.where(vis, a, 0.0)
        acc_ref[...] += jnp.dot(a.astype(BF16), v_ref[0, rows(kb), :],
                                preferred_element_type=F32)
        return jnp.min(run_new)

    least = lax.fori_loop(0, kb_hi - f_hi, lambda i, c: block(kb_hi - 1 - i, True),
                          jnp.zeros((), F32))
    lax.while_loop(lambda st: (st[0] < f_hi) & (st[1] < STICK_DEAD_BITS),
                   lambda st: (st[0] + 1, block(f_hi - 1 - st[0], False)),
                   (jnp.zeros((), jnp.int32), least))
    o_ref[0] = jnp.where(lane_half == 0, acc_ref[:tq], acc_ref[tq:]).astype(o_ref.dtype)


def _stick_attention(q, k, v, *, sk_valid, q_off, k_off):
    bx, sq, qw = q.shape
    rows = k.shape[1]
    n_pairs = qw // LANES
    tq = min(Q_TILE, sq)
    bk = KV_BLOCK
    assert sq % tq == 0 and rows % bk == 0 and tq % 8 == 0 and q_off >= k_off >= 0
    idx = np.arange(bk)
    return pl.pallas_call(
        functools.partial(_stick_kernel, tq=tq, bk=bk, sk_valid=sk_valid, q_off=q_off,
                          k_off=k_off),
        grid=(bx, n_pairs, sq // tq),
        in_specs=[
            pl.BlockSpec((1, tq, LANES), lambda b, p, i: (b, i, p)),
            pl.BlockSpec((1, rows, LANES), lambda b, p, i: (b, 0, p)),
            pl.BlockSpec((1, rows, LANES), lambda b, p, i: (b, 0, p)),
            pl.BlockSpec((bk, bk), lambda b, p, i: (0, 0)),
        ],
        out_specs=pl.BlockSpec((1, tq, LANES), lambda b, p, i: (b, i, p)),
        out_shape=jax.ShapeDtypeStruct((bx, sq, n_pairs * LANES), BF16),
        scratch_shapes=[pltpu.VMEM((2 * tq, 1), F32), pltpu.VMEM((2 * tq, LANES), F32)],
        compiler_params=_cparams(3),
    )(q, k, v, jnp.asarray(idx[:, None] >= idx[None, :], BF16))


def _row_end(qpos, k_off):
    return (qpos // CHUNK + 1) * CHUNK - k_off


def _causal_kernel(q_ref, k_ref, v_ref, o_ref, *, tq, span, n_spans, sk_valid, q_off, k_off,
                   single_tile):
    qpos0 = pl.program_id(2) * tq + q_off
    n_needed = (jnp.minimum(_row_end(qpos0 + tq - 1, k_off), sk_valid) - 1) // span
    shift = int(math.log2(CHUNK))
    lane_half = lax.broadcasted_iota(jnp.int32, (1, LANES), 1) // HEAD_DIM
    nt = (((1,), (1,)), ((), ()))

    for n_full in range(n_spans):
        if single_tile and n_full != (min(_row_end(q_off + tq - 1, k_off), sk_valid) - 1) // span:
            continue

        @pl.when(n_needed == n_full)
        def _(n_full=n_full):
            full, width = n_full * span, (n_full + 1) * span
            qpos = lax.broadcasted_iota(jnp.int32, (tq, 1), 0) + qpos0
            kidx = lax.broadcasted_iota(jnp.int32, (1, span), 1) + full
            vis = (kidx < sk_valid) & (jnp.right_shift(kidx + k_off, shift)
                                       <= jnp.right_shift(qpos, shift))
            outs = []
            for hh in range(2 * (q_ref.shape[-1] // (2 * LANES))):
                lanes = slice(hh * LANES, (hh + 1) * LANES)
                v_lanes = slice(hh // 2 * LANES, (hh // 2 + 1) * LANES)
                q_h = q_ref[0, :, lanes]
                s_tail = lax.dot_general(q_h, k_ref[0, full:width, lanes], nt,
                                         preferred_element_type=F32)
                s_tail = jnp.where(vis, s_tail, MASK_VALUE)
                m = jnp.max(s_tail, axis=1, keepdims=True)
                if n_full:
                    s_full = lax.dot_general(q_h, k_ref[0, 0:full, lanes], nt,
                                             preferred_element_type=F32)
                    m = jnp.maximum(m, jnp.max(s_full, axis=1, keepdims=True))
                p_tail = jnp.exp2(s_tail - m)
                l = jnp.sum(p_tail, axis=1, keepdims=True)
                o = jnp.dot(p_tail.astype(BF16), v_ref[0, full:width, v_lanes],
                            preferred_element_type=F32)
                for c0 in range(0, full, A_PV_CHUNK):
                    p_c = jnp.exp2(s_full[:, c0:c0 + A_PV_CHUNK] - m)
                    l = l + jnp.sum(p_c, axis=1, keepdims=True)
                    o = o + jnp.dot(p_c.astype(BF16), v_ref[0, c0:c0 + A_PV_CHUNK, v_lanes],
                                    preferred_element_type=F32)
                outs.append(o / l)
            for pr in range(len(outs) // 2):
                o_ref[0, :, pr * LANES:(pr + 1) * LANES] = jnp.where(
                    lane_half == 0, outs[2 * pr], outs[2 * pr + 1]).astype(o_ref.dtype)


def _causal_attention(q, k, v, *, span, sk_valid, q_off, k_off):
    bx, sq, qw = q.shape
    rows = k.shape[1]
    pps = A_PAIRS_PER_STEP
    n_pairs = qw // (2 * LANES)
    assert n_pairs % pps == 0
    tq = min(A_Q_TILE, sq)
    assert sq % tq == 0 and rows % span == 0 and q_off % CHUNK == 0 and k_off >= 0
    for q0 in range(0, sq, tq):
        end = min(_row_end(q_off + q0 + tq - 1, k_off), sk_valid)
        first_row_end = min(_row_end(q_off + q0, k_off), sk_valid)
        assert 0 < end <= rows and (end - 1) // span * span <= first_row_end
    return pl.pallas_call(
        functools.partial(_causal_kernel, tq=tq, span=span, n_spans=rows // span,
                          sk_valid=sk_valid, q_off=q_off, k_off=k_off, single_tile=sq == tq),
        grid=(bx, n_pairs // pps, sq // tq),
        in_specs=[
            pl.BlockSpec((1, tq, pps * 2 * LANES), lambda b, p, i: (b, i, p)),
            pl.BlockSpec((1, rows, pps * 2 * LANES), lambda b, p, i: (b, 0, p)),
            pl.BlockSpec((1, rows, pps * LANES), lambda b, p, i: (b, 0, p)),
        ],
        out_specs=pl.BlockSpec((1, tq, pps * LANES), lambda b, p, i: (b, i, p)),
        out_shape=jax.ShapeDtypeStruct((bx, sq, n_pairs * LANES), BF16),
        compiler_params=_cparams(3),
    )(q, k, v)


def _band_kernel(*refs, tq, width, sk_valid, q_off, k_off, front, n_prev, use_sink, per_head_bias):
    refs = list(refs)
    q_ref, k_ref, v_ref = refs[:3]
    rest = refs[3:]
    sink_ref = rest.pop(0) if use_sink else None
    bias_ref, o_ref = rest

    group = q_ref.shape[-1] // LANES
    first_pair = pl.program_id(1) * group
    qpos0 = pl.program_id(2) * tq + q_off
    start = pl.multiple_of((qpos0 // CHUNK - n_prev) * CHUNK - k_off + front, CHUNK)
    band = pl.ds(start, width)
    kidx = lax.broadcasted_iota(jnp.int32, (1, width), 1) + (start - front)
    invalid = jnp.where((kidx >= 0) & (kidx < sk_valid), 0.0, MASK_VALUE)

    lane_half = lax.broadcasted_iota(jnp.int32, (1, LANES), 1) // HEAD_DIM
    shared_kv = k_ref.shape[-1] == LANES
    for j in range(group):
        q_pair = q_ref[0, :, j * LANES:(j + 1) * LANES]
        kv_lanes = slice(0, LANES) if shared_kv else slice(j * LANES, (j + 1) * LANES)
        k_band = k_ref[0, band, kv_lanes]
        v_band = v_ref[0, band, kv_lanes]
        outs = []
        for hh in range(2):
            q_h = jnp.where(lane_half == hh, q_pair, jnp.zeros((), BF16))
            s = lax.dot_general(q_h, k_band, (((1,), (1,)), ((), ())),
                                preferred_element_type=F32)
            s = s + (bias_ref[2 * j + hh] if per_head_bias else bias_ref[...]) + invalid
            m = jnp.max(s, axis=1, keepdims=True)
            if use_sink:
                sink = sink_ref[2 * (first_pair + j) + hh] * LOG2E
                m = jnp.maximum(m, sink)
            p = jnp.exp2(s - m)
            l = jnp.sum(p, axis=1, keepdims=True)
            if use_sink:
                l = l + jnp.exp2(sink - m)
            outs.append(jnp.dot(p.astype(BF16), v_band, preferred_element_type=F32) / l)
        o_ref[0, :, j * LANES:(j + 1) * LANES] = jnp.where(
            lane_half == 0, outs[0], outs[1]).astype(o_ref.dtype)


def _band_mask_tile(rows, width, n_prev):
    r_chunk = np.arange(rows)[:, None] // CHUNK
    c_chunk = np.arange(width)[None, :] // CHUNK - n_prev
    return np.where((c_chunk <= r_chunk) & (c_chunk >= r_chunk - n_prev), 0.0, MASK_VALUE
                    ).astype(np.float32)


def _band_attention(q, k, v, *, width, sk_valid, q_off, k_off, front, n_prev, pairs_per_step,
                    shared_kv, sink=None, bias=None):
    kv_group = pairs_per_step
    kv_lanes = LANES if shared_kv else pairs_per_step * LANES
    bx, sq, qw = q.shape
    rows = k.shape[1]
    n_pairs = qw // LANES
    tq = min(Q_TILE, sq)
    assert sq % tq == 0 and q_off % CHUNK == 0 and (tq % CHUNK == 0 or sq == tq)
    assert ((q_off + sq - tq) // CHUNK - n_prev) * CHUNK - k_off + front + width <= rows
    assert (q_off // CHUNK - n_prev) * CHUNK - k_off + front >= 0
    in_specs = [
        pl.BlockSpec((1, tq, kv_group * LANES), lambda b, g, i: (b, i, g)),
        pl.BlockSpec((1, rows, kv_lanes), lambda b, g, i: (b, 0, g)),
        pl.BlockSpec((1, rows, kv_lanes), lambda b, g, i: (b, 0, g)),
    ]
    args = [q, k, v]
    if sink is not None:
        in_specs.append(pl.BlockSpec(memory_space=pltpu.SMEM))
        args.append(sink)
    per_head_bias = bias.ndim == 3
    if per_head_bias:
        in_specs.append(pl.BlockSpec((2 * kv_group, tq, width), lambda b, g, i: (g, 0, 0)))
    else:
        in_specs.append(pl.BlockSpec((tq, width), lambda b, g, i: (0, 0)))
    args.append(bias)
    return pl.pallas_call(
        functools.partial(_band_kernel, tq=tq, width=width, sk_valid=sk_valid, q_off=q_off,
                          k_off=k_off, front=front, n_prev=n_prev,
                          use_sink=sink is not None, per_head_bias=per_head_bias),
        grid=(bx, n_pairs // kv_group, sq // tq),
        in_specs=in_specs,
        out_specs=pl.BlockSpec((1, tq, kv_group * LANES), lambda b, g, i: (b, i, g)),
        out_shape=jax.ShapeDtypeStruct((bx, sq, n_pairs * LANES), BF16),
        compiler_params=_cparams(3),
    )(*args)


def _band_bias_kernel(e_ref, mask_ref, o_ref, *, width):
    w = e_ref.shape[-1]
    x = jnp.broadcast_to(e_ref[0], (Q_TILE, w))
    toeplitz = pltpu.roll(x, 0, 1, stride=1, stride_axis=0)
    o_ref[0] = toeplitz[:, :width] * LOG2E + mask_ref[...]


def _band_bias(rel_bias):
    heads = rel_bias.shape[0]
    clip = (rel_bias.shape[1] - 1) // 2
    band = D_PREV_CHUNKS * CHUNK
    width = D_BAND_WIDTH
    w = width + Q_TILE
    assert band >= clip and width > band + clip
    top = jnp.broadcast_to(rel_bias[:, -1:], (heads, band - clip + 1))
    mid = jnp.flip(rel_bias[:, :2 * clip], axis=1)
    low = jnp.broadcast_to(rel_bias[:, :1], (heads, width - (band + clip + 1)))
    neg = jnp.broadcast_to(rel_bias[:, -1:], (heads, w - width))
    e = jnp.concatenate([top, mid, low, neg], axis=1)[:, None, :]
    return pl.pallas_call(
        functools.partial(_band_bias_kernel, width=width),
        grid=(heads,),
        in_specs=[pl.BlockSpec((1, 1, w), lambda h: (h, 0, 0)),
                  pl.BlockSpec((Q_TILE, width), lambda h: (0, 0))],
        out_specs=pl.BlockSpec((1, Q_TILE, width), lambda h: (h, 0, 0)),
        out_shape=jax.ShapeDtypeStruct((heads, Q_TILE, width), F32),
        compiler_params=_cparams(1),
    )(e, jnp.asarray(_band_mask_tile(Q_TILE, width, D_PREV_CHUNKS)))


def _rope_tables(pos, n_rot, lane_offsets, rows_repeat=1):
    half = n_rot // 2
    inv = ROPE_THETA ** (-jnp.arange(half, dtype=F32) * 2.0 / n_rot)
    ang = pos.astype(F32)[:, None] * inv[None, :]
    cos, sin = jnp.cos(ang), jnp.sin(ang)
    n = pos.shape[0]
    cos_t = jnp.ones((n, LANES), F32)
    sin_up = jnp.zeros((n, LANES), F32)
    sin_dn = jnp.zeros((n, LANES), F32)
    for o in lane_offsets:
        cos_t = cos_t.at[:, o:o + half].set(cos).at[:, o + half:o + n_rot].set(cos)
        sin_dn = sin_dn.at[:, o:o + half].set(-sin)
        sin_up = sin_up.at[:, o + half:o + n_rot].set(sin)
    def tiles(t):
        if rows_repeat > 1:
            return jnp.tile(t, (rows_repeat, 1))[None]
        rows = min(n, TOKEN_TILE)
        return t.reshape(n // rows, rows, LANES)
    return tiles(cos_t), tiles(sin_up), tiles(sin_dn)


def _tables_for(bx, sx, pos, n_rot, lane_offsets):
    nb, ts = _token_tiling(bx, sx)
    return _rope_tables(pos, n_rot, lane_offsets, rows_repeat=nb if nb > 1 else 1)


def _c_head_order():
    rep = C_HEADS // C_KV_HEADS
    order = []
    for p in range(C_HEADS // 2):
        g2, i = divmod(p, rep)
        order += [rep * (2 * g2) + i, rep * (2 * g2 + 1) + i]
    return np.asarray(order)


def _pad_rows(a, rows):
    return jnp.pad(a, ((0, 0), (0, rows - a.shape[1]), (0, 0)))


def _front_pad(a, rows):
    return jnp.pad(a, ((0, 0), (rows, 0), (0, 0)))


def _with_cache(cache, new, dtype, block=KV_BLOCK):
    full = jnp.concatenate([cache.reshape(cache.shape[0], cache.shape[1], -1).astype(dtype),
                            new.astype(dtype)], axis=1)
    rows = -(-full.shape[1] // block) * block
    return _pad_rows(full, rows), full.shape[1]


def kernel(x_prompt, x_sample, c_prompt, c_sample, cache_a_ckv, cache_a_krope, cache_b_k, cache_b_v,
           cache_c_k, cache_c_v, cache_d_k, cache_d_v, w_mod, b_mod, g_mix, g_ffn, w_ffn_in, w_ffn_out,
           w_a_down, g_a_q, g_a_kv, w_a_uq, w_a_uk, w_a_uv, w_a_o, w_b_qkv, w_b_o,
           w_c_qkv, b_c_qkv, sink_c, w_c_o, w_d_qkv, rel_bias_d, w_d_o, g_final):
    bp, sp, d = x_prompt.shape
    bs, t, _ = x_sample.shape
    depth = w_mod.shape[0]
    past = cache_a_ckv.shape[2]
    pos_p = jnp.arange(sp)
    pos_s = past + jnp.arange(t)

    mods = _adaln(jnp.concatenate([c_prompt, c_sample], axis=0), w_mod, b_mod)

    def mod(i, k):
        m = mods[i, k][:, None, :]
        return m[:bp], m[bp:]

    xp, xs = x_prompt, x_sample
    states = [[] for _ in range(N_MIXERS)]
    for i in range(depth):
        m, j = i % N_MIXERS, i // N_MIXERS
        (sh_p, sh_s), (sc_p, sc_s), (gm_p, gm_s) = mod(i, 0), mod(i, 1), mod(i, 2)
        (shf_p, shf_s), (scf_p, scf_s), (gf_p, gf_s) = mod(i, 3), mod(i, 4), mod(i, 5)
        g_m = g_mix[i][None, :]
        if m == 0:
            n_down = A_Q_LORA + A_KV_LORA + A_ROPE
            wd = jnp.pad(w_a_down[j], ((0, 0), (0, A_Q_LORA + A_KV_LORA + LANES - n_down))).astype(BF16)
            wq = w_a_uq[j].reshape(A_Q_LORA, A_HEADS, A_NOPE + A_ROPE)
            wq = jnp.pad(wq, ((0, 0), (0, 0), (0, LANES - A_NOPE - A_ROPE)))
            x1 = wq[:, :, A_NOPE:A_NOPE + A_ROPE // 2]
            x2 = wq[:, :, A_NOPE + A_ROPE // 2:A_NOPE + A_ROPE]
            wq_rot = jnp.zeros_like(wq).at[:, :, A_NOPE:A_NOPE + A_ROPE].set(
                jnp.concatenate([-x2, x1], axis=-1))
            wq = wq.reshape(A_Q_LORA, A_HEADS * LANES).astype(BF16)
            wq_rot = wq_rot.reshape(A_Q_LORA, A_HEADS * LANES).astype(BF16)
            wk = jnp.pad(w_a_uk[j], ((0, 0), (0, 0), (0, LANES - A_NOPE)))
            wk = wk.reshape(A_KV_LORA, A_HEADS * LANES).astype(BF16)
            place = jnp.zeros((A_ROPE, A_HEADS, LANES), F32)
            place = place.at[jnp.arange(A_ROPE), :, A_NOPE + jnp.arange(A_ROPE)].set(1.0)
            wr = place.reshape(A_ROPE, A_HEADS * LANES).astype(BF16)
            wv = w_a_uv[j].reshape(A_KV_LORA, A_HEADS * A_V).astype(BF16)
            g_q, g_kv = g_a_q[j][None, :], g_a_kv[j][None, :]

            def project(x, sh, sc, pos):
                q_cos, q_up, q_dn = _tables_for(x.shape[0], x.shape[1], pos, A_ROPE, [A_NOPE])
                kt = _tables_for(x.shape[0], x.shape[1], pos, A_ROPE, [0])
                return _proj_a(x, g_m, sh, sc, wd, g_q, g_kv, wq, wq_rot, (q_cos, q_up - q_dn), kt)

            q_p, ckv_p, kr_p = project(xp, sh_p, sc_p, pos_p)
            q_s, ckv_s, kr_s = project(xs, sh_s, sc_s, pos_s)
            k_p, v_p = _expand_a(ckv_p, kr_p, wk, wr, wv)
            ckv_all, n_all = _with_cache(cache_a_ckv[j], ckv_s, F32, KV_BLOCK)
            kr_all, _ = _with_cache(cache_a_krope[j], kr_s, F32, KV_BLOCK)
            k_s, v_s = _expand_a(ckv_all, kr_all, wk, wr, wv)
            o_p = _causal_attention(q_p, k_p, v_p, span=min(A_KV_BLOCK, sp), sk_valid=sp,
                                    q_off=0, k_off=0)
            o_s = _causal_attention(q_s, k_s, v_s, span=KV_BLOCK, sk_valid=n_all, q_off=past,
                                    k_off=0)
            wo = w_a_o[j].astype(BF16)
            states[0].append((ckv_p, kr_p, ckv_s, kr_s))
        elif m == 1 or m == 3:
            heads = B_HEADS if m == 1 else D_HEADS
            w_qkv = (w_b_qkv if m == 1 else w_d_qkv)[j].astype(BF16)
            cache_k, cache_v = (cache_b_k, cache_b_v) if m == 1 else (cache_d_k, cache_d_v)
            q_scale = HEAD_DIM ** -0.5 * LOG2E
            q_p, kf_p, vf_p, kb_p, vb_p = _proj_qkv(xp, g_m, sh_p, sc_p, w_qkv, heads, q_scale)
            q_s, kf_s, vf_s, kb_s, vb_s = _proj_qkv(xs, g_m, sh_s, sc_s, w_qkv, heads, q_scale)
            lc = cache_k.shape[2]
            block = KV_BLOCK if m == 1 else D_BAND_WIDTH
            k_s, n_all = _with_cache(cache_k[j], kb_s, BF16, block)
            v_s, _ = _with_cache(cache_v[j], vb_s, BF16, block)
            if m == 1:
                o_p = _stick_attention(q_p, kb_p, vb_p, sk_valid=sp, q_off=0, k_off=0)
                o_s = _stick_attention(q_s, k_s, v_s, sk_valid=n_all, q_off=past,
                                       k_off=past - lc)
                wo = w_b_o[j].astype(BF16)
                states[1].append((kf_p, vf_p, kf_s, vf_s))
            else:
                bias = _band_bias(rel_bias_d[j])
                front = D_PREV_CHUNKS * CHUNK
                o_p = _band_attention(q_p, _front_pad(kb_p, front), _front_pad(vb_p, front),
                                      width=D_BAND_WIDTH, sk_valid=sp, q_off=0, k_off=0,
                                      front=front, n_prev=D_PREV_CHUNKS, bias=bias,
                                      pairs_per_step=D_PAIRS_PER_STEP, shared_kv=False)
                o_s = _band_attention(q_s, k_s, v_s, width=D_BAND_WIDTH, sk_valid=n_all,
                                      q_off=past, k_off=past - lc, front=0,
                                      n_prev=D_PREV_CHUNKS, bias=bias[:, :t, :],
                                      pairs_per_step=D_PAIRS_PER_STEP, shared_kv=False)
                wo = w_d_o[j].astype(BF16)
                keep = min(D_PREV_CHUNKS * CHUNK, sp)
                k_roll = jnp.concatenate([cache_k[j].reshape(bs, lc, -1), kf_s], axis=1)[:, t:]
                v_roll = jnp.concatenate([cache_v[j].reshape(bs, lc, -1), vf_s], axis=1)[:, t:]
                states[3].append((kf_p[:, sp - keep:], vf_p[:, sp - keep:], k_roll, v_roll))
        else:
            order = _c_head_order()
            q_width, kv_width = C_HEADS * HEAD_DIM, C_KV_HEADS * HEAD_DIM
            col = np.concatenate([(order[:, None] * HEAD_DIM + np.arange(HEAD_DIM)).reshape(-1),
                                  np.arange(q_width, q_width + 2 * kv_width)])
            w_qkv = w_c_qkv[j][:, col].astype(BF16)
            b_qkv = b_c_qkv[j][col][None, :]
            sink = sink_c[j][order]
            wo = w_c_o[j].reshape(C_HEADS, HEAD_DIM, d)[order].reshape(q_width, d).astype(BF16)
            lanes = [0, HEAD_DIM]

            def project(x, sh, sc, pos):
                tb = _tables_for(x.shape[0], x.shape[1], pos, C_ROT, lanes)
                return _proj_c(x, g_m, sh, sc, w_qkv, b_qkv, tb)

            q_p, kf_p, vf_p, kb_p, vb_p = project(xp, sh_p, sc_p, pos_p)
            q_s, kf_s, vf_s, kb_s, vb_s = project(xs, sh_s, sc_s, pos_s)
            lc = cache_c_k.shape[2]
            k_s, n_all = _with_cache(cache_c_k[j], kb_s, BF16, C_BAND_WIDTH)
            v_s, _ = _with_cache(cache_c_v[j], vb_s, BF16, C_BAND_WIDTH)
            group = C_HEADS // C_KV_HEADS
            front = C_PREV_CHUNKS * CHUNK
            band_mask = jnp.asarray(_band_mask_tile(Q_TILE, C_BAND_WIDTH, C_PREV_CHUNKS))
            o_p = _band_attention(q_p, _front_pad(kb_p, front), _front_pad(vb_p, front),
                                  width=C_BAND_WIDTH, sk_valid=sp, q_off=0, k_off=0, front=front,
                                  n_prev=C_PREV_CHUNKS, pairs_per_step=group, shared_kv=True, sink=sink,
                                  bias=band_mask)
            o_s = _band_attention(q_s, k_s, v_s, width=C_BAND_WIDTH, sk_valid=n_all, q_off=past,
                                  k_off=past - lc, front=0, n_prev=C_PREV_CHUNKS, pairs_per_step=group, shared_kv=True,
                                  sink=sink, bias=band_mask[:t])
            keep = min(C_PREV_CHUNKS * CHUNK, sp)
            k_roll = jnp.concatenate([cache_c_k[j].reshape(bs, lc, -1), kf_s], axis=1)[:, t:]
            v_roll = jnp.concatenate([cache_c_v[j].reshape(bs, lc, -1), vf_s], axis=1)[:, t:]
            states[2].append((kf_p[:, sp - keep:], vf_p[:, sp - keep:], k_roll, v_roll))

        g_f = g_ffn[i][None, :]
        w_in, w_out = w_ffn_in[i].astype(BF16), w_ffn_out[i].astype(BF16)
        g_out = g_final[None, :] if i == depth - 1 else None
        xp = _block(xp, o_p, wo, gm_p, g_f, shf_p, scf_p, gf_p, w_in, w_out, g_out)
        xs = _block(xs, o_s, wo, gm_s, g_f, shf_s, scf_s, gf_s, w_in, w_out, g_out)

    y_p, y_s = xp, xs

    def stacked(entries, n_heads=None):
        outs = []
        for parts in zip(*entries):
            a = jnp.stack(parts, axis=0)
            if n_heads is not None:
                a = a.reshape(a.shape[:3] + (n_heads, HEAD_DIM))
            outs.append(a)
        return tuple(outs)

    return ((y_p, y_s) + stacked(states[0]) + stacked(states[1], B_HEADS)
            + stacked(states[2], C_KV_HEADS) + stacked(states[3], D_HEADS))
```

```python
import functools
import math

import numpy as np
import jax
import jax.numpy as jnp
from jax import lax
from jax.experimental import pallas as pl
from jax.experimental.pallas import tpu as pltpu

F32 = jnp.float32
BF16 = jnp.bfloat16

CHUNK = 64
HEAD_DIM = 64
ROPE_THETA = 500000.0
NORM_EPS = 1e-6
N_MIXERS = 4
A_HEADS, A_Q_LORA, A_KV_LORA, A_NOPE, A_ROPE, A_V = 16, 384, 256, 64, 32, 64
B_HEADS = 16
C_HEADS, C_KV_HEADS, C_WINDOW, C_ROT = 16, 4, 128, 16
D_HEADS, D_PREV_CHUNKS, D_REL_CLIP = 16, 8, 128
C_PREV_CHUNKS = C_WINDOW // CHUNK

LANES = 128
V7X_VMEM_LIMIT = 56 * 1024 * 1024

TOKEN_TILE = 512
Q_TILE = 256
KV_BLOCK = 256
A_KV_BLOCK = 512
C_BAND_WIDTH = C_PREV_CHUNKS * CHUNK + Q_TILE
D_BAND_WIDTH = D_PREV_CHUNKS * CHUNK + Q_TILE
A_Q_TILE = 256
A_PAIRS_PER_STEP = 2
A_PV_CHUNK = 512
D_PAIRS_PER_STEP = 4
MASK_VALUE = -1e30
LOG2E = math.log2(math.e)
STICK_DEAD_BITS = 150.0


def _cparams(n_axes):
    return pltpu.CompilerParams(
        dimension_semantics=("parallel",) * n_axes, vmem_limit_bytes=V7X_VMEM_LIMIT)


def _adaln_kernel(c_ref, w_ref, b_ref, o_ref):
    c = c_ref[...]
    a = (c * jax.nn.sigmoid(c)).astype(BF16)
    y = jnp.dot(a, w_ref[0].astype(BF16), preferred_element_type=F32) + b_ref[0]
    o_ref[0, 0] = y


def _adaln(c_all, w_mod, b_mod):
    depth, d, d6 = w_mod.shape
    n = c_all.shape[0]
    return pl.pallas_call(
        _adaln_kernel,
        grid=(depth, d6 // d),
        in_specs=[
            pl.BlockSpec((n, d), lambda i, k: (0, 0)),
            pl.BlockSpec((1, d, d), lambda i, k: (i, 0, k)),
            pl.BlockSpec((1, 1, d), lambda i, k: (i, 0, k)),
        ],
        out_specs=pl.BlockSpec((1, 1, n, d), lambda i, k: (i, k, 0, 0)),
        out_shape=jax.ShapeDtypeStruct((depth, d6 // d, n, d), F32),
        compiler_params=_cparams(2),
    )(c_all, w_mod, b_mod.reshape(depth, 1, d6))


def _token_tiling(bx, sx):
    if sx >= TOKEN_TILE:
        ts = TOKEN_TILE
        while sx % ts:
            ts //= 2
        return 1, ts
    nb = min(bx, TOKEN_TILE // sx)
    while bx % nb:
        nb -= 1
    return nb, sx


def _tok_spec(nb, ts, width):
    return pl.BlockSpec((nb, ts, width), lambda b, s: (b, s, 0))


def _mod_spec(nb, d):
    return pl.BlockSpec((nb, 1, d), lambda b, s: (b, 0, 0))


def _const_spec(shape):
    nd = len(shape)
    return pl.BlockSpec(shape, lambda b, s: (0,) * nd)


def _table_spec(rows):
    return pl.BlockSpec((1, rows, LANES), lambda b, s: (s, 0, 0))


def _rms(x):
    return x * lax.rsqrt(jnp.mean(x * x, axis=-1, keepdims=True) + NORM_EPS)


def _modulated(x_ref, g_ref, shift_ref, scale_ref):
    y = _rms(x_ref[...]) * g_ref[...]
    h = y * (1.0 + scale_ref[...]) + shift_ref[...]
    return h.reshape(-1, h.shape[-1]).astype(BF16)


def _rope_lanes(x, cos_t, sin_up, sin_dn, half):
    return (x * cos_t + pltpu.roll(x, half, 1) * sin_up
            + pltpu.roll(x, LANES - half, 1) * sin_dn)


def _store_tok(ref, lo, val):
    nb, ts = ref.shape[0], ref.shape[1]
    w = val.shape[-1]
    ref[:, :, lo:lo + w] = val.reshape(nb, ts, w).astype(ref.dtype)


def _proj_qkv_kernel(x_ref, g_ref, sh_ref, sc_ref, w_ref, q_ref, kf_ref, vf_ref, kb_ref, vb_ref,
                     *, width, q_scale):
    h = _modulated(x_ref, g_ref, sh_ref, sc_ref)
    q = jnp.dot(h, w_ref[:, 0:width], preferred_element_type=F32)
    _store_tok(q_ref, 0, q * q_scale)
    k = jnp.dot(h, w_ref[:, width:2 * width], preferred_element_type=F32)
    _store_tok(kf_ref, 0, k)
    _store_tok(kb_ref, 0, k)
    v = jnp.dot(h, w_ref[:, 2 * width:3 * width], preferred_element_type=F32)
    _store_tok(vf_ref, 0, v)
    _store_tok(vb_ref, 0, v)


def _proj_qkv(x, g, shift, scale, w_bf16, n_heads, q_scale):
    bx, sx, d = x.shape
    width = n_heads * HEAD_DIM
    nb, ts = _token_tiling(bx, sx)
    out = lambda dt: jax.ShapeDtypeStruct((bx, sx, width), dt)
    return pl.pallas_call(
        functools.partial(_proj_qkv_kernel, width=width, q_scale=q_scale),
        grid=(bx // nb, sx // ts),
        in_specs=[_tok_spec(nb, ts, d), _const_spec((1, d)), _mod_spec(nb, d), _mod_spec(nb, d),
                  _const_spec(w_bf16.shape)],
        out_specs=[_tok_spec(nb, ts, width)] * 5,
        out_shape=[out(BF16), out(F32), out(F32), out(BF16), out(BF16)],
        compiler_params=_cparams(2),
    )(x, g, shift, scale, w_bf16)


def _proj_c_kernel(x_ref, g_ref, sh_ref, sc_ref, w_ref, b_ref, tc_ref, tu_ref, td_ref,
                   q_ref, kf_ref, vf_ref, kb_ref, vb_ref, *, q_width, kv_width, q_scale):
    h = _modulated(x_ref, g_ref, sh_ref, sc_ref)
    qkv = jnp.dot(h, w_ref[...], preferred_element_type=F32) + b_ref[...]
    cos_t, sin_up, sin_dn = tc_ref[0], tu_ref[0], td_ref[0]
    half = C_ROT // 2
    for j in range(q_width // LANES):
        xg = qkv[:, j * LANES:(j + 1) * LANES]
        _store_tok(q_ref, j * LANES, _rope_lanes(xg, cos_t, sin_up, sin_dn, half) * q_scale)
    for j in range(kv_width // LANES):
        lo = q_width + j * LANES
        kg = _rope_lanes(qkv[:, lo:lo + LANES], cos_t, sin_up, sin_dn, half)
        _store_tok(kf_ref, j * LANES, kg)
        _store_tok(kb_ref, j * LANES, kg)
    v = qkv[:, q_width + kv_width:q_width + 2 * kv_width]
    _store_tok(vf_ref, 0, v)
    _store_tok(vb_ref, 0, v)


def _proj_c(x, g, shift, scale, w_bf16, bias, tables):
    bx, sx, d = x.shape
    q_width, kv_width = C_HEADS * HEAD_DIM, C_KV_HEADS * HEAD_DIM
    nb, ts = _token_tiling(bx, sx)
    out = lambda w, dt: jax.ShapeDtypeStruct((bx, sx, w), dt)
    return pl.pallas_call(
        functools.partial(_proj_c_kernel, q_width=q_width, kv_width=kv_width,
                          q_scale=HEAD_DIM ** -0.5 * LOG2E),
        grid=(bx // nb, sx // ts),
        in_specs=[_tok_spec(nb, ts, d), _const_spec((1, d)), _mod_spec(nb, d), _mod_spec(nb, d),
                  _const_spec(w_bf16.shape), _const_spec(bias.shape)] + [_table_spec(nb * ts)] * 3,
        out_specs=[_tok_spec(nb, ts, q_width)] + [_tok_spec(nb, ts, kv_width)] * 4,
        out_shape=[out(q_width, BF16), out(kv_width, F32), out(kv_width, F32),
                   out(kv_width, BF16), out(kv_width, BF16)],
        compiler_params=_cparams(2),
    )(x, g, shift, scale, w_bf16, bias, *tables)


def _proj_a_kernel(x_ref, g_ref, sh_ref, sc_ref, wd_ref, gq_ref, gkv_ref, wq_ref, wqr_ref,
                   qc_ref, qs_ref, kc_ref, ku_ref, kd_ref,
                   q_ref, ckv_ref, kr_ref, *, q_scale):
    h = _modulated(x_ref, g_ref, sh_ref, sc_ref)
    down = jnp.dot(h, wd_ref[...], preferred_element_type=F32)
    cq = (_rms(down[:, :A_Q_LORA]) * gq_ref[...]).astype(BF16)
    ckv = _rms(down[:, A_Q_LORA:A_Q_LORA + A_KV_LORA]) * gkv_ref[...]
    _store_tok(ckv_ref, 0, ckv)
    lo = A_Q_LORA + A_KV_LORA
    half = A_ROPE // 2
    kr = _rope_lanes(down[:, lo:lo + LANES], kc_ref[0], ku_ref[0], kd_ref[0], half)
    _store_tok(kr_ref, 0, kr[:, :A_ROPE])
    q = jnp.dot(cq, wq_ref[...], preferred_element_type=F32)
    q_rot = jnp.dot(cq, wqr_ref[...], preferred_element_type=F32)
    qc, qs = qc_ref[0], qs_ref[0]
    for j in range(A_HEADS):
        lanes = slice(j * LANES, (j + 1) * LANES)
        _store_tok(q_ref, j * LANES, (q[:, lanes] * qc + q_rot[:, lanes] * qs) * q_scale)


def _proj_a(x, g, shift, scale, wd_bf16, g_q, g_kv, wq_bf16, wq_rot_bf16, q_tables, k_tables):
    bx, sx, d = x.shape
    nb, ts = _token_tiling(bx, sx)
    return pl.pallas_call(
        functools.partial(_proj_a_kernel, q_scale=(A_NOPE + A_ROPE) ** -0.5 * LOG2E),
        grid=(bx // nb, sx // ts),
        in_specs=[_tok_spec(nb, ts, d), _const_spec((1, d)), _mod_spec(nb, d), _mod_spec(nb, d),
                  _const_spec(wd_bf16.shape), _const_spec(g_q.shape), _const_spec(g_kv.shape),
                  _const_spec(wq_bf16.shape), _const_spec(wq_rot_bf16.shape)]
                 + [_table_spec(nb * ts)] * 5,
        out_specs=[_tok_spec(nb, ts, A_HEADS * LANES), _tok_spec(nb, ts, A_KV_LORA),
                   _tok_spec(nb, ts, A_ROPE)],
        out_shape=[jax.ShapeDtypeStruct((bx, sx, A_HEADS * LANES), BF16),
                   jax.ShapeDtypeStruct((bx, sx, A_KV_LORA), F32),
                   jax.ShapeDtypeStruct((bx, sx, A_ROPE), F32)],
        compiler_params=_cparams(2),
    )(x, g, shift, scale, wd_bf16, g_q, g_kv, wq_bf16, wq_rot_bf16, *q_tables, *k_tables)


def _expand_a_kernel(ckv_ref, kr_ref, wk_ref, wr_ref, wv_ref, k_ref, v_ref):
    ckv = ckv_ref[...]
    ckv = ckv.reshape(-1, ckv.shape[-1]).astype(BF16)
    kr = kr_ref[...]
    kr = kr.reshape(-1, kr.shape[-1]).astype(BF16)
    k = (jnp.dot(ckv, wk_ref[...], preferred_element_type=F32)
         + jnp.dot(kr, wr_ref[...], preferred_element_type=F32))
    _store_tok(k_ref, 0, k)
    _store_tok(v_ref, 0, jnp.dot(ckv, wv_ref[...], preferred_element_type=F32))


def _expand_a(ckv, kr, wk, wr, wv):
    bx, sx, _ = ckv.shape
    nb, ts = _token_tiling(bx, sx)
    return pl.pallas_call(
        _expand_a_kernel,
        grid=(bx // nb, sx // ts),
        in_specs=[_tok_spec(nb, ts, A_KV_LORA), _tok_spec(nb, ts, A_ROPE),
                  _const_spec(wk.shape), _const_spec(wr.shape), _const_spec(wv.shape)],
        out_specs=[_tok_spec(nb, ts, A_HEADS * LANES), _tok_spec(nb, ts, A_HEADS * A_V)],
        out_shape=[jax.ShapeDtypeStruct((bx, sx, A_HEADS * LANES), BF16),
                   jax.ShapeDtypeStruct((bx, sx, A_HEADS * A_V), BF16)],
        compiler_params=_cparams(2),
    )(ckv, kr, wk, wr, wv)


def _block_kernel(x_ref, o_ref, wo_ref, gm_ref, g_ref, sh_ref, sc_ref, gf_ref, wi_ref, wout_ref,
                  *out_refs, hidden, chunk, final_norm):
    gout_ref, y_ref = out_refs if final_norm else (None,) + out_refs
    nb, ts, d = x_ref.shape
    o = o_ref[...].reshape(nb * ts, -1)
    mix = jnp.dot(o, wo_ref[...], preferred_element_type=F32).reshape(nb, ts, d)
    x1 = x_ref[...] + gm_ref[...] * mix
    h = (_rms(x1) * g_ref[...]) * (1.0 + sc_ref[...]) + sh_ref[...]
    h = h.reshape(nb * ts, d).astype(BF16)
    acc = jnp.zeros((nb * ts, d), F32)
    for c in range(hidden // chunk):
        gate = jnp.dot(h, wi_ref[:, c * chunk:(c + 1) * chunk], preferred_element_type=F32)
        up = jnp.dot(h, wi_ref[:, hidden + c * chunk:hidden + (c + 1) * chunk],
                     preferred_element_type=F32)
        act = (gate * jax.nn.sigmoid(gate) * up).astype(BF16)
        acc = acc + jnp.dot(act, wout_ref[c * chunk:(c + 1) * chunk, :],
                            preferred_element_type=F32)
    x2 = x1 + gf_ref[...] * acc.reshape(nb, ts, d)
    y_ref[...] = _rms(x2) * gout_ref[...] if final_norm else x2


def _block(x, o, wo, gate_m, g_ffn, shift_f, scale_f, gate_f, w_in, w_out, g_out=None):
    bx, sx, d = x.shape
    hidden = w_out.shape[0]
    nb, ts = _token_tiling(bx, sx)
    resident = lambda shape: pl.BlockSpec(shape, lambda b, s: (0,) * len(shape),
                                          pipeline_mode=pl.Buffered(1))
    in_specs = [_tok_spec(nb, ts, d), _tok_spec(nb, ts, o.shape[-1]), resident(wo.shape),
                _mod_spec(nb, d), _const_spec((1, d)), _mod_spec(nb, d), _mod_spec(nb, d),
                _mod_spec(nb, d), resident(w_in.shape), resident(w_out.shape)]
    args = [x, o, wo, gate_m, g_ffn, shift_f, scale_f, gate_f, w_in, w_out]
    if g_out is not None:
        in_specs.append(_const_spec((1, d)))
        args.append(g_out)
    return pl.pallas_call(
        functools.partial(_block_kernel, hidden=hidden, chunk=256, final_norm=g_out is not None),
        grid=(bx // nb, sx // ts),
        in_specs=in_specs,
        out_specs=_tok_spec(nb, ts, d),
        out_shape=jax.ShapeDtypeStruct((bx, sx, d), F32),
        compiler_params=_cparams(2),
    )(*args)


def _stick_kernel(*refs, tq, bk, sk_valid, q_off, k_off, cache_blocks):
    if cache_blocks:
        q_ref, k_ref, v_ref, kc_ref, vc_ref, tri_ref, o_ref, run_ref, acc_ref = refs
    else:
        q_ref, k_ref, v_ref, tri_ref, o_ref, run_ref, acc_ref = refs
    qpos0 = pl.program_id(2) * tq + q_off
    kb_hi = (jnp.minimum(qpos0 + (tq - 1) - k_off, sk_valid) + bk - 1) // bk
    f_hi = jnp.minimum(jnp.minimum(qpos0 - k_off, sk_valid) // bk, kb_hi)

    def visible(kb):
        row = lax.broadcasted_iota(jnp.int32, (2 * tq, 1), 0)
        qpos = jnp.where(row < tq, row, row - tq) + qpos0
        ik = kb * bk + lax.broadcasted_iota(jnp.int32, (1, bk), 1)
        return (ik + k_off < qpos) & (ik < sk_valid)

    lane_half = lax.broadcasted_iota(jnp.int32, (1, LANES), 1) // HEAD_DIM
    q_pair = q_ref[0]
    q_both = jnp.concatenate(
        [jnp.where(lane_half == hh, q_pair, jnp.zeros((), BF16)) for hh in range(2)], axis=0)

    def rows(kb):
        return pl.ds(pl.multiple_of(kb * bk, bk), bk)

    def block(kb, masked, run):
        if cache_blocks and not masked:
            k_blk = kc_ref[0, rows(kb), :].astype(BF16)
            v_blk = vc_ref[0, rows(kb), :].astype(BF16)
        else:
            k_blk = k_ref[0, rows(kb - cache_blocks), :]
            v_blk = v_ref[0, rows(kb - cache_blocks), :]
        y = lax.dot_general(q_both, k_blk, (((1,), (1,)), ((), ())),
                            preferred_element_type=F32)
        sp = jnp.maximum(y, 0.0) + jnp.log2(1.0 + jnp.exp2(-jnp.abs(y)))
        if masked:
            vis = visible(kb)
            sp = jnp.where(vis, sp, 0.0)
        suffix = jnp.dot(sp.astype(BF16), tri_ref[...], preferred_element_type=F32)
        a = jnp.exp2(y - suffix - run)
        if masked:
            a = jnp.where(vis, a, 0.0)
        return (jnp.dot(a.astype(BF16), v_blk, preferred_element_type=F32),
                jnp.sum(sp, axis=1, keepdims=True))

    @pl.when(f_hi > 0)
    def _():
        out_d, sum_d = block(kb_hi - 1, True, 0.0)
        out_f, sum_f = block(f_hi - 1, False, sum_d)
        acc_ref[...] = out_d + out_f
        run_ref[...] = sum_d + sum_f

    @pl.when(f_hi == 0)
    def _():
        out_d, sum_d = block(kb_hi - 1, True, 0.0)
        acc_ref[...] = out_d
        run_ref[...] = sum_d

    def earlier(st):
        out, row_sum = block(f_hi - 1 - st[0], False, run_ref[...])
        acc_ref[...] += out
        run_ref[...] += row_sum
        return st[0] + 1, jnp.min(run_ref[...])

    lax.while_loop(lambda st: (st[0] < f_hi) & (st[1] < STICK_DEAD_BITS), earlier,
                   (jnp.ones((), jnp.int32), jnp.min(run_ref[...])))
    o_ref[0] = jnp.where(lane_half == 0, acc_ref[:tq], acc_ref[tq:]).astype(o_ref.dtype)


def _stick_attention(q, k, v, *, sk_valid, q_off, k_off, cache=None):
    bx, sq, qw = q.shape
    rows = k.shape[1]
    n_pairs = qw // LANES
    tq = min(Q_TILE, sq)
    bk = KV_BLOCK
    assert sq % tq == 0 and rows % bk == 0 and tq % 8 == 0 and q_off >= k_off >= 0
    for q0 in range(0, sq, tq):
        first, last = q_off + q0 - k_off, min(q_off + q0 + tq - 1 - k_off, sk_valid)
        assert first <= sk_valid and -(-last // bk) - first // bk == 1
    kv_spec = lambda n: pl.BlockSpec((1, n, LANES), lambda b, p, i: (b, 0, p))
    in_specs = [pl.BlockSpec((1, tq, LANES), lambda b, p, i: (b, i, p)), kv_spec(rows),
                kv_spec(rows)]
    args = [q, k, v]
    cache_blocks = 0
    if cache is not None:
        n_cache = cache[0].shape[1]
        assert n_cache % bk == 0 and q_off - k_off == n_cache and sq == tq
        cache_blocks = n_cache // bk
        in_specs += [kv_spec(n_cache), kv_spec(n_cache)]
        args += list(cache)
    idx = np.arange(bk)
    in_specs.append(pl.BlockSpec((bk, bk), lambda b, p, i: (0, 0)))
    args.append(jnp.asarray(idx[:, None] >= idx[None, :], BF16))
    return pl.pallas_call(
        functools.partial(_stick_kernel, tq=tq, bk=bk, sk_valid=sk_valid, q_off=q_off,
                          k_off=k_off, cache_blocks=cache_blocks),
        grid=(bx, n_pairs, sq // tq),
        in_specs=in_specs,
        out_specs=pl.BlockSpec((1, tq, LANES), lambda b, p, i: (b, i, p)),
        out_shape=jax.ShapeDtypeStruct((bx, sq, n_pairs * LANES), BF16),
        scratch_shapes=[pltpu.VMEM((2 * tq, 1), F32), pltpu.VMEM((2 * tq, LANES), F32)],
        compiler_params=_cparams(3),
    )(*args)


def _row_end(qpos, k_off):
    return (qpos // CHUNK + 1) * CHUNK - k_off


def _causal_kernel(q_ref, k_ref, v_ref, o_ref, *, tq, span, n_spans, sk_valid, q_off, k_off,
                   single_tile):
    qpos0 = pl.program_id(2) * tq + q_off
    n_needed = (jnp.minimum(_row_end(qpos0 + tq - 1, k_off), sk_valid) - 1) // span
    shift = int(math.log2(CHUNK))
    lane_half = lax.broadcasted_iota(jnp.int32, (1, LANES), 1) // HEAD_DIM
    nt = (((1,), (1,)), ((), ()))

    for n_full in range(n_spans):
        if single_tile and n_full != (min(_row_end(q_off + tq - 1, k_off), sk_valid) - 1) // span:
            continue

        @pl.when(n_needed == n_full)
        def _(n_full=n_full):
            full, width = n_full * span, (n_full + 1) * span
            qpos = lax.broadcasted_iota(jnp.int32, (tq, 1), 0) + qpos0
            kidx = lax.broadcasted_iota(jnp.int32, (1, span), 1) + full
            vis = (kidx < sk_valid) & (jnp.right_shift(kidx + k_off, shift)
                                       <= jnp.right_shift(qpos, shift))
            outs = []
            for hh in range(2 * (q_ref.shape[-1] // (2 * LANES))):
                lanes = slice(hh * LANES, (hh + 1) * LANES)
                v_lanes = slice(hh // 2 * LANES, (hh // 2 + 1) * LANES)
                q_h = q_ref[0, :, lanes]
                s_tail = lax.dot_general(q_h, k_ref[0, full:width, lanes], nt,
                                         preferred_element_type=F32)
                s_tail = jnp.where(vis, s_tail, MASK_VALUE)
                m = jnp.max(s_tail, axis=1, keepdims=True)
                if n_full:
                    s_full = lax.dot_general(q_h, k_ref[0, 0:full, lanes], nt,
                                             preferred_element_type=F32)
                    m = jnp.maximum(m, jnp.max(s_full, axis=1, keepdims=True))
                p_tail = jnp.exp2(s_tail - m)
                l = jnp.sum(p_tail, axis=1, keepdims=True)
                o = jnp.dot(p_tail.astype(BF16), v_ref[0, full:width, v_lanes],
                            preferred_element_type=F32)
                for c0 in range(0, full, A_PV_CHUNK):
                    p_c = jnp.exp2(s_full[:, c0:c0 + A_PV_CHUNK] - m)
                    l = l + jnp.sum(p_c, axis=1, keepdims=True)
                    o = o + jnp.dot(p_c.astype(BF16), v_ref[0, c0:c0 + A_PV_CHUNK, v_lanes],
                                    preferred_element_type=F32)
                outs.append(o / l)
            for pr in range(len(outs) // 2):
                o_ref[0, :, pr * LANES:(pr + 1) * LANES] = jnp.where(
                    lane_half == 0, outs[2 * pr], outs[2 * pr + 1]).astype(o_ref.dtype)


def _causal_attention(q, k, v, *, span, sk_valid, q_off, k_off):
    bx, sq, qw = q.shape
    rows = k.shape[1]
    pps = A_PAIRS_PER_STEP
    n_pairs = qw // (2 * LANES)
    assert n_pairs % pps == 0
    tq = min(A_Q_TILE, sq)
    assert sq % tq == 0 and rows % span == 0 and q_off % CHUNK == 0 and k_off >= 0
    for q0 in range(0, sq, tq):
        end = min(_row_end(q_off + q0 + tq - 1, k_off), sk_valid)
        first_row_end = min(_row_end(q_off + q0, k_off), sk_valid)
        assert 0 < end <= rows and (end - 1) // span * span <= first_row_end
    return pl.pallas_call(
        functools.partial(_causal_kernel, tq=tq, span=span, n_spans=rows // span,
                          sk_valid=sk_valid, q_off=q_off, k_off=k_off, single_tile=sq == tq),
        grid=(bx, n_pairs // pps, sq // tq),
        in_specs=[
            pl.BlockSpec((1, tq, pps * 2 * LANES), lambda b, p, i: (b, i, p)),
            pl.BlockSpec((1, rows, pps * 2 * LANES), lambda b, p, i: (b, 0, p)),
            pl.BlockSpec((1, rows, pps * LANES), lambda b, p, i: (b, 0, p)),
        ],
        out_specs=pl.BlockSpec((1, tq, pps * LANES), lambda b, p, i: (b, i, p)),
        out_shape=jax.ShapeDtypeStruct((bx, sq, n_pairs * LANES), BF16),
        compiler_params=_cparams(3),
    )(q, k, v)


def _band_kernel(*refs, tq, width, sk_valid, q_off, k_off, front, n_prev, use_sink, per_head_bias):
    refs = list(refs)
    q_ref, k_ref, v_ref = refs[:3]
    rest = refs[3:]
    sink_ref = rest.pop(0) if use_sink else None
    bias_ref, o_ref = rest

    group = q_ref.shape[-1] // LANES
    first_pair = pl.program_id(1) * group
    qpos0 = pl.program_id(2) * tq + q_off
    start = pl.multiple_of((qpos0 // CHUNK - n_prev) * CHUNK - k_off + front, CHUNK)
    band = pl.ds(start, width)
    kidx = lax.broadcasted_iota(jnp.int32, (1, width), 1) + (start - front)
    invalid = jnp.where((kidx >= 0) & (kidx < sk_valid), 0.0, MASK_VALUE)

    lane_half = lax.broadcasted_iota(jnp.int32, (1, LANES), 1) // HEAD_DIM
    shared_kv = k_ref.shape[-1] == LANES
    for j in range(group):
        q_pair = q_ref[0, :, j * LANES:(j + 1) * LANES]
        kv_lanes = slice(0, LANES) if shared_kv else slice(j * LANES, (j + 1) * LANES)
        k_band = k_ref[0, band, kv_lanes]
        v_band = v_ref[0, band, kv_lanes]
        outs = []
        for hh in range(2):
            q_h = jnp.where(lane_half == hh, q_pair, jnp.zeros((), BF16))
            s = lax.dot_general(q_h, k_band, (((1,), (1,)), ((), ())),
                                preferred_element_type=F32)
            s = s + (bias_ref[2 * j + hh] if per_head_bias else bias_ref[...]) + invalid
            m = jnp.max(s, axis=1, keepdims=True)
            if use_sink:
                sink = sink_ref[2 * (first_pair + j) + hh] * LOG2E
                m = jnp.maximum(m, sink)
            p = jnp.exp2(s - m)
            l = jnp.sum(p, axis=1, keepdims=True)
            if use_sink:
                l = l + jnp.exp2(sink - m)
            outs.append(jnp.dot(p.astype(BF16), v_band, preferred_element_type=F32) / l)
        o_ref[0, :, j * LANES:(j + 1) * LANES] = jnp.where(
            lane_half == 0, outs[0], outs[1]).astype(o_ref.dtype)


def _band_mask_tile(rows, width, n_prev):
    r_chunk = np.arange(rows)[:, None] // CHUNK
    c_chunk = np.arange(width)[None, :] // CHUNK - n_prev
    return np.where((c_chunk <= r_chunk) & (c_chunk >= r_chunk - n_prev), 0.0, MASK_VALUE
                    ).astype(np.float32)


def _band_attention(q, k, v, *, width, sk_valid, q_off, k_off, front, n_prev, pairs_per_step,
                    shared_kv, sink=None, bias=None):
    kv_group = pairs_per_step
    kv_lanes = LANES if shared_kv else pairs_per_step * LANES
    bx, sq, qw = q.shape
    rows = k.shape[1]
    n_pairs = qw // LANES
    tq = min(Q_TILE, sq)
    assert sq % tq == 0 and q_off % CHUNK == 0 and (tq % CHUNK == 0 or sq == tq)
    assert ((q_off + sq - tq) // CHUNK - n_prev) * CHUNK - k_off + front + width <= rows
    assert (q_off // CHUNK - n_prev) * CHUNK - k_off + front >= 0
    in_specs = [
        pl.BlockSpec((1, tq, kv_group * LANES), lambda b, g, i: (b, i, g)),
        pl.BlockSpec((1, rows, kv_lanes), lambda b, g, i: (b, 0, g)),
        pl.BlockSpec((1, rows, kv_lanes), lambda b, g, i: (b, 0, g)),
    ]
    args = [q, k, v]
    if sink is not None:
        in_specs.append(pl.BlockSpec(memory_space=pltpu.SMEM))
        args.append(sink)
    per_head_bias = bias.ndim == 3
    if per_head_bias:
        in_specs.append(pl.BlockSpec((2 * kv_group, tq, width), lambda b, g, i: (g, 0, 0)))
    else:
        in_specs.append(pl.BlockSpec((tq, width), lambda b, g, i: (0, 0)))
    args.append(bias)
    return pl.pallas_call(
        functools.partial(_band_kernel, tq=tq, width=width, sk_valid=sk_valid, q_off=q_off,
                          k_off=k_off, front=front, n_prev=n_prev,
                          use_sink=sink is not None, per_head_bias=per_head_bias),
        grid=(bx, n_pairs // kv_group, sq // tq),
        in_specs=in_specs,
        out_specs=pl.BlockSpec((1, tq, kv_group * LANES), lambda b, g, i: (b, i, g)),
        out_shape=jax.ShapeDtypeStruct((bx, sq, n_pairs * LANES), BF16),
        compiler_params=_cparams(3),
    )(*args)


def _band_bias_kernel(e_ref, mask_ref, o_ref, *, width):
    w = e_ref.shape[-1]
    x = jnp.broadcast_to(e_ref[0], (Q_TILE, w))
    toeplitz = pltpu.roll(x, 0, 1, stride=1, stride_axis=0)
    o_ref[0] = toeplitz[:, :width] * LOG2E + mask_ref[...]


def _band_bias(rel_bias):
    heads = rel_bias.shape[0]
    clip = (rel_bias.shape[1] - 1) // 2
    band = D_PREV_CHUNKS * CHUNK
    width = D_BAND_WIDTH
    w = width + Q_TILE
    assert band >= clip and width > band + clip
    top = jnp.broadcast_to(rel_bias[:, -1:], (heads, band - clip + 1))
    mid = jnp.flip(rel_bias[:, :2 * clip], axis=1)
    low = jnp.broadcast_to(rel_bias[:, :1], (heads, width - (band + clip + 1)))
    neg = jnp.broadcast_to(rel_bias[:, -1:], (heads, w - width))
    e = jnp.concatenate([top, mid, low, neg], axis=1)[:, None, :]
    return pl.pallas_call(
        functools.partial(_band_bias_kernel, width=width),
        grid=(heads,),
        in_specs=[pl.BlockSpec((1, 1, w), lambda h: (h, 0, 0)),
                  pl.BlockSpec((Q_TILE, width), lambda h: (0, 0))],
        out_specs=pl.BlockSpec((1, Q_TILE, width), lambda h: (h, 0, 0)),
        out_shape=jax.ShapeDtypeStruct((heads, Q_TILE, width), F32),
        compiler_params=_cparams(1),
    )(e, jnp.asarray(_band_mask_tile(Q_TILE, width, D_PREV_CHUNKS)))


def _rope_tables(pos, n_rot, lane_offsets, rows_repeat=1):
    half = n_rot // 2
    inv = ROPE_THETA ** (-jnp.arange(half, dtype=F32) * 2.0 / n_rot)
    ang = pos.astype(F32)[:, None] * inv[None, :]
    cos, sin = jnp.cos(ang), jnp.sin(ang)
    n = pos.shape[0]
    cos_t = jnp.ones((n, LANES), F32)
    sin_up = jnp.zeros((n, LANES), F32)
    sin_dn = jnp.zeros((n, LANES), F32)
    for o in lane_offsets:
        cos_t = cos_t.at[:, o:o + half].set(cos).at[:, o + half:o + n_rot].set(cos)
        sin_dn = sin_dn.at[:, o:o + half].set(-sin)
        sin_up = sin_up.at[:, o + half:o + n_rot].set(sin)
    def tiles(t):
        if rows_repeat > 1:
            return jnp.tile(t, (rows_repeat, 1))[None]
        rows = min(n, TOKEN_TILE)
        return t.reshape(n // rows, rows, LANES)
    return tiles(cos_t), tiles(sin_up), tiles(sin_dn)


def _tables_for(bx, sx, pos, n_rot, lane_offsets):
    nb, ts = _token_tiling(bx, sx)
    return _rope_tables(pos, n_rot, lane_offsets, rows_repeat=nb if nb > 1 else 1)


def _c_head_order():
    rep = C_HEADS // C_KV_HEADS
    order = []
    for p in range(C_HEADS // 2):
        g2, i = divmod(p, rep)
        order += [rep * (2 * g2) + i, rep * (2 * g2 + 1) + i]
    return np.asarray(order)


def _pad_rows(a, rows):
    return jnp.pad(a, ((0, 0), (0, rows - a.shape[1]), (0, 0)))


def _front_pad(a, rows):
    return jnp.pad(a, ((0, 0), (rows, 0), (0, 0)))


def _with_cache(cache, new, dtype, block=KV_BLOCK):
    full = jnp.concatenate([cache.reshape(cache.shape[0], cache.shape[1], -1).astype(dtype),
                            new.astype(dtype)], axis=1)
    rows = -(-full.shape[1] // block) * block
    return _pad_rows(full, rows), full.shape[1]


def kernel(x_prompt, x_sample, c_prompt, c_sample, cache_a_ckv, cache_a_krope, cache_b_k, cache_b_v,
           cache_c_k, cache_c_v, cache_d_k, cache_d_v, w_mod, b_mod, g_mix, g_ffn, w_ffn_in, w_ffn_out,
           w_a_down, g_a_q, g_a_kv, w_a_uq, w_a_uk, w_a_uv, w_a_o, w_b_qkv, w_b_o,
           w_c_qkv, b_c_qkv, sink_c, w_c_o, w_d_qkv, rel_bias_d, w_d_o, g_final):
    bp, sp, d = x_prompt.shape
    bs, t, _ = x_sample.shape
    depth = w_mod.shape[0]
    past = cache_a_ckv.shape[2]
    pos_p = jnp.arange(sp)
    pos_s = past + jnp.arange(t)

    mods = _adaln(jnp.concatenate([c_prompt, c_sample], axis=0), w_mod, b_mod)

    def mod(i, k):
        m = mods[i, k][:, None, :]
        return m[:bp], m[bp:]

    xp, xs = x_prompt, x_sample
    states = [[] for _ in range(N_MIXERS)]
    for i in range(depth):
        m, j = i % N_MIXERS, i // N_MIXERS
        (sh_p, sh_s), (sc_p, sc_s), (gm_p, gm_s) = mod(i, 0), mod(i, 1), mod(i, 2)
        (shf_p, shf_s), (scf_p, scf_s), (gf_p, gf_s) = mod(i, 3), mod(i, 4), mod(i, 5)
        g_m = g_mix[i][None, :]
        if m == 0:
            n_down = A_Q_LORA + A_KV_LORA + A_ROPE
            wd = jnp.pad(w_a_down[j], ((0, 0), (0, A_Q_LORA + A_KV_LORA + LANES - n_down))).astype(BF16)
            wq = w_a_uq[j].reshape(A_Q_LORA, A_HEADS, A_NOPE + A_ROPE)
            wq = jnp.pad(wq, ((0, 0), (0, 0), (0, LANES - A_NOPE - A_ROPE)))
            x1 = wq[:, :, A_NOPE:A_NOPE + A_ROPE // 2]
            x2 = wq[:, :, A_NOPE + A_ROPE // 2:A_NOPE + A_ROPE]
            wq_rot = jnp.zeros_like(wq).at[:, :, A_NOPE:A_NOPE + A_ROPE].set(
                jnp.concatenate([-x2, x1], axis=-1))
            wq = wq.reshape(A_Q_LORA, A_HEADS * LANES).astype(BF16)
            wq_rot = wq_rot.reshape(A_Q_LORA, A_HEADS * LANES).astype(BF16)
            wk = jnp.pad(w_a_uk[j], ((0, 0), (0, 0), (0, LANES - A_NOPE)))
            wk = wk.reshape(A_KV_LORA, A_HEADS * LANES).astype(BF16)
            place = jnp.zeros((A_ROPE, A_HEADS, LANES), F32)
            place = place.at[jnp.arange(A_ROPE), :, A_NOPE + jnp.arange(A_ROPE)].set(1.0)
            wr = place.reshape(A_ROPE, A_HEADS * LANES).astype(BF16)
            wv = w_a_uv[j].reshape(A_KV_LORA, A_HEADS * A_V).astype(BF16)
            g_q, g_kv = g_a_q[j][None, :], g_a_kv[j][None, :]

            def project(x, sh, sc, pos):
                q_cos, q_up, q_dn = _tables_for(x.shape[0], x.shape[1], pos, A_ROPE, [A_NOPE])
                kt = _tables_for(x.shape[0], x.shape[1], pos, A_ROPE, [0])
                return _proj_a(x, g_m, sh, sc, wd, g_q, g_kv, wq, wq_rot, (q_cos, q_up - q_dn), kt)

            q_p, ckv_p, kr_p = project(xp, sh_p, sc_p, pos_p)
            q_s, ckv_s, kr_s = project(xs, sh_s, sc_s, pos_s)
            k_p, v_p = _expand_a(ckv_p, kr_p, wk, wr, wv)
            ckv_all, n_all = _with_cache(cache_a_ckv[j], ckv_s, F32, KV_BLOCK)
            kr_all, _ = _with_cache(cache_a_krope[j], kr_s, F32, KV_BLOCK)
            k_s, v_s = _expand_a(ckv_all, kr_all, wk, wr, wv)
            o_p = _causal_attention(q_p, k_p, v_p, span=min(A_KV_BLOCK, sp), sk_valid=sp,
                                    q_off=0, k_off=0)
            o_s = _causal_attention(q_s, k_s, v_s, span=KV_BLOCK, sk_valid=n_all, q_off=past,
                                    k_off=0)
            wo = w_a_o[j].astype(BF16)
            states[0].append((ckv_p, kr_p, ckv_s, kr_s))
        elif m == 1 or m == 3:
            heads = B_HEADS if m == 1 else D_HEADS
            w_qkv = (w_b_qkv if m == 1 else w_d_qkv)[j].astype(BF16)
            cache_k, cache_v = (cache_b_k, cache_b_v) if m == 1 else (cache_d_k, cache_d_v)
            q_scale = HEAD_DIM ** -0.5 * LOG2E
            q_p, kf_p, vf_p, kb_p, vb_p = _proj_qkv(xp, g_m, sh_p, sc_p, w_qkv, heads, q_scale)
            q_s, kf_s, vf_s, kb_s, vb_s = _proj_qkv(xs, g_m, sh_s, sc_s, w_qkv, heads, q_scale)
            lc = cache_k.shape[2]
            if m == 1:
                o_p = _stick_attention(q_p, kb_p, vb_p, sk_valid=sp, q_off=0, k_off=0)
                new_rows = -(-t // KV_BLOCK) * KV_BLOCK
                caches = (cache_k[j].reshape(bs, lc, -1), cache_v[j].reshape(bs, lc, -1))
                o_s = _stick_attention(q_s, _pad_rows(kb_s, new_rows), _pad_rows(vb_s, new_rows),
                                       sk_valid=lc + t, q_off=past, k_off=past - lc, cache=caches)
                wo = w_b_o[j].astype(BF16)
                states[1].append((kf_p, vf_p, kf_s, vf_s))
            else:
                bias = _band_bias(rel_bias_d[j])
                k_s, n_all = _with_cache(cache_k[j], kb_s, BF16, D_BAND_WIDTH)
                v_s, _ = _with_cache(cache_v[j], vb_s, BF16, D_BAND_WIDTH)
                front = D_PREV_CHUNKS * CHUNK
                o_p = _band_attention(q_p, _front_pad(kb_p, front), _front_pad(vb_p, front),
                                      width=D_BAND_WIDTH, sk_valid=sp, q_off=0, k_off=0,
                                      front=front, n_prev=D_PREV_CHUNKS, bias=bias,
                                      pairs_per_step=D_PAIRS_PER_STEP, shared_kv=False)
                o_s = _band_attention(q_s, k_s, v_s, width=D_BAND_WIDTH, sk_valid=n_all,
                                      q_off=past, k_off=past - lc, front=0,
                                      n_prev=D_PREV_CHUNKS, bias=bias[:, :t, :],
                                      pairs_per_step=D_PAIRS_PER_STEP, shared_kv=False)
                wo = w_d_o[j].astype(BF16)
                keep = min(D_PREV_CHUNKS * CHUNK, sp)
                k_roll = jnp.concatenate([cache_k[j].reshape(bs, lc, -1), kf_s], axis=1)[:, t:]
                v_roll = jnp.concatenate([cache_v[j].reshape(bs, lc, -1), vf_s], axis=1)[:, t:]
                states[3].append((kf_p[:, sp - keep:], vf_p[:, sp - keep:], k_roll, v_roll))
        else:
            order = _c_head_order()
            q_width, kv_width = C_HEADS * HEAD_DIM, C_KV_HEADS * HEAD_DIM
            col = np.concatenate([(order[:, None] * HEAD_DIM + np.arange(HEAD_DIM)).reshape(-1),
                                  np.arange(q_width, q_width + 2 * kv_width)])
            w_qkv = w_c_qkv[j][:, col].astype(BF16)
            b_qkv = b_c_qkv[j][col][None, :]
            sink = sink_c[j][order]
            wo = w_c_o[j].reshape(C_HEADS, HEAD_DIM, d)[order].reshape(q_width, d).astype(BF16)
            lanes = [0, HEAD_DIM]

            def project(x, sh, sc, pos):
                tb = _tables_for(x.shape[0], x.shape[1], pos, C_ROT, lanes)
                return _proj_c(x, g_m, sh, sc, w_qkv, b_qkv, tb)

            q_p, kf_p, vf_p, kb_p, vb_p = project(xp, sh_p, sc_p, pos_p)
            q_s, kf_s, vf_s, kb_s, vb_s = project(xs, sh_s, sc_s, pos_s)
            lc = cache_c_k.shape[2]
            k_s, n_all = _with_cache(cache_c_k[j], kb_s, BF16, C_BAND_WIDTH)
            v_s, _ = _with_cache(cache_c_v[j], vb_s, BF16, C_BAND_WIDTH)
            group = C_HEADS // C_KV_HEADS
            front = C_PREV_CHUNKS * CHUNK
            band_mask = jnp.asarray(_band_mask_tile(Q_TILE, C_BAND_WIDTH, C_PREV_CHUNKS))
            o_p = _band_attention(q_p, _front_pad(kb_p, front), _front_pad(vb_p, front),
                                  width=C_BAND_WIDTH, sk_valid=sp, q_off=0, k_off=0, front=front,
                                  n_prev=C_PREV_CHUNKS, pairs_per_step=group, shared_kv=True, sink=sink,
                                  bias=band_mask)
            o_s = _band_attention(q_s, k_s, v_s, width=C_BAND_WIDTH, sk_valid=n_all, q_off=past,
                                  k_off=past - lc, front=0, n_prev=C_PREV_CHUNKS, pairs_per_step=group, shared_kv=True,
                                  sink=sink, bias=band_mask[:t])
            keep = min(C_PREV_CHUNKS * CHUNK, sp)
            k_roll = jnp.concatenate([cache_c_k[j].reshape(bs, lc, -1), kf_s], axis=1)[:, t:]
            v_roll = jnp.concatenate([cache_c_v[j].reshape(bs, lc, -1), vf_s], axis=1)[:, t:]
            states[2].append((kf_p[:, sp - keep:], vf_p[:, sp - keep:], k_roll, v_roll))

        g_f = g_ffn[i][None, :]
        w_in, w_out = w_ffn_in[i].astype(BF16), w_ffn_out[i].astype(BF16)
        g_out = g_final[None, :] if i == depth - 1 else None
        xp = _block(xp, o_p, wo, gm_p, g_f, shf_p, scf_p, gf_p, w_in, w_out, g_out)
        xs = _block(xs, o_s, wo, gm_s, g_f, shf_s, scf_s, gf_s, w_in, w_out, g_out)

    y_p, y_s = xp, xs

    def stacked(entries, n_heads=None):
        outs = []
        for parts in zip(*entries):
            a = jnp.stack(parts, axis=0)
            if n_heads is not None:
                a = a.reshape(a.shape[:3] + (n_heads, HEAD_DIM))
            outs.append(a)
        return tuple(outs)

    return ((y_p, y_s) + stacked(states[0]) + stacked(states[1], B_HEADS)
            + stacked(states[2], C_KV_HEADS) + stacked(states[3], D_HEADS))
```

```python
import functools
import math

import numpy as np
import jax
import jax.numpy as jnp
from jax import lax
from jax.experimental import pallas as pl
from jax.experimental.pallas import tpu as pltpu

F32 = jnp.float32
BF16 = jnp.bfloat16

CHUNK = 64
HEAD_DIM = 64
ROPE_THETA = 500000.0
NORM_EPS = 1e-6
N_MIXERS = 4
A_HEADS, A_Q_LORA, A_KV_LORA, A_NOPE, A_ROPE, A_V = 16, 384, 256, 64, 32, 64
B_HEADS = 16
C_HEADS, C_KV_HEADS, C_WINDOW, C_ROT = 16, 4, 128, 16
D_HEADS, D_PREV_CHUNKS, D_REL_CLIP = 16, 8, 128
C_PREV_CHUNKS = C_WINDOW // CHUNK

LANES = 128
V7X_VMEM_LIMIT = 56 * 1024 * 1024

TOKEN_TILE = 512
Q_TILE = 256
KV_BLOCK = 256
A_KV_BLOCK = 512
C_BAND_WIDTH = C_PREV_CHUNKS * CHUNK + Q_TILE
D_BAND_WIDTH = D_PREV_CHUNKS * CHUNK + Q_TILE
A_Q_TILE = 256
A_PAIRS_PER_STEP = 2
A_PV_CHUNK = 512
D_PAIRS_PER_STEP = 4
MASK_VALUE = -1e30
LOG2E = math.log2(math.e)
STICK_DEAD_BITS = 150.0


def _cparams(n_axes):
    return pltpu.CompilerParams(
        dimension_semantics=("parallel",) * n_axes, vmem_limit_bytes=V7X_VMEM_LIMIT)


def _adaln_kernel(c_ref, w_ref, b_ref, o_ref):
    c = c_ref[...]
    a = (c * jax.nn.sigmoid(c)).astype(BF16)
    y = jnp.dot(a, w_ref[0].astype(BF16), preferred_element_type=F32) + b_ref[0]
    o_ref[0, 0] = y


def _adaln(c_all, w_mod, b_mod):
    depth, d, d6 = w_mod.shape
    n = c_all.shape[0]
    return pl.pallas_call(
        _adaln_kernel,
        grid=(depth, d6 // d),
        in_specs=[
            pl.BlockSpec((n, d), lambda i, k: (0, 0)),
            pl.BlockSpec((1, d, d), lambda i, k: (i, 0, k)),
            pl.BlockSpec((1, 1, d), lambda i, k: (i, 0, k)),
        ],
        out_specs=pl.BlockSpec((1, 1, n, d), lambda i, k: (i, k, 0, 0)),
        out_shape=jax.ShapeDtypeStruct((depth, d6 // d, n, d), F32),
        compiler_params=_cparams(2),
    )(c_all, w_mod, b_mod.reshape(depth, 1, d6))


def _token_tiling(bx, sx):
    if sx >= TOKEN_TILE:
        ts = TOKEN_TILE
        while sx % ts:
            ts //= 2
        return 1, ts
    nb = min(bx, TOKEN_TILE // sx)
    while bx % nb:
        nb -= 1
    return nb, sx


def _tok_spec(nb, ts, width):
    return pl.BlockSpec((nb, ts, width), lambda b, s: (b, s, 0))


def _mod_spec(nb, d):
    return pl.BlockSpec((nb, 1, d), lambda b, s: (b, 0, 0))


def _const_spec(shape):
    nd = len(shape)
    return pl.BlockSpec(shape, lambda b, s: (0,) * nd)


def _table_spec(rows):
    return pl.BlockSpec((1, rows, LANES), lambda b, s: (s, 0, 0))


def _rms(x):
    return x * lax.rsqrt(jnp.mean(x * x, axis=-1, keepdims=True) + NORM_EPS)


def _modulated(x_ref, g_ref, shift_ref, scale_ref):
    y = _rms(x_ref[...]) * g_ref[...]
    h = y * (1.0 + scale_ref[...]) + shift_ref[...]
    return h.reshape(-1, h.shape[-1]).astype(BF16)


def _rope_lanes(x, cos_t, sin_up, sin_dn, half):
    return (x * cos_t + pltpu.roll(x, half, 1) * sin_up
            + pltpu.roll(x, LANES - half, 1) * sin_dn)


def _store_tok(ref, lo, val):
    nb, ts = ref.shape[0], ref.shape[1]
    w = val.shape[-1]
    ref[:, :, lo:lo + w] = val.reshape(nb, ts, w).astype(ref.dtype)


def _proj_qkv_kernel(x_ref, g_ref, sh_ref, sc_ref, w_ref, q_ref, kf_ref, vf_ref, kb_ref, vb_ref,
                     *, width, q_scale):
    h = _modulated(x_ref, g_ref, sh_ref, sc_ref)
    q = jnp.dot(h, w_ref[:, 0:width], preferred_element_type=F32)
    _store_tok(q_ref, 0, q * q_scale)
    k = jnp.dot(h, w_ref[:, width:2 * width], preferred_element_type=F32)
    _store_tok(kf_ref, 0, k)
    _store_tok(kb_ref, 0, k)
    v = jnp.dot(h, w_ref[:, 2 * width:3 * width], preferred_element_type=F32)
    _store_tok(vf_ref, 0, v)
    _store_tok(vb_ref, 0, v)


def _proj_qkv(x, g, shift, scale, w_bf16, n_heads, q_scale):
    bx, sx, d = x.shape
    width = n_heads * HEAD_DIM
    nb, ts = _token_tiling(bx, sx)
    out = lambda dt: jax.ShapeDtypeStruct((bx, sx, width), dt)
    return pl.pallas_call(
        functools.partial(_proj_qkv_kernel, width=width, q_scale=q_scale),
        grid=(bx // nb, sx // ts),
        in_specs=[_tok_spec(nb, ts, d), _const_spec((1, d)), _mod_spec(nb, d), _mod_spec(nb, d),
                  _const_spec(w_bf16.shape)],
        out_specs=[_tok_spec(nb, ts, width)] * 5,
        out_shape=[out(BF16), out(F32), out(F32), out(BF16), out(BF16)],
        compiler_params=_cparams(2),
    )(x, g, shift, scale, w_bf16)


def _proj_c_kernel(x_ref, g_ref, sh_ref, sc_ref, w_ref, b_ref, tc_ref, tu_ref, td_ref,
                   q_ref, kf_ref, vf_ref, kb_ref, vb_ref, *, q_width, kv_width, q_scale):
    h = _modulated(x_ref, g_ref, sh_ref, sc_ref)
    qkv = jnp.dot(h, w_ref[...], preferred_element_type=F32) + b_ref[...]
    cos_t, sin_up, sin_dn = tc_ref[0], tu_ref[0], td_ref[0]
    half = C_ROT // 2
    for j in range(q_width // LANES):
        xg = qkv[:, j * LANES:(j + 1) * LANES]
        _store_tok(q_ref, j * LANES, _rope_lanes(xg, cos_t, sin_up, sin_dn, half) * q_scale)
    for j in range(kv_width // LANES):
        lo = q_width + j * LANES
        kg = _rope_lanes(qkv[:, lo:lo + LANES], cos_t, sin_up, sin_dn, half)
        _store_tok(kf_ref, j * LANES, kg)
        _store_tok(kb_ref, j * LANES, kg)
    v = qkv[:, q_width + kv_width:q_width + 2 * kv_width]
    _store_tok(vf_ref, 0, v)
    _store_tok(vb_ref, 0, v)


def _proj_c(x, g, shift, scale, w_bf16, bias, tables):
    bx, sx, d = x.shape
    q_width, kv_width = C_HEADS * HEAD_DIM, C_KV_HEADS * HEAD_DIM
    nb, ts = _token_tiling(bx, sx)
    out = lambda w, dt: jax.ShapeDtypeStruct((bx, sx, w), dt)
    return pl.pallas_call(
        functools.partial(_proj_c_kernel, q_width=q_width, kv_width=kv_width,
                          q_scale=HEAD_DIM ** -0.5 * LOG2E),
        grid=(bx // nb, sx // ts),
        in_specs=[_tok_spec(nb, ts, d), _const_spec((1, d)), _mod_spec(nb, d), _mod_spec(nb, d),
                  _const_spec(w_bf16.shape), _const_spec(bias.shape)] + [_table_spec(nb * ts)] * 3,
        out_specs=[_tok_spec(nb, ts, q_width)] + [_tok_spec(nb, ts, kv_width)] * 4,
        out_shape=[out(q_width, BF16), out(kv_width, F32), out(kv_width, F32),
                   out(kv_width, BF16), out(kv_width, BF16)],
        compiler_params=_cparams(2),
    )(x, g, shift, scale, w_bf16, bias, *tables)


def _proj_a_kernel(x_ref, g_ref, sh_ref, sc_ref, wd_ref, gq_ref, gkv_ref, wq_ref, wqr_ref,
                   qc_ref, qs_ref, kc_ref, ku_ref, kd_ref,
                   q_ref, ckv_ref, kr_ref, *, q_scale):
    h = _modulated(x_ref, g_ref, sh_ref, sc_ref)
    down = jnp.dot(h, wd_ref[...], preferred_element_type=F32)
    cq = (_rms(down[:, :A_Q_LORA]) * gq_ref[...]).astype(BF16)
    ckv = _rms(down[:, A_Q_LORA:A_Q_LORA + A_KV_LORA]) * gkv_ref[...]
    _store_tok(ckv_ref, 0, ckv)
    lo = A_Q_LORA + A_KV_LORA
    half = A_ROPE // 2
    kr = _rope_lanes(down[:, lo:lo + LANES], kc_ref[0], ku_ref[0], kd_ref[0], half)
    _store_tok(kr_ref, 0, kr[:, :A_ROPE])
    q = jnp.dot(cq, wq_ref[...], preferred_element_type=F32)
    q_rot = jnp.dot(cq, wqr_ref[...], preferred_element_type=F32)
    qc, qs = qc_ref[0], qs_ref[0]
    for j in range(A_HEADS):
        lanes = slice(j * LANES, (j + 1) * LANES)
        _store_tok(q_ref, j * LANES, (q[:, lanes] * qc + q_rot[:, lanes] * qs) * q_scale)


def _proj_a(x, g, shift, scale, wd_bf16, g_q, g_kv, wq_bf16, wq_rot_bf16, q_tables, k_tables):
    bx, sx, d = x.shape
    nb, ts = _token_tiling(bx, sx)
    return pl.pallas_call(
        functools.partial(_proj_a_kernel, q_scale=(A_NOPE + A_ROPE) ** -0.5 * LOG2E),
        grid=(bx // nb, sx // ts),
        in_specs=[_tok_spec(nb, ts, d), _const_spec((1, d)), _mod_spec(nb, d), _mod_spec(nb, d),
                  _const_spec(wd_bf16.shape), _const_spec(g_q.shape), _const_spec(g_kv.shape),
                  _const_spec(wq_bf16.shape), _const_spec(wq_rot_bf16.shape)]
                 + [_table_spec(nb * ts)] * 5,
        out_specs=[_tok_spec(nb, ts, A_HEADS * LANES), _tok_spec(nb, ts, A_KV_LORA),
                   _tok_spec(nb, ts, A_ROPE)],
        out_shape=[jax.ShapeDtypeStruct((bx, sx, A_HEADS * LANES), BF16),
                   jax.ShapeDtypeStruct((bx, sx, A_KV_LORA), F32),
                   jax.ShapeDtypeStruct((bx, sx, A_ROPE), F32)],
        compiler_params=_cparams(2),
    )(x, g, shift, scale, wd_bf16, g_q, g_kv, wq_bf16, wq_rot_bf16, *q_tables, *k_tables)


def _expand_a_kernel(ckv_ref, kr_ref, wk_ref, wr_ref, wv_ref, k_ref, v_ref):
    ckv = ckv_ref[...]
    ckv = ckv.reshape(-1, ckv.shape[-1]).astype(BF16)
    kr = kr_ref[...]
    kr = kr.reshape(-1, kr.shape[-1]).astype(BF16)
    k = (jnp.dot(ckv, wk_ref[...], preferred_element_type=F32)
         + jnp.dot(kr, wr_ref[...], preferred_element_type=F32))
    _store_tok(k_ref, 0, k)
    _store_tok(v_ref, 0, jnp.dot(ckv, wv_ref[...], preferred_element_type=F32))


def _expand_a(ckv, kr, wk, wr, wv):
    bx, sx, _ = ckv.shape
    nb, ts = _token_tiling(bx, sx)
    return pl.pallas_call(
        _expand_a_kernel,
        grid=(bx // nb, sx // ts),
        in_specs=[_tok_spec(nb, ts, A_KV_LORA), _tok_spec(nb, ts, A_ROPE),
                  _const_spec(wk.shape), _const_spec(wr.shape), _const_spec(wv.shape)],
        out_specs=[_tok_spec(nb, ts, A_HEADS * LANES), _tok_spec(nb, ts, A_HEADS * A_V)],
        out_shape=[jax.ShapeDtypeStruct((bx, sx, A_HEADS * LANES), BF16),
                   jax.ShapeDtypeStruct((bx, sx, A_HEADS * A_V), BF16)],
        compiler_params=_cparams(2),
    )(ckv, kr, wk, wr, wv)


def _latent_decode_kernel(q_ref, kcat_ref, wabs_ref, wv_ref, o_ref, qc_ref,
                          *, sk_valid, q_off, k_off):
    t = q_ref.shape[1]
    n_heads = wabs_ref.shape[0]
    rows = kcat_ref.shape[1]
    for h in range(n_heads):
        qc_ref[h * t:(h + 1) * t, :] = jnp.dot(
            q_ref[0, :, h * LANES:(h + 1) * LANES], wabs_ref[h],
            preferred_element_type=F32).astype(BF16)
    kcat = kcat_ref[0]
    s = lax.dot_general(qc_ref[...], kcat, (((1,), (1,)), ((), ())),
                        preferred_element_type=F32)
    shift = int(math.log2(CHUNK))
    row = lax.broadcasted_iota(jnp.int32, (n_heads * t, 1), 0)
    qpos = row - row // t * t + q_off
    kidx = lax.broadcasted_iota(jnp.int32, (1, rows), 1)
    vis = (kidx < sk_valid) & (jnp.right_shift(kidx + k_off, shift) <= jnp.right_shift(qpos, shift))
    s = jnp.where(vis, s, MASK_VALUE)
    m = jnp.max(s, axis=1, keepdims=True)
    p = jnp.exp2(s - m)
    l = jnp.sum(p, axis=1, keepdims=True)
    o_lat = (jnp.dot(p.astype(BF16), kcat[:, :A_KV_LORA], preferred_element_type=F32) / l
             ).astype(BF16)
    out = jnp.zeros((t, o_ref.shape[-1]), F32)
    for h in range(n_heads):
        out = out + jnp.dot(o_lat[h * t:(h + 1) * t], wv_ref[h], preferred_element_type=F32)
    o_ref[0] = out.astype(o_ref.dtype)


def _latent_decode(q, kcat, wabs, wv_placed, *, sk_valid, q_off, k_off):
    bx, t, _ = q.shape
    rows, width = kcat.shape[1:]
    n_heads = wabs.shape[0]
    out_w = wv_placed.shape[-1]
    return pl.pallas_call(
        functools.partial(_latent_decode_kernel, sk_valid=sk_valid, q_off=q_off, k_off=k_off),
        grid=(bx,),
        in_specs=[pl.BlockSpec((1, t, n_heads * LANES), lambda b: (b, 0, 0)),
                  pl.BlockSpec((1, rows, width), lambda b: (b, 0, 0)),
                  pl.BlockSpec(wabs.shape, lambda b: (0, 0, 0)),
                  pl.BlockSpec(wv_placed.shape, lambda b: (0, 0, 0))],
        out_specs=pl.BlockSpec((1, t, out_w), lambda b: (b, 0, 0)),
        out_shape=jax.ShapeDtypeStruct((bx, t, out_w), BF16),
        scratch_shapes=[pltpu.VMEM((n_heads * t, width), BF16)],
        compiler_params=_cparams(1),
    )(q, kcat, wabs, wv_placed)


def _block_kernel(x_ref, o_ref, wo_ref, gm_ref, g_ref, sh_ref, sc_ref, gf_ref, wi_ref, wout_ref,
                  *out_refs, hidden, chunk, final_norm):
    gout_ref, y_ref = out_refs if final_norm else (None,) + out_refs
    nb, ts, d = x_ref.shape
    o = o_ref[...].reshape(nb * ts, -1)
    mix = jnp.dot(o, wo_ref[...], preferred_element_type=F32).reshape(nb, ts, d)
    x1 = x_ref[...] + gm_ref[...] * mix
    h = (_rms(x1) * g_ref[...]) * (1.0 + sc_ref[...]) + sh_ref[...]
    h = h.reshape(nb * ts, d).astype(BF16)
    acc = jnp.zeros((nb * ts, d), F32)
    for c in range(hidden // chunk):
        gate = jnp.dot(h, wi_ref[:, c * chunk:(c + 1) * chunk], preferred_element_type=F32)
        up = jnp.dot(h, wi_ref[:, hidden + c * chunk:hidden + (c + 1) * chunk],
                     preferred_element_type=F32)
        act = (gate * jax.nn.sigmoid(gate) * up).astype(BF16)
        acc = acc + jnp.dot(act, wout_ref[c * chunk:(c + 1) * chunk, :],
                            preferred_element_type=F32)
    x2 = x1 + gf_ref[...] * acc.reshape(nb, ts, d)
    y_ref[...] = _rms(x2) * gout_ref[...] if final_norm else x2


def _block(x, o, wo, gate_m, g_ffn, shift_f, scale_f, gate_f, w_in, w_out, g_out=None):
    bx, sx, d = x.shape
    hidden = w_out.shape[0]
    nb, ts = _token_tiling(bx, sx)
    resident = lambda shape: pl.BlockSpec(shape, lambda b, s: (0,) * len(shape),
                                          pipeline_mode=pl.Buffered(1))
    in_specs = [_tok_spec(nb, ts, d), _tok_spec(nb, ts, o.shape[-1]), resident(wo.shape),
                _mod_spec(nb, d), _const_spec((1, d)), _mod_spec(nb, d), _mod_spec(nb, d),
                _mod_spec(nb, d), resident(w_in.shape), resident(w_out.shape)]
    args = [x, o, wo, gate_m, g_ffn, shift_f, scale_f, gate_f, w_in, w_out]
    if g_out is not None:
        in_specs.append(_const_spec((1, d)))
        args.append(g_out)
    return pl.pallas_call(
        functools.partial(_block_kernel, hidden=hidden, chunk=256, final_norm=g_out is not None),
        grid=(bx // nb, sx // ts),
        in_specs=in_specs,
        out_specs=_tok_spec(nb, ts, d),
        out_shape=jax.ShapeDtypeStruct((bx, sx, d), F32),
        compiler_params=_cparams(2),
    )(*args)


def _stick_kernel(*refs, tq, bk, sk_valid, q_off, k_off, cache_blocks):
    if cache_blocks:
        q_ref, k_ref, v_ref, kc_ref, vc_ref, tri_ref, o_ref, run_ref, acc_ref = refs
    else:
        q_ref, k_ref, v_ref, tri_ref, o_ref, run_ref, acc_ref = refs
    qpos0 = pl.program_id(2) * tq + q_off
    kb_hi = (jnp.minimum(qpos0 + (tq - 1) - k_off, sk_valid) + bk - 1) // bk
    f_hi = jnp.minimum(jnp.minimum(qpos0 - k_off, sk_valid) // bk, kb_hi)

    def visible(kb):
        row = lax.broadcasted_iota(jnp.int32, (2 * tq, 1), 0)
        qpos = jnp.where(row < tq, row, row - tq) + qpos0
        ik = kb * bk + lax.broadcasted_iota(jnp.int32, (1, bk), 1)
        return (ik + k_off < qpos) & (ik < sk_valid)

    lane_half = lax.broadcasted_iota(jnp.int32, (1, LANES), 1) // HEAD_DIM
    q_pair = q_ref[0]
    q_both = jnp.concatenate(
        [jnp.where(lane_half == hh, q_pair, jnp.zeros((), BF16)) for hh in range(2)], axis=0)

    def rows(kb):
        return pl.ds(pl.multiple_of(kb * bk, bk), bk)

    def block(kb, masked, run):
        if cache_blocks and not masked:
            k_blk = kc_ref[0, rows(kb), :]
            v_blk = vc_ref[0, rows(kb), :]
        else:
            k_blk = k_ref[0, rows(kb - cache_blocks), :]
            v_blk = v_ref[0, rows(kb - cache_blocks), :]
        y = lax.dot_general(q_both, k_blk, (((1,), (1,)), ((), ())),
                            preferred_element_type=F32)
        sp = jnp.maximum(y, 0.0) + jnp.log2(1.0 + jnp.exp2(-jnp.abs(y)))
        if masked:
            vis = visible(kb)
            sp = jnp.where(vis, sp, 0.0)
        suffix = jnp.dot(sp.astype(BF16), tri_ref[...], preferred_element_type=F32)
        a = jnp.exp2(y - suffix - run)
        if masked:
            a = jnp.where(vis, a, 0.0)
        return (jnp.dot(a.astype(BF16), v_blk, preferred_element_type=F32),
                jnp.sum(sp, axis=1, keepdims=True))

    @pl.when(f_hi > 0)
    def _():
        out_d, sum_d = block(kb_hi - 1, True, 0.0)
        out_f, sum_f = block(f_hi - 1, False, sum_d)
        acc_ref[...] = out_d + out_f
        run_ref[...] = sum_d + sum_f

    @pl.when(f_hi == 0)
    def _():
        out_d, sum_d = block(kb_hi - 1, True, 0.0)
        acc_ref[...] = out_d
        run_ref[...] = sum_d

    def earlier(st):
        out, row_sum = block(f_hi - 1 - st[0], False, run_ref[...])
        acc_ref[...] += out
        run_ref[...] += row_sum
        return st[0] + 1, jnp.min(run_ref[...])

    lax.while_loop(lambda st: (st[0] < f_hi) & (st[1] < STICK_DEAD_BITS), earlier,
                   (jnp.ones((), jnp.int32), jnp.min(run_ref[...])))
    o_ref[0] = jnp.where(lane_half == 0, acc_ref[:tq], acc_ref[tq:]).astype(o_ref.dtype)


def _stick_attention(q, k, v, *, sk_valid, q_off, k_off, cache=None):
    bx, sq, qw = q.shape
    rows = k.shape[1]
    n_pairs = qw // LANES
    tq = min(Q_TILE, sq)
    bk = KV_BLOCK
    assert sq % tq == 0 and rows % bk == 0 and tq % 8 == 0 and q_off >= k_off >= 0
    for q0 in range(0, sq, tq):
        first, last = q_off + q0 - k_off, min(q_off + q0 + tq - 1 - k_off, sk_valid)
        assert first <= sk_valid and -(-last // bk) - first // bk == 1
    kv_spec = lambda n: pl.BlockSpec((1, n, LANES), lambda b, p, i: (b, 0, p))
    in_specs = [pl.BlockSpec((1, tq, LANES), lambda b, p, i: (b, i, p)), kv_spec(rows),
                kv_spec(rows)]
    args = [q, k, v]
    cache_blocks = 0
    if cache is not None:
        n_cache = cache[0].shape[1]
        assert n_cache % bk == 0 and q_off - k_off == n_cache and sq == tq
        cache_blocks = n_cache // bk
        in_specs += [kv_spec(n_cache), kv_spec(n_cache)]
        args += list(cache)
    idx = np.arange(bk)
    in_specs.append(pl.BlockSpec((bk, bk), lambda b, p, i: (0, 0)))
    args.append(jnp.asarray(idx[:, None] >= idx[None, :], BF16))
    return pl.pallas_call(
        functools.partial(_stick_kernel, tq=tq, bk=bk, sk_valid=sk_valid, q_off=q_off,
                          k_off=k_off, cache_blocks=cache_blocks),
        grid=(bx, n_pairs, sq // tq),
        in_specs=in_specs,
        out_specs=pl.BlockSpec((1, tq, LANES), lambda b, p, i: (b, i, p)),
        out_shape=jax.ShapeDtypeStruct((bx, sq, n_pairs * LANES), BF16),
        scratch_shapes=[pltpu.VMEM((2 * tq, 1), F32), pltpu.VMEM((2 * tq, LANES), F32)],
        compiler_params=_cparams(3),
    )(*args)


def _row_end(qpos, k_off):
    return (qpos // CHUNK + 1) * CHUNK - k_off


def _causal_kernel(q_ref, k_ref, v_ref, o_ref, *, tq, span, n_spans, sk_valid, q_off, k_off,
                   single_tile):
    qpos0 = pl.program_id(2) * tq + q_off
    n_needed = (jnp.minimum(_row_end(qpos0 + tq - 1, k_off), sk_valid) - 1) // span
    shift = int(math.log2(CHUNK))
    lane_half = lax.broadcasted_iota(jnp.int32, (1, LANES), 1) // HEAD_DIM
    nt = (((1,), (1,)), ((), ()))

    for n_full in range(n_spans):
        if single_tile and n_full != (min(_row_end(q_off + tq - 1, k_off), sk_valid) - 1) // span:
            continue

        @pl.when(n_needed == n_full)
        def _(n_full=n_full):
            full, width = n_full * span, (n_full + 1) * span
            qpos = lax.broadcasted_iota(jnp.int32, (tq, 1), 0) + qpos0
            kidx = lax.broadcasted_iota(jnp.int32, (1, span), 1) + full
            vis = (kidx < sk_valid) & (jnp.right_shift(kidx + k_off, shift)
                                       <= jnp.right_shift(qpos, shift))
            outs = []
            for hh in range(2 * (q_ref.shape[-1] // (2 * LANES))):
                lanes = slice(hh * LANES, (hh + 1) * LANES)
                v_lanes = slice(hh // 2 * LANES, (hh // 2 + 1) * LANES)
                q_h = q_ref[0, :, lanes]
                s_tail = lax.dot_general(q_h, k_ref[0, full:width, lanes], nt,
                                         preferred_element_type=F32)
                s_tail = jnp.where(vis, s_tail, MASK_VALUE)
                m = jnp.max(s_tail, axis=1, keepdims=True)
                if n_full:
                    s_full = lax.dot_general(q_h, k_ref[0, 0:full, lanes], nt,
                                             preferred_element_type=F32)
                    m = jnp.maximum(m, jnp.max(s_full, axis=1, keepdims=True))
                p_tail = jnp.exp2(s_tail - m)
                l = jnp.sum(p_tail, axis=1, keepdims=True)
                o = jnp.dot(p_tail.astype(BF16), v_ref[0, full:width, v_lanes],
                            preferred_element_type=F32)
                for c0 in range(0, full, A_PV_CHUNK):
                    p_c = jnp.exp2(s_full[:, c0:c0 + A_PV_CHUNK] - m)
                    l = l + jnp.sum(p_c, axis=1, keepdims=True)
                    o = o + jnp.dot(p_c.astype(BF16), v_ref[0, c0:c0 + A_PV_CHUNK, v_lanes],
                                    preferred_element_type=F32)
                outs.append(o / l)
            for pr in range(len(outs) // 2):
                o_ref[0, :, pr * LANES:(pr + 1) * LANES] = jnp.where(
                    lane_half == 0, outs[2 * pr], outs[2 * pr + 1]).astype(o_ref.dtype)


def _causal_attention(q, k, v, *, span, sk_valid, q_off, k_off):
    bx, sq, qw = q.shape
    rows = k.shape[1]
    pps = A_PAIRS_PER_STEP
    n_pairs = qw // (2 * LANES)
    assert n_pairs % pps == 0
    tq = min(A_Q_TILE, sq)
    assert sq % tq == 0 and rows % span == 0 and q_off % CHUNK == 0 and k_off >= 0
    for q0 in range(0, sq, tq):
        end = min(_row_end(q_off + q0 + tq - 1, k_off), sk_valid)
        first_row_end = min(_row_end(q_off + q0, k_off), sk_valid)
        assert 0 < end <= rows and (end - 1) // span * span <= first_row_end
    return pl.pallas_call(
        functools.partial(_causal_kernel, tq=tq, span=span, n_spans=rows // span,
                          sk_valid=sk_valid, q_off=q_off, k_off=k_off, single_tile=sq == tq),
        grid=(bx, n_pairs // pps, sq // tq),
        in_specs=[
            pl.BlockSpec((1, tq, pps * 2 * LANES), lambda b, p, i: (b, i, p)),
            pl.BlockSpec((1, rows, pps * 2 * LANES), lambda b, p, i: (b, 0, p)),
            pl.BlockSpec((1, rows, pps * LANES), lambda b, p, i: (b, 0, p)),
        ],
        out_specs=pl.BlockSpec((1, tq, pps * LANES), lambda b, p, i: (b, i, p)),
        out_shape=jax.ShapeDtypeStruct((bx, sq, n_pairs * LANES), BF16),
        compiler_params=_cparams(3),
    )(q, k, v)


def _band_kernel(*refs, tq, width, sk_valid, q_off, k_off, front, n_prev, use_sink, per_head_bias):
    refs = list(refs)
    q_ref, k_ref, v_ref = refs[:3]
    rest = refs[3:]
    sink_ref = rest.pop(0) if use_sink else None
    bias_ref, o_ref = rest

    group = q_ref.shape[-1] // LANES
    first_pair = pl.program_id(1) * group
    qpos0 = pl.program_id(2) * tq + q_off
    start = pl.multiple_of((qpos0 // CHUNK - n_prev) * CHUNK - k_off + front, CHUNK)
    band = pl.ds(start, width)
    kidx = lax.broadcasted_iota(jnp.int32, (1, width), 1) + (start - front)
    invalid = jnp.where((kidx >= 0) & (kidx < sk_valid), 0.0, MASK_VALUE)

    lane_half = lax.broadcasted_iota(jnp.int32, (1, LANES), 1) // HEAD_DIM
    shared_kv = k_ref.shape[-1] == LANES
    for j in range(group):
        q_pair = q_ref[0, :, j * LANES:(j + 1) * LANES]
        kv_lanes = slice(0, LANES) if shared_kv else slice(j * LANES, (j + 1) * LANES)
        k_band = k_ref[0, band, kv_lanes]
        v_band = v_ref[0, band, kv_lanes]
        outs = []
        for hh in range(2):
            q_h = jnp.where(lane_half == hh, q_pair, jnp.zeros((), BF16))
            s = lax.dot_general(q_h, k_band, (((1,), (1,)), ((), ())),
                                preferred_element_type=F32)
            s = s + (bias_ref[2 * j + hh] if per_head_bias else bias_ref[...]) + invalid
            m = jnp.max(s, axis=1, keepdims=True)
            if use_sink:
                sink = sink_ref[2 * (first_pair + j) + hh] * LOG2E
                m = jnp.maximum(m, sink)
            p = jnp.exp2(s - m)
            l = jnp.sum(p, axis=1, keepdims=True)
            if use_sink:
                l = l + jnp.exp2(sink - m)
            outs.append(jnp.dot(p.astype(BF16), v_band, preferred_element_type=F32) / l)
        o_ref[0, :, j * LANES:(j + 1) * LANES] = jnp.where(
            lane_half == 0, outs[0], outs[1]).astype(o_ref.dtype)


def _band_mask_tile(rows, width, n_prev):
    r_chunk = np.arange(rows)[:, None] // CHUNK
    c_chunk = np.arange(width)[None, :] // CHUNK - n_prev
    return np.where((c_chunk <= r_chunk) & (c_chunk >= r_chunk - n_prev), 0.0, MASK_VALUE
                    ).astype(np.float32)


def _band_attention(q, k, v, *, width, sk_valid, q_off, k_off, front, n_prev, pairs_per_step,
                    shared_kv, sink=None, bias=None):
    kv_group = pairs_per_step
    kv_lanes = LANES if shared_kv else pairs_per_step * LANES
    bx, sq, qw = q.shape
    rows = k.shape[1]
    n_pairs = qw // LANES
    tq = min(Q_TILE, sq)
    assert sq % tq == 0 and q_off % CHUNK == 0 and (tq % CHUNK == 0 or sq == tq)
    assert ((q_off + sq - tq) // CHUNK - n_prev) * CHUNK - k_off + front + width <= rows
    assert (q_off // CHUNK - n_prev) * CHUNK - k_off + front >= 0
    in_specs = [
        pl.BlockSpec((1, tq, kv_group * LANES), lambda b, g, i: (b, i, g)),
        pl.BlockSpec((1, rows, kv_lanes), lambda b, g, i: (b, 0, g)),
        pl.BlockSpec((1, rows, kv_lanes), lambda b, g, i: (b, 0, g)),
    ]
    args = [q, k, v]
    if sink is not None:
        in_specs.append(pl.BlockSpec(memory_space=pltpu.SMEM))
        args.append(sink)
    per_head_bias = bias.ndim == 3
    if per_head_bias:
        in_specs.append(pl.BlockSpec((2 * kv_group, tq, width), lambda b, g, i: (g, 0, 0)))
    else:
        in_specs.append(pl.BlockSpec((tq, width), lambda b, g, i: (0, 0)))
    args.append(bias)
    return pl.pallas_call(
        functools.partial(_band_kernel, tq=tq, width=width, sk_valid=sk_valid, q_off=q_off,
                          k_off=k_off, front=front, n_prev=n_prev,
                          use_sink=sink is not None, per_head_bias=per_head_bias),
        grid=(bx, n_pairs // kv_group, sq // tq),
        in_specs=in_specs,
        out_specs=pl.BlockSpec((1, tq, kv_group * LANES), lambda b, g, i: (b, i, g)),
        out_shape=jax.ShapeDtypeStruct((bx, sq, n_pairs * LANES), BF16),
        compiler_params=_cparams(3),
    )(*args)


def _band_bias_kernel(e_ref, mask_ref, o_ref, *, width):
    w = e_ref.shape[-1]
    x = jnp.broadcast_to(e_ref[0], (Q_TILE, w))
    toeplitz = pltpu.roll(x, 0, 1, stride=1, stride_axis=0)
    o_ref[0] = toeplitz[:, :width] * LOG2E + mask_ref[...]


def _band_bias(rel_bias):
    heads = rel_bias.shape[0]
    clip = (rel_bias.shape[1] - 1) // 2
    band = D_PREV_CHUNKS * CHUNK
    width = D_BAND_WIDTH
    w = width + Q_TILE
    assert band >= clip and width > band + clip
    top = jnp.broadcast_to(rel_bias[:, -1:], (heads, band - clip + 1))
    mid = jnp.flip(rel_bias[:, :2 * clip], axis=1)
    low = jnp.broadcast_to(rel_bias[:, :1], (heads, width - (band + clip + 1)))
    neg = jnp.broadcast_to(rel_bias[:, -1:], (heads, w - width))
    e = jnp.concatenate([top, mid, low, neg], axis=1)[:, None, :]
    return pl.pallas_call(
        functools.partial(_band_bias_kernel, width=width),
        grid=(heads,),
        in_specs=[pl.BlockSpec((1, 1, w), lambda h: (h, 0, 0)),
                  pl.BlockSpec((Q_TILE, width), lambda h: (0, 0))],
        out_specs=pl.BlockSpec((1, Q_TILE, width), lambda h: (h, 0, 0)),
        out_shape=jax.ShapeDtypeStruct((heads, Q_TILE, width), F32),
        compiler_params=_cparams(1),
    )(e, jnp.asarray(_band_mask_tile(Q_TILE, width, D_PREV_CHUNKS)))


def _rope_tables(pos, n_rot, lane_offsets, rows_repeat=1):
    half = n_rot // 2
    inv = ROPE_THETA ** (-jnp.arange(half, dtype=F32) * 2.0 / n_rot)
    ang = pos.astype(F32)[:, None] * inv[None, :]
    cos, sin = jnp.cos(ang), jnp.sin(ang)
    n = pos.shape[0]
    cos_t = jnp.ones((n, LANES), F32)
    sin_up = jnp.zeros((n, LANES), F32)
    sin_dn = jnp.zeros((n, LANES), F32)
    for o in lane_offsets:
        cos_t = cos_t.at[:, o:o + half].set(cos).at[:, o + half:o + n_rot].set(cos)
        sin_dn = sin_dn.at[:, o:o + half].set(-sin)
        sin_up = sin_up.at[:, o + half:o + n_rot].set(sin)
    def tiles(t):
        if rows_repeat > 1:
            return jnp.tile(t, (rows_repeat, 1))[None]
        rows = min(n, TOKEN_TILE)
        return t.reshape(n // rows, rows, LANES)
    return tiles(cos_t), tiles(sin_up), tiles(sin_dn)


def _tables_for(bx, sx, pos, n_rot, lane_offsets):
    nb, ts = _token_tiling(bx, sx)
    return _rope_tables(pos, n_rot, lane_offsets, rows_repeat=nb if nb > 1 else 1)


def _c_head_order():
    rep = C_HEADS // C_KV_HEADS
    order = []
    for p in range(C_HEADS // 2):
        g2, i = divmod(p, rep)
        order += [rep * (2 * g2) + i, rep * (2 * g2 + 1) + i]
    return np.asarray(order)


def _pad_rows(a, rows):
    return jnp.pad(a, ((0, 0), (0, rows - a.shape[1]), (0, 0)))


def _front_pad(a, rows):
    return jnp.pad(a, ((0, 0), (rows, 0), (0, 0)))


def _with_cache(cache, new, dtype, block=KV_BLOCK):
    full = jnp.concatenate([cache.reshape(cache.shape[0], cache.shape[1], -1).astype(dtype),
                            new.astype(dtype)], axis=1)
    rows = -(-full.shape[1] // block) * block
    return _pad_rows(full, rows), full.shape[1]


def kernel(x_prompt, x_sample, c_prompt, c_sample, cache_a_ckv, cache_a_krope, cache_b_k, cache_b_v,
           cache_c_k, cache_c_v, cache_d_k, cache_d_v, w_mod, b_mod, g_mix, g_ffn, w_ffn_in, w_ffn_out,
           w_a_down, g_a_q, g_a_kv, w_a_uq, w_a_uk, w_a_uv, w_a_o, w_b_qkv, w_b_o,
           w_c_qkv, b_c_qkv, sink_c, w_c_o, w_d_qkv, rel_bias_d, w_d_o, g_final):
    bp, sp, d = x_prompt.shape
    bs, t, _ = x_sample.shape
    depth = w_mod.shape[0]
    past = cache_a_ckv.shape[2]
    pos_p = jnp.arange(sp)
    pos_s = past + jnp.arange(t)

    mods = _adaln(jnp.concatenate([c_prompt, c_sample], axis=0), w_mod, b_mod)

    def mod(i, k):
        m = mods[i, k][:, None, :]
        return m[:bp], m[bp:]

    xp, xs = x_prompt, x_sample
    states = [[] for _ in range(N_MIXERS)]
    for i in range(depth):
        m, j = i % N_MIXERS, i // N_MIXERS
        (sh_p, sh_s), (sc_p, sc_s), (gm_p, gm_s) = mod(i, 0), mod(i, 1), mod(i, 2)
        (shf_p, shf_s), (scf_p, scf_s), (gf_p, gf_s) = mod(i, 3), mod(i, 4), mod(i, 5)
        g_m = g_mix[i][None, :]
        if m == 0:
            n_down = A_Q_LORA + A_KV_LORA + A_ROPE
            wd = jnp.pad(w_a_down[j], ((0, 0), (0, A_Q_LORA + A_KV_LORA + LANES - n_down))).astype(BF16)
            wq = w_a_uq[j].reshape(A_Q_LORA, A_HEADS, A_NOPE + A_ROPE)
            wq = jnp.pad(wq, ((0, 0), (0, 0), (0, LANES - A_NOPE - A_ROPE)))
            x1 = wq[:, :, A_NOPE:A_NOPE + A_ROPE // 2]
            x2 = wq[:, :, A_NOPE + A_ROPE // 2:A_NOPE + A_ROPE]
            wq_rot = jnp.zeros_like(wq).at[:, :, A_NOPE:A_NOPE + A_ROPE].set(
                jnp.concatenate([-x2, x1], axis=-1))
            wq = wq.reshape(A_Q_LORA, A_HEADS * LANES).astype(BF16)
            wq_rot = wq_rot.reshape(A_Q_LORA, A_HEADS * LANES).astype(BF16)
            wk = jnp.pad(w_a_uk[j], ((0, 0), (0, 0), (0, LANES - A_NOPE)))
            wk = wk.reshape(A_KV_LORA, A_HEADS * LANES).astype(BF16)
            place = jnp.zeros((A_ROPE, A_HEADS, LANES), F32)
            place = place.at[jnp.arange(A_ROPE), :, A_NOPE + jnp.arange(A_ROPE)].set(1.0)
            wr = place.reshape(A_ROPE, A_HEADS * LANES).astype(BF16)
            wv = w_a_uv[j].reshape(A_KV_LORA, A_HEADS * A_V).astype(BF16)
            g_q, g_kv = g_a_q[j][None, :], g_a_kv[j][None, :]

            def project(x, sh, sc, pos):
                q_cos, q_up, q_dn = _tables_for(x.shape[0], x.shape[1], pos, A_ROPE, [A_NOPE])
                kt = _tables_for(x.shape[0], x.shape[1], pos, A_ROPE, [0])
                return _proj_a(x, g_m, sh, sc, wd, g_q, g_kv, wq, wq_rot, (q_cos, q_up - q_dn), kt)

            q_p, ckv_p, kr_p = project(xp, sh_p, sc_p, pos_p)
            q_s, ckv_s, kr_s = project(xs, sh_s, sc_s, pos_s)
            k_p, v_p = _expand_a(ckv_p, kr_p, wk, wr, wv)
            o_p = _causal_attention(q_p, k_p, v_p, span=min(A_KV_BLOCK, sp), sk_valid=sp,
                                    q_off=0, k_off=0)
            n_all = past + t
            rows = -(-n_all // LANES) * LANES
            kcat = jnp.concatenate(
                [jnp.concatenate([cache_a_ckv[j], ckv_s], axis=1),
                 jnp.concatenate([cache_a_krope[j], kr_s], axis=1)], axis=-1).astype(BF16)
            kcat = jnp.pad(kcat, ((0, 0), (0, rows - n_all),
                                  (0, A_KV_LORA + LANES - kcat.shape[-1])))
            wabs = jnp.zeros((A_HEADS, LANES, A_KV_LORA + LANES), F32)
            wabs = wabs.at[:, :A_NOPE, :A_KV_LORA].set(w_a_uk[j].transpose(1, 2, 0))
            wabs = wabs.at[:, A_NOPE + jnp.arange(A_ROPE), A_KV_LORA + jnp.arange(A_ROPE)].set(1.0)
            wv_placed = jnp.zeros((A_HEADS, A_KV_LORA, A_HEADS, A_V), F32)
            wv_placed = wv_placed.at[jnp.arange(A_HEADS), :, jnp.arange(A_HEADS), :].set(
                w_a_uv[j].transpose(1, 0, 2))
            wv_placed = wv_placed.reshape(A_HEADS, A_KV_LORA, A_HEADS * A_V)
            o_s = _latent_decode(q_s, kcat, wabs.astype(BF16), wv_placed.astype(BF16),
                                 sk_valid=n_all, q_off=past, k_off=0)
            wo = w_a_o[j].astype(BF16)
            states[0].append((ckv_p, kr_p, ckv_s, kr_s))
        elif m == 1 or m == 3:
            heads = B_HEADS if m == 1 else D_HEADS
            w_qkv = (w_b_qkv if m == 1 else w_d_qkv)[j].astype(BF16)
            cache_k, cache_v = (cache_b_k, cache_b_v) if m == 1 else (cache_d_k, cache_d_v)
            q_scale = HEAD_DIM ** -0.5 * LOG2E
            q_p, kf_p, vf_p, kb_p, vb_p = _proj_qkv(xp, g_m, sh_p, sc_p, w_qkv, heads, q_scale)
            q_s, kf_s, vf_s, kb_s, vb_s = _proj_qkv(xs, g_m, sh_s, sc_s, w_qkv, heads, q_scale)
            lc = cache_k.shape[2]
            if m == 1:
                o_p = _stick_attention(q_p, kb_p, vb_p, sk_valid=sp, q_off=0, k_off=0)
                new_rows = -(-t // KV_BLOCK) * KV_BLOCK
                caches = (cache_k[j].reshape(bs, lc, -1).astype(BF16),
                          cache_v[j].reshape(bs, lc, -1).astype(BF16))
                o_s = _stick_attention(q_s, _pad_rows(kb_s, new_rows), _pad_rows(vb_s, new_rows),
                                       sk_valid=lc + t, q_off=past, k_off=past - lc, cache=caches)
                wo = w_b_o[j].astype(BF16)
                states[1].append((kf_p, vf_p, kf_s, vf_s))
            else:
                bias = _band_bias(rel_bias_d[j])
                k_s, n_all = _with_cache(cache_k[j], kb_s, BF16, D_BAND_WIDTH)
                v_s, _ = _with_cache(cache_v[j], vb_s, BF16, D_BAND_WIDTH)
                front = D_PREV_CHUNKS * CHUNK
                o_p = _band_attention(q_p, _front_pad(kb_p, front), _front_pad(vb_p, front),
                                      width=D_BAND_WIDTH, sk_valid=sp, q_off=0, k_off=0,
                                      front=front, n_prev=D_PREV_CHUNKS, bias=bias,
                                      pairs_per_step=D_PAIRS_PER_STEP, shared_kv=False)
                o_s = _band_attention(q_s, k_s, v_s, width=D_BAND_WIDTH, sk_valid=n_all,
                                      q_off=past, k_off=past - lc, front=0,
                                      n_prev=D_PREV_CHUNKS, bias=bias[:, :t, :],
                                      pairs_per_step=D_PAIRS_PER_STEP, shared_kv=False)
                wo = w_d_o[j].astype(BF16)
                keep = min(D_PREV_CHUNKS * CHUNK, sp)
                k_roll = jnp.concatenate([cache_k[j].reshape(bs, lc, -1), kf_s], axis=1)[:, t:]
                v_roll = jnp.concatenate([cache_v[j].reshape(bs, lc, -1), vf_s], axis=1)[:, t:]
                states[3].append((kf_p[:, sp - keep:], vf_p[:, sp - keep:], k_roll, v_roll))
        else:
            order = _c_head_order()
            q_width, kv_width = C_HEADS * HEAD_DIM, C_KV_HEADS * HEAD_DIM
            col = np.concatenate([(order[:, None] * HEAD_DIM + np.arange(HEAD_DIM)).reshape(-1),
                                  np.arange(q_width, q_width + 2 * kv_width)])
            w_qkv = w_c_qkv[j][:, col].astype(BF16)
            b_qkv = b_c_qkv[j][col][None, :]
            sink = sink_c[j][order]
            wo = w_c_o[j].reshape(C_HEADS, HEAD_DIM, d)[order].reshape(q_width, d).astype(BF16)
            lanes = [0, HEAD_DIM]

            def project(x, sh, sc, pos):
                tb = _tables_for(x.shape[0], x.shape[1], pos, C_ROT, lanes)
                return _proj_c(x, g_m, sh, sc, w_qkv, b_qkv, tb)

            q_p, kf_p, vf_p, kb_p, vb_p = project(xp, sh_p, sc_p, pos_p)
            q_s, kf_s, vf_s, kb_s, vb_s = project(xs, sh_s, sc_s, pos_s)
            lc = cache_c_k.shape[2]
            k_s, n_all = _with_cache(cache_c_k[j], kb_s, BF16, C_BAND_WIDTH)
            v_s, _ = _with_cache(cache_c_v[j], vb_s, BF16, C_BAND_WIDTH)
            group = C_HEADS // C_KV_HEADS
            front = C_PREV_CHUNKS * CHUNK
            band_mask = jnp.asarray(_band_mask_tile(Q_TILE, C_BAND_WIDTH, C_PREV_CHUNKS))
            o_p = _band_attention(q_p, _front_pad(kb_p, front), _front_pad(vb_p, front),
                                  width=C_BAND_WIDTH, sk_valid=sp, q_off=0, k_off=0, front=front,
                                  n_prev=C_PREV_CHUNKS, pairs_per_step=group, shared_kv=True, sink=sink,
                                  bias=band_mask)
            o_s = _band_attention(q_s, k_s, v_s, width=C_BAND_WIDTH, sk_valid=n_all, q_off=past,
                                  k_off=past - lc, front=0, n_prev=C_PREV_CHUNKS, pairs_per_step=group, shared_kv=True,
                                  sink=sink, bias=band_mask[:t])
            keep = min(C_PREV_CHUNKS * CHUNK, sp)
            k_roll = jnp.concatenate([cache_c_k[j].reshape(bs, lc, -1), kf_s], axis=1)[:, t:]
            v_roll = jnp.concatenate([cache_c_v[j].reshape(bs, lc, -1), vf_s], axis=1)[:, t:]
            states[2].append((kf_p[:, sp - keep:], vf_p[:, sp - keep:], k_roll, v_roll))

        g_f = g_ffn[i][None, :]
        w_in, w_out = w_ffn_in[i].astype(BF16), w_ffn_out[i].astype(BF16)
        g_out = g_final[None, :] if i == depth - 1 else None
        xp = _block(xp, o_p, wo, gm_p, g_f, shf_p, scf_p, gf_p, w_in, w_out, g_out)
        xs = _block(xs, o_s, wo, gm_s, g_f, shf_s, scf_s, gf_s, w_in, w_out, g_out)

    y_p, y_s = xp, xs

    def stacked(entries, n_heads=None):
        outs = []
        for parts in zip(*entries):
            a = jnp.stack(parts, axis=0)
            if n_heads is not None:
                a = a.reshape(a.shape[:3] + (n_heads, HEAD_DIM))
            outs.append(a)
        return tuple(outs)

    return ((y_p, y_s) + stacked(states[0]) + stacked(states[1], B_HEADS)
            + stacked(states[2], C_KV_HEADS) + stacked(states[3], D_HEADS))
```

```python
import functools
import math

import numpy as np
import jax
import jax.numpy as jnp
from jax import lax
from jax.experimental import pallas as pl
from jax.experimental.pallas import tpu as pltpu

F32 = jnp.float32
BF16 = jnp.bfloat16

CHUNK = 64
HEAD_DIM = 64
ROPE_THETA = 500000.0
NORM_EPS = 1e-6
N_MIXERS = 4
A_HEADS, A_Q_LORA, A_KV_LORA, A_NOPE, A_ROPE, A_V = 16, 384, 256, 64, 32, 64
B_HEADS = 16
C_HEADS, C_KV_HEADS, C_WINDOW, C_ROT = 16, 4, 128, 16
D_HEADS, D_PREV_CHUNKS, D_REL_CLIP = 16, 8, 128
C_PREV_CHUNKS = C_WINDOW // CHUNK

LANES = 128
V7X_VMEM_LIMIT = 56 * 1024 * 1024

TOKEN_TILE = 512
Q_TILE = 256
KV_BLOCK = 256
A_KV_BLOCK = 512
C_BAND_WIDTH = C_PREV_CHUNKS * CHUNK + Q_TILE
D_BAND_WIDTH = D_PREV_CHUNKS * CHUNK + Q_TILE
A_Q_TILE = 256
A_PAIRS_PER_STEP = 2
A_PV_CHUNK = 512
D_PAIRS_PER_STEP = 4
MASK_VALUE = -1e30
LOG2E = math.log2(math.e)
STICK_DEAD_BITS = 150.0


def _cparams(n_axes):
    return pltpu.CompilerParams(
        dimension_semantics=("parallel",) * n_axes, vmem_limit_bytes=V7X_VMEM_LIMIT)


def _adaln_kernel(c_ref, w_ref, b_ref, o_ref):
    c = c_ref[...]
    a = (c * jax.nn.sigmoid(c)).astype(BF16)
    y = jnp.dot(a, w_ref[0].astype(BF16), preferred_element_type=F32) + b_ref[0]
    o_ref[0, 0] = y


def _adaln(c_all, w_mod, b_mod):
    depth, d, d6 = w_mod.shape
    n = c_all.shape[0]
    return pl.pallas_call(
        _adaln_kernel,
        grid=(depth, d6 // d),
        in_specs=[
            pl.BlockSpec((n, d), lambda i, k: (0, 0)),
            pl.BlockSpec((1, d, d), lambda i, k: (i, 0, k)),
            pl.BlockSpec((1, 1, d), lambda i, k: (i, 0, k)),
        ],
        out_specs=pl.BlockSpec((1, 1, n, d), lambda i, k: (i, k, 0, 0)),
        out_shape=jax.ShapeDtypeStruct((depth, d6 // d, n, d), F32),
        compiler_params=_cparams(2),
    )(c_all, w_mod, b_mod.reshape(depth, 1, d6))


def _cast_kernel(x_ref, o_ref):
    o_ref[...] = x_ref[...].astype(o_ref.dtype)


def _to_bf16(w):
    n, rows, cols = w.shape
    slab = 256
    assert rows % slab == 0
    return pl.pallas_call(
        _cast_kernel,
        grid=(n, rows // slab),
        in_specs=[pl.BlockSpec((1, slab, cols), lambda i, r: (i, r, 0))],
        out_specs=pl.BlockSpec((1, slab, cols), lambda i, r: (i, r, 0)),
        out_shape=jax.ShapeDtypeStruct(w.shape, BF16),
        compiler_params=_cparams(2),
    )(w)


def _token_tiling(bx, sx):
    if sx >= TOKEN_TILE:
        ts = TOKEN_TILE
        while sx % ts:
            ts //= 2
        return 1, ts
    nb = min(bx, TOKEN_TILE // sx)
    while bx % nb:
        nb -= 1
    return nb, sx


def _tok_spec(nb, ts, width):
    return pl.BlockSpec((nb, ts, width), lambda b, s: (b, s, 0))


def _mod_spec(nb, d):
    return pl.BlockSpec((nb, 1, d), lambda b, s: (b, 0, 0))


def _const_spec(shape):
    nd = len(shape)
    return pl.BlockSpec(shape, lambda b, s: (0,) * nd)


def _table_spec(rows):
    return pl.BlockSpec((1, rows, LANES), lambda b, s: (s, 0, 0))


def _rms(x):
    return x * lax.rsqrt(jnp.mean(x * x, axis=-1, keepdims=True) + NORM_EPS)


def _modulated(x_ref, g_ref, shift_ref, scale_ref):
    y = _rms(x_ref[...]) * g_ref[...]
    h = y * (1.0 + scale_ref[...]) + shift_ref[...]
    return h.reshape(-1, h.shape[-1]).astype(BF16)


def _rope_lanes(x, cos_t, sin_up, sin_dn, half):
    return (x * cos_t + pltpu.roll(x, half, 1) * sin_up
            + pltpu.roll(x, LANES - half, 1) * sin_dn)


def _store_tok(ref, lo, val):
    nb, ts = ref.shape[0], ref.shape[1]
    w = val.shape[-1]
    ref[:, :, lo:lo + w] = val.reshape(nb, ts, w).astype(ref.dtype)


def _proj_qkv_kernel(x_ref, g_ref, sh_ref, sc_ref, w_ref, q_ref, kf_ref, vf_ref, kb_ref, vb_ref,
                     *, width, q_scale):
    h = _modulated(x_ref, g_ref, sh_ref, sc_ref)
    q = jnp.dot(h, w_ref[:, 0:width], preferred_element_type=F32)
    _store_tok(q_ref, 0, q * q_scale)
    k = jnp.dot(h, w_ref[:, width:2 * width], preferred_element_type=F32)
    _store_tok(kf_ref, 0, k)
    _store_tok(kb_ref, 0, k)
    v = jnp.dot(h, w_ref[:, 2 * width:3 * width], preferred_element_type=F32)
    _store_tok(vf_ref, 0, v)
    _store_tok(vb_ref, 0, v)


def _proj_qkv(x, g, shift, scale, w_bf16, n_heads, q_scale):
    bx, sx, d = x.shape
    width = n_heads * HEAD_DIM
    nb, ts = _token_tiling(bx, sx)
    out = lambda dt: jax.ShapeDtypeStruct((bx, sx, width), dt)
    return pl.pallas_call(
        functools.partial(_proj_qkv_kernel, width=width, q_scale=q_scale),
        grid=(bx // nb, sx // ts),
        in_specs=[_tok_spec(nb, ts, d), _const_spec((1, d)), _mod_spec(nb, d), _mod_spec(nb, d),
                  _const_spec(w_bf16.shape)],
        out_specs=[_tok_spec(nb, ts, width)] * 5,
        out_shape=[out(BF16), out(F32), out(F32), out(BF16), out(BF16)],
        compiler_params=_cparams(2),
    )(x, g, shift, scale, w_bf16)


def _proj_c_kernel(x_ref, g_ref, sh_ref, sc_ref, w_ref, b_ref, tc_ref, tu_ref, td_ref,
                   q_ref, kf_ref, vf_ref, kb_ref, vb_ref, *, q_width, kv_width, q_scale):
    h = _modulated(x_ref, g_ref, sh_ref, sc_ref)
    qkv = jnp.dot(h, w_ref[...], preferred_element_type=F32) + b_ref[...]
    cos_t, sin_up, sin_dn = tc_ref[0], tu_ref[0], td_ref[0]
    half = C_ROT // 2
    for j in range(q_width // LANES):
        xg = qkv[:, j * LANES:(j + 1) * LANES]
        _store_tok(q_ref, j * LANES, _rope_lanes(xg, cos_t, sin_up, sin_dn, half) * q_scale)
    for j in range(kv_width // LANES):
        lo = q_width + j * LANES
        kg = _rope_lanes(qkv[:, lo:lo + LANES], cos_t, sin_up, sin_dn, half)
        _store_tok(kf_ref, j * LANES, kg)
        _store_tok(kb_ref, j * LANES, kg)
    v = qkv[:, q_width + kv_width:q_width + 2 * kv_width]
    _store_tok(vf_ref, 0, v)
    _store_tok(vb_ref, 0, v)


def _proj_c(x, g, shift, scale, w_bf16, bias, tables):
    bx, sx, d = x.shape
    q_width, kv_width = C_HEADS * HEAD_DIM, C_KV_HEADS * HEAD_DIM
    nb, ts = _token_tiling(bx, sx)
    out = lambda w, dt: jax.ShapeDtypeStruct((bx, sx, w), dt)
    return pl.pallas_call(
        functools.partial(_proj_c_kernel, q_width=q_width, kv_width=kv_width,
                          q_scale=HEAD_DIM ** -0.5 * LOG2E),
        grid=(bx // nb, sx // ts),
        in_specs=[_tok_spec(nb, ts, d), _const_spec((1, d)), _mod_spec(nb, d), _mod_spec(nb, d),
                  _const_spec(w_bf16.shape), _const_spec(bias.shape)] + [_table_spec(nb * ts)] * 3,
        out_specs=[_tok_spec(nb, ts, q_width)] + [_tok_spec(nb, ts, kv_width)] * 4,
        out_shape=[out(q_width, BF16), out(kv_width, F32), out(kv_width, F32),
                   out(kv_width, BF16), out(kv_width, BF16)],
        compiler_params=_cparams(2),
    )(x, g, shift, scale, w_bf16, bias, *tables)


def _proj_a_kernel(x_ref, g_ref, sh_ref, sc_ref, wd_ref, gq_ref, gkv_ref, wq_ref, wqr_ref,
                   qc_ref, qs_ref, kc_ref, ku_ref, kd_ref,
                   q_ref, ckv_ref, kr_ref, *, q_scale):
    h = _modulated(x_ref, g_ref, sh_ref, sc_ref)
    down = jnp.dot(h, wd_ref[...], preferred_element_type=F32)
    cq = (_rms(down[:, :A_Q_LORA]) * gq_ref[...]).astype(BF16)
    ckv = _rms(down[:, A_Q_LORA:A_Q_LORA + A_KV_LORA]) * gkv_ref[...]
    _store_tok(ckv_ref, 0, ckv)
    lo = A_Q_LORA + A_KV_LORA
    half = A_ROPE // 2
    kr = _rope_lanes(down[:, lo:lo + LANES], kc_ref[0], ku_ref[0], kd_ref[0], half)
    _store_tok(kr_ref, 0, kr[:, :A_ROPE])
    q = jnp.dot(cq, wq_ref[...], preferred_element_type=F32)
    q_rot = jnp.dot(cq, wqr_ref[...], preferred_element_type=F32)
    qc, qs = qc_ref[0], qs_ref[0]
    for j in range(A_HEADS):
        lanes = slice(j * LANES, (j + 1) * LANES)
        _store_tok(q_ref, j * LANES, (q[:, lanes] * qc + q_rot[:, lanes] * qs) * q_scale)


def _proj_a(x, g, shift, scale, wd_bf16, g_q, g_kv, wq_bf16, wq_rot_bf16, q_tables, k_tables):
    bx, sx, d = x.shape
    nb, ts = _token_tiling(bx, sx)
    return pl.pallas_call(
        functools.partial(_proj_a_kernel, q_scale=(A_NOPE + A_ROPE) ** -0.5 * LOG2E),
        grid=(bx // nb, sx // ts),
        in_specs=[_tok_spec(nb, ts, d), _const_spec((1, d)), _mod_spec(nb, d), _mod_spec(nb, d),
                  _const_spec(wd_bf16.shape), _const_spec(g_q.shape), _const_spec(g_kv.shape),
                  _const_spec(wq_bf16.shape), _const_spec(wq_rot_bf16.shape)]
                 + [_table_spec(nb * ts)] * 5,
        out_specs=[_tok_spec(nb, ts, A_HEADS * LANES), _tok_spec(nb, ts, A_KV_LORA),
                   _tok_spec(nb, ts, A_ROPE)],
        out_shape=[jax.ShapeDtypeStruct((bx, sx, A_HEADS * LANES), BF16),
                   jax.ShapeDtypeStruct((bx, sx, A_KV_LORA), F32),
                   jax.ShapeDtypeStruct((bx, sx, A_ROPE), F32)],
        compiler_params=_cparams(2),
    )(x, g, shift, scale, wd_bf16, g_q, g_kv, wq_bf16, wq_rot_bf16, *q_tables, *k_tables)


def _expand_a_kernel(ckv_ref, kr_ref, wk_ref, wr_ref, wv_ref, k_ref, v_ref):
    ckv = ckv_ref[...]
    ckv = ckv.reshape(-1, ckv.shape[-1]).astype(BF16)
    kr = kr_ref[...]
    kr = kr.reshape(-1, kr.shape[-1]).astype(BF16)
    k = (jnp.dot(ckv, wk_ref[...], preferred_element_type=F32)
         + jnp.dot(kr, wr_ref[...], preferred_element_type=F32))
    _store_tok(k_ref, 0, k)
    _store_tok(v_ref, 0, jnp.dot(ckv, wv_ref[...], preferred_element_type=F32))


def _expand_a(ckv, kr, wk, wr, wv):
    bx, sx, _ = ckv.shape
    nb, ts = _token_tiling(bx, sx)
    return pl.pallas_call(
        _expand_a_kernel,
        grid=(bx // nb, sx // ts),
        in_specs=[_tok_spec(nb, ts, A_KV_LORA), _tok_spec(nb, ts, A_ROPE),
                  _const_spec(wk.shape), _const_spec(wr.shape), _const_spec(wv.shape)],
        out_specs=[_tok_spec(nb, ts, A_HEADS * LANES), _tok_spec(nb, ts, A_HEADS * A_V)],
        out_shape=[jax.ShapeDtypeStruct((bx, sx, A_HEADS * LANES), BF16),
                   jax.ShapeDtypeStruct((bx, sx, A_HEADS * A_V), BF16)],
        compiler_params=_cparams(2),
    )(ckv, kr, wk, wr, wv)


def _latent_decode_kernel(q_ref, kcat_ref, wabs_ref, wv_ref, o_ref, qc_ref,
                          *, sk_valid, q_off, k_off):
    t = q_ref.shape[1]
    n_heads = wabs_ref.shape[0]
    rows = kcat_ref.shape[1]
    for h in range(n_heads):
        qc_ref[h * t:(h + 1) * t, :] = jnp.dot(
            q_ref[0, :, h * LANES:(h + 1) * LANES], wabs_ref[h],
            preferred_element_type=F32).astype(BF16)
    kcat = kcat_ref[0]
    s = lax.dot_general(qc_ref[...], kcat, (((1,), (1,)), ((), ())),
                        preferred_element_type=F32)
    shift = int(math.log2(CHUNK))
    row = lax.broadcasted_iota(jnp.int32, (n_heads * t, 1), 0)
    qpos = row - row // t * t + q_off
    kidx = lax.broadcasted_iota(jnp.int32, (1, rows), 1)
    vis = (kidx < sk_valid) & (jnp.right_shift(kidx + k_off, shift) <= jnp.right_shift(qpos, shift))
    s = jnp.where(vis, s, MASK_VALUE)
    m = jnp.max(s, axis=1, keepdims=True)
    p = jnp.exp2(s - m)
    l = jnp.sum(p, axis=1, keepdims=True)
    o_lat = (jnp.dot(p.astype(BF16), kcat[:, :A_KV_LORA], preferred_element_type=F32) / l
             ).astype(BF16)
    out = jnp.zeros((t, o_ref.shape[-1]), F32)
    for h in range(n_heads):
        out = out + jnp.dot(o_lat[h * t:(h + 1) * t], wv_ref[h], preferred_element_type=F32)
    o_ref[0] = out.astype(o_ref.dtype)


def _latent_decode(q, kcat, wabs, wv_placed, *, sk_valid, q_off, k_off):
    bx, t, _ = q.shape
    rows, width = kcat.shape[1:]
    n_heads = wabs.shape[0]
    out_w = wv_placed.shape[-1]
    return pl.pallas_call(
        functools.partial(_latent_decode_kernel, sk_valid=sk_valid, q_off=q_off, k_off=k_off),
        grid=(bx,),
        in_specs=[pl.BlockSpec((1, t, n_heads * LANES), lambda b: (b, 0, 0)),
                  pl.BlockSpec((1, rows, width), lambda b: (b, 0, 0)),
                  pl.BlockSpec(wabs.shape, lambda b: (0, 0, 0)),
                  pl.BlockSpec(wv_placed.shape, lambda b: (0, 0, 0))],
        out_specs=pl.BlockSpec((1, t, out_w), lambda b: (b, 0, 0)),
        out_shape=jax.ShapeDtypeStruct((bx, t, out_w), BF16),
        scratch_shapes=[pltpu.VMEM((n_heads * t, width), BF16)],
        compiler_params=_cparams(1),
    )(q, kcat, wabs, wv_placed)


def _block_kernel(x_ref, o_ref, wo_ref, gm_ref, g_ref, sh_ref, sc_ref, gf_ref, wi_ref, wout_ref,
                  *out_refs, hidden, chunk, final_norm):
    gout_ref, y_ref = out_refs if final_norm else (None,) + out_refs
    nb, ts, d = x_ref.shape
    o = o_ref[...].reshape(nb * ts, -1)
    mix = jnp.dot(o, wo_ref[...], preferred_element_type=F32).reshape(nb, ts, d)
    x1 = x_ref[...] + gm_ref[...] * mix
    h = (_rms(x1) * g_ref[...]) * (1.0 + sc_ref[...]) + sh_ref[...]
    h = h.reshape(nb * ts, d).astype(BF16)
    acc = jnp.zeros((nb * ts, d), F32)
    for c in range(hidden // chunk):
        gate = jnp.dot(h, wi_ref[0, :, c * chunk:(c + 1) * chunk], preferred_element_type=F32)
        up = jnp.dot(h, wi_ref[0, :, hidden + c * chunk:hidden + (c + 1) * chunk],
                     preferred_element_type=F32)
        act = (gate * jax.nn.sigmoid(gate) * up).astype(BF16)
        acc = acc + jnp.dot(act, wout_ref[0, c * chunk:(c + 1) * chunk, :],
                            preferred_element_type=F32)
    x2 = x1 + gf_ref[...] * acc.reshape(nb, ts, d)
    y_ref[...] = _rms(x2) * gout_ref[...] if final_norm else x2


def _block(x, o, wo, gate_m, g_ffn, shift_f, scale_f, gate_f, w_in, w_out, layer, g_out=None):
    bx, sx, d = x.shape
    hidden = w_out.shape[1]
    nb, ts = _token_tiling(bx, sx)
    resident = lambda shape: pl.BlockSpec(shape, lambda b, s: (0,) * len(shape),
                                          pipeline_mode=pl.Buffered(1))
    of_layer = lambda w: pl.BlockSpec((1,) + w.shape[1:], lambda b, s: (layer, 0, 0),
                                      pipeline_mode=pl.Buffered(1))
    in_specs = [_tok_spec(nb, ts, d), _tok_spec(nb, ts, o.shape[-1]), resident(wo.shape),
                _mod_spec(nb, d), _const_spec((1, d)), _mod_spec(nb, d), _mod_spec(nb, d),
                _mod_spec(nb, d), of_layer(w_in), of_layer(w_out)]
    args = [x, o, wo, gate_m, g_ffn, shift_f, scale_f, gate_f, w_in, w_out]
    if g_out is not None:
        in_specs.append(_const_spec((1, d)))
        args.append(g_out)
    return pl.pallas_call(
        functools.partial(_block_kernel, hidden=hidden, chunk=256, final_norm=g_out is not None),
        grid=(bx // nb, sx // ts),
        in_specs=in_specs,
        out_specs=_tok_spec(nb, ts, d),
        out_shape=jax.ShapeDtypeStruct((bx, sx, d), F32),
        compiler_params=_cparams(2),
    )(*args)


def _stick_kernel(q_ref, k_ref, v_ref, tri_ref, o_ref, run_ref, acc_ref,
                  *, tq, bk, sk_valid, q_off, k_off):
    qpos0 = pl.program_id(2) * tq + q_off
    kb_hi = (jnp.minimum(qpos0 + (tq - 1) - k_off, sk_valid) + bk - 1) // bk
    f_hi = jnp.minimum(jnp.minimum(qpos0 - k_off, sk_valid) // bk, kb_hi)

    def visible(kb):
        row = lax.broadcasted_iota(jnp.int32, (2 * tq, 1), 0)
        qpos = jnp.where(row < tq, row, row - tq) + qpos0
        ik = kb * bk + lax.broadcasted_iota(jnp.int32, (1, bk), 1)
        return (ik + k_off < qpos) & (ik < sk_valid)

    lane_half = lax.broadcasted_iota(jnp.int32, (1, LANES), 1) // HEAD_DIM
    q_pair = q_ref[0]
    q_both = jnp.concatenate(
        [jnp.where(lane_half == hh, q_pair, jnp.zeros((), BF16)) for hh in range(2)], axis=0)

    def rows(kb):
        return pl.ds(pl.multiple_of(kb * bk, bk), bk)

    def block(kb, masked, run):
        y = lax.dot_general(q_both, k_ref[0, rows(kb), :], (((1,), (1,)), ((), ())),
                            preferred_element_type=F32)
        sp = jnp.maximum(y, 0.0) + jnp.log2(1.0 + jnp.exp2(-jnp.abs(y)))
        if masked:
            vis = visible(kb)
            sp = jnp.where(vis, sp, 0.0)
        suffix = jnp.dot(sp.astype(BF16), tri_ref[...], preferred_element_type=F32)
        a = jnp.exp2(y - suffix - run)
        if masked:
            a = jnp.where(vis, a, 0.0)
        return (jnp.dot(a.astype(BF16), v_ref[0, rows(kb), :], preferred_element_type=F32),
                jnp.sum(sp, axis=1, keepdims=True))

    @pl.when(f_hi > 0)
    def _():
        out_d, sum_d = block(kb_hi - 1, True, 0.0)
        out_f, sum_f = block(f_hi - 1, False, sum_d)
        acc_ref[...] = out_d + out_f
        run_ref[...] = sum_d + sum_f

    @pl.when(f_hi == 0)
    def _():
        out_d, sum_d = block(kb_hi - 1, True, 0.0)
        acc_ref[...] = out_d
        run_ref[...] = sum_d

    def earlier(st):
        out, row_sum = block(f_hi - 1 - st[0], False, run_ref[...])
        acc_ref[...] += out
        run_ref[...] += row_sum
        return st[0] + 1, jnp.min(run_ref[...])

    lax.while_loop(lambda st: (st[0] < f_hi) & (st[1] < STICK_DEAD_BITS), earlier,
                   (jnp.ones((), jnp.int32), jnp.min(run_ref[...])))
    o_ref[0] = jnp.where(lane_half == 0, acc_ref[:tq], acc_ref[tq:]).astype(o_ref.dtype)


def _stick_attention(q, k, v, *, sk_valid, q_off, k_off):
    bx, sq, qw = q.shape
    rows = k.shape[1]
    n_pairs = qw // LANES
    tq = min(Q_TILE, sq)
    bk = KV_BLOCK
    assert sq % tq == 0 and rows % bk == 0 and tq % 8 == 0 and q_off >= k_off >= 0
    for q0 in range(0, sq, tq):
        first, last = q_off + q0 - k_off, min(q_off + q0 + tq - 1 - k_off, sk_valid)
        assert first <= sk_valid and -(-last // bk) - first // bk == 1
    idx = np.arange(bk)
    return pl.pallas_call(
        functools.partial(_stick_kernel, tq=tq, bk=bk, sk_valid=sk_valid, q_off=q_off,
                          k_off=k_off),
        grid=(bx, n_pairs, sq // tq),
        in_specs=[
            pl.BlockSpec((1, tq, LANES), lambda b, p, i: (b, i, p)),
            pl.BlockSpec((1, rows, LANES), lambda b, p, i: (b, 0, p)),
            pl.BlockSpec((1, rows, LANES), lambda b, p, i: (b, 0, p)),
            pl.BlockSpec((bk, bk), lambda b, p, i: (0, 0)),
        ],
        out_specs=pl.BlockSpec((1, tq, LANES), lambda b, p, i: (b, i, p)),
        out_shape=jax.ShapeDtypeStruct((bx, sq, n_pairs * LANES), BF16),
        scratch_shapes=[pltpu.VMEM((2 * tq, 1), F32), pltpu.VMEM((2 * tq, LANES), F32)],
        compiler_params=_cparams(3),
    )(q, k, v, jnp.asarray(idx[:, None] >= idx[None, :], BF16))


def _row_end(qpos, k_off):
    return (qpos // CHUNK + 1) * CHUNK - k_off


def _causal_kernel(q_ref, k_ref, v_ref, o_ref, *, tq, span, n_spans, sk_valid, q_off, k_off,
                   single_tile):
    qpos0 = pl.program_id(2) * tq + q_off
    n_needed = (jnp.minimum(_row_end(qpos0 + tq - 1, k_off), sk_valid) - 1) // span
    shift = int(math.log2(CHUNK))
    lane_half = lax.broadcasted_iota(jnp.int32, (1, LANES), 1) // HEAD_DIM
    nt = (((1,), (1,)), ((), ()))

    for n_full in range(n_spans):
        if single_tile and n_full != (min(_row_end(q_off + tq - 1, k_off), sk_valid) - 1) // span:
            continue

        @pl.when(n_needed == n_full)
        def _(n_full=n_full):
            full, width = n_full * span, (n_full + 1) * span
            qpos = lax.broadcasted_iota(jnp.int32, (tq, 1), 0) + qpos0
            kidx = lax.broadcasted_iota(jnp.int32, (1, span), 1) + full
            vis = (kidx < sk_valid) & (jnp.right_shift(kidx + k_off, shift)
                                       <= jnp.right_shift(qpos, shift))
            outs = []
            for hh in range(2 * (q_ref.shape[-1] // (2 * LANES))):
                lanes = slice(hh * LANES, (hh + 1) * LANES)
                v_lanes = slice(hh // 2 * LANES, (hh // 2 + 1) * LANES)
                q_h = q_ref[0, :, lanes]
                s_tail = lax.dot_general(q_h, k_ref[0, full:width, lanes], nt,
                                         preferred_element_type=F32)
                s_tail = jnp.where(vis, s_tail, MASK_VALUE)
                m = jnp.max(s_tail, axis=1, keepdims=True)
                if n_full:
                    s_full = lax.dot_general(q_h, k_ref[0, 0:full, lanes], nt,
                                             preferred_element_type=F32)
                    m = jnp.maximum(m, jnp.max(s_full, axis=1, keepdims=True))
                p_tail = jnp.exp2(s_tail - m)
                l = jnp.sum(p_tail, axis=1, keepdims=True)
                o = jnp.dot(p_tail.astype(BF16), v_ref[0, full:width, v_lanes],
                            preferred_element_type=F32)
                for c0 in range(0, full, A_PV_CHUNK):
                    p_c = jnp.exp2(s_full[:, c0:c0 + A_PV_CHUNK] - m)
                    l = l + jnp.sum(p_c, axis=1, keepdims=True)
                    o = o + jnp.dot(p_c.astype(BF16), v_ref[0, c0:c0 + A_PV_CHUNK, v_lanes],
                                    preferred_element_type=F32)
                outs.append(o / l)
            for pr in range(len(outs) // 2):
                o_ref[0, :, pr * LANES:(pr + 1) * LANES] = jnp.where(
                    lane_half == 0, outs[2 * pr], outs[2 * pr + 1]).astype(o_ref.dtype)


def _causal_attention(q, k, v, *, span, sk_valid, q_off, k_off):
    bx, sq, qw = q.shape
    rows = k.shape[1]
    pps = A_PAIRS_PER_STEP
    n_pairs = qw // (2 * LANES)
    assert n_pairs % pps == 0
    tq = min(A_Q_TILE, sq)
    assert sq % tq == 0 and rows % span == 0 and q_off % CHUNK == 0 and k_off >= 0
    for q0 in range(0, sq, tq):
        end = min(_row_end(q_off + q0 + tq - 1, k_off), sk_valid)
        first_row_end = min(_row_end(q_off + q0, k_off), sk_valid)
        assert 0 < end <= rows and (end - 1) // span * span <= first_row_end
    return pl.pallas_call(
        functools.partial(_causal_kernel, tq=tq, span=span, n_spans=rows // span,
                          sk_valid=sk_valid, q_off=q_off, k_off=k_off, single_tile=sq == tq),
        grid=(bx, n_pairs // pps, sq // tq),
        in_specs=[
            pl.BlockSpec((1, tq, pps * 2 * LANES), lambda b, p, i: (b, i, p)),
            pl.BlockSpec((1, rows, pps * 2 * LANES), lambda b, p, i: (b, 0, p)),
            pl.BlockSpec((1, rows, pps * LANES), lambda b, p, i: (b, 0, p)),
        ],
        out_specs=pl.BlockSpec((1, tq, pps * LANES), lambda b, p, i: (b, i, p)),
        out_shape=jax.ShapeDtypeStruct((bx, sq, n_pairs * LANES), BF16),
        compiler_params=_cparams(3),
    )(q, k, v)


def _band_kernel(*refs, tq, width, sk_valid, q_off, k_off, front, n_prev, use_sink, per_head_bias):
    refs = list(refs)
    q_ref, k_ref, v_ref = refs[:3]
    rest = refs[3:]
    sink_ref = rest.pop(0) if use_sink else None
    bias_ref, o_ref = rest

    group = q_ref.shape[-1] // LANES
    first_pair = pl.program_id(1) * group
    qpos0 = pl.program_id(2) * tq + q_off
    start = pl.multiple_of((qpos0 // CHUNK - n_prev) * CHUNK - k_off + front, CHUNK)
    band = pl.ds(start, width)
    kidx = lax.broadcasted_iota(jnp.int32, (1, width), 1) + (start - front)
    invalid = jnp.where((kidx >= 0) & (kidx < sk_valid), 0.0, MASK_VALUE)

    lane_half = lax.broadcasted_iota(jnp.int32, (1, LANES), 1) // HEAD_DIM
    shared_kv = k_ref.shape[-1] == LANES
    for j in range(group):
        q_pair = q_ref[0, :, j * LANES:(j + 1) * LANES]
        kv_lanes = slice(0, LANES) if shared_kv else slice(j * LANES, (j + 1) * LANES)
        k_band = k_ref[0, band, kv_lanes]
        v_band = v_ref[0, band, kv_lanes]
        outs = []
        for hh in range(2):
            q_h = jnp.where(lane_half == hh, q_pair, jnp.zeros((), BF16))
            s = lax.dot_general(q_h, k_band, (((1,), (1,)), ((), ())),
                                preferred_element_type=F32)
            s = s + (bias_ref[2 * j + hh] if per_head_bias else bias_ref[...]) + invalid
            m = jnp.max(s, axis=1, keepdims=True)
            if use_sink:
                sink = sink_ref[2 * (first_pair + j) + hh] * LOG2E
                m = jnp.maximum(m, sink)
            p = jnp.exp2(s - m)
            l = jnp.sum(p, axis=1, keepdims=True)
            if use_sink:
                l = l + jnp.exp2(sink - m)
            outs.append(jnp.dot(p.astype(BF16), v_band, preferred_element_type=F32) / l)
        o_ref[0, :, j * LANES:(j + 1) * LANES] = jnp.where(
            lane_half == 0, outs[0], outs[1]).astype(o_ref.dtype)


def _band_mask_tile(rows, width, n_prev):
    r_chunk = np.arange(rows)[:, None] // CHUNK
    c_chunk = np.arange(width)[None, :] // CHUNK - n_prev
    return np.where((c_chunk <= r_chunk) & (c_chunk >= r_chunk - n_prev), 0.0, MASK_VALUE
                    ).astype(np.float32)


def _band_attention(q, k, v, *, width, sk_valid, q_off, k_off, front, n_prev, pairs_per_step,
                    shared_kv, sink=None, bias=None):
    kv_group = pairs_per_step
    kv_lanes = LANES if shared_kv else pairs_per_step * LANES
    bx, sq, qw = q.shape
    rows = k.shape[1]
    n_pairs = qw // LANES
    tq = min(Q_TILE, sq)
    assert sq % tq == 0 and q_off % CHUNK == 0 and (tq % CHUNK == 0 or sq == tq)
    assert ((q_off + sq - tq) // CHUNK - n_prev) * CHUNK - k_off + front + width <= rows
    assert (q_off // CHUNK - n_prev) * CHUNK - k_off + front >= 0
    in_specs = [
        pl.BlockSpec((1, tq, kv_group * LANES), lambda b, g, i: (b, i, g)),
        pl.BlockSpec((1, rows, kv_lanes), lambda b, g, i: (b, 0, g)),
        pl.BlockSpec((1, rows, kv_lanes), lambda b, g, i: (b, 0, g)),
    ]
    args = [q, k, v]
    if sink is not None:
        in_specs.append(pl.BlockSpec(memory_space=pltpu.SMEM))
        args.append(sink)
    per_head_bias = bias.ndim == 3
    if per_head_bias:
        in_specs.append(pl.BlockSpec((2 * kv_group, tq, width), lambda b, g, i: (g, 0, 0)))
    else:
        in_specs.append(pl.BlockSpec((tq, width), lambda b, g, i: (0, 0)))
    args.append(bias)
    return pl.pallas_call(
        functools.partial(_band_kernel, tq=tq, width=width, sk_valid=sk_valid, q_off=q_off,
                          k_off=k_off, front=front, n_prev=n_prev,
                          use_sink=sink is not None, per_head_bias=per_head_bias),
        grid=(bx, n_pairs // kv_group, sq // tq),
        in_specs=in_specs,
        out_specs=pl.BlockSpec((1, tq, kv_group * LANES), lambda b, g, i: (b, i, g)),
        out_shape=jax.ShapeDtypeStruct((bx, sq, n_pairs * LANES), BF16),
        compiler_params=_cparams(3),
    )(*args)


def _band_bias_kernel(e_ref, mask_ref, o_ref, *, width):
    w = e_ref.shape[-1]
    x = jnp.broadcast_to(e_ref[0], (Q_TILE, w))
    toeplitz = pltpu.roll(x, 0, 1, stride=1, stride_axis=0)
    o_ref[0] = toeplitz[:, :width] * LOG2E + mask_ref[...]


def _band_bias(rel_bias):
    heads = rel_bias.shape[0]
    clip = (rel_bias.shape[1] - 1) // 2
    band = D_PREV_CHUNKS * CHUNK
    width = D_BAND_WIDTH
    w = width + Q_TILE
    assert band >= clip and width > band + clip
    top = jnp.broadcast_to(rel_bias[:, -1:], (heads, band - clip + 1))
    mid = jnp.flip(rel_bias[:, :2 * clip], axis=1)
    low = jnp.broadcast_to(rel_bias[:, :1], (heads, width - (band + clip + 1)))
    neg = jnp.broadcast_to(rel_bias[:, -1:], (heads, w - width))
    e = jnp.concatenate([top, mid, low, neg], axis=1)[:, None, :]
    return pl.pallas_call(
        functools.partial(_band_bias_kernel, width=width),
        grid=(heads,),
        in_specs=[pl.BlockSpec((1, 1, w), lambda h: (h, 0, 0)),
                  pl.BlockSpec((Q_TILE, width), lambda h: (0, 0))],
        out_specs=pl.BlockSpec((1, Q_TILE, width), lambda h: (h, 0, 0)),
        out_shape=jax.ShapeDtypeStruct((heads, Q_TILE, width), F32),
        compiler_params=_cparams(1),
    )(e, jnp.asarray(_band_mask_tile(Q_TILE, width, D_PREV_CHUNKS)))


def _rope_tables(pos, n_rot, lane_offsets, rows_repeat=1):
    half = n_rot // 2
    inv = ROPE_THETA ** (-jnp.arange(half, dtype=F32) * 2.0 / n_rot)
    ang = pos.astype(F32)[:, None] * inv[None, :]
    cos, sin = jnp.cos(ang), jnp.sin(ang)
    n = pos.shape[0]
    cos_t = jnp.ones((n, LANES), F32)
    sin_up = jnp.zeros((n, LANES), F32)
    sin_dn = jnp.zeros((n, LANES), F32)
    for o in lane_offsets:
        cos_t = cos_t.at[:, o:o + half].set(cos).at[:, o + half:o + n_rot].set(cos)
        sin_dn = sin_dn.at[:, o:o + half].set(-sin)
        sin_up = sin_up.at[:, o + half:o + n_rot].set(sin)
    def tiles(t):
        if rows_repeat > 1:
            return jnp.tile(t, (rows_repeat, 1))[None]
        rows = min(n, TOKEN_TILE)
        return t.reshape(n // rows, rows, LANES)
    return tiles(cos_t), tiles(sin_up), tiles(sin_dn)


def _tables_for(bx, sx, pos, n_rot, lane_offsets):
    nb, ts = _token_tiling(bx, sx)
    return _rope_tables(pos, n_rot, lane_offsets, rows_repeat=nb if nb > 1 else 1)


def _c_head_order():
    rep = C_HEADS // C_KV_HEADS
    order = []
    for p in range(C_HEADS // 2):
        g2, i = divmod(p, rep)
        order += [rep * (2 * g2) + i, rep * (2 * g2 + 1) + i]
    return np.asarray(order)


def _pad_rows(a, rows):
    return jnp.pad(a, ((0, 0), (0, rows - a.shape[1]), (0, 0)))


def _front_pad(a, rows):
    return jnp.pad(a, ((0, 0), (rows, 0), (0, 0)))


def _with_cache(cache, new, dtype, block=KV_BLOCK):
    full = jnp.concatenate([cache.reshape(cache.shape[0], cache.shape[1], -1).astype(dtype),
                            new.astype(dtype)], axis=1)
    rows = -(-full.shape[1] // block) * block
    return _pad_rows(full, rows), full.shape[1]


def kernel(x_prompt, x_sample, c_prompt, c_sample, cache_a_ckv, cache_a_krope, cache_b_k, cache_b_v,
           cache_c_k, cache_c_v, cache_d_k, cache_d_v, w_mod, b_mod, g_mix, g_ffn, w_ffn_in, w_ffn_out,
           w_a_down, g_a_q, g_a_kv, w_a_uq, w_a_uk, w_a_uv, w_a_o, w_b_qkv, w_b_o,
           w_c_qkv, b_c_qkv, sink_c, w_c_o, w_d_qkv, rel_bias_d, w_d_o, g_final):
    bp, sp, d = x_prompt.shape
    bs, t, _ = x_sample.shape
    depth = w_mod.shape[0]
    past = cache_a_ckv.shape[2]
    pos_p = jnp.arange(sp)
    pos_s = past + jnp.arange(t)

    mods = _adaln(jnp.concatenate([c_prompt, c_sample], axis=0), w_mod, b_mod)
    w_in_all, w_out_all = _to_bf16(w_ffn_in), _to_bf16(w_ffn_out)

    def mod(i, k):
        m = mods[i, k][:, None, :]
        return m[:bp], m[bp:]

    xp, xs = x_prompt, x_sample
    states = [[] for _ in range(N_MIXERS)]
    for i in range(depth):
        m, j = i % N_MIXERS, i // N_MIXERS
        (sh_p, sh_s), (sc_p, sc_s), (gm_p, gm_s) = mod(i, 0), mod(i, 1), mod(i, 2)
        (shf_p, shf_s), (scf_p, scf_s), (gf_p, gf_s) = mod(i, 3), mod(i, 4), mod(i, 5)
        g_m = g_mix[i][None, :]
        if m == 0:
            n_down = A_Q_LORA + A_KV_LORA + A_ROPE
            wd = jnp.pad(w_a_down[j], ((0, 0), (0, A_Q_LORA + A_KV_LORA + LANES - n_down))).astype(BF16)
            wq = w_a_uq[j].reshape(A_Q_LORA, A_HEADS, A_NOPE + A_ROPE)
            wq = jnp.pad(wq, ((0, 0), (0, 0), (0, LANES - A_NOPE - A_ROPE)))
            x1 = wq[:, :, A_NOPE:A_NOPE + A_ROPE // 2]
            x2 = wq[:, :, A_NOPE + A_ROPE // 2:A_NOPE + A_ROPE]
            wq_rot = jnp.zeros_like(wq).at[:, :, A_NOPE:A_NOPE + A_ROPE].set(
                jnp.concatenate([-x2, x1], axis=-1))
            wq = wq.reshape(A_Q_LORA, A_HEADS * LANES).astype(BF16)
            wq_rot = wq_rot.reshape(A_Q_LORA, A_HEADS * LANES).astype(BF16)
            wk = jnp.pad(w_a_uk[j], ((0, 0), (0, 0), (0, LANES - A_NOPE)))
            wk = wk.reshape(A_KV_LORA, A_HEADS * LANES).astype(BF16)
            place = jnp.zeros((A_ROPE, A_HEADS, LANES), F32)
            place = place.at[jnp.arange(A_ROPE), :, A_NOPE + jnp.arange(A_ROPE)].set(1.0)
            wr = place.reshape(A_ROPE, A_HEADS * LANES).astype(BF16)
            wv = w_a_uv[j].reshape(A_KV_LORA, A_HEADS * A_V).astype(BF16)
            g_q, g_kv = g_a_q[j][None, :], g_a_kv[j][None, :]

            def project(x, sh, sc, pos):
                q_cos, q_up, q_dn = _tables_for(x.shape[0], x.shape[1], pos, A_ROPE, [A_NOPE])
                kt = _tables_for(x.shape[0], x.shape[1], pos, A_ROPE, [0])
                return _proj_a(x, g_m, sh, sc, wd, g_q, g_kv, wq, wq_rot, (q_cos, q_up - q_dn), kt)

            q_p, ckv_p, kr_p = project(xp, sh_p, sc_p, pos_p)
            q_s, ckv_s, kr_s = project(xs, sh_s, sc_s, pos_s)
            k_p, v_p = _expand_a(ckv_p, kr_p, wk, wr, wv)
            o_p = _causal_attention(q_p, k_p, v_p, span=min(A_KV_BLOCK, sp), sk_valid=sp,
                                    q_off=0, k_off=0)
            n_all = past + t
            rows = -(-n_all // LANES) * LANES
            kcat = jnp.concatenate(
                [jnp.concatenate([cache_a_ckv[j], ckv_s], axis=1),
                 jnp.concatenate([cache_a_krope[j], kr_s], axis=1)], axis=-1).astype(BF16)
            kcat = jnp.pad(kcat, ((0, 0), (0, rows - n_all),
                                  (0, A_KV_LORA + LANES - kcat.shape[-1])))
            wabs = jnp.zeros((A_HEADS, LANES, A_KV_LORA + LANES), F32)
            wabs = wabs.at[:, :A_NOPE, :A_KV_LORA].set(w_a_uk[j].transpose(1, 2, 0))
            wabs = wabs.at[:, A_NOPE + jnp.arange(A_ROPE), A_KV_LORA + jnp.arange(A_ROPE)].set(1.0)
            wv_placed = jnp.zeros((A_HEADS, A_KV_LORA, A_HEADS, A_V), F32)
            wv_placed = wv_placed.at[jnp.arange(A_HEADS), :, jnp.arange(A_HEADS), :].set(
                w_a_uv[j].transpose(1, 0, 2))
            wv_placed = wv_placed.reshape(A_HEADS, A_KV_LORA, A_HEADS * A_V)
            o_s = _latent_decode(q_s, kcat, wabs.astype(BF16), wv_placed.astype(BF16),
                                 sk_valid=n_all, q_off=past, k_off=0)
            wo = w_a_o[j].astype(BF16)
            states[0].append((ckv_p, kr_p, ckv_s, kr_s))
        elif m == 1 or m == 3:
            heads = B_HEADS if m == 1 else D_HEADS
            w_qkv = (w_b_qkv if m == 1 else w_d_qkv)[j].astype(BF16)
            cache_k, cache_v = (cache_b_k, cache_b_v) if m == 1 else (cache_d_k, cache_d_v)
            q_scale = HEAD_DIM ** -0.5 * LOG2E
            q_p, kf_p, vf_p, kb_p, vb_p = _proj_qkv(xp, g_m, sh_p, sc_p, w_qkv, heads, q_scale)
            q_s, kf_s, vf_s, kb_s, vb_s = _proj_qkv(xs, g_m, sh_s, sc_s, w_qkv, heads, q_scale)
            lc = cache_k.shape[2]
            if m == 1:
                o_p = _stick_attention(q_p, kb_p, vb_p, sk_valid=sp, q_off=0, k_off=0)
                k_s, n_all = _with_cache(cache_k[j], kb_s, BF16, KV_BLOCK)
                v_s, _ = _with_cache(cache_v[j], vb_s, BF16, KV_BLOCK)
                o_s = _stick_attention(q_s, k_s, v_s, sk_valid=n_all, q_off=past,
                                       k_off=past - lc)
                wo = w_b_o[j].astype(BF16)
                states[1].append((kf_p, vf_p, kf_s, vf_s))
            else:
                bias = _band_bias(rel_bias_d[j])
                k_s, n_all = _with_cache(cache_k[j], kb_s, BF16, D_BAND_WIDTH)
                v_s, _ = _with_cache(cache_v[j], vb_s, BF16, D_BAND_WIDTH)
                front = D_PREV_CHUNKS * CHUNK
                o_p = _band_attention(q_p, _front_pad(kb_p, front), _front_pad(vb_p, front),
                                      width=D_BAND_WIDTH, sk_valid=sp, q_off=0, k_off=0,
                                      front=front, n_prev=D_PREV_CHUNKS, bias=bias,
                                      pairs_per_step=D_PAIRS_PER_STEP, shared_kv=False)
                o_s = _band_attention(q_s, k_s, v_s, width=D_BAND_WIDTH, sk_valid=n_all,
                                      q_off=past, k_off=past - lc, front=0,
                                      n_prev=D_PREV_CHUNKS, bias=bias[:, :t, :],
                                      pairs_per_step=D_PAIRS_PER_STEP, shared_kv=False)
                wo = w_d_o[j].astype(BF16)
                keep = min(D_PREV_CHUNKS * CHUNK, sp)
                k_roll = jnp.concatenate([cache_k[j].reshape(bs, lc, -1), kf_s], axis=1)[:, t:]
                v_roll = jnp.concatenate([cache_v[j].reshape(bs, lc, -1), vf_s], axis=1)[:, t:]
                states[3].append((kf_p[:, sp - keep:], vf_p[:, sp - keep:], k_roll, v_roll))
        else:
            order = _c_head_order()
            q_width, kv_width = C_HEADS * HEAD_DIM, C_KV_HEADS * HEAD_DIM
            col = np.concatenate([(order[:, None] * HEAD_DIM + np.arange(HEAD_DIM)).reshape(-1),
                                  np.arange(q_width, q_width + 2 * kv_width)])
            w_qkv = w_c_qkv[j][:, col].astype(BF16)
            b_qkv = b_c_qkv[j][col][None, :]
            sink = sink_c[j][order]
            wo = w_c_o[j].reshape(C_HEADS, HEAD_DIM, d)[order].reshape(q_width, d).astype(BF16)
            lanes = [0, HEAD_DIM]

            def project(x, sh, sc, pos):
                tb = _tables_for(x.shape[0], x.shape[1], pos, C_ROT, lanes)
                return _proj_c(x, g_m, sh, sc, w_qkv, b_qkv, tb)

            q_p, kf_p, vf_p, kb_p, vb_p = project(xp, sh_p, sc_p, pos_p)
            q_s, kf_s, vf_s, kb_s, vb_s = project(xs, sh_s, sc_s, pos_s)
            lc = cache_c_k.shape[2]
            k_s, n_all = _with_cache(cache_c_k[j], kb_s, BF16, C_BAND_WIDTH)
            v_s, _ = _with_cache(cache_c_v[j], vb_s, BF16, C_BAND_WIDTH)
            group = C_HEADS // C_KV_HEADS
            front = C_PREV_CHUNKS * CHUNK
            band_mask = jnp.asarray(_band_mask_tile(Q_TILE, C_BAND_WIDTH, C_PREV_CHUNKS))
            o_p = _band_attention(q_p, _front_pad(kb_p, front), _front_pad(vb_p, front),
                                  width=C_BAND_WIDTH, sk_valid=sp, q_off=0, k_off=0, front=front,
                                  n_prev=C_PREV_CHUNKS, pairs_per_step=group, shared_kv=True, sink=sink,
                                  bias=band_mask)
            o_s = _band_attention(q_s, k_s, v_s, width=C_BAND_WIDTH, sk_valid=n_all, q_off=past,
                                  k_off=past - lc, front=0, n_prev=C_PREV_CHUNKS, pairs_per_step=group, shared_kv=True,
                                  sink=sink, bias=band_mask[:t])
            keep = min(C_PREV_CHUNKS * CHUNK, sp)
            k_roll = jnp.concatenate([cache_c_k[j].reshape(bs, lc, -1), kf_s], axis=1)[:, t:]
            v_roll = jnp.concatenate([cache_c_v[j].reshape(bs, lc, -1), vf_s], axis=1)[:, t:]
            states[2].append((kf_p[:, sp - keep:], vf_p[:, sp - keep:], k_roll, v_roll))

        g_f = g_ffn[i][None, :]
        g_out = g_final[None, :] if i == depth - 1 else None
        xp = _block(xp, o_p, wo, gm_p, g_f, shf_p, scf_p, gf_p, w_in_all, w_out_all, i, g_out)
        xs = _block(xs, o_s, wo, gm_s, g_f, shf_s, scf_s, gf_s, w_in_all, w_out_all, i, g_out)

    y_p, y_s = xp, xs

    def stacked(entries, n_heads=None):
        outs = []
        for parts in zip(*entries):
            a = jnp.stack(parts, axis=0)
            if n_heads is not None:
                a = a.reshape(a.shape[:3] + (n_heads, HEAD_DIM))
            outs.append(a)
        return tuple(outs)

    return ((y_p, y_s) + stacked(states[0]) + stacked(states[1], B_HEADS)
            + stacked(states[2], C_KV_HEADS) + stacked(states[3], D_HEADS))
```

```python
import functools
import math

import numpy as np
import jax
import jax.numpy as jnp
from jax import lax
from jax.experimental import pallas as pl
from jax.experimental.pallas import tpu as pltpu

F32 = jnp.float32
BF16 = jnp.bfloat16

CHUNK = 64
HEAD_DIM = 64
ROPE_THETA = 500000.0
NORM_EPS = 1e-6
N_MIXERS = 4
A_HEADS, A_Q_LORA, A_KV_LORA, A_NOPE, A_ROPE, A_V = 16, 384, 256, 64, 32, 64
B_HEADS = 16
C_HEADS, C_KV_HEADS, C_WINDOW, C_ROT = 16, 4, 128, 16
D_HEADS, D_PREV_CHUNKS, D_REL_CLIP = 16, 8, 128
C_PREV_CHUNKS = C_WINDOW // CHUNK

LANES = 128
V7X_VMEM_LIMIT = 56 * 1024 * 1024

TOKEN_TILE = 512
Q_TILE = 256
KV_BLOCK = 256
A_KV_BLOCK = 512
C_BAND_WIDTH = C_PREV_CHUNKS * CHUNK + Q_TILE
D_BAND_WIDTH = D_PREV_CHUNKS * CHUNK + Q_TILE
A_Q_TILE = 256
B_PAIRS_PER_STEP = 4
A_PAIRS_PER_STEP = 2
A_PV_CHUNK = 512
D_PAIRS_PER_STEP = 4
MASK_VALUE = -1e30
LOG2E = math.log2(math.e)
STICK_DEAD_BITS = 150.0


def _cparams(n_axes):
    return pltpu.CompilerParams(
        dimension_semantics=("parallel",) * n_axes, vmem_limit_bytes=V7X_VMEM_LIMIT)


def _adaln_kernel(c_ref, w_ref, b_ref, o_ref):
    c = c_ref[...]
    a = (c * jax.nn.sigmoid(c)).astype(BF16)
    y = jnp.dot(a, w_ref[0].astype(BF16), preferred_element_type=F32) + b_ref[0]
    o_ref[0, 0] = y


def _adaln(c_all, w_mod, b_mod):
    depth, d, d6 = w_mod.shape
    n = c_all.shape[0]
    return pl.pallas_call(
        _adaln_kernel,
        grid=(depth, d6 // d),
        in_specs=[
            pl.BlockSpec((n, d), lambda i, k: (0, 0)),
            pl.BlockSpec((1, d, d), lambda i, k: (i, 0, k)),
            pl.BlockSpec((1, 1, d), lambda i, k: (i, 0, k)),
        ],
        out_specs=pl.BlockSpec((1, 1, n, d), lambda i, k: (i, k, 0, 0)),
        out_shape=jax.ShapeDtypeStruct((depth, d6 // d, n, d), F32),
        compiler_params=_cparams(2),
    )(c_all, w_mod, b_mod.reshape(depth, 1, d6))


def _cast_kernel(x_ref, o_ref):
    o_ref[...] = x_ref[...].astype(o_ref.dtype)


def _to_bf16(w):
    n, rows, cols = w.shape
    slab = 256
    assert rows % slab == 0
    return pl.pallas_call(
        _cast_kernel,
        grid=(n, rows // slab),
        in_specs=[pl.BlockSpec((1, slab, cols), lambda i, r: (i, r, 0))],
        out_specs=pl.BlockSpec((1, slab, cols), lambda i, r: (i, r, 0)),
        out_shape=jax.ShapeDtypeStruct(w.shape, BF16),
        compiler_params=_cparams(2),
    )(w)


def _token_tiling(bx, sx):
    if sx >= TOKEN_TILE:
        ts = TOKEN_TILE
        while sx % ts:
            ts //= 2
        return 1, ts
    nb = min(bx, TOKEN_TILE // sx)
    while bx % nb:
        nb -= 1
    return nb, sx


def _tok_spec(nb, ts, width):
    return pl.BlockSpec((nb, ts, width), lambda b, s: (b, s, 0))


def _mod_spec(nb, d):
    return pl.BlockSpec((nb, 1, d), lambda b, s: (b, 0, 0))


def _const_spec(shape):
    nd = len(shape)
    return pl.BlockSpec(shape, lambda b, s: (0,) * nd)


def _table_spec(rows):
    return pl.BlockSpec((1, rows, LANES), lambda b, s: (s, 0, 0))


def _rms(x):
    return x * lax.rsqrt(jnp.mean(x * x, axis=-1, keepdims=True) + NORM_EPS)


def _modulated(x_ref, g_ref, shift_ref, scale_ref):
    y = _rms(x_ref[...]) * g_ref[...]
    h = y * (1.0 + scale_ref[...]) + shift_ref[...]
    return h.reshape(-1, h.shape[-1]).astype(BF16)


def _rope_lanes(x, cos_t, sin_up, sin_dn, half):
    return (x * cos_t + pltpu.roll(x, half, 1) * sin_up
            + pltpu.roll(x, LANES - half, 1) * sin_dn)


def _store_tok(ref, lo, val):
    nb, ts = ref.shape[0], ref.shape[1]
    w = val.shape[-1]
    ref[:, :, lo:lo + w] = val.reshape(nb, ts, w).astype(ref.dtype)


def _proj_qkv_kernel(x_ref, g_ref, sh_ref, sc_ref, w_ref, q_ref, kf_ref, vf_ref, kb_ref, vb_ref,
                     *, width, q_scale):
    h = _modulated(x_ref, g_ref, sh_ref, sc_ref)
    q = jnp.dot(h, w_ref[:, 0:width], preferred_element_type=F32)
    _store_tok(q_ref, 0, q * q_scale)
    k = jnp.dot(h, w_ref[:, width:2 * width], preferred_element_type=F32)
    _store_tok(kf_ref, 0, k)
    _store_tok(kb_ref, 0, k)
    v = jnp.dot(h, w_ref[:, 2 * width:3 * width], preferred_element_type=F32)
    _store_tok(vf_ref, 0, v)
    _store_tok(vb_ref, 0, v)


def _proj_qkv(x, g, shift, scale, w_bf16, n_heads, q_scale):
    bx, sx, d = x.shape
    width = n_heads * HEAD_DIM
    nb, ts = _token_tiling(bx, sx)
    out = lambda dt: jax.ShapeDtypeStruct((bx, sx, width), dt)
    return pl.pallas_call(
        functools.partial(_proj_qkv_kernel, width=width, q_scale=q_scale),
        grid=(bx // nb, sx // ts),
        in_specs=[_tok_spec(nb, ts, d), _const_spec((1, d)), _mod_spec(nb, d), _mod_spec(nb, d),
                  _const_spec(w_bf16.shape)],
        out_specs=[_tok_spec(nb, ts, width)] * 5,
        out_shape=[out(BF16), out(F32), out(F32), out(BF16), out(BF16)],
        compiler_params=_cparams(2),
    )(x, g, shift, scale, w_bf16)


def _proj_c_kernel(x_ref, g_ref, sh_ref, sc_ref, w_ref, b_ref, tc_ref, tu_ref, td_ref,
                   q_ref, kf_ref, vf_ref, kb_ref, vb_ref, *, q_width, kv_width, q_scale):
    h = _modulated(x_ref, g_ref, sh_ref, sc_ref)
    qkv = jnp.dot(h, w_ref[...], preferred_element_type=F32) + b_ref[...]
    cos_t, sin_up, sin_dn = tc_ref[0], tu_ref[0], td_ref[0]
    half = C_ROT // 2
    for j in range(q_width // LANES):
        xg = qkv[:, j * LANES:(j + 1) * LANES]
        _store_tok(q_ref, j * LANES, _rope_lanes(xg, cos_t, sin_up, sin_dn, half) * q_scale)
    for j in range(kv_width // LANES):
        lo = q_width + j * LANES
        kg = _rope_lanes(qkv[:, lo:lo + LANES], cos_t, sin_up, sin_dn, half)
        _store_tok(kf_ref, j * LANES, kg)
        _store_tok(kb_ref, j * LANES, kg)
    v = qkv[:, q_width + kv_width:q_width + 2 * kv_width]
    _store_tok(vf_ref, 0, v)
    _store_tok(vb_ref, 0, v)


def _proj_c(x, g, shift, scale, w_bf16, bias, tables):
    bx, sx, d = x.shape
    q_width, kv_width = C_HEADS * HEAD_DIM, C_KV_HEADS * HEAD_DIM
    nb, ts = _token_tiling(bx, sx)
    out = lambda w, dt: jax.ShapeDtypeStruct((bx, sx, w), dt)
    return pl.pallas_call(
        functools.partial(_proj_c_kernel, q_width=q_width, kv_width=kv_width,
                          q_scale=HEAD_DIM ** -0.5 * LOG2E),
        grid=(bx // nb, sx // ts),
        in_specs=[_tok_spec(nb, ts, d), _const_spec((1, d)), _mod_spec(nb, d), _mod_spec(nb, d),
                  _const_spec(w_bf16.shape), _const_spec(bias.shape)] + [_table_spec(nb * ts)] * 3,
        out_specs=[_tok_spec(nb, ts, q_width)] + [_tok_spec(nb, ts, kv_width)] * 4,
        out_shape=[out(q_width, BF16), out(kv_width, F32), out(kv_width, F32),
                   out(kv_width, BF16), out(kv_width, BF16)],
        compiler_params=_cparams(2),
    )(x, g, shift, scale, w_bf16, bias, *tables)


def _proj_a_kernel(x_ref, g_ref, sh_ref, sc_ref, wd_ref, gq_ref, gkv_ref, wq_ref, wqr_ref,
                   qc_ref, qs_ref, kc_ref, ku_ref, kd_ref,
                   q_ref, ckv_ref, kr_ref, *, q_scale):
    h = _modulated(x_ref, g_ref, sh_ref, sc_ref)
    down = jnp.dot(h, wd_ref[...], preferred_element_type=F32)
    cq = (_rms(down[:, :A_Q_LORA]) * gq_ref[...]).astype(BF16)
    ckv = _rms(down[:, A_Q_LORA:A_Q_LORA + A_KV_LORA]) * gkv_ref[...]
    _store_tok(ckv_ref, 0, ckv)
    lo = A_Q_LORA + A_KV_LORA
    half = A_ROPE // 2
    kr = _rope_lanes(down[:, lo:lo + LANES], kc_ref[0], ku_ref[0], kd_ref[0], half)
    _store_tok(kr_ref, 0, kr[:, :A_ROPE])
    q = jnp.dot(cq, wq_ref[...], preferred_element_type=F32)
    q_rot = jnp.dot(cq, wqr_ref[...], preferred_element_type=F32)
    qc, qs = qc_ref[0], qs_ref[0]
    for j in range(A_HEADS):
        lanes = slice(j * LANES, (j + 1) * LANES)
        _store_tok(q_ref, j * LANES, (q[:, lanes] * qc + q_rot[:, lanes] * qs) * q_scale)


def _proj_a(x, g, shift, scale, wd_bf16, g_q, g_kv, wq_bf16, wq_rot_bf16, q_tables, k_tables):
    bx, sx, d = x.shape
    nb, ts = _token_tiling(bx, sx)
    return pl.pallas_call(
        functools.partial(_proj_a_kernel, q_scale=(A_NOPE + A_ROPE) ** -0.5 * LOG2E),
        grid=(bx // nb, sx // ts),
        in_specs=[_tok_spec(nb, ts, d), _const_spec((1, d)), _mod_spec(nb, d), _mod_spec(nb, d),
                  _const_spec(wd_bf16.shape), _const_spec(g_q.shape), _const_spec(g_kv.shape),
                  _const_spec(wq_bf16.shape), _const_spec(wq_rot_bf16.shape)]
                 + [_table_spec(nb * ts)] * 5,
        out_specs=[_tok_spec(nb, ts, A_HEADS * LANES), _tok_spec(nb, ts, A_KV_LORA),
                   _tok_spec(nb, ts, A_ROPE)],
        out_shape=[jax.ShapeDtypeStruct((bx, sx, A_HEADS * LANES), BF16),
                   jax.ShapeDtypeStruct((bx, sx, A_KV_LORA), F32),
                   jax.ShapeDtypeStruct((bx, sx, A_ROPE), F32)],
        compiler_params=_cparams(2),
    )(x, g, shift, scale, wd_bf16, g_q, g_kv, wq_bf16, wq_rot_bf16, *q_tables, *k_tables)


def _expand_a_kernel(ckv_ref, kr_ref, wk_ref, wr_ref, wv_ref, k_ref, v_ref):
    ckv = ckv_ref[...]
    ckv = ckv.reshape(-1, ckv.shape[-1]).astype(BF16)
    kr = kr_ref[...]
    kr = kr.reshape(-1, kr.shape[-1]).astype(BF16)
    k = (jnp.dot(ckv, wk_ref[...], preferred_element_type=F32)
         + jnp.dot(kr, wr_ref[...], preferred_element_type=F32))
    _store_tok(k_ref, 0, k)
    _store_tok(v_ref, 0, jnp.dot(ckv, wv_ref[...], preferred_element_type=F32))


def _expand_a(ckv, kr, wk, wr, wv):
    bx, sx, _ = ckv.shape
    nb, ts = _token_tiling(bx, sx)
    return pl.pallas_call(
        _expand_a_kernel,
        grid=(bx // nb, sx // ts),
        in_specs=[_tok_spec(nb, ts, A_KV_LORA), _tok_spec(nb, ts, A_ROPE),
                  _const_spec(wk.shape), _const_spec(wr.shape), _const_spec(wv.shape)],
        out_specs=[_tok_spec(nb, ts, A_HEADS * LANES), _tok_spec(nb, ts, A_HEADS * A_V)],
        out_shape=[jax.ShapeDtypeStruct((bx, sx, A_HEADS * LANES), BF16),
                   jax.ShapeDtypeStruct((bx, sx, A_HEADS * A_V), BF16)],
        compiler_params=_cparams(2),
    )(ckv, kr, wk, wr, wv)


def _latent_decode_kernel(q_ref, kcat_ref, wabs_ref, wv_ref, o_ref, qc_ref,
                          *, sk_valid, q_off, k_off):
    t = q_ref.shape[1]
    n_heads = wabs_ref.shape[0]
    rows = kcat_ref.shape[1]
    for h in range(n_heads):
        qc_ref[h * t:(h + 1) * t, :] = jnp.dot(
            q_ref[0, :, h * LANES:(h + 1) * LANES], wabs_ref[h],
            preferred_element_type=F32).astype(BF16)
    kcat = kcat_ref[0]
    s = lax.dot_general(qc_ref[...], kcat, (((1,), (1,)), ((), ())),
                        preferred_element_type=F32)
    shift = int(math.log2(CHUNK))
    row = lax.broadcasted_iota(jnp.int32, (n_heads * t, 1), 0)
    qpos = row - row // t * t + q_off
    kidx = lax.broadcasted_iota(jnp.int32, (1, rows), 1)
    vis = (kidx < sk_valid) & (jnp.right_shift(kidx + k_off, shift) <= jnp.right_shift(qpos, shift))
    s = jnp.where(vis, s, MASK_VALUE)
    m = jnp.max(s, axis=1, keepdims=True)
    p = jnp.exp2(s - m)
    l = jnp.sum(p, axis=1, keepdims=True)
    o_lat = (jnp.dot(p.astype(BF16), kcat[:, :A_KV_LORA], preferred_element_type=F32) / l
             ).astype(BF16)
    out = jnp.zeros((t, o_ref.shape[-1]), F32)
    for h in range(n_heads):
        out = out + jnp.dot(o_lat[h * t:(h + 1) * t], wv_ref[h], preferred_element_type=F32)
    o_ref[0] = out.astype(o_ref.dtype)


def _latent_decode(q, kcat, wabs, wv_placed, *, sk_valid, q_off, k_off):
    bx, t, _ = q.shape
    rows, width = kcat.shape[1:]
    n_heads = wabs.shape[0]
    out_w = wv_placed.shape[-1]
    return pl.pallas_call(
        functools.partial(_latent_decode_kernel, sk_valid=sk_valid, q_off=q_off, k_off=k_off),
        grid=(bx,),
        in_specs=[pl.BlockSpec((1, t, n_heads * LANES), lambda b: (b, 0, 0)),
                  pl.BlockSpec((1, rows, width), lambda b: (b, 0, 0)),
                  pl.BlockSpec(wabs.shape, lambda b: (0, 0, 0)),
                  pl.BlockSpec(wv_placed.shape, lambda b: (0, 0, 0))],
        out_specs=pl.BlockSpec((1, t, out_w), lambda b: (b, 0, 0)),
        out_shape=jax.ShapeDtypeStruct((bx, t, out_w), BF16),
        scratch_shapes=[pltpu.VMEM((n_heads * t, width), BF16)],
        compiler_params=_cparams(1),
    )(q, kcat, wabs, wv_placed)


def _block_kernel(x_ref, o_ref, wo_ref, gm_ref, g_ref, sh_ref, sc_ref, gf_ref, wi_ref, wout_ref,
                  *out_refs, hidden, chunk, final_norm):
    gout_ref, y_ref = out_refs if final_norm else (None,) + out_refs
    nb, ts, d = x_ref.shape
    o = o_ref[...].reshape(nb * ts, -1)
    mix = jnp.dot(o, wo_ref[...], preferred_element_type=F32).reshape(nb, ts, d)
    x1 = x_ref[...] + gm_ref[...] * mix
    h = (_rms(x1) * g_ref[...]) * (1.0 + sc_ref[...]) + sh_ref[...]
    h = h.reshape(nb * ts, d).astype(BF16)
    acc = jnp.zeros((nb * ts, d), F32)
    for c in range(hidden // chunk):
        gate = jnp.dot(h, wi_ref[0, :, c * chunk:(c + 1) * chunk], preferred_element_type=F32)
        up = jnp.dot(h, wi_ref[0, :, hidden + c * chunk:hidden + (c + 1) * chunk],
                     preferred_element_type=F32)
        act = (gate * jax.nn.sigmoid(gate) * up).astype(BF16)
        acc = acc + jnp.dot(act, wout_ref[0, c * chunk:(c + 1) * chunk, :],
                            preferred_element_type=F32)
    x2 = x1 + gf_ref[...] * acc.reshape(nb, ts, d)
    y_ref[...] = _rms(x2) * gout_ref[...] if final_norm else x2


def _block(x, o, wo, gate_m, g_ffn, shift_f, scale_f, gate_f, w_in, w_out, layer, g_out=None):
    bx, sx, d = x.shape
    hidden = w_out.shape[1]
    nb, ts = _token_tiling(bx, sx)
    resident = lambda shape: pl.BlockSpec(shape, lambda b, s: (0,) * len(shape),
                                          pipeline_mode=pl.Buffered(1))
    of_layer = lambda w: pl.BlockSpec((1,) + w.shape[1:], lambda b, s: (layer, 0, 0),
                                      pipeline_mode=pl.Buffered(1))
    in_specs = [_tok_spec(nb, ts, d), _tok_spec(nb, ts, o.shape[-1]), resident(wo.shape),
                _mod_spec(nb, d), _const_spec((1, d)), _mod_spec(nb, d), _mod_spec(nb, d),
                _mod_spec(nb, d), of_layer(w_in), of_layer(w_out)]
    args = [x, o, wo, gate_m, g_ffn, shift_f, scale_f, gate_f, w_in, w_out]
    if g_out is not None:
        in_specs.append(_const_spec((1, d)))
        args.append(g_out)
    return pl.pallas_call(
        functools.partial(_block_kernel, hidden=hidden, chunk=256, final_norm=g_out is not None),
        grid=(bx // nb, sx // ts),
        in_specs=in_specs,
        out_specs=_tok_spec(nb, ts, d),
        out_shape=jax.ShapeDtypeStruct((bx, sx, d), F32),
        compiler_params=_cparams(2),
    )(*args)


def _stick_kernel(q_ref, k_ref, v_ref, tri_ref, o_ref, run_ref, acc_ref,
                  *, tq, bk, sk_valid, q_off, k_off):
    qpos0 = pl.program_id(2) * tq + q_off
    kb_hi = (jnp.minimum(qpos0 + (tq - 1) - k_off, sk_valid) + bk - 1) // bk
    f_hi = jnp.minimum(jnp.minimum(qpos0 - k_off, sk_valid) // bk, kb_hi)

    def visible(kb):
        row = lax.broadcasted_iota(jnp.int32, (2 * tq, 1), 0)
        qpos = jnp.where(row < tq, row, row - tq) + qpos0
        ik = kb * bk + lax.broadcasted_iota(jnp.int32, (1, bk), 1)
        return (ik + k_off < qpos) & (ik < sk_valid)

    lane_half = lax.broadcasted_iota(jnp.int32, (1, LANES), 1) // HEAD_DIM
    n_pairs = q_ref.shape[-1] // LANES
    lanes = [slice(pr * LANES, (pr + 1) * LANES) for pr in range(n_pairs)]
    q_both = [jnp.concatenate(
        [jnp.where(lane_half == hh, q_ref[0, :, lanes[pr]], jnp.zeros((), BF16))
         for hh in range(2)], axis=0) for pr in range(n_pairs)]

    def rows(kb):
        return pl.ds(pl.multiple_of(kb * bk, bk), bk)

    def block(pr, kb, masked, run):
        y = lax.dot_general(q_both[pr], k_ref[0, rows(kb), lanes[pr]], (((1,), (1,)), ((), ())),
                            preferred_element_type=F32)
        sp = jnp.maximum(y, 0.0) + jnp.log2(1.0 + jnp.exp2(-jnp.abs(y)))
        if masked:
            vis = visible(kb)
            sp = jnp.where(vis, sp, 0.0)
        suffix = jnp.dot(sp.astype(BF16), tri_ref[...], preferred_element_type=F32)
        a = jnp.exp2(y - suffix - run)
        if masked:
            a = jnp.where(vis, a, 0.0)
        return (jnp.dot(a.astype(BF16), v_ref[0, rows(kb), lanes[pr]],
                        preferred_element_type=F32),
                jnp.sum(sp, axis=1, keepdims=True))

    @pl.when(f_hi > 0)
    def _():
        for pr in range(n_pairs):
            out_d, sum_d = block(pr, kb_hi - 1, True, 0.0)
            out_f, sum_f = block(pr, f_hi - 1, False, sum_d)
            acc_ref[pr] = out_d + out_f
            run_ref[pr] = sum_d + sum_f

    @pl.when(f_hi == 0)
    def _():
        for pr in range(n_pairs):
            out_d, sum_d = block(pr, kb_hi - 1, True, 0.0)
            acc_ref[pr] = out_d
            run_ref[pr] = sum_d

    def earlier(st):
        for pr in range(n_pairs):
            out, row_sum = block(pr, f_hi - 1 - st[0], False, run_ref[pr])
            acc_ref[pr] += out
            run_ref[pr] += row_sum
        return st[0] + 1, jnp.min(run_ref[...])

    lax.while_loop(lambda st: (st[0] < f_hi) & (st[1] < STICK_DEAD_BITS), earlier,
                   (jnp.ones((), jnp.int32), jnp.min(run_ref[...])))
    for pr in range(n_pairs):
        o_ref[0, :, lanes[pr]] = jnp.where(
            lane_half == 0, acc_ref[pr, :tq], acc_ref[pr, tq:]).astype(o_ref.dtype)


def _stick_attention(q, k, v, *, sk_valid, q_off, k_off):
    bx, sq, qw = q.shape
    rows = k.shape[1]
    n_pairs = qw // LANES
    tq = min(Q_TILE, sq)
    bk = KV_BLOCK
    assert sq % tq == 0 and rows % bk == 0 and tq % 8 == 0 and q_off >= k_off >= 0
    for q0 in range(0, sq, tq):
        first, last = q_off + q0 - k_off, min(q_off + q0 + tq - 1 - k_off, sk_valid)
        assert first <= sk_valid and -(-last // bk) - first // bk == 1
    idx = np.arange(bk)
    pps = B_PAIRS_PER_STEP
    assert n_pairs % pps == 0
    return pl.pallas_call(
        functools.partial(_stick_kernel, tq=tq, bk=bk, sk_valid=sk_valid, q_off=q_off,
                          k_off=k_off),
        grid=(bx, n_pairs // pps, sq // tq),
        in_specs=[
            pl.BlockSpec((1, tq, pps * LANES), lambda b, p, i: (b, i, p)),
            pl.BlockSpec((1, rows, pps * LANES), lambda b, p, i: (b, 0, p)),
            pl.BlockSpec((1, rows, pps * LANES), lambda b, p, i: (b, 0, p)),
            pl.BlockSpec((bk, bk), lambda b, p, i: (0, 0)),
        ],
        out_specs=pl.BlockSpec((1, tq, pps * LANES), lambda b, p, i: (b, i, p)),
        out_shape=jax.ShapeDtypeStruct((bx, sq, n_pairs * LANES), BF16),
        scratch_shapes=[pltpu.VMEM((pps, 2 * tq, 1), F32), pltpu.VMEM((pps, 2 * tq, LANES), F32)],
        compiler_params=_cparams(3),
    )(q, k, v, jnp.asarray(idx[:, None] >= idx[None, :], BF16))


def _row_end(qpos, k_off):
    return (qpos // CHUNK + 1) * CHUNK - k_off


def _causal_kernel(q_ref, k_ref, v_ref, o_ref, *, tq, span, n_spans, sk_valid, q_off, k_off,
                   single_tile):
    qpos0 = pl.program_id(2) * tq + q_off
    n_needed = (jnp.minimum(_row_end(qpos0 + tq - 1, k_off), sk_valid) - 1) // span
    shift = int(math.log2(CHUNK))
    lane_half = lax.broadcasted_iota(jnp.int32, (1, LANES), 1) // HEAD_DIM
    nt = (((1,), (1,)), ((), ()))

    for n_full in range(n_spans):
        if single_tile and n_full != (min(_row_end(q_off + tq - 1, k_off), sk_valid) - 1) // span:
            continue

        @pl.when(n_needed == n_full)
        def _(n_full=n_full):
            full, width = n_full * span, (n_full + 1) * span
            qpos = lax.broadcasted_iota(jnp.int32, (tq, 1), 0) + qpos0
            kidx = lax.broadcasted_iota(jnp.int32, (1, span), 1) + full
            vis = (kidx < sk_valid) & (jnp.right_shift(kidx + k_off, shift)
                                       <= jnp.right_shift(qpos, shift))
            outs = []
            for hh in range(2 * (q_ref.shape[-1] // (2 * LANES))):
                lanes = slice(hh * LANES, (hh + 1) * LANES)
                v_lanes = slice(hh // 2 * LANES, (hh // 2 + 1) * LANES)
                q_h = q_ref[0, :, lanes]
                s_tail = lax.dot_general(q_h, k_ref[0, full:width, lanes], nt,
                                         preferred_element_type=F32)
                s_tail = jnp.where(vis, s_tail, MASK_VALUE)
                m = jnp.max(s_tail, axis=1, keepdims=True)
                if n_full:
                    s_full = lax.dot_general(q_h, k_ref[0, 0:full, lanes], nt,
                                             preferred_element_type=F32)
                    m = jnp.maximum(m, jnp.max(s_full, axis=1, keepdims=True))
                p_tail = jnp.exp2(s_tail - m)
                l = jnp.sum(p_tail, axis=1, keepdims=True)
                o = jnp.dot(p_tail.astype(BF16), v_ref[0, full:width, v_lanes],
                            preferred_element_type=F32)
                for c0 in range(0, full, A_PV_CHUNK):
                    p_c = jnp.exp2(s_full[:, c0:c0 + A_PV_CHUNK] - m)
                    l = l + jnp.sum(p_c, axis=1, keepdims=True)
                    o = o + jnp.dot(p_c.astype(BF16), v_ref[0, c0:c0 + A_PV_CHUNK, v_lanes],
                                    preferred_element_type=F32)
                outs.append(o / l)
            for pr in range(len(outs) // 2):
                o_ref[0, :, pr * LANES:(pr + 1) * LANES] = jnp.where(
                    lane_half == 0, outs[2 * pr], outs[2 * pr + 1]).astype(o_ref.dtype)


def _causal_attention(q, k, v, *, span, sk_valid, q_off, k_off):
    bx, sq, qw = q.shape
    rows = k.shape[1]
    pps = A_PAIRS_PER_STEP
    n_pairs = qw // (2 * LANES)
    assert n_pairs % pps == 0
    tq = min(A_Q_TILE, sq)
    assert sq % tq == 0 and rows % span == 0 and q_off % CHUNK == 0 and k_off >= 0
    for q0 in range(0, sq, tq):
        end = min(_row_end(q_off + q0 + tq - 1, k_off), sk_valid)
        first_row_end = min(_row_end(q_off + q0, k_off), sk_valid)
        assert 0 < end <= rows and (end - 1) // span * span <= first_row_end
    return pl.pallas_call(
        functools.partial(_causal_kernel, tq=tq, span=span, n_spans=rows // span,
                          sk_valid=sk_valid, q_off=q_off, k_off=k_off, single_tile=sq == tq),
        grid=(bx, n_pairs // pps, sq // tq),
        in_specs=[
            pl.BlockSpec((1, tq, pps * 2 * LANES), lambda b, p, i: (b, i, p)),
            pl.BlockSpec((1, rows, pps * 2 * LANES), lambda b, p, i: (b, 0, p)),
            pl.BlockSpec((1, rows, pps * LANES), lambda b, p, i: (b, 0, p)),
        ],
        out_specs=pl.BlockSpec((1, tq, pps * LANES), lambda b, p, i: (b, i, p)),
        out_shape=jax.ShapeDtypeStruct((bx, sq, n_pairs * LANES), BF16),
        compiler_params=_cparams(3),
    )(q, k, v)


def _band_kernel(*refs, tq, width, sk_valid, q_off, k_off, front, n_prev, use_sink, per_head_bias):
    refs = list(refs)
    q_ref, k_ref, v_ref = refs[:3]
    rest = refs[3:]
    sink_ref = rest.pop(0) if use_sink else None
    bias_ref, o_ref = rest

    group = q_ref.shape[-1] // LANES
    first_pair = pl.program_id(1) * group
    qpos0 = pl.program_id(2) * tq + q_off
    start = pl.multiple_of((qpos0 // CHUNK - n_prev) * CHUNK - k_off + front, CHUNK)
    band = pl.ds(start, width)
    kidx = lax.broadcasted_iota(jnp.int32, (1, width), 1) + (start - front)
    invalid = jnp.where((kidx >= 0) & (kidx < sk_valid), 0.0, MASK_VALUE)

    lane_half = lax.broadcasted_iota(jnp.int32, (1, LANES), 1) // HEAD_DIM
    shared_kv = k_ref.shape[-1] == LANES
    for j in range(group):
        q_pair = q_ref[0, :, j * LANES:(j + 1) * LANES]
        kv_lanes = slice(0, LANES) if shared_kv else slice(j * LANES, (j + 1) * LANES)
        k_band = k_ref[0, band, kv_lanes]
        v_band = v_ref[0, band, kv_lanes]
        outs = []
        for hh in range(2):
            q_h = jnp.where(lane_half == hh, q_pair, jnp.zeros((), BF16))
            s = lax.dot_general(q_h, k_band, (((1,), (1,)), ((), ())),
                                preferred_element_type=F32)
            s = s + (bias_ref[2 * j + hh] if per_head_bias else bias_ref[...]) + invalid
            m = jnp.max(s, axis=1, keepdims=True)
            if use_sink:
                sink = sink_ref[2 * (first_pair + j) + hh] * LOG2E
                m = jnp.maximum(m, sink)
            p = jnp.exp2(s - m)
            l = jnp.sum(p, axis=1, keepdims=True)
            if use_sink:
                l = l + jnp.exp2(sink - m)
            outs.append(jnp.dot(p.astype(BF16), v_band, preferred_element_type=F32) / l)
        o_ref[0, :, j * LANES:(j + 1) * LANES] = jnp.where(
            lane_half == 0, outs[0], outs[1]).astype(o_ref.dtype)


def _band_mask_tile(rows, width, n_prev):
    r_chunk = np.arange(rows)[:, None] // CHUNK
    c_chunk = np.arange(width)[None, :] // CHUNK - n_prev
    return np.where((c_chunk <= r_chunk) & (c_chunk >= r_chunk - n_prev), 0.0, MASK_VALUE
                    ).astype(np.float32)


def _band_attention(q, k, v, *, width, sk_valid, q_off, k_off, front, n_prev, pairs_per_step,
                    shared_kv, sink=None, bias=None):
    kv_group = pairs_per_step
    kv_lanes = LANES if shared_kv else pairs_per_step * LANES
    bx, sq, qw = q.shape
    rows = k.shape[1]
    n_pairs = qw // LANES
    tq = min(Q_TILE, sq)
    assert sq % tq == 0 and q_off % CHUNK == 0 and (tq % CHUNK == 0 or sq == tq)
    assert ((q_off + sq - tq) // CHUNK - n_prev) * CHUNK - k_off + front + width <= rows
    assert (q_off // CHUNK - n_prev) * CHUNK - k_off + front >= 0
    in_specs = [
        pl.BlockSpec((1, tq, kv_group * LANES), lambda b, g, i: (b, i, g)),
        pl.BlockSpec((1, rows, kv_lanes), lambda b, g, i: (b, 0, g)),
        pl.BlockSpec((1, rows, kv_lanes), lambda b, g, i: (b, 0, g)),
    ]
    args = [q, k, v]
    if sink is not None:
        in_specs.append(pl.BlockSpec(memory_space=pltpu.SMEM))
        args.append(sink)
    per_head_bias = bias.ndim == 3
    if per_head_bias:
        in_specs.append(pl.BlockSpec((2 * kv_group, tq, width), lambda b, g, i: (g, 0, 0)))
    else:
        in_specs.append(pl.BlockSpec((tq, width), lambda b, g, i: (0, 0)))
    args.append(bias)
    return pl.pallas_call(
        functools.partial(_band_kernel, tq=tq, width=width, sk_valid=sk_valid, q_off=q_off,
                          k_off=k_off, front=front, n_prev=n_prev,
                          use_sink=sink is not None, per_head_bias=per_head_bias),
        grid=(bx, n_pairs // kv_group, sq // tq),
        in_specs=in_specs,
        out_specs=pl.BlockSpec((1, tq, kv_group * LANES), lambda b, g, i: (b, i, g)),
        out_shape=jax.ShapeDtypeStruct((bx, sq, n_pairs * LANES), BF16),
        compiler_params=_cparams(3),
    )(*args)


def _band_bias_kernel(e_ref, mask_ref, o_ref, *, width):
    w = e_ref.shape[-1]
    x = jnp.broadcast_to(e_ref[0], (Q_TILE, w))
    toeplitz = pltpu.roll(x, 0, 1, stride=1, stride_axis=0)
    o_ref[0] = toeplitz[:, :width] * LOG2E + mask_ref[...]


def _band_bias(rel_bias):
    heads = rel_bias.shape[0]
    clip = (rel_bias.shape[1] - 1) // 2
    band = D_PREV_CHUNKS * CHUNK
    width = D_BAND_WIDTH
    w = width + Q_TILE
    assert band >= clip and width > band + clip
    top = jnp.broadcast_to(rel_bias[:, -1:], (heads, band - clip + 1))
    mid = jnp.flip(rel_bias[:, :2 * clip], axis=1)
    low = jnp.broadcast_to(rel_bias[:, :1], (heads, width - (band + clip + 1)))
    neg = jnp.broadcast_to(rel_bias[:, -1:], (heads, w - width))
    e = jnp.concatenate([top, mid, low, neg], axis=1)[:, None, :]
    return pl.pallas_call(
        functools.partial(_band_bias_kernel, width=width),
        grid=(heads,),
        in_specs=[pl.BlockSpec((1, 1, w), lambda h: (h, 0, 0)),
                  pl.BlockSpec((Q_TILE, width), lambda h: (0, 0))],
        out_specs=pl.BlockSpec((1, Q_TILE, width), lambda h: (h, 0, 0)),
        out_shape=jax.ShapeDtypeStruct((heads, Q_TILE, width), F32),
        compiler_params=_cparams(1),
    )(e, jnp.asarray(_band_mask_tile(Q_TILE, width, D_PREV_CHUNKS)))


def _rope_tables(pos, n_rot, lane_offsets, rows_repeat=1):
    half = n_rot // 2
    inv = ROPE_THETA ** (-jnp.arange(half, dtype=F32) * 2.0 / n_rot)
    ang = pos.astype(F32)[:, None] * inv[None, :]
    cos, sin = jnp.cos(ang), jnp.sin(ang)
    n = pos.shape[0]
    cos_t = jnp.ones((n, LANES), F32)
    sin_up = jnp.zeros((n, LANES), F32)
    sin_dn = jnp.zeros((n, LANES), F32)
    for o in lane_offsets:
        cos_t = cos_t.at[:, o:o + half].set(cos).at[:, o + half:o + n_rot].set(cos)
        sin_dn = sin_dn.at[:, o:o + half].set(-sin)
        sin_up = sin_up.at[:, o + half:o + n_rot].set(sin)
    def tiles(t):
        if rows_repeat > 1:
            return jnp.tile(t, (rows_repeat, 1))[None]
        rows = min(n, TOKEN_TILE)
        return t.reshape(n // rows, rows, LANES)
    return tiles(cos_t), tiles(sin_up), tiles(sin_dn)


def _tables_for(bx, sx, pos, n_rot, lane_offsets):
    nb, ts = _token_tiling(bx, sx)
    return _rope_tables(pos, n_rot, lane_offsets, rows_repeat=nb if nb > 1 else 1)


def _c_head_order():
    rep = C_HEADS // C_KV_HEADS
    order = []
    for p in range(C_HEADS // 2):
        g2, i = divmod(p, rep)
        order += [rep * (2 * g2) + i, rep * (2 * g2 + 1) + i]
    return np.asarray(order)


def _pad_rows(a, rows):
    return jnp.pad(a, ((0, 0), (0, rows - a.shape[1]), (0, 0)))


def _front_pad(a, rows):
    return jnp.pad(a, ((0, 0), (rows, 0), (0, 0)))


def _with_cache(cache, new, dtype, block=KV_BLOCK):
    full = jnp.concatenate([cache.reshape(cache.shape[0], cache.shape[1], -1).astype(dtype),
                            new.astype(dtype)], axis=1)
    rows = -(-full.shape[1] // block) * block
    return _pad_rows(full, rows), full.shape[1]


def kernel(x_prompt, x_sample, c_prompt, c_sample, cache_a_ckv, cache_a_krope, cache_b_k, cache_b_v,
           cache_c_k, cache_c_v, cache_d_k, cache_d_v, w_mod, b_mod, g_mix, g_ffn, w_ffn_in, w_ffn_out,
           w_a_down, g_a_q, g_a_kv, w_a_uq, w_a_uk, w_a_uv, w_a_o, w_b_qkv, w_b_o,
           w_c_qkv, b_c_qkv, sink_c, w_c_o, w_d_qkv, rel_bias_d, w_d_o, g_final):
    bp, sp, d = x_prompt.shape
    bs, t, _ = x_sample.shape
    depth = w_mod.shape[0]
    past = cache_a_ckv.shape[2]
    pos_p = jnp.arange(sp)
    pos_s = past + jnp.arange(t)

    mods = _adaln(jnp.concatenate([c_prompt, c_sample], axis=0), w_mod, b_mod)
    w_in_all, w_out_all = _to_bf16(w_ffn_in), _to_bf16(w_ffn_out)

    def mod(i, k):
        m = mods[i, k][:, None, :]
        return m[:bp], m[bp:]

    xp, xs = x_prompt, x_sample
    states = [[] for _ in range(N_MIXERS)]
    for i in range(depth):
        m, j = i % N_MIXERS, i // N_MIXERS
        (sh_p, sh_s), (sc_p, sc_s), (gm_p, gm_s) = mod(i, 0), mod(i, 1), mod(i, 2)
        (shf_p, shf_s), (scf_p, scf_s), (gf_p, gf_s) = mod(i, 3), mod(i, 4), mod(i, 5)
        g_m = g_mix[i][None, :]
        if m == 0:
            n_down = A_Q_LORA + A_KV_LORA + A_ROPE
            wd = jnp.pad(w_a_down[j], ((0, 0), (0, A_Q_LORA + A_KV_LORA + LANES - n_down))).astype(BF16)
            wq = w_a_uq[j].reshape(A_Q_LORA, A_HEADS, A_NOPE + A_ROPE)
            wq = jnp.pad(wq, ((0, 0), (0, 0), (0, LANES - A_NOPE - A_ROPE)))
            x1 = wq[:, :, A_NOPE:A_NOPE + A_ROPE // 2]
            x2 = wq[:, :, A_NOPE + A_ROPE // 2:A_NOPE + A_ROPE]
            wq_rot = jnp.zeros_like(wq).at[:, :, A_NOPE:A_NOPE + A_ROPE].set(
                jnp.concatenate([-x2, x1], axis=-1))
            wq = wq.reshape(A_Q_LORA, A_HEADS * LANES).astype(BF16)
            wq_rot = wq_rot.reshape(A_Q_LORA, A_HEADS * LANES).astype(BF16)
            wk = jnp.pad(w_a_uk[j], ((0, 0), (0, 0), (0, LANES - A_NOPE)))
            wk = wk.reshape(A_KV_LORA, A_HEADS * LANES).astype(BF16)
            place = jnp.zeros((A_ROPE, A_HEADS, LANES), F32)
            place = place.at[jnp.arange(A_ROPE), :, A_NOPE + jnp.arange(A_ROPE)].set(1.0)
            wr = place.reshape(A_ROPE, A_HEADS * LANES).astype(BF16)
            wv = w_a_uv[j].reshape(A_KV_LORA, A_HEADS * A_V).astype(BF16)
            g_q, g_kv = g_a_q[j][None, :], g_a_kv[j][None, :]

            def project(x, sh, sc, pos):
                q_cos, q_up, q_dn = _tables_for(x.shape[0], x.shape[1], pos, A_ROPE, [A_NOPE])
                kt = _tables_for(x.shape[0], x.shape[1], pos, A_ROPE, [0])
                return _proj_a(x, g_m, sh, sc, wd, g_q, g_kv, wq, wq_rot, (q_cos, q_up - q_dn), kt)

            q_p, ckv_p, kr_p = project(xp, sh_p, sc_p, pos_p)
            q_s, ckv_s, kr_s = project(xs, sh_s, sc_s, pos_s)
            k_p, v_p = _expand_a(ckv_p, kr_p, wk, wr, wv)
            o_p = _causal_attention(q_p, k_p, v_p, span=min(A_KV_BLOCK, sp), sk_valid=sp,
                                    q_off=0, k_off=0)
            n_all = past + t
            rows = -(-n_all // LANES) * LANES
            kcat = jnp.concatenate(
                [jnp.concatenate([cache_a_ckv[j], ckv_s], axis=1),
                 jnp.concatenate([cache_a_krope[j], kr_s], axis=1)], axis=-1).astype(BF16)
            kcat = jnp.pad(kcat, ((0, 0), (0, rows - n_all),
                                  (0, A_KV_LORA + LANES - kcat.shape[-1])))
            wabs = jnp.zeros((A_HEADS, LANES, A_KV_LORA + LANES), F32)
            wabs = wabs.at[:, :A_NOPE, :A_KV_LORA].set(w_a_uk[j].transpose(1, 2, 0))
            wabs = wabs.at[:, A_NOPE + jnp.arange(A_ROPE), A_KV_LORA + jnp.arange(A_ROPE)].set(1.0)
            wv_placed = jnp.zeros((A_HEADS, A_KV_LORA, A_HEADS, A_V), F32)
            wv_placed = wv_placed.at[jnp.arange(A_HEADS), :, jnp.arange(A_HEADS), :].set(
                w_a_uv[j].transpose(1, 0, 2))
            wv_placed = wv_placed.reshape(A_HEADS, A_KV_LORA, A_HEADS * A_V)
            o_s = _latent_decode(q_s, kcat, wabs.astype(BF16), wv_placed.astype(BF16),
                                 sk_valid=n_all, q_off=past, k_off=0)
            wo = w_a_o[j].astype(BF16)
            states[0].append((ckv_p, kr_p, ckv_s, kr_s))
        elif m == 1 or m == 3:
            heads = B_HEADS if m == 1 else D_HEADS
            w_qkv = (w_b_qkv if m == 1 else w_d_qkv)[j].astype(BF16)
            cache_k, cache_v = (cache_b_k, cache_b_v) if m == 1 else (cache_d_k, cache_d_v)
            q_scale = HEAD_DIM ** -0.5 * LOG2E
            q_p, kf_p, vf_p, kb_p, vb_p = _proj_qkv(xp, g_m, sh_p, sc_p, w_qkv, heads, q_scale)
            q_s, kf_s, vf_s, kb_s, vb_s = _proj_qkv(xs, g_m, sh_s, sc_s, w_qkv, heads, q_scale)
            lc = cache_k.shape[2]
            if m == 1:
                o_p = _stick_attention(q_p, kb_p, vb_p, sk_valid=sp, q_off=0, k_off=0)
                k_s, n_all = _with_cache(cache_k[j], kb_s, BF16, KV_BLOCK)
                v_s, _ = _with_cache(cache_v[j], vb_s, BF16, KV_BLOCK)
                o_s = _stick_attention(q_s, k_s, v_s, sk_valid=n_all, q_off=past,
                                       k_off=past - lc)
                wo = w_b_o[j].astype(BF16)
                states[1].append((kf_p, vf_p, kf_s, vf_s))
            else:
                bias = _band_bias(rel_bias_d[j])
                k_s, n_all = _with_cache(cache_k[j], kb_s, BF16, D_BAND_WIDTH)
                v_s, _ = _with_cache(cache_v[j], vb_s, BF16, D_BAND_WIDTH)
                front = D_PREV_CHUNKS * CHUNK
                o_p = _band_attention(q_p, _front_pad(kb_p, front), _front_pad(vb_p, front),
                                      width=D_BAND_WIDTH, sk_valid=sp, q_off=0, k_off=0,
                                      front=front, n_prev=D_PREV_CHUNKS, bias=bias,
                                      pairs_per_step=D_PAIRS_PER_STEP, shared_kv=False)
                o_s = _band_attention(q_s, k_s, v_s, width=D_BAND_WIDTH, sk_valid=n_all,
                                      q_off=past, k_off=past - lc, front=0,
                                      n_prev=D_PREV_CHUNKS, bias=bias[:, :t, :],
                                      pairs_per_step=D_PAIRS_PER_STEP, shared_kv=False)
                wo = w_d_o[j].astype(BF16)
                keep = min(D_PREV_CHUNKS * CHUNK, sp)
                k_roll = jnp.concatenate([cache_k[j].reshape(bs, lc, -1), kf_s], axis=1)[:, t:]
                v_roll = jnp.concatenate([cache_v[j].reshape(bs, lc, -1), vf_s], axis=1)[:, t:]
                states[3].append((kf_p[:, sp - keep:], vf_p[:, sp - keep:], k_roll, v_roll))
        else:
            order = _c_head_order()
            q_width, kv_width = C_HEADS * HEAD_DIM, C_KV_HEADS * HEAD_DIM
            col = np.concatenate([(order[:, None] * HEAD_DIM + np.arange(HEAD_DIM)).reshape(-1),
                                  np.arange(q_width, q_width + 2 * kv_width)])
            w_qkv = w_c_qkv[j][:, col].astype(BF16)
            b_qkv = b_c_qkv[j][col][None, :]
            sink = sink_c[j][order]
            wo = w_c_o[j].reshape(C_HEADS, HEAD_DIM, d)[order].reshape(q_width, d).astype(BF16)
            lanes = [0, HEAD_DIM]

            def project(x, sh, sc, pos):
                tb = _tables_for(x.shape[0], x.shape[1], pos, C_ROT, lanes)
                return _proj_c(x, g_m, sh, sc, w_qkv, b_qkv, tb)

            q_p, kf_p, vf_p, kb_p, vb_p = project(xp, sh_p, sc_p, pos_p)
            q_s, kf_s, vf_s, kb_s, vb_s = project(xs, sh_s, sc_s, pos_s)
            lc = cache_c_k.shape[2]
            k_s, n_all = _with_cache(cache_c_k[j], kb_s, BF16, C_BAND_WIDTH)
            v_s, _ = _with_cache(cache_c_v[j], vb_s, BF16, C_BAND_WIDTH)
            group = C_HEADS // C_KV_HEADS
            front = C_PREV_CHUNKS * CHUNK
            band_mask = jnp.asarray(_band_mask_tile(Q_TILE, C_BAND_WIDTH, C_PREV_CHUNKS))
            o_p = _band_attention(q_p, _front_pad(kb_p, front), _front_pad(vb_p, front),
                                  width=C_BAND_WIDTH, sk_valid=sp, q_off=0, k_off=0, front=front,
                                  n_prev=C_PREV_CHUNKS, pairs_per_step=group, shared_kv=True, sink=sink,
                                  bias=band_mask)
            o_s = _band_attention(q_s, k_s, v_s, width=C_BAND_WIDTH, sk_valid=n_all, q_off=past,
                                  k_off=past - lc, front=0, n_prev=C_PREV_CHUNKS, pairs_per_step=group, shared_kv=True,
                                  sink=sink, bias=band_mask[:t])
            keep = min(C_PREV_CHUNKS * CHUNK, sp)
            k_roll = jnp.concatenate([cache_c_k[j].reshape(bs, lc, -1), kf_s], axis=1)[:, t:]
            v_roll = jnp.concatenate([cache_c_v[j].reshape(bs, lc, -1), vf_s], axis=1)[:, t:]
            states[2].append((kf_p[:, sp - keep:], vf_p[:, sp - keep:], k_roll, v_roll))

        g_f = g_ffn[i][None, :]
        g_out = g_final[None, :] if i == depth - 1 else None
        xp = _block(xp, o_p, wo, gm_p, g_f, shf_p, scf_p, gf_p, w_in_all, w_out_all, i, g_out)
        xs = _block(xs, o_s, wo, gm_s, g_f, shf_s, scf_s, gf_s, w_in_all, w_out_all, i, g_out)

    y_p, y_s = xp, xs

    def stacked(entries, n_heads=None):
        outs = []
        for parts in zip(*entries):
            a = jnp.stack(parts, axis=0)
            if n_heads is not None:
                a = a.reshape(a.shape[:3] + (n_heads, HEAD_DIM))
            outs.append(a)
        return tuple(outs)

    return ((y_p, y_s) + stacked(states[0]) + stacked(states[1], B_HEADS)
            + stacked(states[2], C_KV_HEADS) + stacked(states[3], D_HEADS))
```

```python
import functools
import math

import numpy as np
import jax
import jax.numpy as jnp
from jax import lax
from jax.experimental import pallas as pl
from jax.experimental.pallas import tpu as pltpu

F32 = jnp.float32
BF16 = jnp.bfloat16

CHUNK = 64
HEAD_DIM = 64
ROPE_THETA = 500000.0
NORM_EPS = 1e-6
N_MIXERS = 4
A_HEADS, A_Q_LORA, A_KV_LORA, A_NOPE, A_ROPE, A_V = 16, 384, 256, 64, 32, 64
B_HEADS = 16
C_HEADS, C_KV_HEADS, C_WINDOW, C_ROT = 16, 4, 128, 16
D_HEADS, D_PREV_CHUNKS, D_REL_CLIP = 16, 8, 128
C_PREV_CHUNKS = C_WINDOW // CHUNK

LANES = 128
V7X_VMEM_LIMIT = 56 * 1024 * 1024

TOKEN_TILE = 512
Q_TILE = 256
KV_BLOCK = 256
C_BAND_WIDTH = C_PREV_CHUNKS * CHUNK + Q_TILE
D_BAND_WIDTH = D_PREV_CHUNKS * CHUNK + Q_TILE
A_Q_TILE = 256
A_KV_BLOCK = 512
A_PV_CHUNK = 512
A_PAIRS_PER_STEP = 2
B_PAIRS_PER_STEP = 4
D_PAIRS_PER_STEP = 4
MASK_VALUE = -1e30
LOG2E = math.log2(math.e)
STICK_DEAD_BITS = 150.0


def _cparams(n_axes):
    return pltpu.CompilerParams(
        dimension_semantics=("parallel",) * n_axes, vmem_limit_bytes=V7X_VMEM_LIMIT)


def _adaln_kernel(c_ref, w_ref, b_ref, o_ref):
    c = c_ref[...]
    a = (c * jax.nn.sigmoid(c)).astype(BF16)
    y = jnp.dot(a, w_ref[0].astype(BF16), preferred_element_type=F32) + b_ref[0]
    o_ref[0, 0] = y


def _adaln(c_all, w_mod, b_mod):
    depth, d, d6 = w_mod.shape
    n = c_all.shape[0]
    return pl.pallas_call(
        _adaln_kernel,
        grid=(depth, d6 // d),
        in_specs=[
            pl.BlockSpec((n, d), lambda i, k: (0, 0)),
            pl.BlockSpec((1, d, d), lambda i, k: (i, 0, k)),
            pl.BlockSpec((1, 1, d), lambda i, k: (i, 0, k)),
        ],
        out_specs=pl.BlockSpec((1, 1, n, d), lambda i, k: (i, k, 0, 0)),
        out_shape=jax.ShapeDtypeStruct((depth, d6 // d, n, d), F32),
        compiler_params=_cparams(2),
    )(c_all, w_mod, b_mod.reshape(depth, 1, d6))


def _cast_kernel(x_ref, o_ref):
    o_ref[...] = x_ref[...].astype(o_ref.dtype)


def _to_bf16(w):
    n, rows, cols = w.shape
    slab = 256
    assert rows % slab == 0
    return pl.pallas_call(
        _cast_kernel,
        grid=(n, rows // slab),
        in_specs=[pl.BlockSpec((1, slab, cols), lambda i, r: (i, r, 0))],
        out_specs=pl.BlockSpec((1, slab, cols), lambda i, r: (i, r, 0)),
        out_shape=jax.ShapeDtypeStruct(w.shape, BF16),
        compiler_params=_cparams(2),
    )(w)


def _token_tiling(bx, sx):
    if sx >= TOKEN_TILE:
        ts = TOKEN_TILE
        while sx % ts:
            ts //= 2
        return 1, ts
    nb = min(bx, TOKEN_TILE // sx)
    while bx % nb:
        nb -= 1
    return nb, sx


def _tok_spec(nb, ts, width):
    return pl.BlockSpec((nb, ts, width), lambda b, s: (b, s, 0))


def _mod_spec(nb, d):
    return pl.BlockSpec((nb, 1, d), lambda b, s: (b, 0, 0))


def _const_spec(shape):
    nd = len(shape)
    return pl.BlockSpec(shape, lambda b, s: (0,) * nd)


def _table_spec(rows):
    return pl.BlockSpec((1, rows, LANES), lambda b, s: (s, 0, 0))


def _rms(x):
    return x * lax.rsqrt(jnp.mean(x * x, axis=-1, keepdims=True) + NORM_EPS)


def _modulated(x_ref, g_ref, shift_ref, scale_ref):
    y = _rms(x_ref[...]) * g_ref[...]
    h = y * (1.0 + scale_ref[...]) + shift_ref[...]
    return h.reshape(-1, h.shape[-1]).astype(BF16)


def _rope_lanes(x, cos_t, sin_up, sin_dn, half):
    return (x * cos_t + pltpu.roll(x, half, 1) * sin_up
            + pltpu.roll(x, LANES - half, 1) * sin_dn)


def _store_tok(ref, lo, val):
    nb, ts = ref.shape[0], ref.shape[1]
    w = val.shape[-1]
    ref[:, :, lo:lo + w] = val.reshape(nb, ts, w).astype(ref.dtype)


def _proj_qkv_kernel(x_ref, g_ref, sh_ref, sc_ref, w_ref, q_ref, kf_ref, vf_ref, kb_ref, vb_ref,
                     *, width, q_scale):
    h = _modulated(x_ref, g_ref, sh_ref, sc_ref)
    q = jnp.dot(h, w_ref[:, 0:width], preferred_element_type=F32)
    _store_tok(q_ref, 0, q * q_scale)
    k = jnp.dot(h, w_ref[:, width:2 * width], preferred_element_type=F32)
    _store_tok(kf_ref, 0, k)
    _store_tok(kb_ref, 0, k)
    v = jnp.dot(h, w_ref[:, 2 * width:3 * width], preferred_element_type=F32)
    _store_tok(vf_ref, 0, v)
    _store_tok(vb_ref, 0, v)


def _proj_qkv(x, g, shift, scale, w_bf16, n_heads, q_scale):
    bx, sx, d = x.shape
    width = n_heads * HEAD_DIM
    nb, ts = _token_tiling(bx, sx)
    out = lambda dt: jax.ShapeDtypeStruct((bx, sx, width), dt)
    return pl.pallas_call(
        functools.partial(_proj_qkv_kernel, width=width, q_scale=q_scale),
        grid=(bx // nb, sx // ts),
        in_specs=[_tok_spec(nb, ts, d), _const_spec((1, d)), _mod_spec(nb, d), _mod_spec(nb, d),
                  _const_spec(w_bf16.shape)],
        out_specs=[_tok_spec(nb, ts, width)] * 5,
        out_shape=[out(BF16), out(F32), out(F32), out(BF16), out(BF16)],
        compiler_params=_cparams(2),
    )(x, g, shift, scale, w_bf16)


def _proj_c_kernel(x_ref, g_ref, sh_ref, sc_ref, w_ref, b_ref, tc_ref, tu_ref, td_ref,
                   q_ref, kf_ref, vf_ref, kb_ref, vb_ref, *, q_width, kv_width, q_scale):
    h = _modulated(x_ref, g_ref, sh_ref, sc_ref)
    qkv = jnp.dot(h, w_ref[...], preferred_element_type=F32) + b_ref[...]
    cos_t, sin_up, sin_dn = tc_ref[0], tu_ref[0], td_ref[0]
    half = C_ROT // 2
    for j in range(q_width // LANES):
        xg = qkv[:, j * LANES:(j + 1) * LANES]
        _store_tok(q_ref, j * LANES, _rope_lanes(xg, cos_t, sin_up, sin_dn, half) * q_scale)
    for j in range(kv_width // LANES):
        lo = q_width + j * LANES
        kg = _rope_lanes(qkv[:, lo:lo + LANES], cos_t, sin_up, sin_dn, half)
        _store_tok(kf_ref, j * LANES, kg)
        _store_tok(kb_ref, j * LANES, kg)
    v = qkv[:, q_width + kv_width:q_width + 2 * kv_width]
    _store_tok(vf_ref, 0, v)
    _store_tok(vb_ref, 0, v)


def _proj_c(x, g, shift, scale, w_bf16, bias, tables):
    bx, sx, d = x.shape
    q_width, kv_width = C_HEADS * HEAD_DIM, C_KV_HEADS * HEAD_DIM
    nb, ts = _token_tiling(bx, sx)
    out = lambda w, dt: jax.ShapeDtypeStruct((bx, sx, w), dt)
    return pl.pallas_call(
        functools.partial(_proj_c_kernel, q_width=q_width, kv_width=kv_width,
                          q_scale=HEAD_DIM ** -0.5 * LOG2E),
        grid=(bx // nb, sx // ts),
        in_specs=[_tok_spec(nb, ts, d), _const_spec((1, d)), _mod_spec(nb, d), _mod_spec(nb, d),
                  _const_spec(w_bf16.shape), _const_spec(bias.shape)] + [_table_spec(nb * ts)] * 3,
        out_specs=[_tok_spec(nb, ts, q_width)] + [_tok_spec(nb, ts, kv_width)] * 4,
        out_shape=[out(q_width, BF16), out(kv_width, F32), out(kv_width, F32),
                   out(kv_width, BF16), out(kv_width, BF16)],
        compiler_params=_cparams(2),
    )(x, g, shift, scale, w_bf16, bias, *tables)


def _proj_a_kernel(x_ref, g_ref, sh_ref, sc_ref, wd_ref, gq_ref, gkv_ref, wq_ref, wqr_ref,
                   qc_ref, qs_ref, kc_ref, ku_ref, kd_ref,
                   q_ref, ckv_ref, kr_ref, *, q_scale):
    h = _modulated(x_ref, g_ref, sh_ref, sc_ref)
    down = jnp.dot(h, wd_ref[...], preferred_element_type=F32)
    cq = (_rms(down[:, :A_Q_LORA]) * gq_ref[...]).astype(BF16)
    ckv = _rms(down[:, A_Q_LORA:A_Q_LORA + A_KV_LORA]) * gkv_ref[...]
    _store_tok(ckv_ref, 0, ckv)
    lo = A_Q_LORA + A_KV_LORA
    half = A_ROPE // 2
    kr = _rope_lanes(down[:, lo:lo + LANES], kc_ref[0], ku_ref[0], kd_ref[0], half)
    _store_tok(kr_ref, 0, kr[:, :A_ROPE])
    q = jnp.dot(cq, wq_ref[...], preferred_element_type=F32)
    q_rot = jnp.dot(cq, wqr_ref[...], preferred_element_type=F32)
    qc, qs = qc_ref[0], qs_ref[0]
    for j in range(A_HEADS):
        lanes = slice(j * LANES, (j + 1) * LANES)
        _store_tok(q_ref, j * LANES, (q[:, lanes] * qc + q_rot[:, lanes] * qs) * q_scale)


def _proj_a(x, g, shift, scale, wd_bf16, g_q, g_kv, wq_bf16, wq_rot_bf16, q_tables, k_tables):
    bx, sx, d = x.shape
    nb, ts = _token_tiling(bx, sx)
    return pl.pallas_call(
        functools.partial(_proj_a_kernel, q_scale=(A_NOPE + A_ROPE) ** -0.5 * LOG2E),
        grid=(bx // nb, sx // ts),
        in_specs=[_tok_spec(nb, ts, d), _const_spec((1, d)), _mod_spec(nb, d), _mod_spec(nb, d),
                  _const_spec(wd_bf16.shape), _const_spec(g_q.shape), _const_spec(g_kv.shape),
                  _const_spec(wq_bf16.shape), _const_spec(wq_rot_bf16.shape)]
                 + [_table_spec(nb * ts)] * 5,
        out_specs=[_tok_spec(nb, ts, A_HEADS * LANES), _tok_spec(nb, ts, A_KV_LORA),
                   _tok_spec(nb, ts, A_ROPE)],
        out_shape=[jax.ShapeDtypeStruct((bx, sx, A_HEADS * LANES), BF16),
                   jax.ShapeDtypeStruct((bx, sx, A_KV_LORA), F32),
                   jax.ShapeDtypeStruct((bx, sx, A_ROPE), F32)],
        compiler_params=_cparams(2),
    )(x, g, shift, scale, wd_bf16, g_q, g_kv, wq_bf16, wq_rot_bf16, *q_tables, *k_tables)


def _expand_a_kernel(ckv_ref, kr_ref, wk_ref, wr_ref, wv_ref, k_ref, v_ref):
    ckv = ckv_ref[...]
    ckv = ckv.reshape(-1, ckv.shape[-1]).astype(BF16)
    kr = kr_ref[...]
    kr = kr.reshape(-1, kr.shape[-1]).astype(BF16)
    k = (jnp.dot(ckv, wk_ref[...], preferred_element_type=F32)
         + jnp.dot(kr, wr_ref[...], preferred_element_type=F32))
    _store_tok(k_ref, 0, k)
    _store_tok(v_ref, 0, jnp.dot(ckv, wv_ref[...], preferred_element_type=F32))


def _expand_a(ckv, kr, wk, wr, wv):
    bx, sx, _ = ckv.shape
    nb, ts = _token_tiling(bx, sx)
    return pl.pallas_call(
        _expand_a_kernel,
        grid=(bx // nb, sx // ts),
        in_specs=[_tok_spec(nb, ts, A_KV_LORA), _tok_spec(nb, ts, A_ROPE),
                  _const_spec(wk.shape), _const_spec(wr.shape), _const_spec(wv.shape)],
        out_specs=[_tok_spec(nb, ts, A_HEADS * LANES), _tok_spec(nb, ts, A_HEADS * A_V)],
        out_shape=[jax.ShapeDtypeStruct((bx, sx, A_HEADS * LANES), BF16),
                   jax.ShapeDtypeStruct((bx, sx, A_HEADS * A_V), BF16)],
        compiler_params=_cparams(2),
    )(ckv, kr, wk, wr, wv)


def _latent_decode_kernel(q_ref, kcat_ref, wabs_ref, wv_ref, o_ref, qc_ref,
                          *, sk_valid, q_off, k_off):
    t = q_ref.shape[1]
    n_heads = wabs_ref.shape[0]
    rows = kcat_ref.shape[1]
    for h in range(n_heads):
        qc_ref[h * t:(h + 1) * t, :] = jnp.dot(
            q_ref[0, :, h * LANES:(h + 1) * LANES], wabs_ref[h],
            preferred_element_type=F32).astype(BF16)
    kcat = kcat_ref[0]
    s = lax.dot_general(qc_ref[...], kcat, (((1,), (1,)), ((), ())),
                        preferred_element_type=F32)
    shift = int(math.log2(CHUNK))
    row = lax.broadcasted_iota(jnp.int32, (n_heads * t, 1), 0)
    qpos = row - row // t * t + q_off
    kidx = lax.broadcasted_iota(jnp.int32, (1, rows), 1)
    vis = (kidx < sk_valid) & (jnp.right_shift(kidx + k_off, shift) <= jnp.right_shift(qpos, shift))
    s = jnp.where(vis, s, MASK_VALUE)
    m = jnp.max(s, axis=1, keepdims=True)
    p = jnp.exp2(s - m)
    l = jnp.sum(p, axis=1, keepdims=True)
    o_lat = (jnp.dot(p.astype(BF16), kcat[:, :A_KV_LORA], preferred_element_type=F32) / l
             ).astype(BF16)
    out = jnp.zeros((t, o_ref.shape[-1]), F32)
    for h in range(n_heads):
        out = out + jnp.dot(o_lat[h * t:(h + 1) * t], wv_ref[h], preferred_element_type=F32)
    o_ref[0] = out.astype(o_ref.dtype)


def _latent_decode(q, kcat, wabs, wv_placed, *, sk_valid, q_off, k_off):
    bx, t, _ = q.shape
    rows, width = kcat.shape[1:]
    n_heads = wabs.shape[0]
    out_w = wv_placed.shape[-1]
    return pl.pallas_call(
        functools.partial(_latent_decode_kernel, sk_valid=sk_valid, q_off=q_off, k_off=k_off),
        grid=(bx,),
        in_specs=[pl.BlockSpec((1, t, n_heads * LANES), lambda b: (b, 0, 0)),
                  pl.BlockSpec((1, rows, width), lambda b: (b, 0, 0)),
                  pl.BlockSpec(wabs.shape, lambda b: (0, 0, 0)),
                  pl.BlockSpec(wv_placed.shape, lambda b: (0, 0, 0))],
        out_specs=pl.BlockSpec((1, t, out_w), lambda b: (b, 0, 0)),
        out_shape=jax.ShapeDtypeStruct((bx, t, out_w), BF16),
        scratch_shapes=[pltpu.VMEM((n_heads * t, width), BF16)],
        compiler_params=_cparams(1),
    )(q, kcat, wabs, wv_placed)


def _block_kernel(x_ref, o_ref, wo_ref, gm_ref, g_ref, sh_ref, sc_ref, gf_ref, wi_ref, wout_ref,
                  *out_refs, hidden, chunk, final_norm):
    gout_ref, y_ref = out_refs if final_norm else (None,) + out_refs
    nb, ts, d = x_ref.shape
    o = o_ref[...].reshape(nb * ts, -1)
    mix = jnp.dot(o, wo_ref[...], preferred_element_type=F32).reshape(nb, ts, d)
    x1 = x_ref[...] + gm_ref[...] * mix
    h = (_rms(x1) * g_ref[...]) * (1.0 + sc_ref[...]) + sh_ref[...]
    h = h.reshape(nb * ts, d).astype(BF16)
    acc = jnp.zeros((nb * ts, d), F32)
    for c in range(hidden // chunk):
        gate = jnp.dot(h, wi_ref[0, :, c * chunk:(c + 1) * chunk], preferred_element_type=F32)
        up = jnp.dot(h, wi_ref[0, :, hidden + c * chunk:hidden + (c + 1) * chunk],
                     preferred_element_type=F32)
        act = (gate * jax.nn.sigmoid(gate) * up).astype(BF16)
        acc = acc + jnp.dot(act, wout_ref[0, c * chunk:(c + 1) * chunk, :],
                            preferred_element_type=F32)
    x2 = x1 + gf_ref[...] * acc.reshape(nb, ts, d)
    y_ref[...] = _rms(x2) * gout_ref[...] if final_norm else x2


def _block(x, o, wo, gate_m, g_ffn, shift_f, scale_f, gate_f, w_in, w_out, layer, g_out=None):
    bx, sx, d = x.shape
    hidden = w_out.shape[1]
    nb, ts = _token_tiling(bx, sx)
    resident = lambda shape: pl.BlockSpec(shape, lambda b, s: (0,) * len(shape),
                                          pipeline_mode=pl.Buffered(1))
    of_layer = lambda w: pl.BlockSpec((1,) + w.shape[1:], lambda b, s: (layer, 0, 0),
                                      pipeline_mode=pl.Buffered(1))
    in_specs = [_tok_spec(nb, ts, d), _tok_spec(nb, ts, o.shape[-1]), resident(wo.shape),
                _mod_spec(nb, d), _const_spec((1, d)), _mod_spec(nb, d), _mod_spec(nb, d),
                _mod_spec(nb, d), of_layer(w_in), of_layer(w_out)]
    args = [x, o, wo, gate_m, g_ffn, shift_f, scale_f, gate_f, w_in, w_out]
    if g_out is not None:
        in_specs.append(_const_spec((1, d)))
        args.append(g_out)
    return pl.pallas_call(
        functools.partial(_block_kernel, hidden=hidden, chunk=256, final_norm=g_out is not None),
        grid=(bx // nb, sx // ts),
        in_specs=in_specs,
        out_specs=_tok_spec(nb, ts, d),
        out_shape=jax.ShapeDtypeStruct((bx, sx, d), F32),
        compiler_params=_cparams(2),
    )(*args)


def _stick_kernel(q_ref, k_ref, v_ref, tri_ref, o_ref, run_ref, acc_ref,
                  *, tq, bk, sk_valid, q_off, k_off):
    qpos0 = pl.program_id(2) * tq + q_off
    kb_hi = (jnp.minimum(qpos0 + (tq - 1) - k_off, sk_valid) + bk - 1) // bk
    f_hi = jnp.minimum(jnp.minimum(qpos0 - k_off, sk_valid) // bk, kb_hi)

    def visible(kb):
        row = lax.broadcasted_iota(jnp.int32, (2 * tq, 1), 0)
        qpos = jnp.where(row < tq, row, row - tq) + qpos0
        ik = kb * bk + lax.broadcasted_iota(jnp.int32, (1, bk), 1)
        return (ik + k_off < qpos) & (ik < sk_valid)

    lane_half = lax.broadcasted_iota(jnp.int32, (1, LANES), 1) // HEAD_DIM
    n_pairs = q_ref.shape[-1] // LANES
    lanes = [slice(pr * LANES, (pr + 1) * LANES) for pr in range(n_pairs)]
    q_both = [jnp.concatenate(
        [jnp.where(lane_half == hh, q_ref[0, :, lanes[pr]], jnp.zeros((), BF16))
         for hh in range(2)], axis=0) for pr in range(n_pairs)]

    def rows(kb):
        return pl.ds(pl.multiple_of(kb * bk, bk), bk)

    def block(pr, kb, masked, run):
        y = lax.dot_general(q_both[pr], k_ref[0, rows(kb), lanes[pr]], (((1,), (1,)), ((), ())),
                            preferred_element_type=F32)
        sp = jnp.maximum(y, 0.0) + jnp.log2(1.0 + jnp.exp2(-jnp.abs(y)))
        if masked:
            vis = visible(kb)
            sp = jnp.where(vis, sp, 0.0)
        suffix = jnp.dot(sp.astype(BF16), tri_ref[...], preferred_element_type=F32)
        a = jnp.exp2(y - suffix - run)
        if masked:
            a = jnp.where(vis, a, 0.0)
        return (jnp.dot(a.astype(BF16), v_ref[0, rows(kb), lanes[pr]],
                        preferred_element_type=F32),
                jnp.sum(sp, axis=1, keepdims=True))

    @pl.when(f_hi > 0)
    def _():
        for pr in range(n_pairs):
            out_d, sum_d = block(pr, kb_hi - 1, True, 0.0)
            out_f, sum_f = block(pr, f_hi - 1, False, sum_d)
            acc_ref[pr] = out_d + out_f
            run_ref[pr] = sum_d + sum_f

    @pl.when(f_hi == 0)
    def _():
        for pr in range(n_pairs):
            out_d, sum_d = block(pr, kb_hi - 1, True, 0.0)
            acc_ref[pr] = out_d
            run_ref[pr] = sum_d

    def earlier(st):
        for pr in range(n_pairs):
            out, row_sum = block(pr, f_hi - 1 - st[0], False, run_ref[pr])
            acc_ref[pr] += out
            run_ref[pr] += row_sum
        return st[0] + 1, jnp.min(run_ref[...])

    lax.while_loop(lambda st: (st[0] < f_hi) & (st[1] < STICK_DEAD_BITS), earlier,
                   (jnp.ones((), jnp.int32), jnp.min(run_ref[...])))
    for pr in range(n_pairs):
        o_ref[0, :, lanes[pr]] = jnp.where(
            lane_half == 0, acc_ref[pr, :tq], acc_ref[pr, tq:]).astype(o_ref.dtype)


def _stick_attention(q, k, v, *, sk_valid, q_off, k_off):
    bx, sq, qw = q.shape
    rows = k.shape[1]
    n_pairs = qw // LANES
    tq = min(Q_TILE, sq)
    bk = KV_BLOCK
    assert sq % tq == 0 and rows % bk == 0 and tq % 8 == 0 and q_off >= k_off >= 0
    for q0 in range(0, sq, tq):
        first, last = q_off + q0 - k_off, min(q_off + q0 + tq - 1 - k_off, sk_valid)
        assert first <= sk_valid and -(-last // bk) - first // bk == 1
    idx = np.arange(bk)
    pps = B_PAIRS_PER_STEP
    assert n_pairs % pps == 0
    return pl.pallas_call(
        functools.partial(_stick_kernel, tq=tq, bk=bk, sk_valid=sk_valid, q_off=q_off,
                          k_off=k_off),
        grid=(bx, n_pairs // pps, sq // tq),
        in_specs=[
            pl.BlockSpec((1, tq, pps * LANES), lambda b, p, i: (b, i, p)),
            pl.BlockSpec((1, rows, pps * LANES), lambda b, p, i: (b, 0, p)),
            pl.BlockSpec((1, rows, pps * LANES), lambda b, p, i: (b, 0, p)),
            pl.BlockSpec((bk, bk), lambda b, p, i: (0, 0)),
        ],
        out_specs=pl.BlockSpec((1, tq, pps * LANES), lambda b, p, i: (b, i, p)),
        out_shape=jax.ShapeDtypeStruct((bx, sq, n_pairs * LANES), BF16),
        scratch_shapes=[pltpu.VMEM((pps, 2 * tq, 1), F32), pltpu.VMEM((pps, 2 * tq, LANES), F32)],
        compiler_params=_cparams(3),
    )(q, k, v, jnp.asarray(idx[:, None] >= idx[None, :], BF16))


def _row_end(qpos, k_off):
    return (qpos // CHUNK + 1) * CHUNK - k_off


def _causal_kernel(q_ref, k_ref, v_ref, o_ref, *, tq, span, n_spans, sk_valid, q_off, k_off,
                   single_tile):
    qpos0 = pl.program_id(2) * tq + q_off
    n_needed = (jnp.minimum(_row_end(qpos0 + tq - 1, k_off), sk_valid) - 1) // span
    shift = int(math.log2(CHUNK))
    lane_half = lax.broadcasted_iota(jnp.int32, (1, LANES), 1) // HEAD_DIM
    nt = (((1,), (1,)), ((), ()))

    for n_full in range(n_spans):
        if single_tile and n_full != (min(_row_end(q_off + tq - 1, k_off), sk_valid) - 1) // span:
            continue

        @pl.when(n_needed == n_full)
        def _(n_full=n_full):
            full, width = n_full * span, (n_full + 1) * span
            qpos = lax.broadcasted_iota(jnp.int32, (tq, 1), 0) + qpos0
            kidx = lax.broadcasted_iota(jnp.int32, (1, span), 1) + full
            vis = (kidx < sk_valid) & (jnp.right_shift(kidx + k_off, shift)
                                       <= jnp.right_shift(qpos, shift))
            outs = []
            for hh in range(2 * (q_ref.shape[-1] // (2 * LANES))):
                lanes = slice(hh * LANES, (hh + 1) * LANES)
                v_lanes = slice(hh // 2 * LANES, (hh // 2 + 1) * LANES)
                q_h = q_ref[0, :, lanes]
                s_tail = lax.dot_general(q_h, k_ref[0, full:width, lanes], nt,
                                         preferred_element_type=F32)
                s_tail = jnp.where(vis, s_tail, MASK_VALUE)
                m = jnp.max(s_tail, axis=1, keepdims=True)
                if n_full:
                    s_full = lax.dot_general(q_h, k_ref[0, 0:full, lanes], nt,
                                             preferred_element_type=F32)
                    m = jnp.maximum(m, jnp.max(s_full, axis=1, keepdims=True))
                p_tail = jnp.exp2(s_tail - m)
                l = jnp.sum(p_tail, axis=1, keepdims=True)
                o = jnp.dot(p_tail.astype(BF16), v_ref[0, full:width, v_lanes],
                            preferred_element_type=F32)
                for c0 in range(0, full, A_PV_CHUNK):
                    p_c = jnp.exp2(s_full[:, c0:c0 + A_PV_CHUNK] - m)
                    l = l + jnp.sum(p_c, axis=1, keepdims=True)
                    o = o + jnp.dot(p_c.astype(BF16), v_ref[0, c0:c0 + A_PV_CHUNK, v_lanes],
                                    preferred_element_type=F32)
                outs.append(o / l)
            for pr in range(len(outs) // 2):
                o_ref[0, :, pr * LANES:(pr + 1) * LANES] = jnp.where(
                    lane_half == 0, outs[2 * pr], outs[2 * pr + 1]).astype(o_ref.dtype)


def _causal_attention(q, k, v, *, span, sk_valid, q_off, k_off):
    bx, sq, qw = q.shape
    rows = k.shape[1]
    pps = A_PAIRS_PER_STEP
    n_pairs = qw // (2 * LANES)
    assert n_pairs % pps == 0
    tq = min(A_Q_TILE, sq)
    assert sq % tq == 0 and rows % span == 0 and q_off % CHUNK == 0 and k_off >= 0
    for q0 in range(0, sq, tq):
        end = min(_row_end(q_off + q0 + tq - 1, k_off), sk_valid)
        first_row_end = min(_row_end(q_off + q0, k_off), sk_valid)
        assert 0 < end <= rows and (end - 1) // span * span <= first_row_end
    return pl.pallas_call(
        functools.partial(_causal_kernel, tq=tq, span=span, n_spans=rows // span,
                          sk_valid=sk_valid, q_off=q_off, k_off=k_off, single_tile=sq == tq),
        grid=(bx, n_pairs // pps, sq // tq),
        in_specs=[
            pl.BlockSpec((1, tq, pps * 2 * LANES), lambda b, p, i: (b, i, p)),
            pl.BlockSpec((1, rows, pps * 2 * LANES), lambda b, p, i: (b, 0, p)),
            pl.BlockSpec((1, rows, pps * LANES), lambda b, p, i: (b, 0, p)),
        ],
        out_specs=pl.BlockSpec((1, tq, pps * LANES), lambda b, p, i: (b, i, p)),
        out_shape=jax.ShapeDtypeStruct((bx, sq, n_pairs * LANES), BF16),
        compiler_params=_cparams(3),
    )(q, k, v)


def _band_kernel(*refs, tq, width, sk_valid, q_off, k_off, front, n_prev, use_sink, per_head_bias):
    refs = list(refs)
    q_ref, k_ref, v_ref = refs[:3]
    rest = refs[3:]
    sink_ref = rest.pop(0) if use_sink else None
    bias_ref, o_ref = rest

    group = q_ref.shape[-1] // LANES
    first_pair = pl.program_id(1) * group
    qpos0 = pl.program_id(2) * tq + q_off
    start = pl.multiple_of((qpos0 // CHUNK - n_prev) * CHUNK - k_off + front, CHUNK)
    band = pl.ds(start, width)
    kidx = lax.broadcasted_iota(jnp.int32, (1, width), 1) + (start - front)
    invalid = jnp.where((kidx >= 0) & (kidx < sk_valid), 0.0, MASK_VALUE)

    lane_half = lax.broadcasted_iota(jnp.int32, (1, LANES), 1) // HEAD_DIM
    shared_kv = k_ref.shape[-1] == LANES
    for j in range(group):
        q_pair = q_ref[0, :, j * LANES:(j + 1) * LANES]
        kv_lanes = slice(0, LANES) if shared_kv else slice(j * LANES, (j + 1) * LANES)
        k_band = k_ref[0, band, kv_lanes]
        v_band = v_ref[0, band, kv_lanes]
        outs = []
        for hh in range(2):
            q_h = jnp.where(lane_half == hh, q_pair, jnp.zeros((), BF16))
            s = lax.dot_general(q_h, k_band, (((1,), (1,)), ((), ())),
                                preferred_element_type=F32)
            s = s + (bias_ref[2 * j + hh] if per_head_bias else bias_ref[...]) + invalid
            m = jnp.max(s, axis=1, keepdims=True)
            if use_sink:
                sink = sink_ref[2 * (first_pair + j) + hh] * LOG2E
                m = jnp.maximum(m, sink)
            l_lanes = jnp.zeros((tq, LANES), F32)
            o = jnp.zeros((tq, LANES), F32)
            for c0 in range(0, width, LANES):
                p_c = jnp.exp2(s[:, c0:c0 + LANES] - m)
                l_lanes = l_lanes + p_c
                o = o + jnp.dot(p_c.astype(BF16), v_band[c0:c0 + LANES],
                                preferred_element_type=F32)
            l = jnp.sum(l_lanes, axis=1, keepdims=True)
            if use_sink:
                l = l + jnp.exp2(sink - m)
            outs.append(o / l)
        o_ref[0, :, j * LANES:(j + 1) * LANES] = jnp.where(
            lane_half == 0, outs[0], outs[1]).astype(o_ref.dtype)


def _band_mask_tile(rows, width, n_prev):
    r_chunk = np.arange(rows)[:, None] // CHUNK
    c_chunk = np.arange(width)[None, :] // CHUNK - n_prev
    return np.where((c_chunk <= r_chunk) & (c_chunk >= r_chunk - n_prev), 0.0, MASK_VALUE
                    ).astype(np.float32)


def _band_attention(q, k, v, *, width, sk_valid, q_off, k_off, front, n_prev, pairs_per_step,
                    shared_kv, sink=None, bias=None):
    kv_group = pairs_per_step
    kv_lanes = LANES if shared_kv else pairs_per_step * LANES
    bx, sq, qw = q.shape
    rows = k.shape[1]
    n_pairs = qw // LANES
    tq = min(Q_TILE, sq)
    assert sq % tq == 0 and q_off % CHUNK == 0 and (tq % CHUNK == 0 or sq == tq)
    assert ((q_off + sq - tq) // CHUNK - n_prev) * CHUNK - k_off + front + width <= rows
    assert (q_off // CHUNK - n_prev) * CHUNK - k_off + front >= 0
    in_specs = [
        pl.BlockSpec((1, tq, kv_group * LANES), lambda b, g, i: (b, i, g)),
        pl.BlockSpec((1, rows, kv_lanes), lambda b, g, i: (b, 0, g)),
        pl.BlockSpec((1, rows, kv_lanes), lambda b, g, i: (b, 0, g)),
    ]
    args = [q, k, v]
    if sink is not None:
        in_specs.append(pl.BlockSpec(memory_space=pltpu.SMEM))
        args.append(sink)
    per_head_bias = bias.ndim == 3
    if per_head_bias:
        in_specs.append(pl.BlockSpec((2 * kv_group, tq, width), lambda b, g, i: (g, 0, 0)))
    else:
        in_specs.append(pl.BlockSpec((tq, width), lambda b, g, i: (0, 0)))
    args.append(bias)
    return pl.pallas_call(
        functools.partial(_band_kernel, tq=tq, width=width, sk_valid=sk_valid, q_off=q_off,
                          k_off=k_off, front=front, n_prev=n_prev,
                          use_sink=sink is not None, per_head_bias=per_head_bias),
        grid=(bx, n_pairs // kv_group, sq // tq),
        in_specs=in_specs,
        out_specs=pl.BlockSpec((1, tq, kv_group * LANES), lambda b, g, i: (b, i, g)),
        out_shape=jax.ShapeDtypeStruct((bx, sq, n_pairs * LANES), BF16),
        compiler_params=_cparams(3),
    )(*args)


def _band_bias_kernel(e_ref, mask_ref, o_ref, *, width):
    w = e_ref.shape[-1]
    x = jnp.broadcast_to(e_ref[0], (Q_TILE, w))
    toeplitz = pltpu.roll(x, 0, 1, stride=1, stride_axis=0)
    o_ref[0] = toeplitz[:, :width] * LOG2E + mask_ref[...]


def _band_bias(rel_bias):
    heads = rel_bias.shape[0]
    clip = (rel_bias.shape[1] - 1) // 2
    band = D_PREV_CHUNKS * CHUNK
    width = D_BAND_WIDTH
    w = width + Q_TILE
    assert band >= clip and width > band + clip
    top = jnp.broadcast_to(rel_bias[:, -1:], (heads, band - clip + 1))
    mid = jnp.flip(rel_bias[:, :2 * clip], axis=1)
    low = jnp.broadcast_to(rel_bias[:, :1], (heads, width - (band + clip + 1)))
    neg = jnp.broadcast_to(rel_bias[:, -1:], (heads, w - width))
    e = jnp.concatenate([top, mid, low, neg], axis=1)[:, None, :]
    return pl.pallas_call(
        functools.partial(_band_bias_kernel, width=width),
        grid=(heads,),
        in_specs=[pl.BlockSpec((1, 1, w), lambda h: (h, 0, 0)),
                  pl.BlockSpec((Q_TILE, width), lambda h: (0, 0))],
        out_specs=pl.BlockSpec((1, Q_TILE, width), lambda h: (h, 0, 0)),
        out_shape=jax.ShapeDtypeStruct((heads, Q_TILE, width), F32),
        compiler_params=_cparams(1),
    )(e, jnp.asarray(_band_mask_tile(Q_TILE, width, D_PREV_CHUNKS)))


def _rope_tables(pos, n_rot, lane_offsets, rows_repeat=1):
    half = n_rot // 2
    inv = ROPE_THETA ** (-jnp.arange(half, dtype=F32) * 2.0 / n_rot)
    ang = pos.astype(F32)[:, None] * inv[None, :]
    cos, sin = jnp.cos(ang), jnp.sin(ang)
    n = pos.shape[0]
    cos_t = jnp.ones((n, LANES), F32)
    sin_up = jnp.zeros((n, LANES), F32)
    sin_dn = jnp.zeros((n, LANES), F32)
    for o in lane_offsets:
        cos_t = cos_t.at[:, o:o + half].set(cos).at[:, o + half:o + n_rot].set(cos)
        sin_dn = sin_dn.at[:, o:o + half].set(-sin)
        sin_up = sin_up.at[:, o + half:o + n_rot].set(sin)
    def tiles(t):
        if rows_repeat > 1:
            return jnp.tile(t, (rows_repeat, 1))[None]
        rows = min(n, TOKEN_TILE)
        return t.reshape(n // rows, rows, LANES)
    return tiles(cos_t), tiles(sin_up), tiles(sin_dn)


def _tables_for(bx, sx, pos, n_rot, lane_offsets):
    nb, ts = _token_tiling(bx, sx)
    return _rope_tables(pos, n_rot, lane_offsets, rows_repeat=nb if nb > 1 else 1)


def _c_head_order():
    rep = C_HEADS // C_KV_HEADS
    order = []
    for p in range(C_HEADS // 2):
        g2, i = divmod(p, rep)
        order += [rep * (2 * g2) + i, rep * (2 * g2 + 1) + i]
    return np.asarray(order)


def _pad_rows(a, rows):
    return jnp.pad(a, ((0, 0), (0, rows - a.shape[1]), (0, 0)))


def _front_pad(a, rows):
    return jnp.pad(a, ((0, 0), (rows, 0), (0, 0)))


def _with_cache(cache, new, dtype, block=KV_BLOCK):
    full = jnp.concatenate([cache.reshape(cache.shape[0], cache.shape[1], -1).astype(dtype),
                            new.astype(dtype)], axis=1)
    rows = -(-full.shape[1] // block) * block
    return _pad_rows(full, rows), full.shape[1]


def kernel(x_prompt, x_sample, c_prompt, c_sample, cache_a_ckv, cache_a_krope, cache_b_k, cache_b_v,
           cache_c_k, cache_c_v, cache_d_k, cache_d_v, w_mod, b_mod, g_mix, g_ffn, w_ffn_in, w_ffn_out,
           w_a_down, g_a_q, g_a_kv, w_a_uq, w_a_uk, w_a_uv, w_a_o, w_b_qkv, w_b_o,
           w_c_qkv, b_c_qkv, sink_c, w_c_o, w_d_qkv, rel_bias_d, w_d_o, g_final):
    bp, sp, d = x_prompt.shape
    bs, t, _ = x_sample.shape
    depth = w_mod.shape[0]
    past = cache_a_ckv.shape[2]
    pos_p = jnp.arange(sp)
    pos_s = past + jnp.arange(t)

    mods = _adaln(jnp.concatenate([c_prompt, c_sample], axis=0), w_mod, b_mod)
    w_in_all, w_out_all = _to_bf16(w_ffn_in), _to_bf16(w_ffn_out)

    def mod(i, k):
        m = mods[i, k][:, None, :]
        return m[:bp], m[bp:]

    xp, xs = x_prompt, x_sample
    states = [[] for _ in range(N_MIXERS)]
    for i in range(depth):
        m, j = i % N_MIXERS, i // N_MIXERS
        (sh_p, sh_s), (sc_p, sc_s), (gm_p, gm_s) = mod(i, 0), mod(i, 1), mod(i, 2)
        (shf_p, shf_s), (scf_p, scf_s), (gf_p, gf_s) = mod(i, 3), mod(i, 4), mod(i, 5)
        g_m = g_mix[i][None, :]
        if m == 0:
            n_down = A_Q_LORA + A_KV_LORA + A_ROPE
            wd = jnp.pad(w_a_down[j], ((0, 0), (0, A_Q_LORA + A_KV_LORA + LANES - n_down))).astype(BF16)
            wq = w_a_uq[j].reshape(A_Q_LORA, A_HEADS, A_NOPE + A_ROPE)
            wq = jnp.pad(wq, ((0, 0), (0, 0), (0, LANES - A_NOPE - A_ROPE)))
            x1 = wq[:, :, A_NOPE:A_NOPE + A_ROPE // 2]
            x2 = wq[:, :, A_NOPE + A_ROPE // 2:A_NOPE + A_ROPE]
            wq_rot = jnp.zeros_like(wq).at[:, :, A_NOPE:A_NOPE + A_ROPE].set(
                jnp.concatenate([-x2, x1], axis=-1))
            wq = wq.reshape(A_Q_LORA, A_HEADS * LANES).astype(BF16)
            wq_rot = wq_rot.reshape(A_Q_LORA, A_HEADS * LANES).astype(BF16)
            wk = jnp.pad(w_a_uk[j], ((0, 0), (0, 0), (0, LANES - A_NOPE)))
            wk = wk.reshape(A_KV_LORA, A_HEADS * LANES).astype(BF16)
            place = jnp.zeros((A_ROPE, A_HEADS, LANES), F32)
            place = place.at[jnp.arange(A_ROPE), :, A_NOPE + jnp.arange(A_ROPE)].set(1.0)
            wr = place.reshape(A_ROPE, A_HEADS * LANES).astype(BF16)
            wv = w_a_uv[j].reshape(A_KV_LORA, A_HEADS * A_V).astype(BF16)
            g_q, g_kv = g_a_q[j][None, :], g_a_kv[j][None, :]

            def project(x, sh, sc, pos):
                q_cos, q_up, q_dn = _tables_for(x.shape[0], x.shape[1], pos, A_ROPE, [A_NOPE])
                kt = _tables_for(x.shape[0], x.shape[1], pos, A_ROPE, [0])
                return _proj_a(x, g_m, sh, sc, wd, g_q, g_kv, wq, wq_rot, (q_cos, q_up - q_dn), kt)

            q_p, ckv_p, kr_p = project(xp, sh_p, sc_p, pos_p)
            q_s, ckv_s, kr_s = project(xs, sh_s, sc_s, pos_s)
            k_p, v_p = _expand_a(ckv_p, kr_p, wk, wr, wv)
            o_p = _causal_attention(q_p, k_p, v_p, span=min(A_KV_BLOCK, sp), sk_valid=sp,
                                    q_off=0, k_off=0)
            n_all = past + t
            rows = -(-n_all // LANES) * LANES
            kcat = jnp.concatenate(
                [jnp.concatenate([cache_a_ckv[j], ckv_s], axis=1),
                 jnp.concatenate([cache_a_krope[j], kr_s], axis=1)], axis=-1).astype(BF16)
            kcat = jnp.pad(kcat, ((0, 0), (0, rows - n_all),
                                  (0, A_KV_LORA + LANES - kcat.shape[-1])))
            wabs = jnp.zeros((A_HEADS, LANES, A_KV_LORA + LANES), F32)
            wabs = wabs.at[:, :A_NOPE, :A_KV_LORA].set(w_a_uk[j].transpose(1, 2, 0))
            wabs = wabs.at[:, A_NOPE + jnp.arange(A_ROPE), A_KV_LORA + jnp.arange(A_ROPE)].set(1.0)
            wv_placed = jnp.zeros((A_HEADS, A_KV_LORA, A_HEADS, A_V), F32)
            wv_placed = wv_placed.at[jnp.arange(A_HEADS), :, jnp.arange(A_HEADS), :].set(
                w_a_uv[j].transpose(1, 0, 2))
            wv_placed = wv_placed.reshape(A_HEADS, A_KV_LORA, A_HEADS * A_V)
            o_s = _latent_decode(q_s, kcat, wabs.astype(BF16), wv_placed.astype(BF16),
                                 sk_valid=n_all, q_off=past, k_off=0)
            wo = w_a_o[j].astype(BF16)
            states[0].append((ckv_p, kr_p, ckv_s, kr_s))
        elif m == 1 or m == 3:
            heads = B_HEADS if m == 1 else D_HEADS
            w_qkv = (w_b_qkv if m == 1 else w_d_qkv)[j].astype(BF16)
            cache_k, cache_v = (cache_b_k, cache_b_v) if m == 1 else (cache_d_k, cache_d_v)
            q_scale = HEAD_DIM ** -0.5 * LOG2E
            q_p, kf_p, vf_p, kb_p, vb_p = _proj_qkv(xp, g_m, sh_p, sc_p, w_qkv, heads, q_scale)
            q_s, kf_s, vf_s, kb_s, vb_s = _proj_qkv(xs, g_m, sh_s, sc_s, w_qkv, heads, q_scale)
            lc = cache_k.shape[2]
            if m == 1:
                o_p = _stick_attention(q_p, kb_p, vb_p, sk_valid=sp, q_off=0, k_off=0)
                k_s, n_all = _with_cache(cache_k[j], kb_s, BF16, KV_BLOCK)
                v_s, _ = _with_cache(cache_v[j], vb_s, BF16, KV_BLOCK)
                o_s = _stick_attention(q_s, k_s, v_s, sk_valid=n_all, q_off=past,
                                       k_off=past - lc)
                wo = w_b_o[j].astype(BF16)
                states[1].append((kf_p, vf_p, kf_s, vf_s))
            else:
                bias = _band_bias(rel_bias_d[j])
                k_s, n_all = _with_cache(cache_k[j], kb_s, BF16, D_BAND_WIDTH)
                v_s, _ = _with_cache(cache_v[j], vb_s, BF16, D_BAND_WIDTH)
                front = D_PREV_CHUNKS * CHUNK
                o_p = _band_attention(q_p, _front_pad(kb_p, front), _front_pad(vb_p, front),
                                      width=D_BAND_WIDTH, sk_valid=sp, q_off=0, k_off=0,
                                      front=front, n_prev=D_PREV_CHUNKS, bias=bias,
                                      pairs_per_step=D_PAIRS_PER_STEP, shared_kv=False)
                o_s = _band_attention(q_s, k_s, v_s, width=D_BAND_WIDTH, sk_valid=n_all,
                                      q_off=past, k_off=past - lc, front=0,
                                      n_prev=D_PREV_CHUNKS, bias=bias[:, :t, :],
                                      pairs_per_step=D_PAIRS_PER_STEP, shared_kv=False)
                wo = w_d_o[j].astype(BF16)
                keep = min(D_PREV_CHUNKS * CHUNK, sp)
                k_roll = jnp.concatenate([cache_k[j].reshape(bs, lc, -1), kf_s], axis=1)[:, t:]
                v_roll = jnp.concatenate([cache_v[j].reshape(bs, lc, -1), vf_s], axis=1)[:, t:]
                states[3].append((kf_p[:, sp - keep:], vf_p[:, sp - keep:], k_roll, v_roll))
        else:
            order = _c_head_order()
            q_width, kv_width = C_HEADS * HEAD_DIM, C_KV_HEADS * HEAD_DIM
            col = np.concatenate([(order[:, None] * HEAD_DIM + np.arange(HEAD_DIM)).reshape(-1),
                                  np.arange(q_width, q_width + 2 * kv_width)])
            w_qkv = w_c_qkv[j][:, col].astype(BF16)
            b_qkv = b_c_qkv[j][col][None, :]
            sink = sink_c[j][order]
            wo = w_c_o[j].reshape(C_HEADS, HEAD_DIM, d)[order].reshape(q_width, d).astype(BF16)
            lanes = [0, HEAD_DIM]

            def project(x, sh, sc, pos):
                tb = _tables_for(x.shape[0], x.shape[1], pos, C_ROT, lanes)
                return _proj_c(x, g_m, sh, sc, w_qkv, b_qkv, tb)

            q_p, kf_p, vf_p, kb_p, vb_p = project(xp, sh_p, sc_p, pos_p)
            q_s, kf_s, vf_s, kb_s, vb_s = project(xs, sh_s, sc_s, pos_s)
            lc = cache_c_k.shape[2]
            k_s, n_all = _with_cache(cache_c_k[j], kb_s, BF16, C_BAND_WIDTH)
            v_s, _ = _with_cache(cache_c_v[j], vb_s, BF16, C_BAND_WIDTH)
            group = C_HEADS // C_KV_HEADS
            front = C_PREV_CHUNKS * CHUNK
            band_mask = jnp.asarray(_band_mask_tile(Q_TILE, C_BAND_WIDTH, C_PREV_CHUNKS))
            o_p = _band_attention(q_p, _front_pad(kb_p, front), _front_pad(vb_p, front),
                                  width=C_BAND_WIDTH, sk_valid=sp, q_off=0, k_off=0, front=front,
                                  n_prev=C_PREV_CHUNKS, pairs_per_step=group, shared_kv=True, sink=sink,
                                  bias=band_mask)
            o_s = _band_attention(q_s, k_s, v_s, width=C_BAND_WIDTH, sk_valid=n_all, q_off=past,
                                  k_off=past - lc, front=0, n_prev=C_PREV_CHUNKS, pairs_per_step=group, shared_kv=True,
                                  sink=sink, bias=band_mask[:t])
            keep = min(C_PREV_CHUNKS * CHUNK, sp)
            k_roll = jnp.concatenate([cache_c_k[j].reshape(bs, lc, -1), kf_s], axis=1)[:, t:]
            v_roll = jnp.concatenate([cache_c_v[j].reshape(bs, lc, -1), vf_s], axis=1)[:, t:]
            states[2].append((kf_p[:, sp - keep:], vf_p[:, sp - keep:], k_roll, v_roll))

        g_f = g_ffn[i][None, :]
        g_out = g_final[None, :] if i == depth - 1 else None
        xp = _block(xp, o_p, wo, gm_p, g_f, shf_p, scf_p, gf_p, w_in_all, w_out_all, i, g_out)
        xs = _block(xs, o_s, wo, gm_s, g_f, shf_s, scf_s, gf_s, w_in_all, w_out_all, i, g_out)

    y_p, y_s = xp, xs

    def stacked(entries, n_heads=None):
        outs = []
        for parts in zip(*entries):
            a = jnp.stack(parts, axis=0)
            if n_heads is not None:
                a = a.reshape(a.shape[:3] + (n_heads, HEAD_DIM))
            outs.append(a)
        return tuple(outs)

    return ((y_p, y_s) + stacked(states[0]) + stacked(states[1], B_HEADS)
            + stacked(states[2], C_KV_HEADS) + stacked(states[3], D_HEADS))
```

```python
import functools
import math

import numpy as np
import jax
import jax.numpy as jnp
from jax import lax
from jax.experimental import pallas as pl
from jax.experimental.pallas import tpu as pltpu

F32 = jnp.float32
BF16 = jnp.bfloat16

CHUNK = 64
HEAD_DIM = 64
ROPE_THETA = 500000.0
NORM_EPS = 1e-6
N_MIXERS = 4
A_HEADS, A_Q_LORA, A_KV_LORA, A_NOPE, A_ROPE, A_V = 16, 384, 256, 64, 32, 64
B_HEADS = 16
C_HEADS, C_KV_HEADS, C_WINDOW, C_ROT = 16, 4, 128, 16
D_HEADS, D_PREV_CHUNKS, D_REL_CLIP = 16, 8, 128
C_PREV_CHUNKS = C_WINDOW // CHUNK

LANES = 128
V7X_VMEM_LIMIT = 56 * 1024 * 1024

TOKEN_TILE = 512
Q_TILE = 256
KV_BLOCK = 256
C_BAND_WIDTH = C_PREV_CHUNKS * CHUNK + Q_TILE
D_BAND_WIDTH = D_PREV_CHUNKS * CHUNK + Q_TILE
A_Q_TILE = 256
A_KV_BLOCK = 512
A_PV_CHUNK = 512
A_PAIRS_PER_STEP = 2
B_PAIRS_PER_STEP = 4
D_PAIRS_PER_STEP = 4
MASK_VALUE = -1e30
LOG2E = math.log2(math.e)
STICK_DEAD_BITS = 150.0


def _cparams(n_axes):
    return pltpu.CompilerParams(
        dimension_semantics=("parallel",) * n_axes, vmem_limit_bytes=V7X_VMEM_LIMIT)


def _adaln_kernel(c_ref, w_ref, b_ref, o_ref):
    c = c_ref[...]
    a = (c * jax.nn.sigmoid(c)).astype(BF16)
    y = jnp.dot(a, w_ref[0].astype(BF16), preferred_element_type=F32) + b_ref[0]
    o_ref[0, 0] = y


def _adaln(c_all, w_mod, b_mod):
    depth, d, d6 = w_mod.shape
    n = c_all.shape[0]
    return pl.pallas_call(
        _adaln_kernel,
        grid=(depth, d6 // d),
        in_specs=[
            pl.BlockSpec((n, d), lambda i, k: (0, 0)),
            pl.BlockSpec((1, d, d), lambda i, k: (i, 0, k)),
            pl.BlockSpec((1, 1, d), lambda i, k: (i, 0, k)),
        ],
        out_specs=pl.BlockSpec((1, 1, n, d), lambda i, k: (i, k, 0, 0)),
        out_shape=jax.ShapeDtypeStruct((depth, d6 // d, n, d), F32),
        compiler_params=_cparams(2),
    )(c_all, w_mod, b_mod.reshape(depth, 1, d6))


def _cast_kernel(x_ref, o_ref):
    o_ref[...] = x_ref[...].astype(o_ref.dtype)


def _to_bf16(w):
    n, rows, cols = w.shape
    slab = 256
    assert rows % slab == 0
    return pl.pallas_call(
        _cast_kernel,
        grid=(n, rows // slab),
        in_specs=[pl.BlockSpec((1, slab, cols), lambda i, r: (i, r, 0))],
        out_specs=pl.BlockSpec((1, slab, cols), lambda i, r: (i, r, 0)),
        out_shape=jax.ShapeDtypeStruct(w.shape, BF16),
        compiler_params=_cparams(2),
    )(w)


def _token_tiling(bx, sx):
    if sx >= TOKEN_TILE:
        ts = TOKEN_TILE
        while sx % ts:
            ts //= 2
        return 1, ts
    nb = min(bx, TOKEN_TILE // sx)
    while bx % nb:
        nb -= 1
    return nb, sx


def _tok_spec(nb, ts, width):
    return pl.BlockSpec((nb, ts, width), lambda b, s: (b, s, 0))


def _mod_spec(nb, d):
    return pl.BlockSpec((nb, 1, d), lambda b, s: (b, 0, 0))


def _const_spec(shape):
    nd = len(shape)
    return pl.BlockSpec(shape, lambda b, s: (0,) * nd)


def _table_spec(rows):
    return pl.BlockSpec((1, rows, LANES), lambda b, s: (s, 0, 0))


def _rms(x):
    return x * lax.rsqrt(jnp.mean(x * x, axis=-1, keepdims=True) + NORM_EPS)


def _modulated(x_ref, g_ref, shift_ref, scale_ref):
    y = _rms(x_ref[...]) * g_ref[...]
    h = y * (1.0 + scale_ref[...]) + shift_ref[...]
    return h.reshape(-1, h.shape[-1]).astype(BF16)


def _rope_lanes(x, cos_t, sin_up, sin_dn, half):
    return (x * cos_t + pltpu.roll(x, half, 1) * sin_up
            + pltpu.roll(x, LANES - half, 1) * sin_dn)


def _store_tok(ref, lo, val):
    nb, ts = ref.shape[0], ref.shape[1]
    w = val.shape[-1]
    ref[:, :, lo:lo + w] = val.reshape(nb, ts, w).astype(ref.dtype)


def _proj_qkv_kernel(x_ref, g_ref, sh_ref, sc_ref, w_ref, q_ref, kf_ref, vf_ref, kb_ref, vb_ref,
                     *, width, q_scale, n_tiles, front_pad):
    def project():
        h = _modulated(x_ref, g_ref, sh_ref, sc_ref)
        q = jnp.dot(h, w_ref[:, 0:width], preferred_element_type=F32)
        _store_tok(q_ref, 0, q * q_scale)
        k = jnp.dot(h, w_ref[:, width:2 * width], preferred_element_type=F32)
        _store_tok(kf_ref, 0, k)
        _store_tok(kb_ref, 0, k)
        v = jnp.dot(h, w_ref[:, 2 * width:3 * width], preferred_element_type=F32)
        _store_tok(vf_ref, 0, v)
        _store_tok(vb_ref, 0, v)

    if not front_pad:
        project()
        return
    pl.when(pl.program_id(1) < n_tiles)(project)

    @pl.when(pl.program_id(1) == n_tiles)
    def _():
        kb_ref[...] = jnp.zeros_like(kb_ref)
        vb_ref[...] = jnp.zeros_like(vb_ref)


def _proj_qkv(x, g, shift, scale, w_bf16, n_heads, q_scale, front_pad=False):
    bx, sx, d = x.shape
    width = n_heads * HEAD_DIM
    nb, ts = _token_tiling(bx, sx)
    n_tiles = sx // ts
    out = lambda rows, dt: jax.ShapeDtypeStruct((bx, rows, width), dt)
    if front_pad:
        assert nb == 1
        tile = lambda b, s: (b, jnp.minimum(s, n_tiles - 1), 0)
        shifted = lambda b, s: (b, (s + 1) % (n_tiles + 1), 0)
        x_spec = pl.BlockSpec((nb, ts, d), tile)
        out_specs = [pl.BlockSpec((nb, ts, width), tile)] * 3 + [pl.BlockSpec((nb, ts, width), shifted)] * 2
        semantics = ("parallel", "arbitrary")
    else:
        x_spec = _tok_spec(nb, ts, d)
        out_specs = [_tok_spec(nb, ts, width)] * 5
        semantics = ("parallel", "parallel")
    kv_rows = sx + ts if front_pad else sx
    return pl.pallas_call(
        functools.partial(_proj_qkv_kernel, width=width, q_scale=q_scale, n_tiles=n_tiles,
                          front_pad=front_pad),
        grid=(bx // nb, n_tiles + (1 if front_pad else 0)),
        in_specs=[x_spec, _const_spec((1, d)), _mod_spec(nb, d), _mod_spec(nb, d),
                  _const_spec(w_bf16.shape)],
        out_specs=out_specs,
        out_shape=[out(sx, BF16), out(sx, F32), out(sx, F32), out(kv_rows, BF16), out(kv_rows, BF16)],
        compiler_params=pltpu.CompilerParams(dimension_semantics=semantics,
                                             vmem_limit_bytes=V7X_VMEM_LIMIT),
    )(x, g, shift, scale, w_bf16)


def _proj_c_kernel(x_ref, g_ref, sh_ref, sc_ref, w_ref, b_ref, tc_ref, tu_ref, td_ref,
                   q_ref, kf_ref, vf_ref, kb_ref, vb_ref, *, q_width, kv_width, q_scale):
    h = _modulated(x_ref, g_ref, sh_ref, sc_ref)
    qkv = jnp.dot(h, w_ref[...], preferred_element_type=F32) + b_ref[...]
    cos_t, sin_up, sin_dn = tc_ref[0], tu_ref[0], td_ref[0]
    half = C_ROT // 2
    for j in range(q_width // LANES):
        xg = qkv[:, j * LANES:(j + 1) * LANES]
        _store_tok(q_ref, j * LANES, _rope_lanes(xg, cos_t, sin_up, sin_dn, half) * q_scale)
    for j in range(kv_width // LANES):
        lo = q_width + j * LANES
        kg = _rope_lanes(qkv[:, lo:lo + LANES], cos_t, sin_up, sin_dn, half)
        _store_tok(kf_ref, j * LANES, kg)
        _store_tok(kb_ref, j * LANES, kg)
    v = qkv[:, q_width + kv_width:q_width + 2 * kv_width]
    _store_tok(vf_ref, 0, v)
    _store_tok(vb_ref, 0, v)


def _proj_c(x, g, shift, scale, w_bf16, bias, tables):
    bx, sx, d = x.shape
    q_width, kv_width = C_HEADS * HEAD_DIM, C_KV_HEADS * HEAD_DIM
    nb, ts = _token_tiling(bx, sx)
    out = lambda w, dt: jax.ShapeDtypeStruct((bx, sx, w), dt)
    return pl.pallas_call(
        functools.partial(_proj_c_kernel, q_width=q_width, kv_width=kv_width,
                          q_scale=HEAD_DIM ** -0.5 * LOG2E),
        grid=(bx // nb, sx // ts),
        in_specs=[_tok_spec(nb, ts, d), _const_spec((1, d)), _mod_spec(nb, d), _mod_spec(nb, d),
                  _const_spec(w_bf16.shape), _const_spec(bias.shape)] + [_table_spec(nb * ts)] * 3,
        out_specs=[_tok_spec(nb, ts, q_width)] + [_tok_spec(nb, ts, kv_width)] * 4,
        out_shape=[out(q_width, BF16), out(kv_width, F32), out(kv_width, F32),
                   out(kv_width, BF16), out(kv_width, BF16)],
        compiler_params=_cparams(2),
    )(x, g, shift, scale, w_bf16, bias, *tables)


def _proj_a_kernel(x_ref, g_ref, sh_ref, sc_ref, wd_ref, gq_ref, gkv_ref, wq_ref, wqr_ref,
                   qc_ref, qs_ref, kc_ref, ku_ref, kd_ref,
                   q_ref, ckv_ref, kr_ref, *, q_scale):
    h = _modulated(x_ref, g_ref, sh_ref, sc_ref)
    down = jnp.dot(h, wd_ref[...], preferred_element_type=F32)
    cq = (_rms(down[:, :A_Q_LORA]) * gq_ref[...]).astype(BF16)
    ckv = _rms(down[:, A_Q_LORA:A_Q_LORA + A_KV_LORA]) * gkv_ref[...]
    _store_tok(ckv_ref, 0, ckv)
    lo = A_Q_LORA + A_KV_LORA
    half = A_ROPE // 2
    kr = _rope_lanes(down[:, lo:lo + LANES], kc_ref[0], ku_ref[0], kd_ref[0], half)
    _store_tok(kr_ref, 0, kr[:, :A_ROPE])
    q = jnp.dot(cq, wq_ref[...], preferred_element_type=F32)
    q_rot = jnp.dot(cq, wqr_ref[...], preferred_element_type=F32)
    qc, qs = qc_ref[0], qs_ref[0]
    for j in range(A_HEADS):
        lanes = slice(j * LANES, (j + 1) * LANES)
        _store_tok(q_ref, j * LANES, (q[:, lanes] * qc + q_rot[:, lanes] * qs) * q_scale)


def _proj_a(x, g, shift, scale, wd_bf16, g_q, g_kv, wq_bf16, wq_rot_bf16, q_tables, k_tables):
    bx, sx, d = x.shape
    nb, ts = _token_tiling(bx, sx)
    return pl.pallas_call(
        functools.partial(_proj_a_kernel, q_scale=(A_NOPE + A_ROPE) ** -0.5 * LOG2E),
        grid=(bx // nb, sx // ts),
        in_specs=[_tok_spec(nb, ts, d), _const_spec((1, d)), _mod_spec(nb, d), _mod_spec(nb, d),
                  _const_spec(wd_bf16.shape), _const_spec(g_q.shape), _const_spec(g_kv.shape),
                  _const_spec(wq_bf16.shape), _const_spec(wq_rot_bf16.shape)]
                 + [_table_spec(nb * ts)] * 5,
        out_specs=[_tok_spec(nb, ts, A_HEADS * LANES), _tok_spec(nb, ts, A_KV_LORA),
                   _tok_spec(nb, ts, A_ROPE)],
        out_shape=[jax.ShapeDtypeStruct((bx, sx, A_HEADS * LANES), BF16),
                   jax.ShapeDtypeStruct((bx, sx, A_KV_LORA), F32),
                   jax.ShapeDtypeStruct((bx, sx, A_ROPE), F32)],
        compiler_params=_cparams(2),
    )(x, g, shift, scale, wd_bf16, g_q, g_kv, wq_bf16, wq_rot_bf16, *q_tables, *k_tables)


def _expand_a_kernel(ckv_ref, kr_ref, wk_ref, wr_ref, wv_ref, k_ref, v_ref):
    ckv = ckv_ref[...]
    ckv = ckv.reshape(-1, ckv.shape[-1]).astype(BF16)
    kr = kr_ref[...]
    kr = kr.reshape(-1, kr.shape[-1]).astype(BF16)
    k = (jnp.dot(ckv, wk_ref[...], preferred_element_type=F32)
         + jnp.dot(kr, wr_ref[...], preferred_element_type=F32))
    _store_tok(k_ref, 0, k)
    _store_tok(v_ref, 0, jnp.dot(ckv, wv_ref[...], preferred_element_type=F32))


def _expand_a(ckv, kr, wk, wr, wv):
    bx, sx, _ = ckv.shape
    nb, ts = _token_tiling(bx, sx)
    return pl.pallas_call(
        _expand_a_kernel,
        grid=(bx // nb, sx // ts),
        in_specs=[_tok_spec(nb, ts, A_KV_LORA), _tok_spec(nb, ts, A_ROPE),
                  _const_spec(wk.shape), _const_spec(wr.shape), _const_spec(wv.shape)],
        out_specs=[_tok_spec(nb, ts, A_HEADS * LANES), _tok_spec(nb, ts, A_HEADS * A_V)],
        out_shape=[jax.ShapeDtypeStruct((bx, sx, A_HEADS * LANES), BF16),
                   jax.ShapeDtypeStruct((bx, sx, A_HEADS * A_V), BF16)],
        compiler_params=_cparams(2),
    )(ckv, kr, wk, wr, wv)


def _latent_decode_kernel(q_ref, kcat_ref, wabs_ref, wv_ref, o_ref, qc_ref,
                          *, sk_valid, q_off, k_off):
    t = q_ref.shape[1]
    n_heads = wabs_ref.shape[0]
    rows = kcat_ref.shape[1]
    for h in range(n_heads):
        qc_ref[h * t:(h + 1) * t, :] = jnp.dot(
            q_ref[0, :, h * LANES:(h + 1) * LANES], wabs_ref[h],
            preferred_element_type=F32).astype(BF16)
    kcat = kcat_ref[0]
    s = lax.dot_general(qc_ref[...], kcat, (((1,), (1,)), ((), ())),
                        preferred_element_type=F32)
    shift = int(math.log2(CHUNK))
    row = lax.broadcasted_iota(jnp.int32, (n_heads * t, 1), 0)
    qpos = row - row // t * t + q_off
    kidx = lax.broadcasted_iota(jnp.int32, (1, rows), 1)
    vis = (kidx < sk_valid) & (jnp.right_shift(kidx + k_off, shift) <= jnp.right_shift(qpos, shift))
    s = jnp.where(vis, s, MASK_VALUE)
    m = jnp.max(s, axis=1, keepdims=True)
    p = jnp.exp2(s - m)
    l = jnp.sum(p, axis=1, keepdims=True)
    o_lat = (jnp.dot(p.astype(BF16), kcat[:, :A_KV_LORA], preferred_element_type=F32) / l
             ).astype(BF16)
    out = jnp.zeros((t, o_ref.shape[-1]), F32)
    for h in range(n_heads):
        out = out + jnp.dot(o_lat[h * t:(h + 1) * t], wv_ref[h], preferred_element_type=F32)
    o_ref[0] = out.astype(o_ref.dtype)


def _latent_decode(q, kcat, wabs, wv_placed, *, sk_valid, q_off, k_off):
    bx, t, _ = q.shape
    rows, width = kcat.shape[1:]
    n_heads = wabs.shape[0]
    out_w = wv_placed.shape[-1]
    return pl.pallas_call(
        functools.partial(_latent_decode_kernel, sk_valid=sk_valid, q_off=q_off, k_off=k_off),
        grid=(bx,),
        in_specs=[pl.BlockSpec((1, t, n_heads * LANES), lambda b: (b, 0, 0)),
                  pl.BlockSpec((1, rows, width), lambda b: (b, 0, 0)),
                  pl.BlockSpec(wabs.shape, lambda b: (0, 0, 0)),
                  pl.BlockSpec(wv_placed.shape, lambda b: (0, 0, 0))],
        out_specs=pl.BlockSpec((1, t, out_w), lambda b: (b, 0, 0)),
        out_shape=jax.ShapeDtypeStruct((bx, t, out_w), BF16),
        scratch_shapes=[pltpu.VMEM((n_heads * t, width), BF16)],
        compiler_params=_cparams(1),
    )(q, kcat, wabs, wv_placed)


def _block_kernel(x_ref, o_ref, wo_ref, gm_ref, g_ref, sh_ref, sc_ref, gf_ref, wi_ref, wout_ref,
                  *out_refs, hidden, chunk, final_norm):
    gout_ref, y_ref = out_refs if final_norm else (None,) + out_refs
    nb, ts, d = x_ref.shape
    o = o_ref[...].reshape(nb * ts, -1)
    mix = jnp.dot(o, wo_ref[...], preferred_element_type=F32).reshape(nb, ts, d)
    x1 = x_ref[...] + gm_ref[...] * mix
    h = (_rms(x1) * g_ref[...]) * (1.0 + sc_ref[...]) + sh_ref[...]
    h = h.reshape(nb * ts, d).astype(BF16)
    acc = jnp.zeros((nb * ts, d), F32)
    for c in range(hidden // chunk):
        gate = jnp.dot(h, wi_ref[0, :, c * chunk:(c + 1) * chunk], preferred_element_type=F32)
        up = jnp.dot(h, wi_ref[0, :, hidden + c * chunk:hidden + (c + 1) * chunk],
                     preferred_element_type=F32)
        act = (gate * jax.nn.sigmoid(gate) * up).astype(BF16)
        acc = acc + jnp.dot(act, wout_ref[0, c * chunk:(c + 1) * chunk, :],
                            preferred_element_type=F32)
    x2 = x1 + gf_ref[...] * acc.reshape(nb, ts, d)
    y_ref[...] = _rms(x2) * gout_ref[...] if final_norm else x2


def _block(x, o, wo, gate_m, g_ffn, shift_f, scale_f, gate_f, w_in, w_out, layer, g_out=None):
    bx, sx, d = x.shape
    hidden = w_out.shape[1]
    nb, ts = _token_tiling(bx, sx)
    resident = lambda shape: pl.BlockSpec(shape, lambda b, s: (0,) * len(shape),
                                          pipeline_mode=pl.Buffered(1))
    of_layer = lambda w: pl.BlockSpec((1,) + w.shape[1:], lambda b, s: (layer, 0, 0),
                                      pipeline_mode=pl.Buffered(1))
    in_specs = [_tok_spec(nb, ts, d), _tok_spec(nb, ts, o.shape[-1]), resident(wo.shape),
                _mod_spec(nb, d), _const_spec((1, d)), _mod_spec(nb, d), _mod_spec(nb, d),
                _mod_spec(nb, d), of_layer(w_in), of_layer(w_out)]
    args = [x, o, wo, gate_m, g_ffn, shift_f, scale_f, gate_f, w_in, w_out]
    if g_out is not None:
        in_specs.append(_const_spec((1, d)))
        args.append(g_out)
    return pl.pallas_call(
        functools.partial(_block_kernel, hidden=hidden, chunk=256, final_norm=g_out is not None),
        grid=(bx // nb, sx // ts),
        in_specs=in_specs,
        out_specs=_tok_spec(nb, ts, d),
        out_shape=jax.ShapeDtypeStruct((bx, sx, d), F32),
        compiler_params=_cparams(2),
    )(*args)


def _stick_kernel(q_ref, k_ref, v_ref, tri_ref, o_ref, run_ref, acc_ref,
                  *, tq, bk, sk_valid, q_off, k_off):
    qpos0 = pl.program_id(2) * tq + q_off
    kb_hi = (jnp.minimum(qpos0 + (tq - 1) - k_off, sk_valid) + bk - 1) // bk
    f_hi = jnp.minimum(jnp.minimum(qpos0 - k_off, sk_valid) // bk, kb_hi)

    def visible(kb):
        row = lax.broadcasted_iota(jnp.int32, (2 * tq, 1), 0)
        qpos = jnp.where(row < tq, row, row - tq) + qpos0
        ik = kb * bk + lax.broadcasted_iota(jnp.int32, (1, bk), 1)
        return (ik + k_off < qpos) & (ik < sk_valid)

    lane_half = lax.broadcasted_iota(jnp.int32, (1, LANES), 1) // HEAD_DIM
    n_pairs = q_ref.shape[-1] // LANES
    lanes = [slice(pr * LANES, (pr + 1) * LANES) for pr in range(n_pairs)]
    q_both = [jnp.concatenate(
        [jnp.where(lane_half == hh, q_ref[0, :, lanes[pr]], jnp.zeros((), BF16))
         for hh in range(2)], axis=0) for pr in range(n_pairs)]

    def rows(kb):
        return pl.ds(pl.multiple_of(kb * bk, bk), bk)

    def block(pr, kb, masked, run):
        y = lax.dot_general(q_both[pr], k_ref[0, rows(kb), lanes[pr]], (((1,), (1,)), ((), ())),
                            preferred_element_type=F32)
        sp = jnp.maximum(y, 0.0) + jnp.log2(1.0 + jnp.exp2(-jnp.abs(y)))
        if masked:
            vis = visible(kb)
            sp = jnp.where(vis, sp, 0.0)
        suffix = jnp.dot(sp.astype(BF16), tri_ref[...], preferred_element_type=F32)
        a = jnp.exp2(y - suffix - run)
        if masked:
            a = jnp.where(vis, a, 0.0)
        return (jnp.dot(a.astype(BF16), v_ref[0, rows(kb), lanes[pr]],
                        preferred_element_type=F32),
                jnp.sum(sp, axis=1, keepdims=True))

    @pl.when(f_hi > 0)
    def _():
        for pr in range(n_pairs):
            out_d, sum_d = block(pr, kb_hi - 1, True, 0.0)
            out_f, sum_f = block(pr, f_hi - 1, False, sum_d)
            acc_ref[pr] = out_d + out_f
            run_ref[pr] = sum_d + sum_f

    @pl.when(f_hi == 0)
    def _():
        for pr in range(n_pairs):
            out_d, sum_d = block(pr, kb_hi - 1, True, 0.0)
            acc_ref[pr] = out_d
            run_ref[pr] = sum_d

    def earlier(st):
        for pr in range(n_pairs):
            out, row_sum = block(pr, f_hi - 1 - st[0], False, run_ref[pr])
            acc_ref[pr] += out
            run_ref[pr] += row_sum
        return st[0] + 1, jnp.min(run_ref[...])

    lax.while_loop(lambda st: (st[0] < f_hi) & (st[1] < STICK_DEAD_BITS), earlier,
                   (jnp.ones((), jnp.int32), jnp.min(run_ref[...])))
    for pr in range(n_pairs):
        o_ref[0, :, lanes[pr]] = jnp.where(
            lane_half == 0, acc_ref[pr, :tq], acc_ref[pr, tq:]).astype(o_ref.dtype)


def _stick_attention(q, k, v, *, sk_valid, q_off, k_off):
    bx, sq, qw = q.shape
    rows = k.shape[1]
    n_pairs = qw // LANES
    tq = min(Q_TILE, sq)
    bk = KV_BLOCK
    assert sq % tq == 0 and rows % bk == 0 and tq % 8 == 0 and q_off >= k_off >= 0
    for q0 in range(0, sq, tq):
        first, last = q_off + q0 - k_off, min(q_off + q0 + tq - 1 - k_off, sk_valid)
        assert first <= sk_valid and -(-last // bk) - first // bk == 1
    idx = np.arange(bk)
    pps = B_PAIRS_PER_STEP
    assert n_pairs % pps == 0
    return pl.pallas_call(
        functools.partial(_stick_kernel, tq=tq, bk=bk, sk_valid=sk_valid, q_off=q_off,
                          k_off=k_off),
        grid=(bx, n_pairs // pps, sq // tq),
        in_specs=[
            pl.BlockSpec((1, tq, pps * LANES), lambda b, p, i: (b, i, p)),
            pl.BlockSpec((1, rows, pps * LANES), lambda b, p, i: (b, 0, p)),
            pl.BlockSpec((1, rows, pps * LANES), lambda b, p, i: (b, 0, p)),
            pl.BlockSpec((bk, bk), lambda b, p, i: (0, 0)),
        ],
        out_specs=pl.BlockSpec((1, tq, pps * LANES), lambda b, p, i: (b, i, p)),
        out_shape=jax.ShapeDtypeStruct((bx, sq, n_pairs * LANES), BF16),
        scratch_shapes=[pltpu.VMEM((pps, 2 * tq, 1), F32), pltpu.VMEM((pps, 2 * tq, LANES), F32)],
        compiler_params=_cparams(3),
    )(q, k, v, jnp.asarray(idx[:, None] >= idx[None, :], BF16))


def _row_end(qpos, k_off):
    return (qpos // CHUNK + 1) * CHUNK - k_off


def _causal_kernel(q_ref, k_ref, v_ref, o_ref, *, tq, span, n_spans, sk_valid, q_off, k_off,
                   single_tile):
    qpos0 = pl.program_id(2) * tq + q_off
    n_needed = (jnp.minimum(_row_end(qpos0 + tq - 1, k_off), sk_valid) - 1) // span
    shift = int(math.log2(CHUNK))
    lane_half = lax.broadcasted_iota(jnp.int32, (1, LANES), 1) // HEAD_DIM
    nt = (((1,), (1,)), ((), ()))

    for n_full in range(n_spans):
        if single_tile and n_full != (min(_row_end(q_off + tq - 1, k_off), sk_valid) - 1) // span:
            continue

        @pl.when(n_needed == n_full)
        def _(n_full=n_full):
            full, width = n_full * span, (n_full + 1) * span
            qpos = lax.broadcasted_iota(jnp.int32, (tq, 1), 0) + qpos0
            kidx = lax.broadcasted_iota(jnp.int32, (1, span), 1) + full
            vis = (kidx < sk_valid) & (jnp.right_shift(kidx + k_off, shift)
                                       <= jnp.right_shift(qpos, shift))
            outs = []
            for hh in range(2 * (q_ref.shape[-1] // (2 * LANES))):
                lanes = slice(hh * LANES, (hh + 1) * LANES)
                v_lanes = slice(hh // 2 * LANES, (hh // 2 + 1) * LANES)
                q_h = q_ref[0, :, lanes]
                s_tail = lax.dot_general(q_h, k_ref[0, full:width, lanes], nt,
                                         preferred_element_type=F32)
                s_tail = jnp.where(vis, s_tail, MASK_VALUE)
                m = jnp.max(s_tail, axis=1, keepdims=True)
                if n_full:
                    s_full = lax.dot_general(q_h, k_ref[0, 0:full, lanes], nt,
                                             preferred_element_type=F32)
                    m = jnp.maximum(m, jnp.max(s_full, axis=1, keepdims=True))
                p_tail = jnp.exp2(s_tail - m)
                l = jnp.sum(p_tail, axis=1, keepdims=True)
                o = jnp.dot(p_tail.astype(BF16), v_ref[0, full:width, v_lanes],
                            preferred_element_type=F32)
                for c0 in range(0, full, A_PV_CHUNK):
                    p_c = jnp.exp2(s_full[:, c0:c0 + A_PV_CHUNK] - m)
                    l = l + jnp.sum(p_c, axis=1, keepdims=True)
                    o = o + jnp.dot(p_c.astype(BF16), v_ref[0, c0:c0 + A_PV_CHUNK, v_lanes],
                                    preferred_element_type=F32)
                outs.append(o / l)
            for pr in range(len(outs) // 2):
                o_ref[0, :, pr * LANES:(pr + 1) * LANES] = jnp.where(
                    lane_half == 0, outs[2 * pr], outs[2 * pr + 1]).astype(o_ref.dtype)


def _causal_attention(q, k, v, *, span, sk_valid, q_off, k_off):
    bx, sq, qw = q.shape
    rows = k.shape[1]
    pps = A_PAIRS_PER_STEP
    n_pairs = qw // (2 * LANES)
    assert n_pairs % pps == 0
    tq = min(A_Q_TILE, sq)
    assert sq % tq == 0 and rows % span == 0 and q_off % CHUNK == 0 and k_off >= 0
    for q0 in range(0, sq, tq):
        end = min(_row_end(q_off + q0 + tq - 1, k_off), sk_valid)
        first_row_end = min(_row_end(q_off + q0, k_off), sk_valid)
        assert 0 < end <= rows and (end - 1) // span * span <= first_row_end
    return pl.pallas_call(
        functools.partial(_causal_kernel, tq=tq, span=span, n_spans=rows // span,
                          sk_valid=sk_valid, q_off=q_off, k_off=k_off, single_tile=sq == tq),
        grid=(bx, n_pairs // pps, sq // tq),
        in_specs=[
            pl.BlockSpec((1, tq, pps * 2 * LANES), lambda b, p, i: (b, i, p)),
            pl.BlockSpec((1, rows, pps * 2 * LANES), lambda b, p, i: (b, 0, p)),
            pl.BlockSpec((1, rows, pps * LANES), lambda b, p, i: (b, 0, p)),
        ],
        out_specs=pl.BlockSpec((1, tq, pps * LANES), lambda b, p, i: (b, i, p)),
        out_shape=jax.ShapeDtypeStruct((bx, sq, n_pairs * LANES), BF16),
        compiler_params=_cparams(3),
    )(q, k, v)


def _band_kernel(*refs, tq, width, sk_valid, q_off, k_off, front, n_prev, use_sink, per_head_bias):
    refs = list(refs)
    q_ref, k_ref, v_ref = refs[:3]
    rest = refs[3:]
    sink_ref = rest.pop(0) if use_sink else None
    bias_ref, o_ref = rest

    group = q_ref.shape[-1] // LANES
    first_pair = pl.program_id(1) * group
    qpos0 = pl.program_id(2) * tq + q_off
    start = pl.multiple_of((qpos0 // CHUNK - n_prev) * CHUNK - k_off + front, CHUNK)
    band = pl.ds(start, width)
    kidx = lax.broadcasted_iota(jnp.int32, (1, width), 1) + (start - front)
    invalid = jnp.where((kidx >= 0) & (kidx < sk_valid), 0.0, MASK_VALUE)

    lane_half = lax.broadcasted_iota(jnp.int32, (1, LANES), 1) // HEAD_DIM
    shared_kv = k_ref.shape[-1] == LANES
    for j in range(group):
        q_pair = q_ref[0, :, j * LANES:(j + 1) * LANES]
        kv_lanes = slice(0, LANES) if shared_kv else slice(j * LANES, (j + 1) * LANES)
        k_band = k_ref[0, band, kv_lanes]
        v_band = v_ref[0, band, kv_lanes]
        outs = []
        for hh in range(2):
            q_h = jnp.where(lane_half == hh, q_pair, jnp.zeros((), BF16))
            s = lax.dot_general(q_h, k_band, (((1,), (1,)), ((), ())),
                                preferred_element_type=F32)
            s = s + (bias_ref[2 * j + hh] if per_head_bias else bias_ref[...]) + invalid
            m = jnp.max(s, axis=1, keepdims=True)
            if use_sink:
                sink = sink_ref[2 * (first_pair + j) + hh] * LOG2E
                m = jnp.maximum(m, sink)
            l_lanes = jnp.zeros((tq, LANES), F32)
            o = jnp.zeros((tq, LANES), F32)
            for c0 in range(0, width, LANES):
                p_c = jnp.exp2(s[:, c0:c0 + LANES] - m)
                l_lanes = l_lanes + p_c
                o = o + jnp.dot(p_c.astype(BF16), v_band[c0:c0 + LANES],
                                preferred_element_type=F32)
            l = jnp.sum(l_lanes, axis=1, keepdims=True)
            if use_sink:
                l = l + jnp.exp2(sink - m)
            outs.append(o / l)
        o_ref[0, :, j * LANES:(j + 1) * LANES] = jnp.where(
            lane_half == 0, outs[0], outs[1]).astype(o_ref.dtype)


def _band_mask_tile(rows, width, n_prev):
    r_chunk = np.arange(rows)[:, None] // CHUNK
    c_chunk = np.arange(width)[None, :] // CHUNK - n_prev
    return np.where((c_chunk <= r_chunk) & (c_chunk >= r_chunk - n_prev), 0.0, MASK_VALUE
                    ).astype(np.float32)


def _band_attention(q, k, v, *, width, sk_valid, q_off, k_off, front, n_prev, pairs_per_step,
                    shared_kv, sink=None, bias=None):
    kv_group = pairs_per_step
    kv_lanes = LANES if shared_kv else pairs_per_step * LANES
    bx, sq, qw = q.shape
    rows = k.shape[1]
    n_pairs = qw // LANES
    tq = min(Q_TILE, sq)
    assert sq % tq == 0 and q_off % CHUNK == 0 and (tq % CHUNK == 0 or sq == tq)
    assert ((q_off + sq - tq) // CHUNK - n_prev) * CHUNK - k_off + front + width <= rows
    assert (q_off // CHUNK - n_prev) * CHUNK - k_off + front >= 0
    in_specs = [
        pl.BlockSpec((1, tq, kv_group * LANES), lambda b, g, i: (b, i, g)),
        pl.BlockSpec((1, rows, kv_lanes), lambda b, g, i: (b, 0, g)),
        pl.BlockSpec((1, rows, kv_lanes), lambda b, g, i: (b, 0, g)),
    ]
    args = [q, k, v]
    if sink is not None:
        in_specs.append(pl.BlockSpec(memory_space=pltpu.SMEM))
        args.append(sink)
    per_head_bias = bias.ndim == 3
    if per_head_bias:
        in_specs.append(pl.BlockSpec((2 * kv_group, tq, width), lambda b, g, i: (g, 0, 0)))
    else:
        in_specs.append(pl.BlockSpec((tq, width), lambda b, g, i: (0, 0)))
    args.append(bias)
    return pl.pallas_call(
        functools.partial(_band_kernel, tq=tq, width=width, sk_valid=sk_valid, q_off=q_off,
                          k_off=k_off, front=front, n_prev=n_prev,
                          use_sink=sink is not None, per_head_bias=per_head_bias),
        grid=(bx, n_pairs // kv_group, sq // tq),
        in_specs=in_specs,
        out_specs=pl.BlockSpec((1, tq, kv_group * LANES), lambda b, g, i: (b, i, g)),
        out_shape=jax.ShapeDtypeStruct((bx, sq, n_pairs * LANES), BF16),
        compiler_params=_cparams(3),
    )(*args)


def _band_bias_kernel(e_ref, mask_ref, o_ref, *, width):
    w = e_ref.shape[-1]
    x = jnp.broadcast_to(e_ref[0], (Q_TILE, w))
    toeplitz = pltpu.roll(x, 0, 1, stride=1, stride_axis=0)
    o_ref[0] = toeplitz[:, :width] * LOG2E + mask_ref[...]


def _band_bias(rel_bias):
    heads = rel_bias.shape[0]
    clip = (rel_bias.shape[1] - 1) // 2
    band = D_PREV_CHUNKS * CHUNK
    width = D_BAND_WIDTH
    w = width + Q_TILE
    assert band >= clip and width > band + clip
    top = jnp.broadcast_to(rel_bias[:, -1:], (heads, band - clip + 1))
    mid = jnp.flip(rel_bias[:, :2 * clip], axis=1)
    low = jnp.broadcast_to(rel_bias[:, :1], (heads, width - (band + clip + 1)))
    neg = jnp.broadcast_to(rel_bias[:, -1:], (heads, w - width))
    e = jnp.concatenate([top, mid, low, neg], axis=1)[:, None, :]
    return pl.pallas_call(
        functools.partial(_band_bias_kernel, width=width),
        grid=(heads,),
        in_specs=[pl.BlockSpec((1, 1, w), lambda h: (h, 0, 0)),
                  pl.BlockSpec((Q_TILE, width), lambda h: (0, 0))],
        out_specs=pl.BlockSpec((1, Q_TILE, width), lambda h: (h, 0, 0)),
        out_shape=jax.ShapeDtypeStruct((heads, Q_TILE, width), F32),
        compiler_params=_cparams(1),
    )(e, jnp.asarray(_band_mask_tile(Q_TILE, width, D_PREV_CHUNKS)))


def _rope_tables(pos, n_rot, lane_offsets, rows_repeat=1):
    half = n_rot // 2
    inv = ROPE_THETA ** (-jnp.arange(half, dtype=F32) * 2.0 / n_rot)
    ang = pos.astype(F32)[:, None] * inv[None, :]
    cos, sin = jnp.cos(ang), jnp.sin(ang)
    n = pos.shape[0]
    cos_t = jnp.ones((n, LANES), F32)
    sin_up = jnp.zeros((n, LANES), F32)
    sin_dn = jnp.zeros((n, LANES), F32)
    for o in lane_offsets:
        cos_t = cos_t.at[:, o:o + half].set(cos).at[:, o + half:o + n_rot].set(cos)
        sin_dn = sin_dn.at[:, o:o + half].set(-sin)
        sin_up = sin_up.at[:, o + half:o + n_rot].set(sin)
    def tiles(t):
        if rows_repeat > 1:
            return jnp.tile(t, (rows_repeat, 1))[None]
        rows = min(n, TOKEN_TILE)
        return t.reshape(n // rows, rows, LANES)
    return tiles(cos_t), tiles(sin_up), tiles(sin_dn)


def _tables_for(bx, sx, pos, n_rot, lane_offsets):
    nb, ts = _token_tiling(bx, sx)
    return _rope_tables(pos, n_rot, lane_offsets, rows_repeat=nb if nb > 1 else 1)


def _c_head_order():
    rep = C_HEADS // C_KV_HEADS
    order = []
    for p in range(C_HEADS // 2):
        g2, i = divmod(p, rep)
        order += [rep * (2 * g2) + i, rep * (2 * g2 + 1) + i]
    return np.asarray(order)


def _pad_rows(a, rows):
    return jnp.pad(a, ((0, 0), (0, rows - a.shape[1]), (0, 0)))


def _front_pad(a, rows):
    return jnp.pad(a, ((0, 0), (rows, 0), (0, 0)))


def _with_cache(cache, new, dtype, block=KV_BLOCK):
    full = jnp.concatenate([cache.reshape(cache.shape[0], cache.shape[1], -1).astype(dtype),
                            new.astype(dtype)], axis=1)
    rows = -(-full.shape[1] // block) * block
    return _pad_rows(full, rows), full.shape[1]


def kernel(x_prompt, x_sample, c_prompt, c_sample, cache_a_ckv, cache_a_krope, cache_b_k, cache_b_v,
           cache_c_k, cache_c_v, cache_d_k, cache_d_v, w_mod, b_mod, g_mix, g_ffn, w_ffn_in, w_ffn_out,
           w_a_down, g_a_q, g_a_kv, w_a_uq, w_a_uk, w_a_uv, w_a_o, w_b_qkv, w_b_o,
           w_c_qkv, b_c_qkv, sink_c, w_c_o, w_d_qkv, rel_bias_d, w_d_o, g_final):
    bp, sp, d = x_prompt.shape
    bs, t, _ = x_sample.shape
    depth = w_mod.shape[0]
    past = cache_a_ckv.shape[2]
    pos_p = jnp.arange(sp)
    pos_s = past + jnp.arange(t)

    mods = _adaln(jnp.concatenate([c_prompt, c_sample], axis=0), w_mod, b_mod)
    w_in_all, w_out_all = _to_bf16(w_ffn_in), _to_bf16(w_ffn_out)

    def mod(i, k):
        m = mods[i, k][:, None, :]
        return m[:bp], m[bp:]

    xp, xs = x_prompt, x_sample
    states = [[] for _ in range(N_MIXERS)]
    for i in range(depth):
        m, j = i % N_MIXERS, i // N_MIXERS
        (sh_p, sh_s), (sc_p, sc_s), (gm_p, gm_s) = mod(i, 0), mod(i, 1), mod(i, 2)
        (shf_p, shf_s), (scf_p, scf_s), (gf_p, gf_s) = mod(i, 3), mod(i, 4), mod(i, 5)
        g_m = g_mix[i][None, :]
        if m == 0:
            n_down = A_Q_LORA + A_KV_LORA + A_ROPE
            wd = jnp.pad(w_a_down[j], ((0, 0), (0, A_Q_LORA + A_KV_LORA + LANES - n_down))).astype(BF16)
            wq = w_a_uq[j].reshape(A_Q_LORA, A_HEADS, A_NOPE + A_ROPE)
            wq = jnp.pad(wq, ((0, 0), (0, 0), (0, LANES - A_NOPE - A_ROPE)))
            x1 = wq[:, :, A_NOPE:A_NOPE + A_ROPE // 2]
            x2 = wq[:, :, A_NOPE + A_ROPE // 2:A_NOPE + A_ROPE]
            wq_rot = jnp.zeros_like(wq).at[:, :, A_NOPE:A_NOPE + A_ROPE].set(
                jnp.concatenate([-x2, x1], axis=-1))
            wq = wq.reshape(A_Q_LORA, A_HEADS * LANES).astype(BF16)
            wq_rot = wq_rot.reshape(A_Q_LORA, A_HEADS * LANES).astype(BF16)
            wk = jnp.pad(w_a_uk[j], ((0, 0), (0, 0), (0, LANES - A_NOPE)))
            wk = wk.reshape(A_KV_LORA, A_HEADS * LANES).astype(BF16)
            place = jnp.zeros((A_ROPE, A_HEADS, LANES), F32)
            place = place.at[jnp.arange(A_ROPE), :, A_NOPE + jnp.arange(A_ROPE)].set(1.0)
            wr = place.reshape(A_ROPE, A_HEADS * LANES).astype(BF16)
            wv = w_a_uv[j].reshape(A_KV_LORA, A_HEADS * A_V).astype(BF16)
            g_q, g_kv = g_a_q[j][None, :], g_a_kv[j][None, :]

            def project(x, sh, sc, pos):
                q_cos, q_up, q_dn = _tables_for(x.shape[0], x.shape[1], pos, A_ROPE, [A_NOPE])
                kt = _tables_for(x.shape[0], x.shape[1], pos, A_ROPE, [0])
                return _proj_a(x, g_m, sh, sc, wd, g_q, g_kv, wq, wq_rot, (q_cos, q_up - q_dn), kt)

            q_p, ckv_p, kr_p = project(xp, sh_p, sc_p, pos_p)
            q_s, ckv_s, kr_s = project(xs, sh_s, sc_s, pos_s)
            k_p, v_p = _expand_a(ckv_p, kr_p, wk, wr, wv)
            o_p = _causal_attention(q_p, k_p, v_p, span=min(A_KV_BLOCK, sp), sk_valid=sp,
                                    q_off=0, k_off=0)
            n_all = past + t
            rows = -(-n_all // LANES) * LANES
            kcat = jnp.concatenate(
                [jnp.concatenate([cache_a_ckv[j], ckv_s], axis=1),
                 jnp.concatenate([cache_a_krope[j], kr_s], axis=1)], axis=-1).astype(BF16)
            kcat = jnp.pad(kcat, ((0, 0), (0, rows - n_all),
                                  (0, A_KV_LORA + LANES - kcat.shape[-1])))
            wabs = jnp.zeros((A_HEADS, LANES, A_KV_LORA + LANES), F32)
            wabs = wabs.at[:, :A_NOPE, :A_KV_LORA].set(w_a_uk[j].transpose(1, 2, 0))
            wabs = wabs.at[:, A_NOPE + jnp.arange(A_ROPE), A_KV_LORA + jnp.arange(A_ROPE)].set(1.0)
            wv_placed = jnp.zeros((A_HEADS, A_KV_LORA, A_HEADS, A_V), F32)
            wv_placed = wv_placed.at[jnp.arange(A_HEADS), :, jnp.arange(A_HEADS), :].set(
                w_a_uv[j].transpose(1, 0, 2))
            wv_placed = wv_placed.reshape(A_HEADS, A_KV_LORA, A_HEADS * A_V)
            o_s = _latent_decode(q_s, kcat, wabs.astype(BF16), wv_placed.astype(BF16),
                                 sk_valid=n_all, q_off=past, k_off=0)
            wo = w_a_o[j].astype(BF16)
            states[0].append((ckv_p, kr_p, ckv_s, kr_s))
        elif m == 1 or m == 3:
            heads = B_HEADS if m == 1 else D_HEADS
            w_qkv = (w_b_qkv if m == 1 else w_d_qkv)[j].astype(BF16)
            cache_k, cache_v = (cache_b_k, cache_b_v) if m == 1 else (cache_d_k, cache_d_v)
            q_scale = HEAD_DIM ** -0.5 * LOG2E
            padded = m == 3 and D_PREV_CHUNKS * CHUNK == _token_tiling(bp, sp)[1]
            q_p, kf_p, vf_p, kb_p, vb_p = _proj_qkv(xp, g_m, sh_p, sc_p, w_qkv, heads, q_scale,
                                                    front_pad=padded)
            q_s, kf_s, vf_s, kb_s, vb_s = _proj_qkv(xs, g_m, sh_s, sc_s, w_qkv, heads, q_scale)
            lc = cache_k.shape[2]
            if m == 1:
                o_p = _stick_attention(q_p, kb_p, vb_p, sk_valid=sp, q_off=0, k_off=0)
                k_s, n_all = _with_cache(cache_k[j], kb_s, BF16, KV_BLOCK)
                v_s, _ = _with_cache(cache_v[j], vb_s, BF16, KV_BLOCK)
                o_s = _stick_attention(q_s, k_s, v_s, sk_valid=n_all, q_off=past,
                                       k_off=past - lc)
                wo = w_b_o[j].astype(BF16)
                states[1].append((kf_p, vf_p, kf_s, vf_s))
            else:
                bias = _band_bias(rel_bias_d[j])
                k_s, n_all = _with_cache(cache_k[j], kb_s, BF16, D_BAND_WIDTH)
                v_s, _ = _with_cache(cache_v[j], vb_s, BF16, D_BAND_WIDTH)
                front = D_PREV_CHUNKS * CHUNK
                if not padded:
                    kb_p, vb_p = _front_pad(kb_p, front), _front_pad(vb_p, front)
                o_p = _band_attention(q_p, kb_p, vb_p,
                                      width=D_BAND_WIDTH, sk_valid=sp, q_off=0, k_off=0,
                                      front=front, n_prev=D_PREV_CHUNKS, bias=bias,
                                      pairs_per_step=D_PAIRS_PER_STEP, shared_kv=False)
                o_s = _band_attention(q_s, k_s, v_s, width=D_BAND_WIDTH, sk_valid=n_all,
                                      q_off=past, k_off=past - lc, front=0,
                                      n_prev=D_PREV_CHUNKS, bias=bias[:, :t, :],
                                      pairs_per_step=D_PAIRS_PER_STEP, shared_kv=False)
                wo = w_d_o[j].astype(BF16)
                keep = min(D_PREV_CHUNKS * CHUNK, sp)
                k_roll = jnp.concatenate([cache_k[j].reshape(bs, lc, -1), kf_s], axis=1)[:, t:]
                v_roll = jnp.concatenate([cache_v[j].reshape(bs, lc, -1), vf_s], axis=1)[:, t:]
                states[3].append((kf_p[:, sp - keep:], vf_p[:, sp - keep:], k_roll, v_roll))
        else:
            order = _c_head_order()
            q_width, kv_width = C_HEADS * HEAD_DIM, C_KV_HEADS * HEAD_DIM
            col = np.concatenate([(order[:, None] * HEAD_DIM + np.arange(HEAD_DIM)).reshape(-1),
                                  np.arange(q_width, q_width + 2 * kv_width)])
            w_qkv = w_c_qkv[j][:, col].astype(BF16)
            b_qkv = b_c_qkv[j][col][None, :]
            sink = sink_c[j][order]
            wo = w_c_o[j].reshape(C_HEADS, HEAD_DIM, d)[order].reshape(q_width, d).astype(BF16)
            lanes = [0, HEAD_DIM]

            def project(x, sh, sc, pos):
                tb = _tables_for(x.shape[0], x.shape[1], pos, C_ROT, lanes)
                return _proj_c(x, g_m, sh, sc, w_qkv, b_qkv, tb)

            q_p, kf_p, vf_p, kb_p, vb_p = project(xp, sh_p, sc_p, pos_p)
            q_s, kf_s, vf_s, kb_s, vb_s = project(xs, sh_s, sc_s, pos_s)
            lc = cache_c_k.shape[2]
            k_s, n_all = _with_cache(cache_c_k[j], kb_s, BF16, C_BAND_WIDTH)
            v_s, _ = _with_cache(cache_c_v[j], vb_s, BF16, C_BAND_WIDTH)
            group = C_HEADS // C_KV_HEADS
            front = C_PREV_CHUNKS * CHUNK
            band_mask = jnp.asarray(_band_mask_tile(Q_TILE, C_BAND_WIDTH, C_PREV_CHUNKS))
            o_p = _band_attention(q_p, _front_pad(kb_p, front), _front_pad(vb_p, front),
                                  width=C_BAND_WIDTH, sk_valid=sp, q_off=0, k_off=0, front=front,
                                  n_prev=C_PREV_CHUNKS, pairs_per_step=group, shared_kv=True, sink=sink,
                                  bias=band_mask)
            o_s = _band_attention(q_s, k_s, v_s, width=C_BAND_WIDTH, sk_valid=n_all, q_off=past,
                                  k_off=past - lc, front=0, n_prev=C_PREV_CHUNKS, pairs_per_step=group, shared_kv=True,
                                  sink=sink, bias=band_mask[:t])
            keep = min(C_PREV_CHUNKS * CHUNK, sp)
            k_roll = jnp.concatenate([cache_c_k[j].reshape(bs, lc, -1), kf_s], axis=1)[:, t:]
            v_roll = jnp.concatenate([cache_c_v[j].reshape(bs, lc, -1), vf_s], axis=1)[:, t:]
            states[2].append((kf_p[:, sp - keep:], vf_p[:, sp - keep:], k_roll, v_roll))

        g_f = g_ffn[i][None, :]
        g_out = g_final[None, :] if i == depth - 1 else None
        xp = _block(xp, o_p, wo, gm_p, g_f, shf_p, scf_p, gf_p, w_in_all, w_out_all, i, g_out)
        xs = _block(xs, o_s, wo, gm_s, g_f, shf_s, scf_s, gf_s, w_in_all, w_out_all, i, g_out)

    y_p, y_s = xp, xs

    def stacked(entries, n_heads=None):
        outs = []
        for parts in zip(*entries):
            a = jnp.stack(parts, axis=0)
            if n_heads is not None:
                a = a.reshape(a.shape[:3] + (n_heads, HEAD_DIM))
            outs.append(a)
        return tuple(outs)

    return ((y_p, y_s) + stacked(states[0]) + stacked(states[1], B_HEADS)
            + stacked(states[2], C_KV_HEADS) + stacked(states[3], D_HEADS))
```

```python
import functools
import math

import numpy as np
import jax
import jax.numpy as jnp
from jax import lax
from jax.experimental import pallas as pl
from jax.experimental.pallas import tpu as pltpu

F32 = jnp.float32
BF16 = jnp.bfloat16

CHUNK = 64
HEAD_DIM = 64
ROPE_THETA = 500000.0
NORM_EPS = 1e-6
N_MIXERS = 4
A_HEADS, A_Q_LORA, A_KV_LORA, A_NOPE, A_ROPE, A_V = 16, 384, 256, 64, 32, 64
B_HEADS = 16
C_HEADS, C_KV_HEADS, C_WINDOW, C_ROT = 16, 4, 128, 16
D_HEADS, D_PREV_CHUNKS, D_REL_CLIP = 16, 8, 128
C_PREV_CHUNKS = C_WINDOW // CHUNK

LANES = 128
V7X_VMEM_LIMIT = 56 * 1024 * 1024

TOKEN_TILE = 512
Q_TILE = 256
KV_BLOCK = 256
C_BAND_WIDTH = C_PREV_CHUNKS * CHUNK + Q_TILE
D_BAND_WIDTH = D_PREV_CHUNKS * CHUNK + Q_TILE
A_Q_TILE = 256
A_KV_BLOCK = 512
A_PV_CHUNK = 512
A_PAIRS_PER_STEP = 2
B_PAIRS_PER_STEP = 4
D_PAIRS_PER_STEP = 4
MASK_VALUE = -1e30
LOG2E = math.log2(math.e)
STICK_DEAD_BITS = 150.0


def _cparams(n_axes):
    return pltpu.CompilerParams(
        dimension_semantics=("parallel",) * n_axes, vmem_limit_bytes=V7X_VMEM_LIMIT)


def _adaln_kernel(c_ref, w_ref, b_ref, o_ref):
    c = c_ref[...]
    a = (c * jax.nn.sigmoid(c)).astype(BF16)
    y = jnp.dot(a, w_ref[0].astype(BF16), preferred_element_type=F32) + b_ref[0]
    o_ref[0, 0] = y


def _adaln(c_all, w_mod, b_mod):
    depth, d, d6 = w_mod.shape
    n = c_all.shape[0]
    return pl.pallas_call(
        _adaln_kernel,
        grid=(depth, d6 // d),
        in_specs=[
            pl.BlockSpec((n, d), lambda i, k: (0, 0)),
            pl.BlockSpec((1, d, d), lambda i, k: (i, 0, k)),
            pl.BlockSpec((1, 1, d), lambda i, k: (i, 0, k)),
        ],
        out_specs=pl.BlockSpec((1, 1, n, d), lambda i, k: (i, k, 0, 0)),
        out_shape=jax.ShapeDtypeStruct((depth, d6 // d, n, d), F32),
        compiler_params=_cparams(2),
    )(c_all, w_mod, b_mod.reshape(depth, 1, d6))


def _cast_kernel(x_ref, o_ref):
    o_ref[...] = x_ref[...].astype(o_ref.dtype)


def _to_bf16(w):
    n, rows, cols = w.shape
    slab = 256
    assert rows % slab == 0
    return pl.pallas_call(
        _cast_kernel,
        grid=(n, rows // slab),
        in_specs=[pl.BlockSpec((1, slab, cols), lambda i, r: (i, r, 0))],
        out_specs=pl.BlockSpec((1, slab, cols), lambda i, r: (i, r, 0)),
        out_shape=jax.ShapeDtypeStruct(w.shape, BF16),
        compiler_params=_cparams(2),
    )(w)


def _token_tiling(bx, sx):
    if sx >= TOKEN_TILE:
        ts = TOKEN_TILE
        while sx % ts:
            ts //= 2
        return 1, ts
    nb = min(bx, TOKEN_TILE // sx)
    while bx % nb:
        nb -= 1
    return nb, sx


def _tok_spec(nb, ts, width):
    return pl.BlockSpec((nb, ts, width), lambda b, s: (b, s, 0))


def _mod_spec(nb, d):
    return pl.BlockSpec((nb, 1, d), lambda b, s: (b, 0, 0))


def _const_spec(shape):
    nd = len(shape)
    return pl.BlockSpec(shape, lambda b, s: (0,) * nd)


def _table_spec(rows):
    return pl.BlockSpec((1, rows, LANES), lambda b, s: (s, 0, 0))


def _rms(x):
    return x * lax.rsqrt(jnp.mean(x * x, axis=-1, keepdims=True) + NORM_EPS)


def _modulated(x_ref, g_ref, shift_ref, scale_ref):
    y = _rms(x_ref[...]) * g_ref[...]
    h = y * (1.0 + scale_ref[...]) + shift_ref[...]
    return h.reshape(-1, h.shape[-1]).astype(BF16)


def _rope_lanes(x, cos_t, sin_up, sin_dn, half):
    return (x * cos_t + pltpu.roll(x, half, 1) * sin_up
            + pltpu.roll(x, LANES - half, 1) * sin_dn)


def _store_tok(ref, lo, val):
    nb, ts = ref.shape[0], ref.shape[1]
    w = val.shape[-1]
    ref[:, :, lo:lo + w] = val.reshape(nb, ts, w).astype(ref.dtype)


def _proj_qkv_kernel(x_ref, g_ref, sh_ref, sc_ref, w_ref, q_ref, kf_ref, vf_ref, kb_ref, vb_ref,
                     *, width, q_scale, n_tiles, front_pad):
    def project():
        h = _modulated(x_ref, g_ref, sh_ref, sc_ref)
        q = jnp.dot(h, w_ref[:, 0:width], preferred_element_type=F32)
        _store_tok(q_ref, 0, q * q_scale)
        k = jnp.dot(h, w_ref[:, width:2 * width], preferred_element_type=F32)
        _store_tok(kf_ref, 0, k)
        _store_tok(kb_ref, 0, k)
        v = jnp.dot(h, w_ref[:, 2 * width:3 * width], preferred_element_type=F32)
        _store_tok(vf_ref, 0, v)
        _store_tok(vb_ref, 0, v)

    if not front_pad:
        project()
        return
    pl.when(pl.program_id(1) < n_tiles)(project)

    @pl.when(pl.program_id(1) == n_tiles)
    def _():
        kb_ref[...] = jnp.zeros_like(kb_ref)
        vb_ref[...] = jnp.zeros_like(vb_ref)


def _proj_qkv(x, g, shift, scale, w_bf16, n_heads, q_scale, front_pad=False):
    bx, sx, d = x.shape
    width = n_heads * HEAD_DIM
    nb, ts = _token_tiling(bx, sx)
    n_tiles = sx // ts
    out = lambda rows, dt: jax.ShapeDtypeStruct((bx, rows, width), dt)
    if front_pad:
        assert nb == 1
        tile = lambda b, s: (b, jnp.minimum(s, n_tiles - 1), 0)
        shifted = lambda b, s: (b, (s + 1) % (n_tiles + 1), 0)
        x_spec = pl.BlockSpec((nb, ts, d), tile)
        out_specs = [pl.BlockSpec((nb, ts, width), tile)] * 3 + [pl.BlockSpec((nb, ts, width), shifted)] * 2
        semantics = ("parallel", "arbitrary")
    else:
        x_spec = _tok_spec(nb, ts, d)
        out_specs = [_tok_spec(nb, ts, width)] * 5
        semantics = ("parallel", "parallel")
    kv_rows = sx + ts if front_pad else sx
    return pl.pallas_call(
        functools.partial(_proj_qkv_kernel, width=width, q_scale=q_scale, n_tiles=n_tiles,
                          front_pad=front_pad),
        grid=(bx // nb, n_tiles + (1 if front_pad else 0)),
        in_specs=[x_spec, _const_spec((1, d)), _mod_spec(nb, d), _mod_spec(nb, d),
                  _const_spec(w_bf16.shape)],
        out_specs=out_specs,
        out_shape=[out(sx, BF16), out(sx, F32), out(sx, F32), out(kv_rows, BF16), out(kv_rows, BF16)],
        compiler_params=pltpu.CompilerParams(dimension_semantics=semantics,
                                             vmem_limit_bytes=V7X_VMEM_LIMIT),
    )(x, g, shift, scale, w_bf16)


def _proj_c_kernel(x_ref, g_ref, sh_ref, sc_ref, w_ref, b_ref, tc_ref, tu_ref, td_ref,
                   q_ref, kf_ref, vf_ref, kb_ref, vb_ref, *, q_width, kv_width, q_scale):
    h = _modulated(x_ref, g_ref, sh_ref, sc_ref)
    qkv = jnp.dot(h, w_ref[...], preferred_element_type=F32) + b_ref[...]
    cos_t, sin_up, sin_dn = tc_ref[0], tu_ref[0], td_ref[0]
    half = C_ROT // 2
    for j in range(q_width // LANES):
        xg = qkv[:, j * LANES:(j + 1) * LANES]
        _store_tok(q_ref, j * LANES, _rope_lanes(xg, cos_t, sin_up, sin_dn, half) * q_scale)
    for j in range(kv_width // LANES):
        lo = q_width + j * LANES
        kg = _rope_lanes(qkv[:, lo:lo + LANES], cos_t, sin_up, sin_dn, half)
        _store_tok(kf_ref, j * LANES, kg)
        _store_tok(kb_ref, j * LANES, kg)
    v = qkv[:, q_width + kv_width:q_width + 2 * kv_width]
    _store_tok(vf_ref, 0, v)
    _store_tok(vb_ref, 0, v)


def _proj_c(x, g, shift, scale, w_bf16, bias, tables):
    bx, sx, d = x.shape
    q_width, kv_width = C_HEADS * HEAD_DIM, C_KV_HEADS * HEAD_DIM
    nb, ts = _token_tiling(bx, sx)
    out = lambda w, dt: jax.ShapeDtypeStruct((bx, sx, w), dt)
    return pl.pallas_call(
        functools.partial(_proj_c_kernel, q_width=q_width, kv_width=kv_width,
                          q_scale=HEAD_DIM ** -0.5 * LOG2E),
        grid=(bx // nb, sx // ts),
        in_specs=[_tok_spec(nb, ts, d), _const_spec((1, d)), _mod_spec(nb, d), _mod_spec(nb, d),
                  _const_spec(w_bf16.shape), _const_spec(bias.shape)] + [_table_spec(nb * ts)] * 3,
        out_specs=[_tok_spec(nb, ts, q_width)] + [_tok_spec(nb, ts, kv_width)] * 4,
        out_shape=[out(q_width, BF16), out(kv_width, F32), out(kv_width, F32),
                   out(kv_width, BF16), out(kv_width, BF16)],
        compiler_params=_cparams(2),
    )(x, g, shift, scale, w_bf16, bias, *tables)


def _proj_a_kernel(*refs, q_scale, expand):
    refs = list(refs)
    x_ref, g_ref, sh_ref, sc_ref, wd_ref, gq_ref, gkv_ref, wq_ref, wqr_ref = refs[:9]
    rest = refs[9:]
    wk_ref, wr_ref, wv_ref = (rest.pop(0), rest.pop(0), rest.pop(0)) if expand else (None,) * 3
    qc_ref, qs_ref, kc_ref, ku_ref, kd_ref, q_ref, ckv_ref, kr_ref = rest[:8]
    h = _modulated(x_ref, g_ref, sh_ref, sc_ref)
    down = jnp.dot(h, wd_ref[...], preferred_element_type=F32)
    cq = (_rms(down[:, :A_Q_LORA]) * gq_ref[...]).astype(BF16)
    ckv = _rms(down[:, A_Q_LORA:A_Q_LORA + A_KV_LORA]) * gkv_ref[...]
    _store_tok(ckv_ref, 0, ckv)
    lo = A_Q_LORA + A_KV_LORA
    half = A_ROPE // 2
    kr = _rope_lanes(down[:, lo:lo + LANES], kc_ref[0], ku_ref[0], kd_ref[0], half)
    _store_tok(kr_ref, 0, kr[:, :A_ROPE])
    if expand:
        k_ref, v_ref = rest[8:]
        ckv_b = ckv.astype(BF16)
        k = (jnp.dot(ckv_b, wk_ref[...], preferred_element_type=F32)
             + jnp.dot(kr.astype(BF16), wr_ref[...], preferred_element_type=F32))
        _store_tok(k_ref, 0, k)
        _store_tok(v_ref, 0, jnp.dot(ckv_b, wv_ref[...], preferred_element_type=F32))
    q = jnp.dot(cq, wq_ref[...], preferred_element_type=F32)
    q_rot = jnp.dot(cq, wqr_ref[...], preferred_element_type=F32)
    qc, qs = qc_ref[0], qs_ref[0]
    for j in range(A_HEADS):
        lanes = slice(j * LANES, (j + 1) * LANES)
        _store_tok(q_ref, j * LANES, (q[:, lanes] * qc + q_rot[:, lanes] * qs) * q_scale)


def _proj_a(x, g, shift, scale, wd_bf16, g_q, g_kv, wq_bf16, wq_rot_bf16, q_tables, k_tables,
            expand_weights=None):
    bx, sx, d = x.shape
    nb, ts = _token_tiling(bx, sx)
    expand = expand_weights is not None
    extra = list(expand_weights) if expand else []
    out_specs = [_tok_spec(nb, ts, A_HEADS * LANES), _tok_spec(nb, ts, A_KV_LORA),
                 _tok_spec(nb, ts, A_ROPE)]
    out_shape = [jax.ShapeDtypeStruct((bx, sx, A_HEADS * LANES), BF16),
                 jax.ShapeDtypeStruct((bx, sx, A_KV_LORA), F32),
                 jax.ShapeDtypeStruct((bx, sx, A_ROPE), F32)]
    if expand:
        out_specs += [_tok_spec(nb, ts, A_HEADS * LANES), _tok_spec(nb, ts, A_HEADS * A_V)]
        out_shape += [jax.ShapeDtypeStruct((bx, sx, A_HEADS * LANES), BF16),
                      jax.ShapeDtypeStruct((bx, sx, A_HEADS * A_V), BF16)]
    return pl.pallas_call(
        functools.partial(_proj_a_kernel, q_scale=(A_NOPE + A_ROPE) ** -0.5 * LOG2E,
                          expand=expand),
        grid=(bx // nb, sx // ts),
        in_specs=[_tok_spec(nb, ts, d), _const_spec((1, d)), _mod_spec(nb, d), _mod_spec(nb, d),
                  _const_spec(wd_bf16.shape), _const_spec(g_q.shape), _const_spec(g_kv.shape),
                  _const_spec(wq_bf16.shape), _const_spec(wq_rot_bf16.shape)]
                 + [_const_spec(w.shape) for w in extra] + [_table_spec(nb * ts)] * 5,
        out_specs=out_specs,
        out_shape=out_shape,
        compiler_params=_cparams(2),
    )(x, g, shift, scale, wd_bf16, g_q, g_kv, wq_bf16, wq_rot_bf16, *extra, *q_tables, *k_tables)


def _latent_decode_kernel(q_ref, kcat_ref, wabs_ref, wv_ref, o_ref, qc_ref,
                          *, sk_valid, q_off, k_off):
    t = q_ref.shape[1]
    n_heads = wabs_ref.shape[0]
    rows = kcat_ref.shape[1]
    for h in range(n_heads):
        qc_ref[h * t:(h + 1) * t, :] = jnp.dot(
            q_ref[0, :, h * LANES:(h + 1) * LANES], wabs_ref[h],
            preferred_element_type=F32).astype(BF16)
    kcat = kcat_ref[0]
    s = lax.dot_general(qc_ref[...], kcat, (((1,), (1,)), ((), ())),
                        preferred_element_type=F32)
    shift = int(math.log2(CHUNK))
    row = lax.broadcasted_iota(jnp.int32, (n_heads * t, 1), 0)
    qpos = row - row // t * t + q_off
    kidx = lax.broadcasted_iota(jnp.int32, (1, rows), 1)
    vis = (kidx < sk_valid) & (jnp.right_shift(kidx + k_off, shift) <= jnp.right_shift(qpos, shift))
    s = jnp.where(vis, s, MASK_VALUE)
    m = jnp.max(s, axis=1, keepdims=True)
    p = jnp.exp2(s - m)
    l = jnp.sum(p, axis=1, keepdims=True)
    o_lat = (jnp.dot(p.astype(BF16), kcat[:, :A_KV_LORA], preferred_element_type=F32) / l
             ).astype(BF16)
    out = jnp.zeros((t, o_ref.shape[-1]), F32)
    for h in range(n_heads):
        out = out + jnp.dot(o_lat[h * t:(h + 1) * t], wv_ref[h], preferred_element_type=F32)
    o_ref[0] = out.astype(o_ref.dtype)


def _latent_decode(q, kcat, wabs, wv_placed, *, sk_valid, q_off, k_off):
    bx, t, _ = q.shape
    rows, width = kcat.shape[1:]
    n_heads = wabs.shape[0]
    out_w = wv_placed.shape[-1]
    return pl.pallas_call(
        functools.partial(_latent_decode_kernel, sk_valid=sk_valid, q_off=q_off, k_off=k_off),
        grid=(bx,),
        in_specs=[pl.BlockSpec((1, t, n_heads * LANES), lambda b: (b, 0, 0)),
                  pl.BlockSpec((1, rows, width), lambda b: (b, 0, 0)),
                  pl.BlockSpec(wabs.shape, lambda b: (0, 0, 0)),
                  pl.BlockSpec(wv_placed.shape, lambda b: (0, 0, 0))],
        out_specs=pl.BlockSpec((1, t, out_w), lambda b: (b, 0, 0)),
        out_shape=jax.ShapeDtypeStruct((bx, t, out_w), BF16),
        scratch_shapes=[pltpu.VMEM((n_heads * t, width), BF16)],
        compiler_params=_cparams(1),
    )(q, kcat, wabs, wv_placed)


def _block_kernel(x_ref, o_ref, wo_ref, gm_ref, g_ref, sh_ref, sc_ref, gf_ref, wi_ref, wout_ref,
                  *out_refs, hidden, chunk, final_norm):
    gout_ref, y_ref = out_refs if final_norm else (None,) + out_refs
    nb, ts, d = x_ref.shape
    o = o_ref[...].reshape(nb * ts, -1)
    mix = jnp.dot(o, wo_ref[...], preferred_element_type=F32).reshape(nb, ts, d)
    x1 = x_ref[...] + gm_ref[...] * mix
    h = (_rms(x1) * g_ref[...]) * (1.0 + sc_ref[...]) + sh_ref[...]
    h = h.reshape(nb * ts, d).astype(BF16)
    acc = jnp.zeros((nb * ts, d), F32)
    for c in range(hidden // chunk):
        gate = jnp.dot(h, wi_ref[0, :, c * chunk:(c + 1) * chunk], preferred_element_type=F32)
        up = jnp.dot(h, wi_ref[0, :, hidden + c * chunk:hidden + (c + 1) * chunk],
                     preferred_element_type=F32)
        act = (gate * jax.nn.sigmoid(gate) * up).astype(BF16)
        acc = acc + jnp.dot(act, wout_ref[0, c * chunk:(c + 1) * chunk, :],
                            preferred_element_type=F32)
    x2 = x1 + gf_ref[...] * acc.reshape(nb, ts, d)
    y_ref[...] = _rms(x2) * gout_ref[...] if final_norm else x2


def _block(x, o, wo, gate_m, g_ffn, shift_f, scale_f, gate_f, w_in, w_out, layer, g_out=None):
    bx, sx, d = x.shape
    hidden = w_out.shape[1]
    nb, ts = _token_tiling(bx, sx)
    resident = lambda shape: pl.BlockSpec(shape, lambda b, s: (0,) * len(shape),
                                          pipeline_mode=pl.Buffered(1))
    of_layer = lambda w: pl.BlockSpec((1,) + w.shape[1:], lambda b, s: (layer, 0, 0),
                                      pipeline_mode=pl.Buffered(1))
    in_specs = [_tok_spec(nb, ts, d), _tok_spec(nb, ts, o.shape[-1]), resident(wo.shape),
                _mod_spec(nb, d), _const_spec((1, d)), _mod_spec(nb, d), _mod_spec(nb, d),
                _mod_spec(nb, d), of_layer(w_in), of_layer(w_out)]
    args = [x, o, wo, gate_m, g_ffn, shift_f, scale_f, gate_f, w_in, w_out]
    if g_out is not None:
        in_specs.append(_const_spec((1, d)))
        args.append(g_out)
    return pl.pallas_call(
        functools.partial(_block_kernel, hidden=hidden, chunk=256, final_norm=g_out is not None),
        grid=(bx // nb, sx // ts),
        in_specs=in_specs,
        out_specs=_tok_spec(nb, ts, d),
        out_shape=jax.ShapeDtypeStruct((bx, sx, d), F32),
        compiler_params=_cparams(2),
    )(*args)


def _stick_kernel(q_ref, k_ref, v_ref, tri_ref, o_ref, run_ref, acc_ref,
                  *, tq, bk, sk_valid, q_off, k_off):
    qpos0 = pl.program_id(2) * tq + q_off
    kb_hi = (jnp.minimum(qpos0 + (tq - 1) - k_off, sk_valid) + bk - 1) // bk
    f_hi = jnp.minimum(jnp.minimum(qpos0 - k_off, sk_valid) // bk, kb_hi)

    def visible(kb):
        row = lax.broadcasted_iota(jnp.int32, (2 * tq, 1), 0)
        qpos = jnp.where(row < tq, row, row - tq) + qpos0
        ik = kb * bk + lax.broadcasted_iota(jnp.int32, (1, bk), 1)
        return (ik + k_off < qpos) & (ik < sk_valid)

    lane_half = lax.broadcasted_iota(jnp.int32, (1, LANES), 1) // HEAD_DIM
    n_pairs = q_ref.shape[-1] // LANES
    lanes = [slice(pr * LANES, (pr + 1) * LANES) for pr in range(n_pairs)]
    q_both = [jnp.concatenate(
        [jnp.where(lane_half == hh, q_ref[0, :, lanes[pr]], jnp.zeros((), BF16))
         for hh in range(2)], axis=0) for pr in range(n_pairs)]

    def rows(kb):
        return pl.ds(pl.multiple_of(kb * bk, bk), bk)

    def block(pr, kb, masked, run):
        y = lax.dot_general(q_both[pr], k_ref[0, rows(kb), lanes[pr]], (((1,), (1,)), ((), ())),
                            preferred_element_type=F32)
        sp = jnp.maximum(y, 0.0) + jnp.log2(1.0 + jnp.exp2(-jnp.abs(y)))
        if masked:
            vis = visible(kb)
            sp = jnp.where(vis, sp, 0.0)
        suffix = jnp.dot(sp.astype(BF16), tri_ref[...], preferred_element_type=F32)
        a = jnp.exp2(y - suffix - run)
        if masked:
            a = jnp.where(vis, a, 0.0)
        return (jnp.dot(a.astype(BF16), v_ref[0, rows(kb), lanes[pr]],
                        preferred_element_type=F32),
                jnp.sum(sp, axis=1, keepdims=True))

    @pl.when(f_hi > 0)
    def _():
        for pr in range(n_pairs):
            out_d, sum_d = block(pr, kb_hi - 1, True, 0.0)
            out_f, sum_f = block(pr, f_hi - 1, False, sum_d)
            acc_ref[pr] = out_d + out_f
            run_ref[pr] = sum_d + sum_f

    @pl.when(f_hi == 0)
    def _():
        for pr in range(n_pairs):
            out_d, sum_d = block(pr, kb_hi - 1, True, 0.0)
            acc_ref[pr] = out_d
            run_ref[pr] = sum_d

    def earlier(st):
        for pr in range(n_pairs):
            out, row_sum = block(pr, f_hi - 1 - st[0], False, run_ref[pr])
            acc_ref[pr] += out
            run_ref[pr] += row_sum
        return st[0] + 1, jnp.min(run_ref[...])

    lax.while_loop(lambda st: (st[0] < f_hi) & (st[1] < STICK_DEAD_BITS), earlier,
                   (jnp.ones((), jnp.int32), jnp.min(run_ref[...])))
    for pr in range(n_pairs):
        o_ref[0, :, lanes[pr]] = jnp.where(
            lane_half == 0, acc_ref[pr, :tq], acc_ref[pr, tq:]).astype(o_ref.dtype)


def _stick_attention(q, k, v, *, sk_valid, q_off, k_off):
    bx, sq, qw = q.shape
    rows = k.shape[1]
    n_pairs = qw // LANES
    tq = min(Q_TILE, sq)
    bk = KV_BLOCK
    assert sq % tq == 0 and rows % bk == 0 and tq % 8 == 0 and q_off >= k_off >= 0
    for q0 in range(0, sq, tq):
        first, last = q_off + q0 - k_off, min(q_off + q0 + tq - 1 - k_off, sk_valid)
        assert first <= sk_valid and -(-last // bk) - first // bk == 1
    idx = np.arange(bk)
    pps = B_PAIRS_PER_STEP
    assert n_pairs % pps == 0
    return pl.pallas_call(
        functools.partial(_stick_kernel, tq=tq, bk=bk, sk_valid=sk_valid, q_off=q_off,
                          k_off=k_off),
        grid=(bx, n_pairs // pps, sq // tq),
        in_specs=[
            pl.BlockSpec((1, tq, pps * LANES), lambda b, p, i: (b, i, p)),
            pl.BlockSpec((1, rows, pps * LANES), lambda b, p, i: (b, 0, p)),
            pl.BlockSpec((1, rows, pps * LANES), lambda b, p, i: (b, 0, p)),
            pl.BlockSpec((bk, bk), lambda b, p, i: (0, 0)),
        ],
        out_specs=pl.BlockSpec((1, tq, pps * LANES), lambda b, p, i: (b, i, p)),
        out_shape=jax.ShapeDtypeStruct((bx, sq, n_pairs * LANES), BF16),
        scratch_shapes=[pltpu.VMEM((pps, 2 * tq, 1), F32), pltpu.VMEM((pps, 2 * tq, LANES), F32)],
        compiler_params=_cparams(3),
    )(q, k, v, jnp.asarray(idx[:, None] >= idx[None, :], BF16))


def _row_end(qpos, k_off):
    return (qpos // CHUNK + 1) * CHUNK - k_off


def _causal_kernel(q_ref, k_ref, v_ref, o_ref, *, tq, span, n_spans, sk_valid, q_off, k_off,
                   single_tile):
    qpos0 = pl.program_id(2) * tq + q_off
    n_needed = (jnp.minimum(_row_end(qpos0 + tq - 1, k_off), sk_valid) - 1) // span
    shift = int(math.log2(CHUNK))
    lane_half = lax.broadcasted_iota(jnp.int32, (1, LANES), 1) // HEAD_DIM
    nt = (((1,), (1,)), ((), ()))

    for n_full in range(n_spans):
        if single_tile and n_full != (min(_row_end(q_off + tq - 1, k_off), sk_valid) - 1) // span:
            continue

        @pl.when(n_needed == n_full)
        def _(n_full=n_full):
            full, width = n_full * span, (n_full + 1) * span
            qpos = lax.broadcasted_iota(jnp.int32, (tq, 1), 0) + qpos0
            kidx = lax.broadcasted_iota(jnp.int32, (1, span), 1) + full
            vis = (kidx < sk_valid) & (jnp.right_shift(kidx + k_off, shift)
                                       <= jnp.right_shift(qpos, shift))
            outs = []
            for hh in range(2 * (q_ref.shape[-1] // (2 * LANES))):
                lanes = slice(hh * LANES, (hh + 1) * LANES)
                v_lanes = slice(hh // 2 * LANES, (hh // 2 + 1) * LANES)
                q_h = q_ref[0, :, lanes]
                s_tail = lax.dot_general(q_h, k_ref[0, full:width, lanes], nt,
                                         preferred_element_type=F32)
                s_tail = jnp.where(vis, s_tail, MASK_VALUE)
                m = jnp.max(s_tail, axis=1, keepdims=True)
                if n_full:
                    s_full = lax.dot_general(q_h, k_ref[0, 0:full, lanes], nt,
                                             preferred_element_type=F32)
                    m = jnp.maximum(m, jnp.max(s_full, axis=1, keepdims=True))
                p_tail = jnp.exp2(s_tail - m)
                l = jnp.sum(p_tail, axis=1, keepdims=True)
                o = jnp.dot(p_tail.astype(BF16), v_ref[0, full:width, v_lanes],
                            preferred_element_type=F32)
                for c0 in range(0, full, A_PV_CHUNK):
                    p_c = jnp.exp2(s_full[:, c0:c0 + A_PV_CHUNK] - m)
                    l = l + jnp.sum(p_c, axis=1, keepdims=True)
                    o = o + jnp.dot(p_c.astype(BF16), v_ref[0, c0:c0 + A_PV_CHUNK, v_lanes],
                                    preferred_element_type=F32)
                outs.append(o / l)
            for pr in range(len(outs) // 2):
                o_ref[0, :, pr * LANES:(pr + 1) * LANES] = jnp.where(
                    lane_half == 0, outs[2 * pr], outs[2 * pr + 1]).astype(o_ref.dtype)


def _causal_attention(q, k, v, *, span, sk_valid, q_off, k_off):
    bx, sq, qw = q.shape
    rows = k.shape[1]
    pps = A_PAIRS_PER_STEP
    n_pairs = qw // (2 * LANES)
    assert n_pairs % pps == 0
    tq = min(A_Q_TILE, sq)
    assert sq % tq == 0 and rows % span == 0 and q_off % CHUNK == 0 and k_off >= 0
    for q0 in range(0, sq, tq):
        end = min(_row_end(q_off + q0 + tq - 1, k_off), sk_valid)
        first_row_end = min(_row_end(q_off + q0, k_off), sk_valid)
        assert 0 < end <= rows and (end - 1) // span * span <= first_row_end
    return pl.pallas_call(
        functools.partial(_causal_kernel, tq=tq, span=span, n_spans=rows // span,
                          sk_valid=sk_valid, q_off=q_off, k_off=k_off, single_tile=sq == tq),
        grid=(bx, n_pairs // pps, sq // tq),
        in_specs=[
            pl.BlockSpec((1, tq, pps * 2 * LANES), lambda b, p, i: (b, i, p)),
            pl.BlockSpec((1, rows, pps * 2 * LANES), lambda b, p, i: (b, 0, p)),
            pl.BlockSpec((1, rows, pps * LANES), lambda b, p, i: (b, 0, p)),
        ],
        out_specs=pl.BlockSpec((1, tq, pps * LANES), lambda b, p, i: (b, i, p)),
        out_shape=jax.ShapeDtypeStruct((bx, sq, n_pairs * LANES), BF16),
        compiler_params=_cparams(3),
    )(q, k, v)


def _band_kernel(*refs, tq, width, sk_valid, q_off, k_off, front, n_prev, use_sink, per_head_bias):
    refs = list(refs)
    q_ref, k_ref, v_ref = refs[:3]
    rest = refs[3:]
    sink_ref = rest.pop(0) if use_sink else None
    bias_ref, o_ref = rest

    group = q_ref.shape[-1] // LANES
    first_pair = pl.program_id(1) * group
    qpos0 = pl.program_id(2) * tq + q_off
    start = pl.multiple_of((qpos0 // CHUNK - n_prev) * CHUNK - k_off + front, CHUNK)
    band = pl.ds(start, width)
    kidx = lax.broadcasted_iota(jnp.int32, (1, width), 1) + (start - front)
    invalid = jnp.where((kidx >= 0) & (kidx < sk_valid), 0.0, MASK_VALUE)

    lane_half = lax.broadcasted_iota(jnp.int32, (1, LANES), 1) // HEAD_DIM
    shared_kv = k_ref.shape[-1] == LANES
    for j in range(group):
        q_pair = q_ref[0, :, j * LANES:(j + 1) * LANES]
        kv_lanes = slice(0, LANES) if shared_kv else slice(j * LANES, (j + 1) * LANES)
        k_band = k_ref[0, band, kv_lanes]
        v_band = v_ref[0, band, kv_lanes]
        outs = []
        for hh in range(2):
            q_h = jnp.where(lane_half == hh, q_pair, jnp.zeros((), BF16))
            s = lax.dot_general(q_h, k_band, (((1,), (1,)), ((), ())),
                                preferred_element_type=F32)
            s = s + (bias_ref[2 * j + hh] if per_head_bias else bias_ref[...]) + invalid
            m = jnp.max(s, axis=1, keepdims=True)
            if use_sink:
                sink = sink_ref[2 * (first_pair + j) + hh] * LOG2E
                m = jnp.maximum(m, sink)
            l_lanes = jnp.zeros((tq, LANES), F32)
            o = jnp.zeros((tq, LANES), F32)
            for c0 in range(0, width, LANES):
                p_c = jnp.exp2(s[:, c0:c0 + LANES] - m)
                l_lanes = l_lanes + p_c
                o = o + jnp.dot(p_c.astype(BF16), v_band[c0:c0 + LANES],
                                preferred_element_type=F32)
            l = jnp.sum(l_lanes, axis=1, keepdims=True)
            if use_sink:
                l = l + jnp.exp2(sink - m)
            outs.append(o / l)
        o_ref[0, :, j * LANES:(j + 1) * LANES] = jnp.where(
            lane_half == 0, outs[0], outs[1]).astype(o_ref.dtype)


def _band_mask_tile(rows, width, n_prev):
    r_chunk = np.arange(rows)[:, None] // CHUNK
    c_chunk = np.arange(width)[None, :] // CHUNK - n_prev
    return np.where((c_chunk <= r_chunk) & (c_chunk >= r_chunk - n_prev), 0.0, MASK_VALUE
                    ).astype(np.float32)


def _band_attention(q, k, v, *, width, sk_valid, q_off, k_off, front, n_prev, pairs_per_step,
                    shared_kv, sink=None, bias=None):
    kv_group = pairs_per_step
    kv_lanes = LANES if shared_kv else pairs_per_step * LANES
    bx, sq, qw = q.shape
    rows = k.shape[1]
    n_pairs = qw // LANES
    tq = min(Q_TILE, sq)
    assert sq % tq == 0 and q_off % CHUNK == 0 and (tq % CHUNK == 0 or sq == tq)
    assert ((q_off + sq - tq) // CHUNK - n_prev) * CHUNK - k_off + front + width <= rows
    assert (q_off // CHUNK - n_prev) * CHUNK - k_off + front >= 0
    in_specs = [
        pl.BlockSpec((1, tq, kv_group * LANES), lambda b, g, i: (b, i, g)),
        pl.BlockSpec((1, rows, kv_lanes), lambda b, g, i: (b, 0, g)),
        pl.BlockSpec((1, rows, kv_lanes), lambda b, g, i: (b, 0, g)),
    ]
    args = [q, k, v]
    if sink is not None:
        in_specs.append(pl.BlockSpec(memory_space=pltpu.SMEM))
        args.append(sink)
    per_head_bias = bias.ndim == 3
    if per_head_bias:
        in_specs.append(pl.BlockSpec((2 * kv_group, tq, width), lambda b, g, i: (g, 0, 0)))
    else:
        in_specs.append(pl.BlockSpec((tq, width), lambda b, g, i: (0, 0)))
    args.append(bias)
    return pl.pallas_call(
        functools.partial(_band_kernel, tq=tq, width=width, sk_valid=sk_valid, q_off=q_off,
                          k_off=k_off, front=front, n_prev=n_prev,
                          use_sink=sink is not None, per_head_bias=per_head_bias),
        grid=(bx, n_pairs // kv_group, sq // tq),
        in_specs=in_specs,
        out_specs=pl.BlockSpec((1, tq, kv_group * LANES), lambda b, g, i: (b, i, g)),
        out_shape=jax.ShapeDtypeStruct((bx, sq, n_pairs * LANES), BF16),
        compiler_params=_cparams(3),
    )(*args)


def _band_bias_kernel(e_ref, mask_ref, o_ref, *, width):
    w = e_ref.shape[-1]
    x = jnp.broadcast_to(e_ref[0], (Q_TILE, w))
    toeplitz = pltpu.roll(x, 0, 1, stride=1, stride_axis=0)
    o_ref[0] = toeplitz[:, :width] * LOG2E + mask_ref[...]


def _band_bias(rel_bias):
    heads = rel_bias.shape[0]
    clip = (rel_bias.shape[1] - 1) // 2
    band = D_PREV_CHUNKS * CHUNK
    width = D_BAND_WIDTH
    w = width + Q_TILE
    assert band >= clip and width > band + clip
    top = jnp.broadcast_to(rel_bias[:, -1:], (heads, band - clip + 1))
    mid = jnp.flip(rel_bias[:, :2 * clip], axis=1)
    low = jnp.broadcast_to(rel_bias[:, :1], (heads, width - (band + clip + 1)))
    neg = jnp.broadcast_to(rel_bias[:, -1:], (heads, w - width))
    e = jnp.concatenate([top, mid, low, neg], axis=1)[:, None, :]
    return pl.pallas_call(
        functools.partial(_band_bias_kernel, width=width),
        grid=(heads,),
        in_specs=[pl.BlockSpec((1, 1, w), lambda h: (h, 0, 0)),
                  pl.BlockSpec((Q_TILE, width), lambda h: (0, 0))],
        out_specs=pl.BlockSpec((1, Q_TILE, width), lambda h: (h, 0, 0)),
        out_shape=jax.ShapeDtypeStruct((heads, Q_TILE, width), F32),
        compiler_params=_cparams(1),
    )(e, jnp.asarray(_band_mask_tile(Q_TILE, width, D_PREV_CHUNKS)))


def _rope_tables(pos, n_rot, lane_offsets, rows_repeat=1):
    half = n_rot // 2
    inv = ROPE_THETA ** (-jnp.arange(half, dtype=F32) * 2.0 / n_rot)
    ang = pos.astype(F32)[:, None] * inv[None, :]
    cos, sin = jnp.cos(ang), jnp.sin(ang)
    n = pos.shape[0]
    cos_t = jnp.ones((n, LANES), F32)
    sin_up = jnp.zeros((n, LANES), F32)
    sin_dn = jnp.zeros((n, LANES), F32)
    for o in lane_offsets:
        cos_t = cos_t.at[:, o:o + half].set(cos).at[:, o + half:o + n_rot].set(cos)
        sin_dn = sin_dn.at[:, o:o + half].set(-sin)
        sin_up = sin_up.at[:, o + half:o + n_rot].set(sin)
    def tiles(t):
        if rows_repeat > 1:
            return jnp.tile(t, (rows_repeat, 1))[None]
        rows = min(n, TOKEN_TILE)
        return t.reshape(n // rows, rows, LANES)
    return tiles(cos_t), tiles(sin_up), tiles(sin_dn)


def _tables_for(bx, sx, pos, n_rot, lane_offsets):
    nb, ts = _token_tiling(bx, sx)
    return _rope_tables(pos, n_rot, lane_offsets, rows_repeat=nb if nb > 1 else 1)


def _c_head_order():
    rep = C_HEADS // C_KV_HEADS
    order = []
    for p in range(C_HEADS // 2):
        g2, i = divmod(p, rep)
        order += [rep * (2 * g2) + i, rep * (2 * g2 + 1) + i]
    return np.asarray(order)


def _pad_rows(a, rows):
    return jnp.pad(a, ((0, 0), (0, rows - a.shape[1]), (0, 0)))


def _front_pad(a, rows):
    return jnp.pad(a, ((0, 0), (rows, 0), (0, 0)))


def _with_cache(cache, new, dtype, block=KV_BLOCK):
    full = jnp.concatenate([cache.reshape(cache.shape[0], cache.shape[1], -1).astype(dtype),
                            new.astype(dtype)], axis=1)
    rows = -(-full.shape[1] // block) * block
    return _pad_rows(full, rows), full.shape[1]


def kernel(x_prompt, x_sample, c_prompt, c_sample, cache_a_ckv, cache_a_krope, cache_b_k, cache_b_v,
           cache_c_k, cache_c_v, cache_d_k, cache_d_v, w_mod, b_mod, g_mix, g_ffn, w_ffn_in, w_ffn_out,
           w_a_down, g_a_q, g_a_kv, w_a_uq, w_a_uk, w_a_uv, w_a_o, w_b_qkv, w_b_o,
           w_c_qkv, b_c_qkv, sink_c, w_c_o, w_d_qkv, rel_bias_d, w_d_o, g_final):
    bp, sp, d = x_prompt.shape
    bs, t, _ = x_sample.shape
    depth = w_mod.shape[0]
    past = cache_a_ckv.shape[2]
    pos_p = jnp.arange(sp)
    pos_s = past + jnp.arange(t)

    mods = _adaln(jnp.concatenate([c_prompt, c_sample], axis=0), w_mod, b_mod)
    w_in_all, w_out_all = _to_bf16(w_ffn_in), _to_bf16(w_ffn_out)

    def mod(i, k):
        m = mods[i, k][:, None, :]
        return m[:bp], m[bp:]

    xp, xs = x_prompt, x_sample
    states = [[] for _ in range(N_MIXERS)]
    for i in range(depth):
        m, j = i % N_MIXERS, i // N_MIXERS
        (sh_p, sh_s), (sc_p, sc_s), (gm_p, gm_s) = mod(i, 0), mod(i, 1), mod(i, 2)
        (shf_p, shf_s), (scf_p, scf_s), (gf_p, gf_s) = mod(i, 3), mod(i, 4), mod(i, 5)
        g_m = g_mix[i][None, :]
        if m == 0:
            n_down = A_Q_LORA + A_KV_LORA + A_ROPE
            wd = jnp.pad(w_a_down[j], ((0, 0), (0, A_Q_LORA + A_KV_LORA + LANES - n_down))).astype(BF16)
            wq = w_a_uq[j].reshape(A_Q_LORA, A_HEADS, A_NOPE + A_ROPE)
            wq = jnp.pad(wq, ((0, 0), (0, 0), (0, LANES - A_NOPE - A_ROPE)))
            x1 = wq[:, :, A_NOPE:A_NOPE + A_ROPE // 2]
            x2 = wq[:, :, A_NOPE + A_ROPE // 2:A_NOPE + A_ROPE]
            wq_rot = jnp.zeros_like(wq).at[:, :, A_NOPE:A_NOPE + A_ROPE].set(
                jnp.concatenate([-x2, x1], axis=-1))
            wq = wq.reshape(A_Q_LORA, A_HEADS * LANES).astype(BF16)
            wq_rot = wq_rot.reshape(A_Q_LORA, A_HEADS * LANES).astype(BF16)
            wk = jnp.pad(w_a_uk[j], ((0, 0), (0, 0), (0, LANES - A_NOPE)))
            wk = wk.reshape(A_KV_LORA, A_HEADS * LANES).astype(BF16)
            place = jnp.zeros((A_ROPE, A_HEADS, LANES), F32)
            place = place.at[jnp.arange(A_ROPE), :, A_NOPE + jnp.arange(A_ROPE)].set(1.0)
            wr = place.reshape(A_ROPE, A_HEADS * LANES).astype(BF16)
            wv = w_a_uv[j].reshape(A_KV_LORA, A_HEADS * A_V).astype(BF16)
            g_q, g_kv = g_a_q[j][None, :], g_a_kv[j][None, :]

            def project(x, sh, sc, pos, expand_weights=None):
                q_cos, q_up, q_dn = _tables_for(x.shape[0], x.shape[1], pos, A_ROPE, [A_NOPE])
                kt = _tables_for(x.shape[0], x.shape[1], pos, A_ROPE, [0])
                return _proj_a(x, g_m, sh, sc, wd, g_q, g_kv, wq, wq_rot, (q_cos, q_up - q_dn), kt,
                               expand_weights)

            wr_lanes = jnp.pad(wr, ((0, LANES - A_ROPE), (0, 0)))
            q_p, ckv_p, kr_p, k_p, v_p = project(xp, sh_p, sc_p, pos_p, (wk, wr_lanes, wv))
            q_s, ckv_s, kr_s = project(xs, sh_s, sc_s, pos_s)
            o_p = _causal_attention(q_p, k_p, v_p, span=min(A_KV_BLOCK, sp), sk_valid=sp,
                                    q_off=0, k_off=0)
            n_all = past + t
            rows = -(-n_all // LANES) * LANES
            kcat = jnp.concatenate(
                [jnp.concatenate([cache_a_ckv[j], ckv_s], axis=1),
                 jnp.concatenate([cache_a_krope[j], kr_s], axis=1)], axis=-1).astype(BF16)
            kcat = jnp.pad(kcat, ((0, 0), (0, rows - n_all),
                                  (0, A_KV_LORA + LANES - kcat.shape[-1])))
            wabs = jnp.zeros((A_HEADS, LANES, A_KV_LORA + LANES), F32)
            wabs = wabs.at[:, :A_NOPE, :A_KV_LORA].set(w_a_uk[j].transpose(1, 2, 0))
            wabs = wabs.at[:, A_NOPE + jnp.arange(A_ROPE), A_KV_LORA + jnp.arange(A_ROPE)].set(1.0)
            wv_placed = jnp.zeros((A_HEADS, A_KV_LORA, A_HEADS, A_V), F32)
            wv_placed = wv_placed.at[jnp.arange(A_HEADS), :, jnp.arange(A_HEADS), :].set(
                w_a_uv[j].transpose(1, 0, 2))
            wv_placed = wv_placed.reshape(A_HEADS, A_KV_LORA, A_HEADS * A_V)
            o_s = _latent_decode(q_s, kcat, wabs.astype(BF16), wv_placed.astype(BF16),
                                 sk_valid=n_all, q_off=past, k_off=0)
            wo = w_a_o[j].astype(BF16)
            states[0].append((ckv_p, kr_p, ckv_s, kr_s))
        elif m == 1 or m == 3:
            heads = B_HEADS if m == 1 else D_HEADS
            w_qkv = (w_b_qkv if m == 1 else w_d_qkv)[j].astype(BF16)
            cache_k, cache_v = (cache_b_k, cache_b_v) if m == 1 else (cache_d_k, cache_d_v)
            q_scale = HEAD_DIM ** -0.5 * LOG2E
            padded = m == 3 and D_PREV_CHUNKS * CHUNK == _token_tiling(bp, sp)[1]
            q_p, kf_p, vf_p, kb_p, vb_p = _proj_qkv(xp, g_m, sh_p, sc_p, w_qkv, heads, q_scale,
                                                    front_pad=padded)
            q_s, kf_s, vf_s, kb_s, vb_s = _proj_qkv(xs, g_m, sh_s, sc_s, w_qkv, heads, q_scale)
            lc = cache_k.shape[2]
            if m == 1:
                o_p = _stick_attention(q_p, kb_p, vb_p, sk_valid=sp, q_off=0, k_off=0)
                k_s, n_all = _with_cache(cache_k[j], kb_s, BF16, KV_BLOCK)
                v_s, _ = _with_cache(cache_v[j], vb_s, BF16, KV_BLOCK)
                o_s = _stick_attention(q_s, k_s, v_s, sk_valid=n_all, q_off=past,
                                       k_off=past - lc)
                wo = w_b_o[j].astype(BF16)
                states[1].append((kf_p, vf_p, kf_s, vf_s))
            else:
                bias = _band_bias(rel_bias_d[j])
                k_s, n_all = _with_cache(cache_k[j], kb_s, BF16, D_BAND_WIDTH)
                v_s, _ = _with_cache(cache_v[j], vb_s, BF16, D_BAND_WIDTH)
                front = D_PREV_CHUNKS * CHUNK
                if not padded:
                    kb_p, vb_p = _front_pad(kb_p, front), _front_pad(vb_p, front)
                o_p = _band_attention(q_p, kb_p, vb_p,
                                      width=D_BAND_WIDTH, sk_valid=sp, q_off=0, k_off=0,
                                      front=front, n_prev=D_PREV_CHUNKS, bias=bias,
                                      pairs_per_step=D_PAIRS_PER_STEP, shared_kv=False)
                o_s = _band_attention(q_s, k_s, v_s, width=D_BAND_WIDTH, sk_valid=n_all,
                                      q_off=past, k_off=past - lc, front=0,
                                      n_prev=D_PREV_CHUNKS, bias=bias[:, :t, :],
                                      pairs_per_step=D_PAIRS_PER_STEP, shared_kv=False)
                wo = w_d_o[j].astype(BF16)
                keep = min(D_PREV_CHUNKS * CHUNK, sp)
                k_roll = jnp.concatenate([cache_k[j].reshape(bs, lc, -1), kf_s], axis=1)[:, t:]
                v_roll = jnp.concatenate([cache_v[j].reshape(bs, lc, -1), vf_s], axis=1)[:, t:]
                states[3].append((kf_p[:, sp - keep:], vf_p[:, sp - keep:], k_roll, v_roll))
        else:
            order = _c_head_order()
            q_width, kv_width = C_HEADS * HEAD_DIM, C_KV_HEADS * HEAD_DIM
            col = np.concatenate([(order[:, None] * HEAD_DIM + np.arange(HEAD_DIM)).reshape(-1),
                                  np.arange(q_width, q_width + 2 * kv_width)])
            w_qkv = w_c_qkv[j][:, col].astype(BF16)
            b_qkv = b_c_qkv[j][col][None, :]
            sink = sink_c[j][order]
            wo = w_c_o[j].reshape(C_HEADS, HEAD_DIM, d)[order].reshape(q_width, d).astype(BF16)
            lanes = [0, HEAD_DIM]

            def project(x, sh, sc, pos):
                tb = _tables_for(x.shape[0], x.shape[1], pos, C_ROT, lanes)
                return _proj_c(x, g_m, sh, sc, w_qkv, b_qkv, tb)

            q_p, kf_p, vf_p, kb_p, vb_p = project(xp, sh_p, sc_p, pos_p)
            q_s, kf_s, vf_s, kb_s, vb_s = project(xs, sh_s, sc_s, pos_s)
            lc = cache_c_k.shape[2]
            k_s, n_all = _with_cache(cache_c_k[j], kb_s, BF16, C_BAND_WIDTH)
            v_s, _ = _with_cache(cache_c_v[j], vb_s, BF16, C_BAND_WIDTH)
            group = C_HEADS // C_KV_HEADS
            front = C_PREV_CHUNKS * CHUNK
            band_mask = jnp.asarray(_band_mask_tile(Q_TILE, C_BAND_WIDTH, C_PREV_CHUNKS))
            o_p = _band_attention(q_p, _front_pad(kb_p, front), _front_pad(vb_p, front),
                                  width=C_BAND_WIDTH, sk_valid=sp, q_off=0, k_off=0, front=front,
                                  n_prev=C_PREV_CHUNKS, pairs_per_step=group, shared_kv=True, sink=sink,
                                  bias=band_mask)
            o_s = _band_attention(q_s, k_s, v_s, width=C_BAND_WIDTH, sk_valid=n_all, q_off=past,
                                  k_off=past - lc, front=0, n_prev=C_PREV_CHUNKS, pairs_per_step=group, shared_kv=True,
                                  sink=sink, bias=band_mask[:t])
            keep = min(C_PREV_CHUNKS * CHUNK, sp)
            k_roll = jnp.concatenate([cache_c_k[j].reshape(bs, lc, -1), kf_s], axis=1)[:, t:]
            v_roll = jnp.concatenate([cache_c_v[j].reshape(bs, lc, -1), vf_s], axis=1)[:, t:]
            states[2].append((kf_p[:, sp - keep:], vf_p[:, sp - keep:], k_roll, v_roll))

        g_f = g_ffn[i][None, :]
        g_out = g_final[None, :] if i == depth - 1 else None
        xp = _block(xp, o_p, wo, gm_p, g_f, shf_p, scf_p, gf_p, w_in_all, w_out_all, i, g_out)
        xs = _block(xs, o_s, wo, gm_s, g_f, shf_s, scf_s, gf_s, w_in_all, w_out_all, i, g_out)

    y_p, y_s = xp, xs

    def stacked(entries, n_heads=None):
        outs = []
        for parts in zip(*entries):
            a = jnp.stack(parts, axis=0)
            if n_heads is not None:
                a = a.reshape(a.shape[:3] + (n_heads, HEAD_DIM))
            outs.append(a)
        return tuple(outs)

    return ((y_p, y_s) + stacked(states[0]) + stacked(states[1], B_HEADS)
            + stacked(states[2], C_KV_HEADS) + stacked(states[3], D_HEADS))
```
